```python
import math
import jax, jax.numpy as jnp
from jax import lax
import numpy as np

D_MODEL = 1024
BATCH = 4
SEQ = 4096
DEPTH = 1

D_MIX = D_MODEL
D_REC = D_MIX // 2
REC_BLOCKS = 8
REC_BLOCK_DIM = D_REC // REC_BLOCKS
CONV_WIDTH = 4
RG_C = 8.0
N_ATT_HEADS = 4
ATT_HALF_DIM = 64
ATT_V_DIM = 2 * ATT_HALF_DIM
D_ATT = N_ATT_HEADS * ATT_V_DIM
D_QK = N_ATT_HEADS * 2 * ATT_HALF_DIM
D_IN = 2 * D_REC + 2 * D_QK + D_ATT
ROPE_DIM = ATT_HALF_DIM // 4
ROPE_THETA = 500000.0
Q_BLOCK = 128
N_EXPERTS = 64
TOP_K = 8
N_GROUPS = 8
TOPK_GROUPS = 4
D_EXPERT = 256
D_SHARED = 256
ROUTE_SCALE = 2.5
MOE_BLOCK = 128
D_PLE = 256
EPS = 1e-6

kernel_name = "hymba_rglru_diffattn_moe_ple"


def rms_norm(x, g):
    xf = x.astype(jnp.float32)
    y = xf * lax.rsqrt(jnp.mean(xf * xf, axis=-1, keepdims=True) + EPS)
    return (y * g.astype(jnp.float32)).astype(x.dtype)


def rope_tables(positions):
    inv_freq = ROPE_THETA ** (-jnp.arange(0, ROPE_DIM, 2, dtype=jnp.float32) / ROPE_DIM)
    ang = positions.astype(jnp.float32)[..., None] * inv_freq
    return jnp.cos(ang), jnp.sin(ang)


def partial_rope(x, cos, sin):
    half = ROPE_DIM // 2
    c = cos[:, :, None, None, :]
    s = sin[:, :, None, None, :]
    xf = x.astype(jnp.float32)
    x1 = xf[..., :half]
    x2 = xf[..., half:ROPE_DIM]
    out = jnp.concatenate([x1 * c - x2 * s, x2 * c + x1 * s, xf[..., ROPE_DIM:]], axis=-1)
    return out.astype(x.dtype)


def rglru_mixer(xr, gate, conv_w, conv_b, w_a, b_a, w_i, b_i, rg_lambda):
    B, S, _ = xr.shape
    xc = lax.conv_general_dilated(
        xr, conv_w[:, None, :], window_strides=(1,), padding=[(CONV_WIDTH - 1, 0)],
        dimension_numbers=("NWC", "WIO", "NWC"), feature_group_count=D_REC) + conv_b
    xb = xc.reshape(B, S, REC_BLOCKS, REC_BLOCK_DIM)
    r = jax.nn.sigmoid(jnp.einsum("bsgi,gij->bsgj", xb, w_a) + b_a).reshape(B, S, D_REC)
    i = jax.nn.sigmoid(jnp.einsum("bsgi,gij->bsgj", xb, w_i) + b_i).reshape(B, S, D_REC)
    log_a = -RG_C * r.astype(jnp.float32) * jax.nn.softplus(-rg_lambda.astype(jnp.float32))
    a = jnp.exp(log_a)
    mult = jnp.sqrt(-jnp.expm1(2.0 * log_a))
    u = mult * i.astype(jnp.float32) * xc.astype(jnp.float32)

    def combine(left, right):
        a1, b1 = left
        a2, b2 = right
        return a1 * a2, a2 * b1 + b2

    _, h = lax.associative_scan(combine, (a, u), axis=1)
    y = h * jax.nn.gelu(gate.astype(jnp.float32))
    return y.astype(xr.dtype)


def diff_attention(q, k, v, cos, sin, lq1, lk1, lq2, lk2, g_sub, lam_init):
    B, S, _ = q.shape
    H, dh = N_ATT_HEADS, ATT_HALF_DIM
    q = partial_rope(q.reshape(B, S, H, 2, dh), cos, sin)
    k = partial_rope(k.reshape(B, S, H, 2, dh), cos, sin)
    v = v.reshape(B, S, H, ATT_V_DIM)
    f32 = jnp.float32
    lam = (jnp.exp(jnp.sum(lq1.astype(f32) * lk1.astype(f32)))
           - jnp.exp(jnp.sum(lq2.astype(f32) * lk2.astype(f32))) + lam_init)
    kh = k.transpose(0, 2, 3, 1, 4)
    vh = v.transpose(0, 2, 1, 3)
    nb = S // Q_BLOCK
    qb = q.transpose(0, 2, 3, 1, 4).reshape(B, H, 2, nb, Q_BLOCK, dh).transpose(3, 0, 1, 2, 4, 5)
    k_pos = jnp.arange(S)
    scale = dh ** -0.5

    def block(args):
        qblk, bi = args
        s = jnp.einsum("bhcqd,bhckd->bhcqk", qblk, kh).astype(f32) * scale
        q_pos = bi * Q_BLOCK + jnp.arange(Q_BLOCK)
        mask = k_pos[None, :] <= q_pos[:, None]
        s = jnp.where(mask, s, -jnp.inf)
        pr = jax.nn.softmax(s, axis=-1)
        w = pr[:, :, 0] - lam * pr[:, :, 1]
        return jnp.einsum("bhqk,bhkd->bhqd", w.astype(vh.dtype), vh)

    o = lax.map(block, (qb, jnp.arange(nb)))
    o = o.transpose(1, 0, 3, 2, 4).reshape(B, S, H, ATT_V_DIM)
    o = rms_norm(o, g_sub) * (1.0 - lam_init)
    return o.reshape(B, S, D_ATT)


def moe_ffn(h, w_router, e_bias, w1, w3, w2, ws1, ws3, ws2):
    B, S, D = h.shape
    T = B * S
    E, K, M = N_EXPERTS, TOP_K, MOE_BLOCK
    hf = h.reshape(T, D)
    scores = jax.nn.sigmoid((hf @ w_router).astype(jnp.float32))
    sel = scores + e_bias.astype(jnp.float32)
    grp = sel.reshape(T, N_GROUPS, E // N_GROUPS)
    grp_score = lax.top_k(grp, 2)[0].sum(-1)
    _, top_groups = lax.top_k(grp_score, TOPK_GROUPS)
    gmask = jax.nn.one_hot(top_groups, N_GROUPS).sum(1) > 0
    sel = jnp.where(jnp.repeat(gmask, E // N_GROUPS, axis=-1), sel, -jnp.inf)
    _, topi = lax.top_k(sel, K)
    topw = jnp.take_along_axis(scores, topi, axis=-1)
    topw = topw / jnp.sum(topw, axis=-1, keepdims=True) * ROUTE_SCALE
    N = T * K
    e_flat = topi.reshape(N).astype(jnp.int32)
    t_flat = jnp.repeat(jnp.arange(T, dtype=jnp.int32), K)
    w_flat = topw.reshape(N)
    order = jnp.argsort(e_flat)
    e_sorted = e_flat[order]
    t_sorted = t_flat[order]
    w_sorted = w_flat[order]
    counts = jnp.bincount(e_flat, length=E).astype(jnp.int32)
    padded = (counts + M - 1) // M * M
    pad_end = jnp.cumsum(padded)
    pad_start = pad_end - padded
    start = jnp.cumsum(counts) - counts
    dest = pad_start[e_sorted] + jnp.arange(N, dtype=jnp.int32) - start[e_sorted]
    P = N + E * M
    nblk = P // M
    row_token = jnp.zeros((P,), jnp.int32).at[dest].set(t_sorted)
    xbuf = hf[row_token].reshape(nblk, M, D)
    blk_expert = jnp.minimum(
        jnp.searchsorted(pad_end, jnp.arange(nblk, dtype=jnp.int32) * M, side="right"), E - 1)

    def expert_block(args):
        xb, e = args
        return (jax.nn.silu(xb @ w1[e]) * (xb @ w3[e])) @ w2[e]

    ybuf = lax.map(expert_block, (xbuf, blk_expert)).reshape(P, D)
    y_routed = jax.ops.segment_sum(ybuf[dest] * w_sorted[:, None].astype(ybuf.dtype),
                                   t_sorted, num_segments=T)
    y_shared = (jax.nn.silu(hf @ ws1) * (hf @ ws3)) @ ws2
    return (y_routed + y_shared).reshape(B, S, D)


def setup_inputs(seed: int = 0) -> dict:
    key = jax.random.key(seed)
    ks = jax.random.split(key, 32)
    L, D, E, F = DEPTH, D_MODEL, N_EXPERTS, D_EXPERT
    nrm = lambda k, shp, s: jax.random.normal(k, shp, jnp.float32) * s
    gain = lambda k, shp: 1.0 + 0.02 * jax.random.normal(k, shp, jnp.float32)
    a0 = jax.random.uniform(ks[9], (L, D_REC), jnp.float32, 0.9, 0.999)
    s0 = a0 ** (1.0 / RG_C)
    rg_lambda = jnp.log(s0) - jnp.log1p(-s0)
    return {
        "x": nrm(ks[0], (BATCH, SEQ, D), 1.0),
        "p": nrm(ks[1], (DEPTH, BATCH, SEQ, D_PLE), 1.0),
        "positions": jnp.broadcast_to(jnp.arange(SEQ, dtype=jnp.int32)[None, :], (BATCH, SEQ)),
        "ln_mix": gain(ks[2], (L, D)),
        "w_in": nrm(ks[3], (L, D, D_IN), D ** -0.5),
        "conv_w": nrm(ks[4], (L, CONV_WIDTH, D_REC), CONV_WIDTH ** -0.5),
        "conv_b": nrm(ks[5], (L, D_REC), 0.01),
        "w_a": nrm(ks[6], (L, REC_BLOCKS, REC_BLOCK_DIM, REC_BLOCK_DIM), REC_BLOCK_DIM ** -0.5),
        "b_a": nrm(ks[7], (L, REC_BLOCKS, REC_BLOCK_DIM), 0.01),
        "w_i": nrm(ks[8], (L, REC_BLOCKS, REC_BLOCK_DIM, REC_BLOCK_DIM), REC_BLOCK_DIM ** -0.5),
        "b_i": nrm(ks[10], (L, REC_BLOCKS, REC_BLOCK_DIM), 0.01),
        "rg_lambda": rg_lambda,
        "g_rec": gain(ks[11], (L, D_REC)),
        "lq1": nrm(ks[12], (L, ATT_HALF_DIM), 0.1),
        "lk1": nrm(ks[13], (L, ATT_HALF_DIM), 0.1),
        "lq2": nrm(ks[14], (L, ATT_HALF_DIM), 0.1),
        "lk2": nrm(ks[15], (L, ATT_HALF_DIM), 0.1),
        "g_sub": gain(ks[16], (L, ATT_V_DIM)),
        "w_out": nrm(ks[17], (L, D_MIX, D), D_MIX ** -0.5),
        "ln_moe": gain(ks[18], (L, D)),
        "w_router": nrm(ks[19], (L, D, E), D ** -0.5),
        "e_bias": nrm(ks[20], (L, E), 0.01),
        "w1": nrm(ks[21], (L, E, D, F), D ** -0.5),
        "w3": nrm(ks[22], (L, E, D, F), D ** -0.5),
        "w2": nrm(ks[23], (L, E, F, D), F ** -0.5),
        "ws1": nrm(ks[24], (L, D, D_SHARED), D ** -0.5),
        "ws3": nrm(ks[25], (L, D, D_SHARED), D ** -0.5),
        "ws2": nrm(ks[26], (L, D_SHARED, D), D_SHARED ** -0.5),
        "ln_ple": gain(ks[27], (L, D)),
        "w_ple_gate": nrm(ks[28], (L, D, D), D ** -0.5),
        "w_ple_proj": nrm(ks[29], (L, D_PLE, D), D_PLE ** -0.5),
        "ln_f": gain(ks[30], (D,)),
    }


def reference(x, p, positions, ln_mix, w_in, conv_w, conv_b, w_a, b_a, w_i, b_i, rg_lambda,
              g_rec, lq1, lk1, lq2, lk2, g_sub, w_out, ln_moe, w_router, e_bias, w1, w3, w2,
              ws1, ws3, ws2, ln_ple, w_ple_gate, w_ple_proj, ln_f):
    cos, sin = rope_tables(positions)
    h = x
    splits = [D_REC, 2 * D_REC, 2 * D_REC + D_QK, 2 * D_REC + 2 * D_QK]
    for l in range(DEPTH):
        lam_init = 0.8 - 0.6 * math.exp(-0.3 * l)
        hn = rms_norm(h, ln_mix[l])
        z = hn @ w_in[l]
        xr, gate, q, k, v = jnp.split(z, splits, axis=-1)
        y_rec = rms_norm(rglru_mixer(xr, gate, conv_w[l], conv_b[l], w_a[l], b_a[l],
                                     w_i[l], b_i[l], rg_lambda[l]), g_rec[l])
        y_att = diff_attention(q, k, v, cos, sin, lq1[l], lk1[l], lq2[l], lk2[l], g_sub[l], lam_init)
        h = h + jnp.concatenate([y_rec, y_att], axis=-1) @ w_out[l]
        h = h + moe_ffn(rms_norm(h, ln_moe[l]), w_router[l], e_bias[l], w1[l], w3[l], w2[l],
                        ws1[l], ws3[l], ws2[l])
        ple_gate = jax.nn.sigmoid(rms_norm(h, ln_ple[l]) @ w_ple_gate[l])
        h = h + ple_gate * (p[l] @ w_ple_proj[l])
    return rms_norm(h, ln_f)
```

```python
import functools
import math

import jax
import jax.numpy as jnp
from jax import lax
from jax.experimental import pallas as pl
from jax.experimental.pallas import tpu as pltpu
from jax.experimental.pallas import tpu_sc as plsc

F32 = jnp.float32
BF16 = jnp.bfloat16
U32 = jnp.uint32
I32 = jnp.int32

D_MODEL = 1024
D_REC = 512
REC_BLOCKS = 8
CONV_WIDTH = 4
RG_C = 8.0
N_HEADS = 4
HALF_DIM = 64
V_DIM = 128
D_ATT = N_HEADS * V_DIM
D_QK = N_HEADS * 2 * HALF_DIM
ROPE_DIM = 16
ROPE_THETA = 500000.0
N_EXPERTS = 64
TOP_K = 8
N_GROUPS = 8
GROUP_SIZE = N_EXPERTS // N_GROUPS
TOPK_GROUPS = 4
D_EXPERT = 256
ROUTE_SCALE = 2.5
D_PLE = 256
EPS = 1e-6

LANES = 128
SUBLANES = 8
VMEM_LIMIT = 56 * 1024 * 1024

MIX_ROWS = 256
ATT_Q = 256
ATT_K = 256
ROUTE_ROWS = 512
EXPERT_ROWS = 256
TAIL_ROWS = 256
SC_WINDOW = 128
PACK_W = 256
NEG_BIG = -1e30


def _rms(x, g):
    return x * lax.rsqrt(jnp.mean(x * x, axis=-1, keepdims=True) + EPS) * g


def _dot(a, b):
    return jnp.dot(a, b, preferred_element_type=F32)


def _pack_pair(lo, hi):
    lo_bits = lax.bitcast_convert_type(lo.astype(BF16).astype(F32), U32)
    hi_bits = lax.bitcast_convert_type(hi.astype(BF16).astype(F32), U32)
    return (lo_bits >> 16) | (hi_bits & jnp.uint32(0xFFFF0000))


def _unpack_pair(p):
    lo = lax.bitcast_convert_type(p << 16, F32)
    hi = lax.bitcast_convert_type(p & jnp.uint32(0xFFFF0000), F32)
    return lo, hi


def _pack_row(x):
    w = PACK_W
    return _pack_pair(x[:, 0:w], x[:, w:2 * w]), _pack_pair(x[:, 2 * w:3 * w], x[:, 3 * w:4 * w])


def _unpack_row(pa, pb):
    c0, c1 = _unpack_pair(pa)
    c2, c3 = _unpack_pair(pb)
    return jnp.concatenate([c0, c1, c2, c3], axis=1)


def _shift_rows(a, s, fill, row):
    n, c = a.shape
    if s % SUBLANES == 0:
        return jnp.concatenate([jnp.full((s, c), fill, a.dtype), a[:n - s]], axis=0)
    return jnp.where(row >= s, pltpu.roll(a, s, 0), fill)


def _mix_in_kernel(x_ref, cos_ref, sa_ref, sb_ref, lnm_ref, win_ref, cw_ref, cb_ref, wa_ref, ba_ref,
                   wi_ref, bi_ref, lam_ref, grec_ref,
                   yrec_ref, q_ref, k_ref, v_ref, tail_ref, hcarry_ref):
    tm = x_ref.shape[0]

    @pl.when(pl.program_id(1) == 0)
    def _():
        tail_ref[...] = jnp.zeros_like(tail_ref)
        hcarry_ref[...] = jnp.zeros_like(hcarry_ref)

    hn = _rms(x_ref[...], lnm_ref[...]).astype(BF16)
    xr = _dot(hn, win_ref[:, 0:D_REC])

    row = lax.broadcasted_iota(I32, (tm, D_REC), 0)
    row8 = lax.broadcasted_iota(I32, (SUBLANES, D_REC), 0)
    tail = tail_ref[...]
    xc = cb_ref[...] + cw_ref[CONV_WIDTH - 1:CONV_WIDTH, :] * xr
    for j in range(1, CONV_WIDTH):
        rolled = pltpu.roll(xr, j, 0)
        head = jnp.where(row8 < j, pltpu.roll(tail, j, 0), rolled[:SUBLANES])
        shifted = jnp.concatenate([head, rolled[SUBLANES:]], axis=0)
        xc = xc + cw_ref[CONV_WIDTH - 1 - j:CONV_WIDTH - j, :] * shifted
    tail_ref[...] = xr[tm - SUBLANES:, :]

    xcb = xc.astype(BF16)
    half = D_REC // 2
    ra = jnp.concatenate([_dot(xcb[:, :half], wa_ref[0]), _dot(xcb[:, half:], wa_ref[1])], axis=1)
    ri = jnp.concatenate([_dot(xcb[:, :half], wi_ref[0]), _dot(xcb[:, half:], wi_ref[1])], axis=1)
    r = jax.nn.sigmoid(ra + ba_ref[...])
    ig = jax.nn.sigmoid(ri + bi_ref[...])
    lam = lam_ref[...]
    softplus_neg = jnp.maximum(-lam, 0.0) + jnp.log(1.0 + jnp.exp(-jnp.abs(lam)))
    log_a = -RG_C * r * softplus_neg
    a = jnp.exp(log_a)
    u = jnp.sqrt(1.0 - jnp.exp(2.0 * log_a)) * ig * xc

    s = 1
    while s < tm:
        u = u + a * _shift_rows(u, s, 0.0, row)
        a = a * _shift_rows(a, s, 1.0, row)
        s *= 2
    h = u + a * hcarry_ref[...]
    hcarry_ref[...] = h[tm - 1:tm, :]

    gate = _dot(hn, win_ref[:, D_REC:2 * D_REC])
    y = h * jax.nn.gelu(gate)
    yrec_ref[...] = _rms(y, grec_ref[...]).astype(BF16)

    cosf, sa, sb = cos_ref[...], sa_ref[...], sb_ref[...]
    scale = HALF_DIM ** -0.5
    for name, out_ref, off, mul in (("q", q_ref, 2 * D_REC, scale), ("k", k_ref, 2 * D_REC + D_QK, 1.0)):
        for c in range(D_QK // LANES):
            zc = _dot(hn, win_ref[:, off + c * LANES: off + (c + 1) * LANES])
            rot = zc * cosf + pltpu.roll(zc, LANES - ROPE_DIM // 2, 1) * sa + pltpu.roll(zc, ROPE_DIM // 2, 1) * sb
            out_ref[:, c * LANES:(c + 1) * LANES] = (rot * mul).astype(BF16)
    v_ref[...] = _dot(hn, win_ref[:, 2 * D_REC + 2 * D_QK:]).astype(BF16)


def _mix_in(x2, cos_t, sa_t, sb_t, ln_mix, w_in, conv_w, conv_b, wa_bd, b_a, wi_bd, b_i, rg_lambda, g_rec,
            batch, seq):
    tm = MIX_ROWS
    nt = seq // tm
    d_in = w_in.shape[1]
    row_map = lambda b, i: (b * nt + i, 0)
    fixed2 = lambda b, i: (0, 0)
    fixed3 = lambda b, i: (0, 0, 0)
    t = batch * seq
    out_shapes = (
        jax.ShapeDtypeStruct((t, D_REC), BF16),
        jax.ShapeDtypeStruct((t, D_QK), BF16),
        jax.ShapeDtypeStruct((t, D_QK), BF16),
        jax.ShapeDtypeStruct((t, D_ATT), BF16),
    )
    return pl.pallas_call(
        _mix_in_kernel,
        grid=(batch, nt),
        in_specs=[
            pl.BlockSpec((tm, D_MODEL), row_map),
            pl.BlockSpec((tm, LANES), row_map),
            pl.BlockSpec((tm, LANES), row_map),
            pl.BlockSpec((tm, LANES), row_map),
            pl.BlockSpec((1, D_MODEL), fixed2),
            pl.BlockSpec((D_MODEL, d_in), fixed2),
            pl.BlockSpec((CONV_WIDTH, D_REC), fixed2),
            pl.BlockSpec((1, D_REC), fixed2),
            pl.BlockSpec((2, D_REC // 2, D_REC // 2), fixed3),
            pl.BlockSpec((1, D_REC), fixed2),
            pl.BlockSpec((2, D_REC // 2, D_REC // 2), fixed3),
            pl.BlockSpec((1, D_REC), fixed2),
            pl.BlockSpec((1, D_REC), fixed2),
            pl.BlockSpec((1, D_REC), fixed2),
        ],
        out_specs=[
            pl.BlockSpec((tm, D_REC), row_map),
            pl.BlockSpec((tm, D_QK), row_map),
            pl.BlockSpec((tm, D_QK), row_map),
            pl.BlockSpec((tm, D_ATT), row_map),
        ],
        out_shape=out_shapes,
        scratch_shapes=[pltpu.VMEM((SUBLANES, D_REC), F32), pltpu.VMEM((1, D_REC), F32)],
        compiler_params=pltpu.CompilerParams(
            dimension_semantics=("arbitrary", "arbitrary"), vmem_limit_bytes=VMEM_LIMIT),
        name="mix_in",
    )(x2, cos_t, sa_t, sb_t, ln_mix, w_in, conv_w, conv_b, wa_bd, b_a, wi_bd, b_i, rg_lambda, g_rec)


def _attn_kernel(lq1_ref, lk1_ref, lq2_ref, lk2_ref, gsub_ref, q_ref, k_ref, v_ref, o_ref, *, lam_init):
    tq = q_ref.shape[0]
    tk = ATT_K
    i = pl.program_id(2)

    q = q_ref[...]
    lane = lax.broadcasted_iota(I32, (tq, LANES), 1)
    zero = jnp.zeros_like(q)
    qq = jnp.concatenate([jnp.where(lane < HALF_DIM, q, zero), jnp.where(lane >= HALF_DIM, q, zero)], axis=0)

    def step(j, carry, masked):
        m, l, acc = carry
        kj = k_ref[pl.ds(pl.multiple_of(j * tk, tk), tk), :]
        vj = v_ref[pl.ds(pl.multiple_of(j * tk, tk), tk), :]
        s = lax.dot_general(qq, kj, (((1,), (1,)), ((), ())), preferred_element_type=F32)
        if masked:
            r = lax.broadcasted_iota(I32, (2 * tq, tk), 0)
            qpos = i * tq + jnp.where(r >= tq, r - tq, r)
            kpos = j * tk + lax.broadcasted_iota(I32, (2 * tq, tk), 1)
            s = jnp.where(kpos <= qpos, s, NEG_BIG)
        m_new = jnp.maximum(m, jnp.max(s, axis=-1, keepdims=True))
        alpha = jnp.exp(m - m_new)
        p = jnp.exp(s - m_new)
        l = alpha * l + jnp.sum(p, axis=-1, keepdims=True)
        acc = alpha * acc + _dot(p.astype(BF16), vj)
        return m_new, l, acc

    n_full = (i * tq) // tk
    n_all = ((i + 1) * tq + tk - 1) // tk
    init = (jnp.full((2 * tq, 1), NEG_BIG, F32), jnp.zeros((2 * tq, 1), F32), jnp.zeros((2 * tq, V_DIM), F32))
    carry = lax.fori_loop(0, n_full, lambda j, c: step(j, c, False), init)
    m, l, acc = lax.fori_loop(n_full, n_all, lambda j, c: step(j, c, True), carry)

    lam = (jnp.exp(jnp.sum(lq1_ref[...] * lk1_ref[...], axis=-1, keepdims=True))
           - jnp.exp(jnp.sum(lq2_ref[...] * lk2_ref[...], axis=-1, keepdims=True)) + lam_init)
    o = acc / l
    o = o[:tq] - lam * o[tq:]
    o_ref[...] = (_rms(o, gsub_ref[...]) * (1.0 - lam_init)).astype(BF16)


def _attention(q, k, v, lq1, lk1, lq2, lk2, g_sub, batch, seq, lam_init):
    tq = ATT_Q
    nq = seq // tq
    vec = lambda b, h, i: (0, 0)
    return pl.pallas_call(
        functools.partial(_attn_kernel, lam_init=lam_init),
        grid=(batch, N_HEADS, nq),
        in_specs=[
            pl.BlockSpec((1, HALF_DIM), vec),
            pl.BlockSpec((1, HALF_DIM), vec),
            pl.BlockSpec((1, HALF_DIM), vec),
            pl.BlockSpec((1, HALF_DIM), vec),
            pl.BlockSpec((1, V_DIM), vec),
            pl.BlockSpec((tq, LANES), lambda b, h, i: (b * nq + i, h)),
            pl.BlockSpec((seq, LANES), lambda b, h, i: (b, h)),
            pl.BlockSpec((seq, V_DIM), lambda b, h, i: (b, h)),
        ],
        out_specs=pl.BlockSpec((tq, V_DIM), lambda b, h, i: (b * nq + i, h)),
        out_shape=jax.ShapeDtypeStruct((batch * seq, D_ATT), BF16),
        compiler_params=pltpu.CompilerParams(
            dimension_semantics=("arbitrary", "arbitrary", "arbitrary"), vmem_limit_bytes=VMEM_LIMIT),
        name="attention",
    )(lq1, lk1, lq2, lk2, g_sub, q, k, v)


def _sublane_total(x, op):
    return op(x, axis=0, keepdims=True)


def _route_kernel(x_ref, yrec_ref, yatt_ref, woa_ref, wob_ref, lnmoe_ref, wrt_ref, ebias_ref, tri_ref, low_ref,
                  h1_ref, hpa_ref, hpb_ref, ek_ref, wk_ref, rk_ref, cnt_ref, carry_ref):
    tm = x_ref.shape[0]
    e_n = N_EXPERTS

    @pl.when(pl.program_id(0) == 0)
    def _():
        carry_ref[...] = jnp.zeros_like(carry_ref)

    h1 = x_ref[...] + _dot(yrec_ref[...], woa_ref[...]) + _dot(yatt_ref[...], wob_ref[...])
    h1_ref[...] = h1
    hn = _rms(h1, lnmoe_ref[...])
    pa, pb = _pack_row(hn)
    hpa_ref[...] = pa
    hpb_ref[...] = pb

    logits = lax.dot_general(wrt_ref[...], hn, (((1,), (1,)), ((), ())),
                             precision=lax.Precision.HIGHEST, preferred_element_type=F32)
    scores = jax.nn.sigmoid(logits)
    sel = scores + ebias_ref[...]

    sel3 = sel.reshape(N_GROUPS, GROUP_SIZE, tm)
    idx3 = lax.broadcasted_iota(I32, (N_GROUPS, GROUP_SIZE, tm), 1)
    m1 = jnp.max(sel3, axis=1, keepdims=True)
    first = jnp.min(jnp.where(sel3 == m1, idx3, GROUP_SIZE), axis=1, keepdims=True)
    m2 = jnp.max(jnp.where(idx3 == first, -jnp.inf, sel3), axis=1, keepdims=True)
    gscore = (m1 + m2).reshape(N_GROUPS, tm)

    gidx = lax.broadcasted_iota(I32, (N_GROUPS, tm), 0)
    beaten = jnp.zeros((N_GROUPS, tm), I32)
    for g in range(N_GROUPS):
        other = gscore[g:g + 1, :]
        beats = (other > gscore) | ((other == gscore) & (g < gidx))
        beaten = beaten + beats.astype(I32)
    gkeep = beaten < TOPK_GROUPS
    keep = jnp.broadcast_to(gkeep.reshape(N_GROUPS, 1, tm), (N_GROUPS, GROUP_SIZE, tm)).reshape(e_n, tm)
    selm = jnp.where(keep, sel, -jnp.inf)

    eidx = lax.broadcasted_iota(I32, (e_n, tm), 0)
    beaten = jnp.zeros((e_n, tm), I32)
    for e in range(e_n):
        other = selm[e:e + 1, :]
        beats = (other > selm) | ((other == selm) & (e < eidx))
        beaten = beaten + beats.astype(I32)
    chosen = beaten < TOP_K
    chosen_f = chosen.astype(F32)

    wsum = _sublane_total(jnp.where(chosen, scores, 0.0), jnp.sum)
    weight = scores * (ROUTE_SCALE / wsum)

    chosen_b = chosen_f.astype(BF16)
    prefix = _dot(chosen_b, tri_ref[...])
    rank = prefix + carry_ref[...]
    carry_new = carry_ref[...] + jnp.sum(chosen_f, axis=1, keepdims=True)
    carry_ref[...] = carry_new
    cnt_ref[...] = carry_new.astype(I32)

    slot = _dot(low_ref[...], chosen_b)
    eidx_f = eidx.astype(F32)
    ek, wk, rk = [], [], []
    for kk in range(TOP_K):
        pick = chosen & (slot == float(kk))
        ek.append(_sublane_total(jnp.where(pick, eidx_f, 0.0), jnp.sum))
        wk.append(_sublane_total(jnp.where(pick, weight, 0.0), jnp.sum))
        rk.append(_sublane_total(jnp.where(pick, rank, 0.0), jnp.sum))
    ek_ref[...] = jnp.concatenate(ek, axis=0).astype(I32)
    wk_ref[...] = jnp.concatenate(wk, axis=0)
    rk_ref[...] = jnp.concatenate(rk, axis=0).astype(I32)


def _route(x2, y_rec, y_att, wo_a, wo_b, ln_moe, w_router_t, e_bias_col):
    t = x2.shape[0]
    tm = ROUTE_ROWS
    nt = t // tm
    row_map = lambda i: (i, 0)
    col_map = lambda i: (0, i)
    fixed = lambda i: (0, 0)
    tri = (lax.broadcasted_iota(I32, (tm, tm), 0) < lax.broadcasted_iota(I32, (tm, tm), 1)).astype(BF16)
    low = (lax.broadcasted_iota(I32, (N_EXPERTS, N_EXPERTS), 1)
           < lax.broadcasted_iota(I32, (N_EXPERTS, N_EXPERTS), 0)).astype(BF16)
    out_shapes = (
        jax.ShapeDtypeStruct((t, D_MODEL), F32),
        jax.ShapeDtypeStruct((t, PACK_W), U32),
        jax.ShapeDtypeStruct((t, PACK_W), U32),
        jax.ShapeDtypeStruct((TOP_K, t), I32),
        jax.ShapeDtypeStruct((TOP_K, t), F32),
        jax.ShapeDtypeStruct((TOP_K, t), I32),
        jax.ShapeDtypeStruct((N_EXPERTS, 1), I32),
    )
    return pl.pallas_call(
        _route_kernel,
        grid=(nt,),
        in_specs=[
            pl.BlockSpec((tm, D_MODEL), row_map),
            pl.BlockSpec((tm, D_REC), row_map),
            pl.BlockSpec((tm, D_ATT), row_map),
            pl.BlockSpec((D_REC, D_MODEL), fixed),
            pl.BlockSpec((D_ATT, D_MODEL), fixed),
            pl.BlockSpec((1, D_MODEL), fixed),
            pl.BlockSpec((N_EXPERTS, D_MODEL), fixed),
            pl.BlockSpec((N_EXPERTS, 1), fixed),
            pl.BlockSpec((tm, tm), fixed),
            pl.BlockSpec((N_EXPERTS, N_EXPERTS), fixed),
        ],
        out_specs=[
            pl.BlockSpec((tm, D_MODEL), row_map),
            pl.BlockSpec((tm, PACK_W), row_map),
            pl.BlockSpec((tm, PACK_W), row_map),
            pl.BlockSpec((TOP_K, tm), col_map),
            pl.BlockSpec((TOP_K, tm), col_map),
            pl.BlockSpec((TOP_K, tm), col_map),
            pl.BlockSpec((N_EXPERTS, 1), fixed),
        ],
        out_shape=out_shapes,
        scratch_shapes=[pltpu.VMEM((N_EXPERTS, 1), F32)],
        compiler_params=pltpu.CompilerParams(dimension_semantics=("arbitrary",), vmem_limit_bytes=VMEM_LIMIT),
        name="route",
    )(x2, y_rec, y_att, wo_a, wo_b, ln_moe, w_router_t, e_bias_col, tri, low)


def _sc_mesh():
    return plsc.VectorSubcoreMesh(core_axis_name="core", subcore_axis_name="subcore")


def _sc_dispatch(rows, dest, n_out):
    t, w = rows.shape
    kk = dest.shape[0]

    @pl.kernel(out_type=jax.ShapeDtypeStruct((n_out, w), rows.dtype), mesh=_sc_mesh(), scratch_types=[])
    def kern(x_hbm, i_hbm, o_hbm):
        def body(x_vmem, i_vmem):
            for k in range(kk):
                pltpu.sync_copy(x_vmem, o_hbm.at[i_vmem.at[k]])

        pltpu.emit_pipeline(
            body,
            grid=(t // SC_WINDOW,),
            in_specs=[pl.BlockSpec((SC_WINDOW, w), lambda i: (i, 0)),
                      pl.BlockSpec((kk, SC_WINDOW), lambda i: (0, i))],
            out_specs=[],
            core_axis_name=("core", "subcore"),
            dimension_semantics=(pltpu.PARALLEL,),
        )(x_hbm, i_hbm)

    return kern(rows, dest)


def _sc_combine(rows, dest):
    kk, t = dest.shape
    w = rows.shape[1]
    flat = dest.reshape(1, kk * t)

    @pl.kernel(out_type=jax.ShapeDtypeStruct((kk * t, w), rows.dtype), mesh=_sc_mesh(), scratch_types=[])
    def kern(y_hbm, i_hbm, o_hbm):
        def body(i_vmem, o_vmem):
            pltpu.sync_copy(y_hbm.at[i_vmem.at[0]], o_vmem)

        pltpu.emit_pipeline(
            body,
            grid=(kk * t // SC_WINDOW,),
            in_specs=[pl.BlockSpec((1, SC_WINDOW), lambda i: (0, i))],
            out_specs=[pl.BlockSpec((SC_WINDOW, w), lambda i: (i, 0))],
            core_axis_name=("core", "subcore"),
            dimension_semantics=(pltpu.PARALLEL,),
        )(i_hbm, o_hbm)

    return kern(rows, flat).reshape(kk, t, w)


def _experts_kernel(blk_expert_ref, n_used_ref, xa_ref, xb_ref, w1_ref, w3_ref, w2_ref, ya_ref, yb_ref,
                    w1b_ref, w3b_ref, w2b_ref):
    b = pl.program_id(0)
    prev = blk_expert_ref[jnp.maximum(b - 1, 0)]
    fresh = (b == 0) | (blk_expert_ref[b] != prev)

    @pl.when(fresh)
    def _():
        w1b_ref[...] = w1_ref[0].astype(BF16)
        w3b_ref[...] = w3_ref[0].astype(BF16)
        w2b_ref[...] = w2_ref[0].astype(BF16)

    @pl.when(b < n_used_ref[0])
    def _():
        x = _unpack_row(xa_ref[...], xb_ref[...]).astype(BF16)
        a = _dot(x, w1b_ref[...])
        g = _dot(x, w3b_ref[...])
        hmid = (jax.nn.silu(a) * g).astype(BF16)
        y = _dot(hmid, w2b_ref[...])
        pa, pb = _pack_row(y)
        ya_ref[...] = pa
        yb_ref[...] = pb


def _experts(xa, xb, w1, w3, w2, blk_expert, n_used):
    p = xa.shape[0]
    m = EXPERT_ROWS
    nblk = p // m
    row_map = lambda b, be, nu: (b, 0)
    w_map = lambda b, be, nu: (be[b], 0, 0)
    grid_spec = pltpu.PrefetchScalarGridSpec(
        num_scalar_prefetch=2,
        grid=(nblk,),
        in_specs=[
            pl.BlockSpec((m, PACK_W), row_map),
            pl.BlockSpec((m, PACK_W), row_map),
            pl.BlockSpec((1, D_MODEL, D_EXPERT), w_map),
            pl.BlockSpec((1, D_MODEL, D_EXPERT), w_map),
            pl.BlockSpec((1, D_EXPERT, D_MODEL), w_map),
        ],
        out_specs=[pl.BlockSpec((m, PACK_W), row_map), pl.BlockSpec((m, PACK_W), row_map)],
        scratch_shapes=[pltpu.VMEM((D_MODEL, D_EXPERT), BF16), pltpu.VMEM((D_MODEL, D_EXPERT), BF16),
                        pltpu.VMEM((D_EXPERT, D_MODEL), BF16)],
    )
    return pl.pallas_call(
        _experts_kernel,
        grid_spec=grid_spec,
        out_shape=(jax.ShapeDtypeStruct((p, PACK_W), U32), jax.ShapeDtypeStruct((p, PACK_W), U32)),
        compiler_params=pltpu.CompilerParams(dimension_semantics=("arbitrary",), vmem_limit_bytes=VMEM_LIMIT),
        name="experts",
    )(blk_expert, n_used, xa, xb, w1, w3, w2)


def _tail_kernel(h1_ref, ga_ref, gb_ref, wk_ref, p_ref, lnmoe_ref, ws1_ref, ws3_ref, ws2_ref, lnple_ref,
                 wpg_ref, wpp_ref, lnf_ref, o_ref):
    h1 = h1_ref[...]
    hn = _rms(h1, lnmoe_ref[...]).astype(BF16)
    shared = _dot((jax.nn.silu(_dot(hn, ws1_ref[...])) * _dot(hn, ws3_ref[...])).astype(BF16), ws2_ref[...])
    wk = wk_ref[...]
    routed = jnp.zeros_like(h1)
    for kk in range(TOP_K):
        routed = routed + wk[:, kk:kk + 1] * _unpack_row(ga_ref[kk], gb_ref[kk])
    h2 = h1 + routed + shared
    gate = jax.nn.sigmoid(_dot(_rms(h2, lnple_ref[...]).astype(BF16), wpg_ref[...]))
    h3 = h2 + gate * _dot(p_ref[...].astype(BF16), wpp_ref[...])
    o_ref[...] = _rms(h3, lnf_ref[...])


def _tail(h1, ga, gb, wk_t, p2, ln_moe, ws1, ws3, ws2, ln_ple, w_pg, w_pp, ln_f):
    t = h1.shape[0]
    tm = TAIL_ROWS
    row_map = lambda i: (i, 0)
    fixed = lambda i: (0, 0)
    g_map = lambda i: (0, i, 0)
    d_sh = ws1.shape[1]
    return pl.pallas_call(
        _tail_kernel,
        grid=(t // tm,),
        in_specs=[
            pl.BlockSpec((tm, D_MODEL), row_map),
            pl.BlockSpec((TOP_K, tm, PACK_W), g_map),
            pl.BlockSpec((TOP_K, tm, PACK_W), g_map),
            pl.BlockSpec((tm, TOP_K), row_map),
            pl.BlockSpec((tm, D_PLE), row_map),
            pl.BlockSpec((1, D_MODEL), fixed),
            pl.BlockSpec((D_MODEL, d_sh), fixed),
            pl.BlockSpec((D_MODEL, d_sh), fixed),
            pl.BlockSpec((d_sh, D_MODEL), fixed),
            pl.BlockSpec((1, D_MODEL), fixed),
            pl.BlockSpec((D_MODEL, D_MODEL), fixed),
            pl.BlockSpec((D_PLE, D_MODEL), fixed),
            pl.BlockSpec((1, D_MODEL), fixed),
        ],
        out_specs=pl.BlockSpec((tm, D_MODEL), row_map),
        out_shape=jax.ShapeDtypeStruct((t, D_MODEL), F32),
        compiler_params=pltpu.CompilerParams(dimension_semantics=("arbitrary",), vmem_limit_bytes=VMEM_LIMIT),
        name="tail",
    )(h1, ga, gb, wk_t, p2, ln_moe, ws1, ws3, ws2, ln_ple, w_pg, w_pp, ln_f)


def _rope_tables(positions):
    half = ROPE_DIM // 2
    inv_freq = ROPE_THETA ** (-jnp.arange(0, ROPE_DIM, 2, dtype=F32) / ROPE_DIM)
    ang = positions.reshape(-1).astype(F32)[:, None] * inv_freq
    cos, sin = jnp.cos(ang), jnp.sin(ang)
    t = ang.shape[0]
    ones = jnp.ones((t, HALF_DIM - ROPE_DIM), F32)
    zeros = lambda n: jnp.zeros((t, n), F32)
    cos64 = jnp.concatenate([cos, cos, ones], axis=1)
    sa64 = jnp.concatenate([-sin, zeros(HALF_DIM - half)], axis=1)
    sb64 = jnp.concatenate([zeros(half), sin, zeros(HALF_DIM - ROPE_DIM)], axis=1)
    tile2 = lambda a: jnp.concatenate([a, a], axis=1)
    return tile2(cos64), tile2(sa64), tile2(sb64)


def _block_diag_tiles(w):
    nb, bd, _ = w.shape
    per = nb // 2
    tiles = []
    for tix in range(2):
        rows = []
        for j in range(per):
            rows.append(jnp.concatenate(
                [w[tix * per + j] if c == j else jnp.zeros((bd, bd), w.dtype) for c in range(per)], axis=1))
        tiles.append(jnp.concatenate(rows, axis=0))
    return jnp.stack(tiles).astype(BF16)


def _layer(h, p_l, tables, lam_init, ln_mix, w_in, conv_w, conv_b, w_a, b_a, w_i, b_i, rg_lambda, g_rec,
           lq1, lk1, lq2, lk2, g_sub, w_out, ln_moe, w_router, e_bias, w1, w3, w2, ws1, ws3, ws2,
           ln_ple, w_ple_gate, w_ple_proj, ln_out):
    batch, seq, _ = h.shape
    t = batch * seq
    x2 = h.reshape(t, D_MODEL)
    row = lambda a: a.reshape(1, -1)
    cos_t, sa_t, sb_t = tables

    y_rec, q, k, v = _mix_in(
        x2, cos_t, sa_t, sb_t, row(ln_mix), w_in.astype(BF16), conv_w, row(conv_b),
        _block_diag_tiles(w_a), row(b_a), _block_diag_tiles(w_i), row(b_i), row(rg_lambda), row(g_rec),
        batch, seq)
    y_att = _attention(q, k, v, row(lq1), row(lk1), row(lq2), row(lk2), row(g_sub), batch, seq, lam_init)

    w_out_b = w_out.astype(BF16)
    h1, hpa, hpb, ek, wk, rk, counts = _route(
        x2, y_rec, y_att, w_out_b[:D_REC], w_out_b[D_REC:], row(ln_moe), w_router.T, e_bias.reshape(-1, 1))

    m = EXPERT_ROWS
    counts = counts.reshape(-1)
    padded = (counts + m - 1) // m * m
    pad_end = jnp.cumsum(padded)
    pad_start = pad_end - padded
    n_rows = t * TOP_K + N_EXPERTS * m
    nblk = n_rows // m
    n_used = (pad_end[-1] // m).astype(I32).reshape(1)
    blk_expert = jnp.minimum(
        jnp.searchsorted(pad_end, jnp.arange(nblk, dtype=I32) * m, side="right"), N_EXPERTS - 1).astype(I32)
    blk_expert = jnp.where(jnp.arange(nblk) < n_used[0], blk_expert, blk_expert[jnp.maximum(n_used[0] - 1, 0)])
    dest = pad_start[ek] + rk

    xa = _sc_dispatch(hpa, dest, n_rows)
    xb = _sc_dispatch(hpb, dest, n_rows)
    ya, yb = _experts(xa, xb, w1, w3, w2, blk_expert, n_used)
    ga = _sc_combine(ya, dest)
    gb = _sc_combine(yb, dest)

    out = _tail(h1, ga, gb, wk.T, p_l.reshape(t, D_PLE), row(ln_moe), ws1.astype(BF16), ws3.astype(BF16),
                ws2.astype(BF16), row(ln_ple), w_ple_gate.astype(BF16), w_ple_proj.astype(BF16), row(ln_out))
    return out.reshape(batch, seq, D_MODEL)


def kernel(x, p, positions, ln_mix, w_in, conv_w, conv_b, w_a, b_a, w_i, b_i, rg_lambda, g_rec, lq1, lk1, lq2,
           lk2, g_sub, w_out, ln_moe, w_router, e_bias, w1, w3, w2, ws1, ws3, ws2, ln_ple, w_ple_gate,
           w_ple_proj, ln_f):
    depth = w_in.shape[0]
    assert depth == 1, "the fused tail applies the final norm; one layer supported"
    tables = _rope_tables(positions)
    lam_init = 0.8 - 0.6 * math.exp(-0.3 * 0)
    return _layer(x, p[0], tables, lam_init, ln_mix[0], w_in[0], conv_w[0], conv_b[0], w_a[0], b_a[0], w_i[0],
                  b_i[0], rg_lambda[0], g_rec[0], lq1[0], lk1[0], lq2[0], lk2[0], g_sub[0], w_out[0], ln_moe[0],
                  w_router[0], e_bias[0], w1[0], w3[0], w2[0], ws1[0], ws3[0], ws2[0], ln_ple[0], w_ple_gate[0],
                  w_ple_proj[0], ln_f)
```

```python
import functools
import math

import jax
import jax.numpy as jnp
from jax import lax
from jax.experimental import pallas as pl
from jax.experimental.pallas import tpu as pltpu
from jax.experimental.pallas import tpu_sc as plsc

F32 = jnp.float32
BF16 = jnp.bfloat16
U32 = jnp.uint32
I32 = jnp.int32

D_MODEL = 1024
D_REC = 512
REC_BLOCKS = 8
CONV_WIDTH = 4
RG_C = 8.0
N_HEADS = 4
HALF_DIM = 64
V_DIM = 128
D_ATT = N_HEADS * V_DIM
D_QK = N_HEADS * 2 * HALF_DIM
ROPE_DIM = 16
ROPE_THETA = 500000.0
N_EXPERTS = 64
TOP_K = 8
N_GROUPS = 8
GROUP_SIZE = N_EXPERTS // N_GROUPS
TOPK_GROUPS = 4
D_EXPERT = 256
ROUTE_SCALE = 2.5
D_PLE = 256
EPS = 1e-6

LANES = 128
SUBLANES = 8
VMEM_LIMIT = 56 * 1024 * 1024

MIX_ROWS = 512
ATT_Q = 256
ATT_LANES = 256
ROUTE_ROWS = 512
EXPERT_ROWS = 256
TAIL_ROWS = 256
SC_WINDOW = 128
PACK_W = 256
NEG_BIG = -1e30


def _rms(x, g):
    return x * lax.rsqrt(jnp.mean(x * x, axis=-1, keepdims=True) + EPS) * g


def _dot(a, b):
    return jnp.dot(a, b, preferred_element_type=F32)


def _pack_pair(lo, hi):
    lo_bits = lax.bitcast_convert_type(lo.astype(BF16).astype(F32), U32)
    hi_bits = lax.bitcast_convert_type(hi.astype(BF16).astype(F32), U32)
    return (lo_bits >> 16) | (hi_bits & jnp.uint32(0xFFFF0000))


def _unpack_pair(p):
    lo = lax.bitcast_convert_type(p << 16, F32)
    hi = lax.bitcast_convert_type(p & jnp.uint32(0xFFFF0000), F32)
    return lo, hi


def _pack_row(x):
    w = PACK_W
    return _pack_pair(x[:, 0:w], x[:, w:2 * w]), _pack_pair(x[:, 2 * w:3 * w], x[:, 3 * w:4 * w])


def _unpack_row(pa, pb):
    c0, c1 = _unpack_pair(pa)
    c2, c3 = _unpack_pair(pb)
    return jnp.concatenate([c0, c1, c2, c3], axis=1)


def _shift_rows(a, s, fill, row):
    n, c = a.shape
    if s % SUBLANES == 0:
        return jnp.concatenate([jnp.full((s, c), fill, a.dtype), a[:n - s]], axis=0)
    return jnp.where(row >= s, pltpu.roll(a, s, 0), fill)


def _mix_in_kernel(x_ref, pos_ref, invf_ref, pat_ref, lnm_ref, win_ref, cw_ref, cb_ref, wa_ref, ba_ref,
                   wi_ref, bi_ref, lam_ref, grec_ref,
                   yrec_ref, q_ref, k_ref, vt_ref, tail_ref, hcarry_ref):
    tm = x_ref.shape[0]

    @pl.when(pl.program_id(1) == 0)
    def _():
        tail_ref[...] = jnp.zeros_like(tail_ref)
        hcarry_ref[...] = jnp.zeros_like(hcarry_ref)

    hn = _rms(x_ref[...], lnm_ref[...]).astype(BF16)
    xr = _dot(hn, win_ref[:, 0:D_REC])

    row = lax.broadcasted_iota(I32, (tm, D_REC), 0)
    row8 = lax.broadcasted_iota(I32, (SUBLANES, D_REC), 0)
    tail = tail_ref[...]
    xc = cb_ref[...] + cw_ref[CONV_WIDTH - 1:CONV_WIDTH, :] * xr
    for j in range(1, CONV_WIDTH):
        rolled = pltpu.roll(xr, j, 0)
        head = jnp.where(row8 < j, pltpu.roll(tail, j, 0), rolled[:SUBLANES])
        shifted = jnp.concatenate([head, rolled[SUBLANES:]], axis=0)
        xc = xc + cw_ref[CONV_WIDTH - 1 - j:CONV_WIDTH - j, :] * shifted
    tail_ref[...] = xr[tm - SUBLANES:, :]

    xcb = xc.astype(BF16)
    half = D_REC // 2
    ra = jnp.concatenate([_dot(xcb[:, :half], wa_ref[0]), _dot(xcb[:, half:], wa_ref[1])], axis=1)
    ri = jnp.concatenate([_dot(xcb[:, :half], wi_ref[0]), _dot(xcb[:, half:], wi_ref[1])], axis=1)
    r = jax.nn.sigmoid(ra + ba_ref[...])
    ig = jax.nn.sigmoid(ri + bi_ref[...])
    lam = lam_ref[...]
    softplus_neg = jnp.maximum(-lam, 0.0) + jnp.log(1.0 + jnp.exp(-jnp.abs(lam)))
    log_a = -RG_C * r * softplus_neg
    a = jnp.exp(log_a)
    u = jnp.sqrt(1.0 - jnp.exp(2.0 * log_a)) * ig * xc

    s = 1
    while s < tm:
        u = u + a * _shift_rows(u, s, 0.0, row)
        a = a * _shift_rows(a, s, 1.0, row)
        s *= 2
    h = u + a * hcarry_ref[...]
    hcarry_ref[...] = h[tm - 1:tm, :]

    gate = _dot(hn, win_ref[:, D_REC:2 * D_REC])
    y = h * jax.nn.gelu(gate)
    yrec_ref[...] = _rms(y, grec_ref[...]).astype(BF16)

    ang = invf_ref[...] * pos_ref[...].astype(F32)
    cs = jnp.concatenate([jnp.cos(ang), jnp.sin(ang)], axis=0)
    tabs = lax.dot_general(cs, pat_ref[...], (((0,), (0,)), ((), ())),
                           precision=lax.Precision.HIGHEST, preferred_element_type=F32)
    lane64 = lax.broadcasted_iota(I32, (1, LANES), 1) % HALF_DIM
    cosf = tabs[:, 0:LANES] + (lane64 >= ROPE_DIM).astype(F32)
    sa, sb = tabs[:, LANES:2 * LANES], tabs[:, 2 * LANES:3 * LANES]
    scale = HALF_DIM ** -0.5 * math.log2(math.e)
    for name, out_ref, off, mul in (("q", q_ref, 2 * D_REC, scale), ("k", k_ref, 2 * D_REC + D_QK, 1.0)):
        for c in range(D_QK // LANES):
            zc = _dot(hn, win_ref[:, off + c * LANES: off + (c + 1) * LANES])
            rot = zc * cosf + pltpu.roll(zc, LANES - ROPE_DIM // 2, 1) * sa + pltpu.roll(zc, ROPE_DIM // 2, 1) * sb
            out_ref[:, c * LANES:(c + 1) * LANES] = (rot * mul).astype(BF16)
    vt_ref[0] = _dot(hn, win_ref[:, 2 * D_REC + 2 * D_QK:]).T.astype(BF16)


def _mix_in(x2, pos_row, inv_freq, rope_pat, ln_mix, w_in, conv_w, conv_b, wa_bd, b_a, wi_bd, b_i, rg_lambda,
            g_rec, batch, seq):
    tm = MIX_ROWS
    nt = seq // tm
    d_in = w_in.shape[1]
    row_map = lambda b, i: (b * nt + i, 0)
    fixed2 = lambda b, i: (0, 0)
    fixed3 = lambda b, i: (0, 0, 0)
    t = batch * seq
    out_shapes = (
        jax.ShapeDtypeStruct((t, D_REC), BF16),
        jax.ShapeDtypeStruct((t, D_QK), BF16),
        jax.ShapeDtypeStruct((t, D_QK), BF16),
        jax.ShapeDtypeStruct((t // tm, D_ATT, tm), BF16),
    )
    return pl.pallas_call(
        _mix_in_kernel,
        grid=(batch, nt),
        in_specs=[
            pl.BlockSpec((tm, D_MODEL), row_map),
            pl.BlockSpec((1, tm), lambda b, i: (0, b * nt + i)),
            pl.BlockSpec((ROPE_DIM // 2, 1), fixed2),
            pl.BlockSpec((ROPE_DIM, 3 * LANES), fixed2),
            pl.BlockSpec((1, D_MODEL), fixed2),
            pl.BlockSpec((D_MODEL, d_in), fixed2),
            pl.BlockSpec((CONV_WIDTH, D_REC), fixed2),
            pl.BlockSpec((1, D_REC), fixed2),
            pl.BlockSpec((2, D_REC // 2, D_REC // 2), fixed3),
            pl.BlockSpec((1, D_REC), fixed2),
            pl.BlockSpec((2, D_REC // 2, D_REC // 2), fixed3),
            pl.BlockSpec((1, D_REC), fixed2),
            pl.BlockSpec((1, D_REC), fixed2),
            pl.BlockSpec((1, D_REC), fixed2),
        ],
        out_specs=[
            pl.BlockSpec((tm, D_REC), row_map),
            pl.BlockSpec((tm, D_QK), row_map),
            pl.BlockSpec((tm, D_QK), row_map),
            pl.BlockSpec((1, D_ATT, tm), lambda b, i: (b * nt + i, 0, 0)),
        ],
        out_shape=out_shapes,
        scratch_shapes=[pltpu.VMEM((SUBLANES, D_REC), F32), pltpu.VMEM((1, D_REC), F32)],
        compiler_params=pltpu.CompilerParams(
            dimension_semantics=("arbitrary", "arbitrary"), vmem_limit_bytes=VMEM_LIMIT),
        name="mix_in",
    )(x2, pos_row, inv_freq, rope_pat, ln_mix, w_in, conv_w, conv_b, wa_bd, b_a, wi_bd, b_i, rg_lambda, g_rec)


def _attn_kernel(lq1_ref, lk1_ref, lq2_ref, lk2_ref, gsub_ref, q_ref, k_ref, vt_ref, o_ref,
                 m_ref, l_ref, acc_ref, *, lam_init):
    tq = q_ref.shape[0]
    tk = vt_ref.shape[2]
    i = pl.program_id(2)

    qt = q_ref[...].astype(F32).T
    dim = lax.broadcasted_iota(I32, (LANES, tq), 0)
    qqt = jnp.concatenate([jnp.where(dim < HALF_DIM, qt, 0.0), jnp.where(dim >= HALF_DIM, qt, 0.0)],
                          axis=1).astype(BF16)

    m_ref[...] = jnp.full_like(m_ref, NEG_BIG)
    l_ref[...] = jnp.zeros_like(l_ref)
    acc_ref[...] = jnp.zeros_like(acc_ref)

    def step(j, masked):
        kj = k_ref[pl.ds(pl.multiple_of(j * tk, tk), tk), :]
        vtj = vt_ref[j]
        chunks = [slice(c * ATT_LANES, (c + 1) * ATT_LANES) for c in range(2 * tq // ATT_LANES)]
        scores = [_dot(kj, qqt[:, cols]) for cols in chunks]
        for cols, s in zip(chunks, scores):
            if masked:
                key = lax.broadcasted_iota(I32, s.shape, 0)
                qry = lax.broadcasted_iota(I32, s.shape, 1) + cols.start % tq
                s = jnp.where(key - qry <= i * tq - j * tk, s, NEG_BIG)
            m_prev = m_ref[:, cols]
            m_new = jnp.maximum(m_prev, jnp.max(s, axis=0, keepdims=True))
            alpha = jnp.exp2(m_prev - m_new)
            p = jnp.exp2(s - m_new)
            l_ref[:, cols] = alpha * l_ref[:, cols] + jnp.sum(p, axis=0, keepdims=True)
            acc_ref[:, cols] = alpha * acc_ref[:, cols] + _dot(vtj, p.astype(BF16))
            m_ref[:, cols] = m_new

    n_full = (i * tq) // tk

    def full_step(j, carry):
        step(j, False)
        return carry

    lax.fori_loop(0, n_full, full_step, 0)
    step(n_full, True)

    lam = (jnp.exp(jnp.sum(lq1_ref[...] * lk1_ref[...], axis=-1, keepdims=True))
           - jnp.exp(jnp.sum(lq2_ref[...] * lk2_ref[...], axis=-1, keepdims=True)) + lam_init)
    o = acc_ref[...] / l_ref[...]
    o = o[:, :tq] - lam * o[:, tq:]
    o = o * lax.rsqrt(jnp.mean(o * o, axis=0, keepdims=True) + EPS) * gsub_ref[...]
    o_ref[...] = (o * (1.0 - lam_init)).T.astype(BF16)


def _attention(q, k, vt, lq1, lk1, lq2, lk2, g_sub_col, batch, seq, lam_init):
    tq = ATT_Q
    nq = seq // tq
    tk = vt.shape[2]
    nk = seq // tk
    vec = lambda b, h, i: (0, 0)
    return pl.pallas_call(
        functools.partial(_attn_kernel, lam_init=lam_init),
        grid=(batch, N_HEADS, nq),
        in_specs=[
            pl.BlockSpec((1, HALF_DIM), vec),
            pl.BlockSpec((1, HALF_DIM), vec),
            pl.BlockSpec((1, HALF_DIM), vec),
            pl.BlockSpec((1, HALF_DIM), vec),
            pl.BlockSpec((V_DIM, 1), vec),
            pl.BlockSpec((tq, LANES), lambda b, h, i: (b * nq + i, h)),
            pl.BlockSpec((seq, LANES), lambda b, h, i: (b, h)),
            pl.BlockSpec((nk, V_DIM, tk), lambda b, h, i: (b, h, 0)),
        ],
        out_specs=pl.BlockSpec((tq, V_DIM), lambda b, h, i: (b * nq + i, h)),
        out_shape=jax.ShapeDtypeStruct((batch * seq, D_ATT), BF16),
        scratch_shapes=[pltpu.VMEM((1, 2 * tq), F32), pltpu.VMEM((1, 2 * tq), F32),
                        pltpu.VMEM((V_DIM, 2 * tq), F32)],
        compiler_params=pltpu.CompilerParams(
            dimension_semantics=("arbitrary", "arbitrary", "arbitrary"), vmem_limit_bytes=VMEM_LIMIT),
        name="attention",
    )(lq1, lk1, lq2, lk2, g_sub_col, q, k, vt)


def _sublane_total(x, op):
    return op(x, axis=0, keepdims=True)


def _route_kernel(x_ref, yrec_ref, yatt_ref, woa_ref, wob_ref, lnmoe_ref, wrt_ref, ebias_ref, tri_ref, low_ref,
                  h1_ref, hpa_ref, hpb_ref, ek_ref, wk_ref, rk_ref, cnt_ref, carry_ref):
    tm = x_ref.shape[0]
    e_n = N_EXPERTS

    @pl.when(pl.program_id(0) == 0)
    def _():
        carry_ref[...] = jnp.zeros_like(carry_ref)

    h1 = x_ref[...] + _dot(yrec_ref[...], woa_ref[...]) + _dot(yatt_ref[...], wob_ref[...])
    h1_ref[...] = h1
    hn = _rms(h1, lnmoe_ref[...])
    pa, pb = _pack_row(hn)
    hpa_ref[...] = pa
    hpb_ref[...] = pb

    logits = lax.dot_general(wrt_ref[...], hn, (((1,), (1,)), ((), ())),
                             precision=lax.Precision.HIGHEST, preferred_element_type=F32)
    scores = jax.nn.sigmoid(logits)
    sel = scores + ebias_ref[...]

    sel3 = sel.reshape(N_GROUPS, GROUP_SIZE, tm)
    idx3 = lax.broadcasted_iota(I32, (N_GROUPS, GROUP_SIZE, tm), 1)
    m1 = jnp.max(sel3, axis=1, keepdims=True)
    first = jnp.min(jnp.where(sel3 == m1, idx3, GROUP_SIZE), axis=1, keepdims=True)
    m2 = jnp.max(jnp.where(idx3 == first, -jnp.inf, sel3), axis=1, keepdims=True)
    gscore = (m1 + m2).reshape(N_GROUPS, tm)

    gidx = lax.broadcasted_iota(I32, (N_GROUPS, tm), 0)
    beaten = jnp.zeros((N_GROUPS, tm), I32)
    for g in range(N_GROUPS):
        other = gscore[g:g + 1, :]
        beats = (other > gscore) | ((other == gscore) & (g < gidx))
        beaten = beaten + beats.astype(I32)
    gkeep = beaten < TOPK_GROUPS
    keep = jnp.broadcast_to(gkeep.reshape(N_GROUPS, 1, tm), (N_GROUPS, GROUP_SIZE, tm)).reshape(e_n, tm)
    selm = jnp.where(keep, sel, -jnp.inf)

    eidx = lax.broadcasted_iota(I32, (e_n, tm), 0)
    beaten = jnp.zeros((e_n, tm), I32)
    for e in range(e_n):
        other = selm[e:e + 1, :]
        beats = (other > selm) | ((other == selm) & (e < eidx))
        beaten = beaten + beats.astype(I32)
    chosen = beaten < TOP_K
    chosen_f = chosen.astype(F32)

    wsum = _sublane_total(jnp.where(chosen, scores, 0.0), jnp.sum)
    weight = scores * (ROUTE_SCALE / wsum)

    chosen_b = chosen_f.astype(BF16)
    prefix = _dot(chosen_b, tri_ref[...])
    rank = prefix + carry_ref[...]
    carry_new = carry_ref[...] + jnp.sum(chosen_f, axis=1, keepdims=True)
    carry_ref[...] = carry_new
    cnt_ref[...] = carry_new.astype(I32)

    slot = _dot(low_ref[...], chosen_b)
    eidx_f = eidx.astype(F32)
    ek, wk, rk = [], [], []
    for kk in range(TOP_K):
        pick = chosen & (slot == float(kk))
        ek.append(_sublane_total(jnp.where(pick, eidx_f, 0.0), jnp.sum))
        wk.append(_sublane_total(jnp.where(pick, weight, 0.0), jnp.sum))
        rk.append(_sublane_total(jnp.where(pick, rank, 0.0), jnp.sum))
    ek_ref[...] = jnp.concatenate(ek, axis=0).astype(I32)
    wk_ref[...] = jnp.concatenate(wk, axis=0)
    rk_ref[...] = jnp.concatenate(rk, axis=0).astype(I32)


def _route(x2, y_rec, y_att, wo_a, wo_b, ln_moe, w_router_t, e_bias_col):
    t = x2.shape[0]
    tm = ROUTE_ROWS
    nt = t // tm
    row_map = lambda i: (i, 0)
    col_map = lambda i: (0, i)
    fixed = lambda i: (0, 0)
    tri = (lax.broadcasted_iota(I32, (tm, tm), 0) < lax.broadcasted_iota(I32, (tm, tm), 1)).astype(BF16)
    low = (lax.broadcasted_iota(I32, (N_EXPERTS, N_EXPERTS), 1)
           < lax.broadcasted_iota(I32, (N_EXPERTS, N_EXPERTS), 0)).astype(BF16)
    out_shapes = (
        jax.ShapeDtypeStruct((t, D_MODEL), F32),
        jax.ShapeDtypeStruct((t, PACK_W), U32),
        jax.ShapeDtypeStruct((t, PACK_W), U32),
        jax.ShapeDtypeStruct((TOP_K, t), I32),
        jax.ShapeDtypeStruct((TOP_K, t), F32),
        jax.ShapeDtypeStruct((TOP_K, t), I32),
        jax.ShapeDtypeStruct((N_EXPERTS, 1), I32),
    )
    return pl.pallas_call(
        _route_kernel,
        grid=(nt,),
        in_specs=[
            pl.BlockSpec((tm, D_MODEL), row_map),
            pl.BlockSpec((tm, D_REC), row_map),
            pl.BlockSpec((tm, D_ATT), row_map),
            pl.BlockSpec((D_REC, D_MODEL), fixed),
            pl.BlockSpec((D_ATT, D_MODEL), fixed),
            pl.BlockSpec((1, D_MODEL), fixed),
            pl.BlockSpec((N_EXPERTS, D_MODEL), fixed),
            pl.BlockSpec((N_EXPERTS, 1), fixed),
            pl.BlockSpec((tm, tm), fixed),
            pl.BlockSpec((N_EXPERTS, N_EXPERTS), fixed),
        ],
        out_specs=[
            pl.BlockSpec((tm, D_MODEL), row_map),
            pl.BlockSpec((tm, PACK_W), row_map),
            pl.BlockSpec((tm, PACK_W), row_map),
            pl.BlockSpec((TOP_K, tm), col_map),
            pl.BlockSpec((TOP_K, tm), col_map),
            pl.BlockSpec((TOP_K, tm), col_map),
            pl.BlockSpec((N_EXPERTS, 1), fixed),
        ],
        out_shape=out_shapes,
        scratch_shapes=[pltpu.VMEM((N_EXPERTS, 1), F32)],
        compiler_params=pltpu.CompilerParams(dimension_semantics=("arbitrary",), vmem_limit_bytes=VMEM_LIMIT),
        name="route",
    )(x2, y_rec, y_att, wo_a, wo_b, ln_moe, w_router_t, e_bias_col, tri, low)


def _plan_kernel(pad_start_ref, ek_ref, rk_ref, dest_ref):
    ek = ek_ref[...]

    def add_expert(e, base):
        return jnp.where(ek == e, pad_start_ref[e], base)

    dest_ref[...] = rk_ref[...] + lax.fori_loop(0, N_EXPERTS, add_expert, jnp.zeros_like(ek))


def _plan(pad_start, ek, rk):
    kk, t = ek.shape
    tl = min(t, 2048)
    col_map = lambda i, ps: (0, i)
    grid_spec = pltpu.PrefetchScalarGridSpec(
        num_scalar_prefetch=1,
        grid=(t // tl,),
        in_specs=[pl.BlockSpec((kk, tl), col_map), pl.BlockSpec((kk, tl), col_map)],
        out_specs=pl.BlockSpec((kk, tl), col_map),
    )
    return pl.pallas_call(
        _plan_kernel,
        grid_spec=grid_spec,
        out_shape=jax.ShapeDtypeStruct((kk, t), I32),
        compiler_params=pltpu.CompilerParams(dimension_semantics=("arbitrary",)),
        name="plan",
    )(pad_start, ek, rk)


def _sc_mesh():
    return plsc.VectorSubcoreMesh(core_axis_name="core", subcore_axis_name="subcore")


def _sc_dispatch(rows, dest, n_out):
    t, w = rows.shape
    kk = dest.shape[0]

    @pl.kernel(out_type=jax.ShapeDtypeStruct((n_out, w), rows.dtype), mesh=_sc_mesh(), scratch_types=[])
    def kern(x_hbm, i_hbm, o_hbm):
        def body(x_vmem, i_vmem):
            for k in range(kk):
                pltpu.sync_copy(x_vmem, o_hbm.at[i_vmem.at[k]])

        pltpu.emit_pipeline(
            body,
            grid=(t // SC_WINDOW,),
            in_specs=[pl.BlockSpec((SC_WINDOW, w), lambda i: (i, 0)),
                      pl.BlockSpec((kk, SC_WINDOW), lambda i: (0, i))],
            out_specs=[],
            core_axis_name=("core", "subcore"),
            dimension_semantics=(pltpu.PARALLEL,),
        )(x_hbm, i_hbm)

    return kern(rows, dest)


def _sc_combine(rows, dest):
    kk, t = dest.shape
    w = rows.shape[1]
    flat = dest.reshape(1, kk * t)

    @pl.kernel(out_type=jax.ShapeDtypeStruct((kk * t, w), rows.dtype), mesh=_sc_mesh(), scratch_types=[])
    def kern(y_hbm, i_hbm, o_hbm):
        def body(i_vmem, o_vmem):
            pltpu.sync_copy(y_hbm.at[i_vmem.at[0]], o_vmem)

        pltpu.emit_pipeline(
            body,
            grid=(kk * t // SC_WINDOW,),
            in_specs=[pl.BlockSpec((1, SC_WINDOW), lambda i: (0, i))],
            out_specs=[pl.BlockSpec((SC_WINDOW, w), lambda i: (i, 0))],
            core_axis_name=("core", "subcore"),
            dimension_semantics=(pltpu.PARALLEL,),
        )(i_hbm, o_hbm)

    return kern(rows, flat).reshape(kk, t, w)


def _experts_kernel(blk_expert_ref, n_used_ref, xa_ref, xb_ref, w1_ref, w3_ref, w2_ref, ya_ref, yb_ref,
                    w1b_ref, w3b_ref, w2b_ref):
    b = pl.program_id(0)
    prev = blk_expert_ref[jnp.maximum(b - 1, 0)]
    fresh = (b == 0) | (blk_expert_ref[b] != prev)

    @pl.when(fresh)
    def _():
        w1b_ref[...] = w1_ref[0].astype(BF16)
        w3b_ref[...] = w3_ref[0].astype(BF16)
        w2b_ref[...] = w2_ref[0].astype(BF16)

    @pl.when(b < n_used_ref[0])
    def _():
        x = _unpack_row(xa_ref[...], xb_ref[...]).astype(BF16)
        a = _dot(x, w1b_ref[...])
        g = _dot(x, w3b_ref[...])
        hmid = (jax.nn.silu(a) * g).astype(BF16)
        y = _dot(hmid, w2b_ref[...])
        pa, pb = _pack_row(y)
        ya_ref[...] = pa
        yb_ref[...] = pb


def _experts(xa, xb, w1, w3, w2, blk_expert, n_used):
    p = xa.shape[0]
    m = EXPERT_ROWS
    nblk = p // m
    row_map = lambda b, be, nu: (b, 0)
    w_map = lambda b, be, nu: (be[b], 0, 0)
    grid_spec = pltpu.PrefetchScalarGridSpec(
        num_scalar_prefetch=2,
        grid=(nblk,),
        in_specs=[
            pl.BlockSpec((m, PACK_W), row_map),
            pl.BlockSpec((m, PACK_W), row_map),
            pl.BlockSpec((1, D_MODEL, D_EXPERT), w_map),
            pl.BlockSpec((1, D_MODEL, D_EXPERT), w_map),
            pl.BlockSpec((1, D_EXPERT, D_MODEL), w_map),
        ],
        out_specs=[pl.BlockSpec((m, PACK_W), row_map), pl.BlockSpec((m, PACK_W), row_map)],
        scratch_shapes=[pltpu.VMEM((D_MODEL, D_EXPERT), BF16), pltpu.VMEM((D_MODEL, D_EXPERT), BF16),
                        pltpu.VMEM((D_EXPERT, D_MODEL), BF16)],
    )
    return pl.pallas_call(
        _experts_kernel,
        grid_spec=grid_spec,
        out_shape=(jax.ShapeDtypeStruct((p, PACK_W), U32), jax.ShapeDtypeStruct((p, PACK_W), U32)),
        compiler_params=pltpu.CompilerParams(dimension_semantics=("arbitrary",), vmem_limit_bytes=VMEM_LIMIT),
        name="experts",
    )(blk_expert, n_used, xa, xb, w1, w3, w2)


def _tail_kernel(h1_ref, ga_ref, gb_ref, wk_ref, p_ref, lnmoe_ref, ws1_ref, ws3_ref, ws2_ref, lnple_ref,
                 wpg_ref, wpp_ref, lnf_ref, o_ref):
    h1 = h1_ref[...]
    hn = _rms(h1, lnmoe_ref[...]).astype(BF16)
    shared = _dot((jax.nn.silu(_dot(hn, ws1_ref[...])) * _dot(hn, ws3_ref[...])).astype(BF16), ws2_ref[...])
    wk = wk_ref[...]
    routed = jnp.zeros_like(h1)
    for kk in range(TOP_K):
        routed = routed + wk[:, kk:kk + 1] * _unpack_row(ga_ref[kk], gb_ref[kk])
    h2 = h1 + routed + shared
    gate = jax.nn.sigmoid(_dot(_rms(h2, lnple_ref[...]).astype(BF16), wpg_ref[...]))
    h3 = h2 + gate * _dot(p_ref[...].astype(BF16), wpp_ref[...])
    o_ref[...] = _rms(h3, lnf_ref[...])


def _tail(h1, ga, gb, wk_t, p2, ln_moe, ws1, ws3, ws2, ln_ple, w_pg, w_pp, ln_f):
    t = h1.shape[0]
    tm = TAIL_ROWS
    row_map = lambda i: (i, 0)
    fixed = lambda i: (0, 0)
    g_map = lambda i: (0, i, 0)
    d_sh = ws1.shape[1]
    return pl.pallas_call(
        _tail_kernel,
        grid=(t // tm,),
        in_specs=[
            pl.BlockSpec((tm, D_MODEL), row_map),
            pl.BlockSpec((TOP_K, tm, PACK_W), g_map),
            pl.BlockSpec((TOP_K, tm, PACK_W), g_map),
            pl.BlockSpec((tm, TOP_K), row_map),
            pl.BlockSpec((tm, D_PLE), row_map),
            pl.BlockSpec((1, D_MODEL), fixed),
            pl.BlockSpec((D_MODEL, d_sh), fixed),
            pl.BlockSpec((D_MODEL, d_sh), fixed),
            pl.BlockSpec((d_sh, D_MODEL), fixed),
            pl.BlockSpec((1, D_MODEL), fixed),
            pl.BlockSpec((D_MODEL, D_MODEL), fixed),
            pl.BlockSpec((D_PLE, D_MODEL), fixed),
            pl.BlockSpec((1, D_MODEL), fixed),
        ],
        out_specs=pl.BlockSpec((tm, D_MODEL), row_map),
        out_shape=jax.ShapeDtypeStruct((t, D_MODEL), F32),
        compiler_params=pltpu.CompilerParams(dimension_semantics=("arbitrary",), vmem_limit_bytes=VMEM_LIMIT),
        name="tail",
    )(h1, ga, gb, wk_t, p2, ln_moe, ws1, ws3, ws2, ln_ple, w_pg, w_pp, ln_f)


def _rope_constants():
    half = ROPE_DIM // 2
    inv_freq = (ROPE_THETA ** (-jnp.arange(0, ROPE_DIM, 2, dtype=F32) / ROPE_DIM)).reshape(half, 1)
    f = lax.broadcasted_iota(I32, (ROPE_DIM, LANES), 0)
    l64 = lax.broadcasted_iota(I32, (ROPE_DIM, LANES), 1) % HALF_DIM
    cos_pat = ((f < half) & (l64 < ROPE_DIM) & (l64 % half == f)).astype(F32)
    sa_pat = -((f >= half) & (l64 < half) & (l64 == f - half)).astype(F32)
    sb_pat = ((f >= half) & (l64 >= half) & (l64 < ROPE_DIM) & (l64 - half == f - half)).astype(F32)
    return inv_freq, jnp.concatenate([cos_pat, sa_pat, sb_pat], axis=1)


def _block_diag_tiles(w):
    nb, bd, _ = w.shape
    per = nb // 2
    tiles = []
    for tix in range(2):
        rows = []
        for j in range(per):
            rows.append(jnp.concatenate(
                [w[tix * per + j] if c == j else jnp.zeros((bd, bd), w.dtype) for c in range(per)], axis=1))
        tiles.append(jnp.concatenate(rows, axis=0))
    return jnp.stack(tiles).astype(BF16)


def _layer(h, p_l, positions, lam_init, ln_mix, w_in, conv_w, conv_b, w_a, b_a, w_i, b_i, rg_lambda, g_rec,
           lq1, lk1, lq2, lk2, g_sub, w_out, ln_moe, w_router, e_bias, w1, w3, w2, ws1, ws3, ws2,
           ln_ple, w_ple_gate, w_ple_proj, ln_out):
    batch, seq, _ = h.shape
    t = batch * seq
    x2 = h.reshape(t, D_MODEL)
    row = lambda a: a.reshape(1, -1)
    inv_freq, rope_pat = _rope_constants()

    y_rec, q, k, vt = _mix_in(
        x2, positions.reshape(1, t), inv_freq, rope_pat, row(ln_mix), w_in.astype(BF16), conv_w, row(conv_b),
        _block_diag_tiles(w_a), row(b_a), _block_diag_tiles(w_i), row(b_i), row(rg_lambda), row(g_rec),
        batch, seq)
    y_att = _attention(q, k, vt, row(lq1), row(lk1), row(lq2), row(lk2), g_sub.reshape(-1, 1), batch, seq,
                       lam_init)

    w_out_b = w_out.astype(BF16)
    h1, hpa, hpb, ek, wk, rk, counts = _route(
        x2, y_rec, y_att, w_out_b[:D_REC], w_out_b[D_REC:], row(ln_moe), w_router.T, e_bias.reshape(-1, 1))

    m = EXPERT_ROWS
    counts = counts.reshape(-1)
    padded = (counts + m - 1) // m * m
    pad_end = jnp.cumsum(padded)
    pad_start = pad_end - padded
    n_rows = t * TOP_K + N_EXPERTS * m
    nblk = n_rows // m
    n_used = (pad_end[-1] // m).astype(I32).reshape(1)
    blk_row = jnp.minimum(jnp.arange(nblk, dtype=I32), n_used[0] - 1) * m
    blk_expert = jnp.sum((pad_end[None, :] <= blk_row[:, None]).astype(I32), axis=1)
    dest = _plan(pad_start.astype(I32), ek, rk)

    xa = _sc_dispatch(hpa, dest, n_rows)
    xb = _sc_dispatch(hpb, dest, n_rows)
    ya, yb = _experts(xa, xb, w1, w3, w2, blk_expert, n_used)
    ga = _sc_combine(ya, dest)
    gb = _sc_combine(yb, dest)

    out = _tail(h1, ga, gb, wk.T, p_l.reshape(t, D_PLE), row(ln_moe), ws1.astype(BF16), ws3.astype(BF16),
                ws2.astype(BF16), row(ln_ple), w_ple_gate.astype(BF16), w_ple_proj.astype(BF16), row(ln_out))
    return out.reshape(batch, seq, D_MODEL)


def kernel(x, p, positions, ln_mix, w_in, conv_w, conv_b, w_a, b_a, w_i, b_i, rg_lambda, g_rec, lq1, lk1, lq2,
           lk2, g_sub, w_out, ln_moe, w_router, e_bias, w1, w3, w2, ws1, ws3, ws2, ln_ple, w_ple_gate,
           w_ple_proj, ln_f):
    depth = w_in.shape[0]
    assert depth == 1, "the fused tail applies the final norm; one layer supported"
    lam_init = 0.8 - 0.6 * math.exp(-0.3 * 0)
    return _layer(x, p[0], positions, lam_init, ln_mix[0], w_in[0], conv_w[0], conv_b[0], w_a[0], b_a[0], w_i[0],
                  b_i[0], rg_lambda[0], g_rec[0], lq1[0], lk1[0], lq2[0], lk2[0], g_sub[0], w_out[0], ln_moe[0],
                  w_router[0], e_bias[0], w1[0], w3[0], w2[0], ws1[0], ws3[0], ws2[0], ln_ple[0], w_ple_gate[0],
                  w_ple_proj[0], ln_f)
```

```python
import functools
import math

import jax
import jax.numpy as jnp
from jax import lax
from jax.experimental import pallas as pl
from jax.experimental.pallas import tpu as pltpu
from jax.experimental.pallas import tpu_sc as plsc

F32 = jnp.float32
BF16 = jnp.bfloat16
U32 = jnp.uint32
I32 = jnp.int32

D_MODEL = 1024
D_REC = 512
REC_BLOCKS = 8
CONV_WIDTH = 4
RG_C = 8.0
N_HEADS = 4
HALF_DIM = 64
V_DIM = 128
D_ATT = N_HEADS * V_DIM
D_QK = N_HEADS * 2 * HALF_DIM
ROPE_DIM = 16
ROPE_THETA = 500000.0
N_EXPERTS = 64
TOP_K = 8
N_GROUPS = 8
GROUP_SIZE = N_EXPERTS // N_GROUPS
TOPK_GROUPS = 4
D_EXPERT = 256
ROUTE_SCALE = 2.5
D_PLE = 256
EPS = 1e-6

LANES = 128
SUBLANES = 8
VMEM_LIMIT = 56 * 1024 * 1024

MIX_ROWS = 512
ATT_Q = 256
ROUTE_ROWS = 512
EXPERT_ROWS = 512
TAIL_ROWS = 256
SC_WINDOW = 128
PACK_W = 256
NEG_BIG = -1e30


def _rms(x, g):
    return x * lax.rsqrt(jnp.mean(x * x, axis=-1, keepdims=True) + EPS) * g


def _dot(a, b):
    return jnp.dot(a, b, preferred_element_type=F32)


def _pack_pair(lo, hi):
    lo_bits = lax.bitcast_convert_type(lo.astype(BF16).astype(F32), U32)
    hi_bits = lax.bitcast_convert_type(hi.astype(BF16).astype(F32), U32)
    return (lo_bits >> 16) | (hi_bits & jnp.uint32(0xFFFF0000))


def _unpack_pair(p):
    lo = lax.bitcast_convert_type(p << 16, F32)
    hi = lax.bitcast_convert_type(p & jnp.uint32(0xFFFF0000), F32)
    return lo, hi


def _pack_row(x):
    w = PACK_W
    return _pack_pair(x[:, 0:w], x[:, w:2 * w]), _pack_pair(x[:, 2 * w:3 * w], x[:, 3 * w:4 * w])


def _unpack_row(pa, pb):
    c0, c1 = _unpack_pair(pa)
    c2, c3 = _unpack_pair(pb)
    return jnp.concatenate([c0, c1, c2, c3], axis=1)


def _shift_rows(a, s, fill, row):
    n, c = a.shape
    if s % SUBLANES == 0:
        return jnp.concatenate([jnp.full((s, c), fill, a.dtype), a[:n - s]], axis=0)
    return jnp.where(row >= s, pltpu.roll(a, s, 0), fill)


def _mix_in_kernel(x_ref, pos_ref, invf_ref, pat_ref, lnm_ref, win_ref, cw_ref, cb_ref, wa_ref, ba_ref,
                   wi_ref, bi_ref, lam_ref, grec_ref,
                   yrec_ref, q_ref, k_ref, vt_ref, tail_ref, hcarry_ref):
    tm = x_ref.shape[0]

    @pl.when(pl.program_id(1) == 0)
    def _():
        tail_ref[...] = jnp.zeros_like(tail_ref)
        hcarry_ref[...] = jnp.zeros_like(hcarry_ref)

    hn = _rms(x_ref[...], lnm_ref[...]).astype(BF16)
    xr = _dot(hn, win_ref[:, 0:D_REC])

    row = lax.broadcasted_iota(I32, (tm, D_REC), 0)
    row8 = lax.broadcasted_iota(I32, (SUBLANES, D_REC), 0)
    tail = tail_ref[...]
    xc = cb_ref[...] + cw_ref[CONV_WIDTH - 1:CONV_WIDTH, :] * xr
    for j in range(1, CONV_WIDTH):
        rolled = pltpu.roll(xr, j, 0)
        head = jnp.where(row8 < j, pltpu.roll(tail, j, 0), rolled[:SUBLANES])
        shifted = jnp.concatenate([head, rolled[SUBLANES:]], axis=0)
        xc = xc + cw_ref[CONV_WIDTH - 1 - j:CONV_WIDTH - j, :] * shifted
    tail_ref[...] = xr[tm - SUBLANES:, :]

    xcb = xc.astype(BF16)
    half = D_REC // 2
    ra = jnp.concatenate([_dot(xcb[:, :half], wa_ref[0]), _dot(xcb[:, half:], wa_ref[1])], axis=1)
    ri = jnp.concatenate([_dot(xcb[:, :half], wi_ref[0]), _dot(xcb[:, half:], wi_ref[1])], axis=1)
    r = jax.nn.sigmoid(ra + ba_ref[...])
    ig = jax.nn.sigmoid(ri + bi_ref[...])
    lam = lam_ref[...]
    softplus_neg = jnp.maximum(-lam, 0.0) + jnp.log(1.0 + jnp.exp(-jnp.abs(lam)))
    log_a = -RG_C * r * softplus_neg
    a = jnp.exp(log_a)
    u = jnp.sqrt(1.0 - jnp.exp(2.0 * log_a)) * ig * xc

    s = 1
    while s < tm:
        u = u + a * _shift_rows(u, s, 0.0, row)
        a = a * _shift_rows(a, s, 1.0, row)
        s *= 2
    h = u + a * hcarry_ref[...]
    hcarry_ref[...] = h[tm - 1:tm, :]

    gate = _dot(hn, win_ref[:, D_REC:2 * D_REC])
    y = h * jax.nn.gelu(gate)
    yrec_ref[...] = _rms(y, grec_ref[...]).astype(BF16)

    ang = invf_ref[...] * pos_ref[...].astype(F32)
    cs = jnp.concatenate([jnp.cos(ang), jnp.sin(ang)], axis=0)
    tabs = lax.dot_general(cs, pat_ref[...], (((0,), (0,)), ((), ())),
                           precision=lax.Precision.HIGHEST, preferred_element_type=F32)
    lane64 = lax.broadcasted_iota(I32, (1, LANES), 1) % HALF_DIM
    cosf = tabs[:, 0:LANES] + (lane64 >= ROPE_DIM).astype(F32)
    sa, sb = tabs[:, LANES:2 * LANES], tabs[:, 2 * LANES:3 * LANES]
    scale = HALF_DIM ** -0.5 * math.log2(math.e)
    for name, out_ref, off, mul in (("q", q_ref, 2 * D_REC, scale), ("k", k_ref, 2 * D_REC + D_QK, 1.0)):
        for c in range(D_QK // LANES):
            zc = _dot(hn, win_ref[:, off + c * LANES: off + (c + 1) * LANES])
            rot = zc * cosf + pltpu.roll(zc, LANES - ROPE_DIM // 2, 1) * sa + pltpu.roll(zc, ROPE_DIM // 2, 1) * sb
            out_ref[:, c * LANES:(c + 1) * LANES] = (rot * mul).astype(BF16)
    vt_ref[0] = _dot(hn, win_ref[:, 2 * D_REC + 2 * D_QK:]).T.astype(BF16)


def _mix_in(x2, pos_row, inv_freq, rope_pat, ln_mix, w_in, conv_w, conv_b, wa_bd, b_a, wi_bd, b_i, rg_lambda,
            g_rec, batch, seq):
    tm = MIX_ROWS
    nt = seq // tm
    d_in = w_in.shape[1]
    row_map = lambda b, i: (b * nt + i, 0)
    fixed2 = lambda b, i: (0, 0)
    fixed3 = lambda b, i: (0, 0, 0)
    t = batch * seq
    out_shapes = (
        jax.ShapeDtypeStruct((t, D_REC), BF16),
        jax.ShapeDtypeStruct((t, D_QK), BF16),
        jax.ShapeDtypeStruct((t, D_QK), BF16),
        jax.ShapeDtypeStruct((t // tm, D_ATT, tm), BF16),
    )
    return pl.pallas_call(
        _mix_in_kernel,
        grid=(batch, nt),
        in_specs=[
            pl.BlockSpec((tm, D_MODEL), row_map),
            pl.BlockSpec((1, tm), lambda b, i: (0, b * nt + i)),
            pl.BlockSpec((ROPE_DIM // 2, 1), fixed2),
            pl.BlockSpec((ROPE_DIM, 3 * LANES), fixed2),
            pl.BlockSpec((1, D_MODEL), fixed2),
            pl.BlockSpec((D_MODEL, d_in), fixed2),
            pl.BlockSpec((CONV_WIDTH, D_REC), fixed2),
            pl.BlockSpec((1, D_REC), fixed2),
            pl.BlockSpec((2, D_REC // 2, D_REC // 2), fixed3),
            pl.BlockSpec((1, D_REC), fixed2),
            pl.BlockSpec((2, D_REC // 2, D_REC // 2), fixed3),
            pl.BlockSpec((1, D_REC), fixed2),
            pl.BlockSpec((1, D_REC), fixed2),
            pl.BlockSpec((1, D_REC), fixed2),
        ],
        out_specs=[
            pl.BlockSpec((tm, D_REC), row_map),
            pl.BlockSpec((tm, D_QK), row_map),
            pl.BlockSpec((tm, D_QK), row_map),
            pl.BlockSpec((1, D_ATT, tm), lambda b, i: (b * nt + i, 0, 0)),
        ],
        out_shape=out_shapes,
        scratch_shapes=[pltpu.VMEM((SUBLANES, D_REC), F32), pltpu.VMEM((1, D_REC), F32)],
        compiler_params=pltpu.CompilerParams(
            dimension_semantics=("arbitrary", "arbitrary"), vmem_limit_bytes=VMEM_LIMIT),
        name="mix_in",
    )(x2, pos_row, inv_freq, rope_pat, ln_mix, w_in, conv_w, conv_b, wa_bd, b_a, wi_bd, b_i, rg_lambda, g_rec)


def _attn_kernel(lq1_ref, lk1_ref, lq2_ref, lk2_ref, gsub_ref, q_ref, k_ref, vt_ref, o_ref,
                 m_ref, l_ref, acc_ref, aprev_ref, s0_ref, s1_ref, p0_ref, p1_ref, *, lam_init):
    tq = q_ref.shape[0]
    tk = vt_ref.shape[2]
    i = pl.program_id(2)

    qt = q_ref[...].astype(F32).T
    dim = lax.broadcasted_iota(I32, (LANES, tq), 0)
    qqt = jnp.concatenate([jnp.where(dim < HALF_DIM, qt, 0.0), jnp.where(dim >= HALF_DIM, qt, 0.0)],
                          axis=1).astype(BF16)

    n = (i * tq) // tk

    def scores(j):
        return _dot(k_ref[pl.ds(pl.multiple_of(j * tk, tk), tk), :], qqt)

    def softmax_update(s):
        m_prev = m_ref[...]
        m_new = jnp.maximum(m_prev, jnp.max(s, axis=0, keepdims=True))
        alpha = jnp.exp2(m_prev - m_new)
        p = jnp.exp2(s - m_new)
        l_ref[...] = alpha * l_ref[...] + jnp.sum(p, axis=0, keepdims=True)
        m_ref[...] = m_new
        return p.astype(BF16), alpha

    def pipe_step(j, s_cur, s_nxt, p_cur, p_prev):
        s_nxt[...] = scores(j + 1)
        p, alpha = softmax_update(s_cur[...])
        acc_ref[...] = aprev_ref[...] * acc_ref[...] + _dot(vt_ref[jnp.maximum(j - 1, 0)], p_prev[...])
        aprev_ref[...] = alpha
        p_cur[...] = p

    m_ref[...] = jnp.full_like(m_ref, NEG_BIG)
    l_ref[...] = jnp.zeros_like(l_ref)
    acc_ref[...] = jnp.zeros_like(acc_ref)
    aprev_ref[...] = jnp.ones_like(aprev_ref)
    odd = n % 2

    @pl.when(odd == 0)
    def _():
        p1_ref[...] = jnp.zeros_like(p1_ref)
        s0_ref[...] = scores(0)

    @pl.when(odd == 1)
    def _():
        p0_ref[...] = jnp.zeros_like(p0_ref)
        s1_ref[...] = scores(0)
        pipe_step(0, s1_ref, s0_ref, p1_ref, p0_ref)

    def pair(t, carry):
        j = 2 * t + odd
        pipe_step(j, s0_ref, s1_ref, p0_ref, p1_ref)
        pipe_step(j + 1, s1_ref, s0_ref, p1_ref, p0_ref)
        return carry

    lax.fori_loop(0, n // 2, pair, 0)

    s = s0_ref[...]
    key = lax.broadcasted_iota(I32, s.shape, 0)
    qry = lax.broadcasted_iota(I32, s.shape, 1) % tq
    p, alpha = softmax_update(jnp.where(key - qry <= i * tq - n * tk, s, NEG_BIG))
    acc = aprev_ref[...] * acc_ref[...] + _dot(vt_ref[jnp.maximum(n - 1, 0)], p1_ref[...])
    acc = alpha * acc + _dot(vt_ref[n], p)

    lam = (jnp.exp(jnp.sum(lq1_ref[...] * lk1_ref[...], axis=-1, keepdims=True))
           - jnp.exp(jnp.sum(lq2_ref[...] * lk2_ref[...], axis=-1, keepdims=True)) + lam_init)
    o = acc / l_ref[...]
    o = o[:, :tq] - lam * o[:, tq:]
    o = o * lax.rsqrt(jnp.mean(o * o, axis=0, keepdims=True) + EPS) * gsub_ref[...]
    o_ref[...] = (o * (1.0 - lam_init)).T.astype(BF16)


def _attention(q, k, vt, lq1, lk1, lq2, lk2, g_sub_col, batch, seq, lam_init):
    tq = ATT_Q
    nq = seq // tq
    tk = vt.shape[2]
    nk = seq // tk
    vec = lambda b, h, i: (0, 0)
    return pl.pallas_call(
        functools.partial(_attn_kernel, lam_init=lam_init),
        grid=(batch, N_HEADS, nq),
        in_specs=[
            pl.BlockSpec((1, HALF_DIM), vec),
            pl.BlockSpec((1, HALF_DIM), vec),
            pl.BlockSpec((1, HALF_DIM), vec),
            pl.BlockSpec((1, HALF_DIM), vec),
            pl.BlockSpec((V_DIM, 1), vec),
            pl.BlockSpec((tq, LANES), lambda b, h, i: (b * nq + i, h)),
            pl.BlockSpec((seq, LANES), lambda b, h, i: (b, h)),
            pl.BlockSpec((nk, V_DIM, tk), lambda b, h, i: (b, h, 0)),
        ],
        out_specs=pl.BlockSpec((tq, V_DIM), lambda b, h, i: (b * nq + i, h)),
        out_shape=jax.ShapeDtypeStruct((batch * seq, D_ATT), BF16),
        scratch_shapes=[pltpu.VMEM((1, 2 * tq), F32), pltpu.VMEM((1, 2 * tq), F32),
                        pltpu.VMEM((V_DIM, 2 * tq), F32), pltpu.VMEM((1, 2 * tq), F32),
                        pltpu.VMEM((tk, 2 * tq), F32), pltpu.VMEM((tk, 2 * tq), F32),
                        pltpu.VMEM((tk, 2 * tq), BF16), pltpu.VMEM((tk, 2 * tq), BF16)],
        compiler_params=pltpu.CompilerParams(
            dimension_semantics=("arbitrary", "arbitrary", "arbitrary"), vmem_limit_bytes=VMEM_LIMIT),
        name="attention",
    )(lq1, lk1, lq2, lk2, g_sub_col, q, k, vt)


def _sublane_total(x, op):
    return op(x, axis=0, keepdims=True)


def _route_kernel(x_ref, yrec_ref, yatt_ref, woa_ref, wob_ref, lnmoe_ref, wrt_ref, ebias_ref, tri_ref, low_ref,
                  h1_ref, hpa_ref, hpb_ref, ek_ref, wk_ref, rk_ref, cnt_ref, carry_ref):
    tm = x_ref.shape[0]
    e_n = N_EXPERTS

    @pl.when(pl.program_id(0) == 0)
    def _():
        carry_ref[...] = jnp.zeros_like(carry_ref)

    h1 = x_ref[...] + _dot(yrec_ref[...], woa_ref[...]) + _dot(yatt_ref[...], wob_ref[...])
    h1_ref[...] = h1
    hn = _rms(h1, lnmoe_ref[...])
    pa, pb = _pack_row(hn)
    hpa_ref[...] = pa
    hpb_ref[...] = pb

    logits = lax.dot_general(wrt_ref[...], hn, (((1,), (1,)), ((), ())),
                             precision=lax.Precision.HIGHEST, preferred_element_type=F32)
    scores = jax.nn.sigmoid(logits)
    sel = scores + ebias_ref[...]

    sel3 = sel.reshape(N_GROUPS, GROUP_SIZE, tm)
    idx3 = lax.broadcasted_iota(I32, (N_GROUPS, GROUP_SIZE, tm), 1)
    m1 = jnp.max(sel3, axis=1, keepdims=True)
    first = jnp.min(jnp.where(sel3 == m1, idx3, GROUP_SIZE), axis=1, keepdims=True)
    m2 = jnp.max(jnp.where(idx3 == first, -jnp.inf, sel3), axis=1, keepdims=True)
    gscore = (m1 + m2).reshape(N_GROUPS, tm)

    gidx = lax.broadcasted_iota(I32, (N_GROUPS, tm), 0)
    beaten = jnp.zeros((N_GROUPS, tm), I32)
    for g in range(N_GROUPS):
        other = gscore[g:g + 1, :]
        beats = (other > gscore) | ((other == gscore) & (g < gidx))
        beaten = beaten + beats.astype(I32)
    gkeep = beaten < TOPK_GROUPS
    keep = jnp.broadcast_to(gkeep.reshape(N_GROUPS, 1, tm), (N_GROUPS, GROUP_SIZE, tm)).reshape(e_n, tm)
    selm = jnp.where(keep, sel, -jnp.inf)

    eidx = lax.broadcasted_iota(I32, (e_n, tm), 0)
    beaten = jnp.zeros((e_n, tm), I32)
    for e in range(e_n):
        other = selm[e:e + 1, :]
        beats = (other > selm) | ((other == selm) & (e < eidx))
        beaten = beaten + beats.astype(I32)
    chosen = beaten < TOP_K
    chosen_f = chosen.astype(F32)

    wsum = _sublane_total(jnp.where(chosen, scores, 0.0), jnp.sum)
    weight = scores * (ROUTE_SCALE / wsum)

    chosen_b = chosen_f.astype(BF16)
    prefix = _dot(chosen_b, tri_ref[...])
    rank = prefix + carry_ref[...]
    carry_new = carry_ref[...] + jnp.sum(chosen_f, axis=1, keepdims=True)
    carry_ref[...] = carry_new
    cnt_ref[...] = carry_new.astype(I32)

    slot = _dot(low_ref[...], chosen_b)
    eidx_f = eidx.astype(F32)
    ek, wk, rk = [], [], []
    for kk in range(TOP_K):
        pick = chosen & (slot == float(kk))
        ek.append(_sublane_total(jnp.where(pick, eidx_f, 0.0), jnp.sum))
        wk.append(_sublane_total(jnp.where(pick, weight, 0.0), jnp.sum))
        rk.append(_sublane_total(jnp.where(pick, rank, 0.0), jnp.sum))
    ek_ref[...] = jnp.concatenate(ek, axis=0).astype(I32)
    wk_ref[...] = jnp.concatenate(wk, axis=0)
    rk_ref[...] = jnp.concatenate(rk, axis=0).astype(I32)


def _route(x2, y_rec, y_att, wo_a, wo_b, ln_moe, w_router_t, e_bias_col):
    t = x2.shape[0]
    tm = ROUTE_ROWS
    nt = t // tm
    row_map = lambda i: (i, 0)
    col_map = lambda i: (0, i)
    fixed = lambda i: (0, 0)
    tri = (lax.broadcasted_iota(I32, (tm, tm), 0) < lax.broadcasted_iota(I32, (tm, tm), 1)).astype(BF16)
    low = (lax.broadcasted_iota(I32, (N_EXPERTS, N_EXPERTS), 1)
           < lax.broadcasted_iota(I32, (N_EXPERTS, N_EXPERTS), 0)).astype(BF16)
    out_shapes = (
        jax.ShapeDtypeStruct((t, D_MODEL), F32),
        jax.ShapeDtypeStruct((t, PACK_W), U32),
        jax.ShapeDtypeStruct((t, PACK_W), U32),
        jax.ShapeDtypeStruct((TOP_K, t), I32),
        jax.ShapeDtypeStruct((TOP_K, t), F32),
        jax.ShapeDtypeStruct((TOP_K, t), I32),
        jax.ShapeDtypeStruct((N_EXPERTS, 1), I32),
    )
    return pl.pallas_call(
        _route_kernel,
        grid=(nt,),
        in_specs=[
            pl.BlockSpec((tm, D_MODEL), row_map),
            pl.BlockSpec((tm, D_REC), row_map),
            pl.BlockSpec((tm, D_ATT), row_map),
            pl.BlockSpec((D_REC, D_MODEL), fixed),
            pl.BlockSpec((D_ATT, D_MODEL), fixed),
            pl.BlockSpec((1, D_MODEL), fixed),
            pl.BlockSpec((N_EXPERTS, D_MODEL), fixed),
            pl.BlockSpec((N_EXPERTS, 1), fixed),
            pl.BlockSpec((tm, tm), fixed),
            pl.BlockSpec((N_EXPERTS, N_EXPERTS), fixed),
        ],
        out_specs=[
            pl.BlockSpec((tm, D_MODEL), row_map),
            pl.BlockSpec((tm, PACK_W), row_map),
            pl.BlockSpec((tm, PACK_W), row_map),
            pl.BlockSpec((TOP_K, tm), col_map),
            pl.BlockSpec((TOP_K, tm), col_map),
            pl.BlockSpec((TOP_K, tm), col_map),
            pl.BlockSpec((N_EXPERTS, 1), fixed),
        ],
        out_shape=out_shapes,
        scratch_shapes=[pltpu.VMEM((N_EXPERTS, 1), F32)],
        compiler_params=pltpu.CompilerParams(dimension_semantics=("arbitrary",), vmem_limit_bytes=VMEM_LIMIT),
        name="route",
    )(x2, y_rec, y_att, wo_a, wo_b, ln_moe, w_router_t, e_bias_col, tri, low)


def _plan_kernel(pad_start_ref, ek_ref, rk_ref, dest_ref):
    ek = ek_ref[...]

    def add_expert(e, base):
        return jnp.where(ek == e, pad_start_ref[e], base)

    dest_ref[...] = rk_ref[...] + lax.fori_loop(0, N_EXPERTS, add_expert, jnp.zeros_like(ek))


def _plan(pad_start, ek, rk):
    kk, t = ek.shape
    tl = min(t, 2048)
    col_map = lambda i, ps: (0, i)
    grid_spec = pltpu.PrefetchScalarGridSpec(
        num_scalar_prefetch=1,
        grid=(t // tl,),
        in_specs=[pl.BlockSpec((kk, tl), col_map), pl.BlockSpec((kk, tl), col_map)],
        out_specs=pl.BlockSpec((kk, tl), col_map),
    )
    return pl.pallas_call(
        _plan_kernel,
        grid_spec=grid_spec,
        out_shape=jax.ShapeDtypeStruct((kk, t), I32),
        compiler_params=pltpu.CompilerParams(dimension_semantics=("arbitrary",)),
        name="plan",
    )(pad_start, ek, rk)


def _sc_mesh():
    return plsc.VectorSubcoreMesh(core_axis_name="core", subcore_axis_name="subcore")


def _sc_dispatch(rows, dest, n_out):
    t, w = rows.shape
    kk = dest.shape[0]

    @pl.kernel(out_type=jax.ShapeDtypeStruct((n_out, w), rows.dtype), mesh=_sc_mesh(), scratch_types=[])
    def kern(x_hbm, i_hbm, o_hbm):
        def body(x_vmem, i_vmem):
            for k in range(kk):
                pltpu.sync_copy(x_vmem, o_hbm.at[i_vmem.at[k]])

        pltpu.emit_pipeline(
            body,
            grid=(t // SC_WINDOW,),
            in_specs=[pl.BlockSpec((SC_WINDOW, w), lambda i: (i, 0)),
                      pl.BlockSpec((kk, SC_WINDOW), lambda i: (0, i))],
            out_specs=[],
            core_axis_name=("core", "subcore"),
            dimension_semantics=(pltpu.PARALLEL,),
        )(x_hbm, i_hbm)

    return kern(rows, dest)


def _sc_combine(rows, dest):
    kk, t = dest.shape
    w = rows.shape[1]
    flat = dest.reshape(1, kk * t)

    @pl.kernel(out_type=jax.ShapeDtypeStruct((kk * t, w), rows.dtype), mesh=_sc_mesh(), scratch_types=[])
    def kern(y_hbm, i_hbm, o_hbm):
        def body(i_vmem, o_vmem):
            pltpu.sync_copy(y_hbm.at[i_vmem.at[0]], o_vmem)

        pltpu.emit_pipeline(
            body,
            grid=(kk * t // SC_WINDOW,),
            in_specs=[pl.BlockSpec((1, SC_WINDOW), lambda i: (0, i))],
            out_specs=[pl.BlockSpec((SC_WINDOW, w), lambda i: (i, 0))],
            core_axis_name=("core", "subcore"),
            dimension_semantics=(pltpu.PARALLEL,),
        )(i_hbm, o_hbm)

    return kern(rows, flat).reshape(kk, t, w)


def _experts_kernel(blk_expert_ref, n_used_ref, xa_ref, xb_ref, w1_ref, w3_ref, w2_ref, ya_ref, yb_ref,
                    w1b_ref, w3b_ref, w2b_ref):
    b = pl.program_id(0)
    prev = blk_expert_ref[jnp.maximum(b - 1, 0)]
    fresh = (b == 0) | (blk_expert_ref[b] != prev)

    @pl.when(fresh)
    def _():
        w1b_ref[...] = w1_ref[0].astype(BF16)
        w3b_ref[...] = w3_ref[0].astype(BF16)
        w2b_ref[...] = w2_ref[0].astype(BF16)

    @pl.when(b < n_used_ref[0])
    def _():
        x = _unpack_row(xa_ref[...], xb_ref[...]).astype(BF16)
        a = _dot(x, w1b_ref[...])
        g = _dot(x, w3b_ref[...])
        hmid = (jax.nn.silu(a) * g).astype(BF16)
        y = _dot(hmid, w2b_ref[...])
        pa, pb = _pack_row(y)
        ya_ref[...] = pa
        yb_ref[...] = pb


def _experts(xa, xb, w1, w3, w2, blk_expert, n_used):
    p = xa.shape[0]
    m = EXPERT_ROWS
    nblk = p // m
    row_map = lambda b, be, nu: (b, 0)
    w_map = lambda b, be, nu: (be[b], 0, 0)
    grid_spec = pltpu.PrefetchScalarGridSpec(
        num_scalar_prefetch=2,
        grid=(nblk,),
        in_specs=[
            pl.BlockSpec((m, PACK_W), row_map),
            pl.BlockSpec((m, PACK_W), row_map),
            pl.BlockSpec((1, D_MODEL, D_EXPERT), w_map),
            pl.BlockSpec((1, D_MODEL, D_EXPERT), w_map),
            pl.BlockSpec((1, D_EXPERT, D_MODEL), w_map),
        ],
        out_specs=[pl.BlockSpec((m, PACK_W), row_map), pl.BlockSpec((m, PACK_W), row_map)],
        scratch_shapes=[pltpu.VMEM((D_MODEL, D_EXPERT), BF16), pltpu.VMEM((D_MODEL, D_EXPERT), BF16),
                        pltpu.VMEM((D_EXPERT, D_MODEL), BF16)],
    )
    return pl.pallas_call(
        _experts_kernel,
        grid_spec=grid_spec,
        out_shape=(jax.ShapeDtypeStruct((p, PACK_W), U32), jax.ShapeDtypeStruct((p, PACK_W), U32)),
        compiler_params=pltpu.CompilerParams(dimension_semantics=("arbitrary",), vmem_limit_bytes=VMEM_LIMIT),
        name="experts",
    )(blk_expert, n_used, xa, xb, w1, w3, w2)


def _tail_kernel(h1_ref, ga_ref, gb_ref, wk_ref, p_ref, lnmoe_ref, ws1_ref, ws3_ref, ws2_ref, lnple_ref,
                 wpg_ref, wpp_ref, lnf_ref, o_ref):
    h1 = h1_ref[...]
    hn = _rms(h1, lnmoe_ref[...]).astype(BF16)
    shared = _dot((jax.nn.silu(_dot(hn, ws1_ref[...])) * _dot(hn, ws3_ref[...])).astype(BF16), ws2_ref[...])
    wk = wk_ref[...]
    routed = jnp.zeros_like(h1)
    for kk in range(TOP_K):
        routed = routed + wk[:, kk:kk + 1] * _unpack_row(ga_ref[kk], gb_ref[kk])
    h2 = h1 + routed + shared
    gate = jax.nn.sigmoid(_dot(_rms(h2, lnple_ref[...]).astype(BF16), wpg_ref[...]))
    h3 = h2 + gate * _dot(p_ref[...].astype(BF16), wpp_ref[...])
    o_ref[...] = _rms(h3, lnf_ref[...])


def _tail(h1, ga, gb, wk_t, p2, ln_moe, ws1, ws3, ws2, ln_ple, w_pg, w_pp, ln_f):
    t = h1.shape[0]
    tm = TAIL_ROWS
    row_map = lambda i: (i, 0)
    fixed = lambda i: (0, 0)
    g_map = lambda i: (0, i, 0)
    d_sh = ws1.shape[1]
    return pl.pallas_call(
        _tail_kernel,
        grid=(t // tm,),
        in_specs=[
            pl.BlockSpec((tm, D_MODEL), row_map),
            pl.BlockSpec((TOP_K, tm, PACK_W), g_map),
            pl.BlockSpec((TOP_K, tm, PACK_W), g_map),
            pl.BlockSpec((tm, TOP_K), row_map),
            pl.BlockSpec((tm, D_PLE), row_map),
            pl.BlockSpec((1, D_MODEL), fixed),
            pl.BlockSpec((D_MODEL, d_sh), fixed),
            pl.BlockSpec((D_MODEL, d_sh), fixed),
            pl.BlockSpec((d_sh, D_MODEL), fixed),
            pl.BlockSpec((1, D_MODEL), fixed),
            pl.BlockSpec((D_MODEL, D_MODEL), fixed),
            pl.BlockSpec((D_PLE, D_MODEL), fixed),
            pl.BlockSpec((1, D_MODEL), fixed),
        ],
        out_specs=pl.BlockSpec((tm, D_MODEL), row_map),
        out_shape=jax.ShapeDtypeStruct((t, D_MODEL), F32),
        compiler_params=pltpu.CompilerParams(dimension_semantics=("arbitrary",), vmem_limit_bytes=VMEM_LIMIT),
        name="tail",
    )(h1, ga, gb, wk_t, p2, ln_moe, ws1, ws3, ws2, ln_ple, w_pg, w_pp, ln_f)


def _rope_constants():
    half = ROPE_DIM // 2
    inv_freq = (ROPE_THETA ** (-jnp.arange(0, ROPE_DIM, 2, dtype=F32) / ROPE_DIM)).reshape(half, 1)
    f = lax.broadcasted_iota(I32, (ROPE_DIM, LANES), 0)
    l64 = lax.broadcasted_iota(I32, (ROPE_DIM, LANES), 1) % HALF_DIM
    cos_pat = ((f < half) & (l64 < ROPE_DIM) & (l64 % half == f)).astype(F32)
    sa_pat = -((f >= half) & (l64 < half) & (l64 == f - half)).astype(F32)
    sb_pat = ((f >= half) & (l64 >= half) & (l64 < ROPE_DIM) & (l64 - half == f - half)).astype(F32)
    return inv_freq, jnp.concatenate([cos_pat, sa_pat, sb_pat], axis=1)


def _block_diag_tiles(w):
    nb, bd, _ = w.shape
    per = nb // 2
    tiles = []
    for tix in range(2):
        rows = []
        for j in range(per):
            rows.append(jnp.concatenate(
                [w[tix * per + j] if c == j else jnp.zeros((bd, bd), w.dtype) for c in range(per)], axis=1))
        tiles.append(jnp.concatenate(rows, axis=0))
    return jnp.stack(tiles).astype(BF16)


def _layer(h, p_l, positions, lam_init, ln_mix, w_in, conv_w, conv_b, w_a, b_a, w_i, b_i, rg_lambda, g_rec,
           lq1, lk1, lq2, lk2, g_sub, w_out, ln_moe, w_router, e_bias, w1, w3, w2, ws1, ws3, ws2,
           ln_ple, w_ple_gate, w_ple_proj, ln_out):
    batch, seq, _ = h.shape
    t = batch * seq
    x2 = h.reshape(t, D_MODEL)
    row = lambda a: a.reshape(1, -1)
    inv_freq, rope_pat = _rope_constants()

    y_rec, q, k, vt = _mix_in(
        x2, positions.reshape(1, t), inv_freq, rope_pat, row(ln_mix), w_in.astype(BF16), conv_w, row(conv_b),
        _block_diag_tiles(w_a), row(b_a), _block_diag_tiles(w_i), row(b_i), row(rg_lambda), row(g_rec),
        batch, seq)
    y_att = _attention(q, k, vt, row(lq1), row(lk1), row(lq2), row(lk2), g_sub.reshape(-1, 1), batch, seq,
                       lam_init)

    w_out_b = w_out.astype(BF16)
    h1, hpa, hpb, ek, wk, rk, counts = _route(
        x2, y_rec, y_att, w_out_b[:D_REC], w_out_b[D_REC:], row(ln_moe), w_router.T, e_bias.reshape(-1, 1))

    m = EXPERT_ROWS
    counts = counts.reshape(-1)
    padded = (counts + m - 1) // m * m
    pad_end = jnp.cumsum(padded)
    pad_start = pad_end - padded
    n_rows = t * TOP_K + N_EXPERTS * m
    nblk = n_rows // m
    n_used = (pad_end[-1] // m).astype(I32).reshape(1)
    blk_row = jnp.minimum(jnp.arange(nblk, dtype=I32), n_used[0] - 1) * m
    blk_expert = jnp.sum((pad_end[None, :] <= blk_row[:, None]).astype(I32), axis=1)
    dest = _plan(pad_start.astype(I32), ek, rk)

    xa = _sc_dispatch(hpa, dest, n_rows)
    xb = _sc_dispatch(hpb, dest, n_rows)
    ya, yb = _experts(xa, xb, w1, w3, w2, blk_expert, n_used)
    ga = _sc_combine(ya, dest)
    gb = _sc_combine(yb, dest)

    out = _tail(h1, ga, gb, wk.T, p_l.reshape(t, D_PLE), row(ln_moe), ws1.astype(BF16), ws3.astype(BF16),
                ws2.astype(BF16), row(ln_ple), w_ple_gate.astype(BF16), w_ple_proj.astype(BF16), row(ln_out))
    return out.reshape(batch, seq, D_MODEL)


def kernel(x, p, positions, ln_mix, w_in, conv_w, conv_b, w_a, b_a, w_i, b_i, rg_lambda, g_rec, lq1, lk1, lq2,
           lk2, g_sub, w_out, ln_moe, w_router, e_bias, w1, w3, w2, ws1, ws3, ws2, ln_ple, w_ple_gate,
           w_ple_proj, ln_f):
    depth = w_in.shape[0]
    assert depth == 1, "the fused tail applies the final norm; one layer supported"
    lam_init = 0.8 - 0.6 * math.exp(-0.3 * 0)
    return _layer(x, p[0], positions, lam_init, ln_mix[0], w_in[0], conv_w[0], conv_b[0], w_a[0], b_a[0], w_i[0],
                  b_i[0], rg_lambda[0], g_rec[0], lq1[0], lk1[0], lq2[0], lk2[0], g_sub[0], w_out[0], ln_moe[0],
                  w_router[0], e_bias[0], w1[0], w3[0], w2[0], ws1[0], ws3[0], ws2[0], ln_ple[0], w_ple_gate[0],
                  w_ple_proj[0], ln_f)
```

```python
import functools
import math

import jax
import jax.numpy as jnp
from jax import lax
from jax.experimental import pallas as pl
from jax.experimental.pallas import tpu as pltpu
from jax.experimental.pallas import tpu_sc as plsc

F32 = jnp.float32
BF16 = jnp.bfloat16
U32 = jnp.uint32
I32 = jnp.int32

D_MODEL = 1024
D_REC = 512
REC_BLOCKS = 8
CONV_WIDTH = 4
RG_C = 8.0
N_HEADS = 4
HALF_DIM = 64
V_DIM = 128
D_ATT = N_HEADS * V_DIM
D_QK = N_HEADS * 2 * HALF_DIM
ROPE_DIM = 16
ROPE_THETA = 500000.0
N_EXPERTS = 64
TOP_K = 8
N_GROUPS = 8
GROUP_SIZE = N_EXPERTS // N_GROUPS
TOPK_GROUPS = 4
D_EXPERT = 256
ROUTE_SCALE = 2.5
D_PLE = 256
EPS = 1e-6

LANES = 128
SUBLANES = 8
VMEM_LIMIT = 56 * 1024 * 1024

MIX_ROWS = 512
ATT_Q = 512
ROUTE_ROWS = 512
EXPERT_ROWS = 512
TAIL_ROWS = 256
SC_WINDOW = 128
PACK_W = 256
NEG_BIG = -1e30


def _rms(x, g):
    return x * lax.rsqrt(jnp.mean(x * x, axis=-1, keepdims=True) + EPS) * g


def _dot(a, b):
    return jnp.dot(a, b, preferred_element_type=F32)


def _pack_pair(lo, hi):
    lo_bits = lax.bitcast_convert_type(lo.astype(BF16).astype(F32), U32)
    hi_bits = lax.bitcast_convert_type(hi.astype(BF16).astype(F32), U32)
    return (lo_bits >> 16) | (hi_bits & jnp.uint32(0xFFFF0000))


def _unpack_pair(p):
    lo = lax.bitcast_convert_type(p << 16, F32)
    hi = lax.bitcast_convert_type(p & jnp.uint32(0xFFFF0000), F32)
    return lo, hi


def _pack_row(x):
    w = PACK_W
    return _pack_pair(x[:, 0:w], x[:, w:2 * w]), _pack_pair(x[:, 2 * w:3 * w], x[:, 3 * w:4 * w])


def _unpack_row(pa, pb):
    c0, c1 = _unpack_pair(pa)
    c2, c3 = _unpack_pair(pb)
    return jnp.concatenate([c0, c1, c2, c3], axis=1)


def _shift_rows(a, s, fill, row):
    n, c = a.shape
    if s % SUBLANES == 0:
        return jnp.concatenate([jnp.full((s, c), fill, a.dtype), a[:n - s]], axis=0)
    return jnp.where(row >= s, pltpu.roll(a, s, 0), fill)


def _mix_in_kernel(x_ref, pos_ref, invf_ref, pat_ref, lnm_ref, win_ref, cw_ref, cb_ref, wa_ref, ba_ref,
                   wi_ref, bi_ref, lam_ref, grec_ref,
                   yrec_ref, q_ref, k_ref, vt_ref, tail_ref, hcarry_ref):
    tm = x_ref.shape[0]

    @pl.when(pl.program_id(1) == 0)
    def _():
        tail_ref[...] = jnp.zeros_like(tail_ref)
        hcarry_ref[...] = jnp.zeros_like(hcarry_ref)

    hn = _rms(x_ref[...], lnm_ref[...]).astype(BF16)
    xr = _dot(hn, win_ref[:, 0:D_REC])

    row = lax.broadcasted_iota(I32, (tm, D_REC), 0)
    row8 = lax.broadcasted_iota(I32, (SUBLANES, D_REC), 0)
    tail = tail_ref[...]
    xc = cb_ref[...] + cw_ref[CONV_WIDTH - 1:CONV_WIDTH, :] * xr
    for j in range(1, CONV_WIDTH):
        rolled = pltpu.roll(xr, j, 0)
        head = jnp.where(row8 < j, pltpu.roll(tail, j, 0), rolled[:SUBLANES])
        shifted = jnp.concatenate([head, rolled[SUBLANES:]], axis=0)
        xc = xc + cw_ref[CONV_WIDTH - 1 - j:CONV_WIDTH - j, :] * shifted
    tail_ref[...] = xr[tm - SUBLANES:, :]

    xcb = xc.astype(BF16)
    half = D_REC // 2
    ra = jnp.concatenate([_dot(xcb[:, :half], wa_ref[0]), _dot(xcb[:, half:], wa_ref[1])], axis=1)
    ri = jnp.concatenate([_dot(xcb[:, :half], wi_ref[0]), _dot(xcb[:, half:], wi_ref[1])], axis=1)
    r = jax.nn.sigmoid(ra + ba_ref[...])
    ig = jax.nn.sigmoid(ri + bi_ref[...])
    lam = lam_ref[...]
    softplus_neg = jnp.maximum(-lam, 0.0) + jnp.log(1.0 + jnp.exp(-jnp.abs(lam)))
    log_a = -RG_C * r * softplus_neg
    a = jnp.exp(log_a)
    u = jnp.sqrt(1.0 - jnp.exp(2.0 * log_a)) * ig * xc

    s = 1
    while s < tm:
        u = u + a * _shift_rows(u, s, 0.0, row)
        a = a * _shift_rows(a, s, 1.0, row)
        s *= 2
    h = u + a * hcarry_ref[...]
    hcarry_ref[...] = h[tm - 1:tm, :]

    gate = _dot(hn, win_ref[:, D_REC:2 * D_REC])
    y = h * jax.nn.gelu(gate)
    yrec_ref[...] = _rms(y, grec_ref[...]).astype(BF16)

    ang = invf_ref[...] * pos_ref[...].astype(F32)
    cs = jnp.concatenate([jnp.cos(ang), jnp.sin(ang)], axis=0)
    tabs = lax.dot_general(cs, pat_ref[...], (((0,), (0,)), ((), ())),
                           precision=lax.Precision.HIGHEST, preferred_element_type=F32)
    lane64 = lax.broadcasted_iota(I32, (1, LANES), 1) % HALF_DIM
    cosf = tabs[:, 0:LANES] + (lane64 >= ROPE_DIM).astype(F32)
    sa, sb = tabs[:, LANES:2 * LANES], tabs[:, 2 * LANES:3 * LANES]
    scale = HALF_DIM ** -0.5 * math.log2(math.e)
    for name, out_ref, off, mul in (("q", q_ref, 2 * D_REC, scale), ("k", k_ref, 2 * D_REC + D_QK, 1.0)):
        for c in range(D_QK // LANES):
            zc = _dot(hn, win_ref[:, off + c * LANES: off + (c + 1) * LANES])
            rot = zc * cosf + pltpu.roll(zc, LANES - ROPE_DIM // 2, 1) * sa + pltpu.roll(zc, ROPE_DIM // 2, 1) * sb
            out_ref[:, c * LANES:(c + 1) * LANES] = (rot * mul).astype(BF16)
    vt_ref[0] = _dot(hn, win_ref[:, 2 * D_REC + 2 * D_QK:]).T.astype(BF16)


def _mix_in(x2, pos_row, inv_freq, rope_pat, ln_mix, w_in, conv_w, conv_b, wa_bd, b_a, wi_bd, b_i, rg_lambda,
            g_rec, batch, seq):
    tm = MIX_ROWS
    nt = seq // tm
    d_in = w_in.shape[1]
    row_map = lambda b, i: (b * nt + i, 0)
    fixed2 = lambda b, i: (0, 0)
    fixed3 = lambda b, i: (0, 0, 0)
    t = batch * seq
    out_shapes = (
        jax.ShapeDtypeStruct((t, D_REC), BF16),
        jax.ShapeDtypeStruct((t, D_QK), BF16),
        jax.ShapeDtypeStruct((t, D_QK), BF16),
        jax.ShapeDtypeStruct((t // tm, D_ATT, tm), BF16),
    )
    return pl.pallas_call(
        _mix_in_kernel,
        grid=(batch, nt),
        in_specs=[
            pl.BlockSpec((tm, D_MODEL), row_map),
            pl.BlockSpec((1, tm), lambda b, i: (0, b * nt + i)),
            pl.BlockSpec((ROPE_DIM // 2, 1), fixed2),
            pl.BlockSpec((ROPE_DIM, 3 * LANES), fixed2),
            pl.BlockSpec((1, D_MODEL), fixed2),
            pl.BlockSpec((D_MODEL, d_in), fixed2),
            pl.BlockSpec((CONV_WIDTH, D_REC), fixed2),
            pl.BlockSpec((1, D_REC), fixed2),
            pl.BlockSpec((2, D_REC // 2, D_REC // 2), fixed3),
            pl.BlockSpec((1, D_REC), fixed2),
            pl.BlockSpec((2, D_REC // 2, D_REC // 2), fixed3),
            pl.BlockSpec((1, D_REC), fixed2),
            pl.BlockSpec((1, D_REC), fixed2),
            pl.BlockSpec((1, D_REC), fixed2),
        ],
        out_specs=[
            pl.BlockSpec((tm, D_REC), row_map),
            pl.BlockSpec((tm, D_QK), row_map),
            pl.BlockSpec((tm, D_QK), row_map),
            pl.BlockSpec((1, D_ATT, tm), lambda b, i: (b * nt + i, 0, 0)),
        ],
        out_shape=out_shapes,
        scratch_shapes=[pltpu.VMEM((SUBLANES, D_REC), F32), pltpu.VMEM((1, D_REC), F32)],
        compiler_params=pltpu.CompilerParams(
            dimension_semantics=("arbitrary", "arbitrary"), vmem_limit_bytes=VMEM_LIMIT),
        name="mix_in",
    )(x2, pos_row, inv_freq, rope_pat, ln_mix, w_in, conv_w, conv_b, wa_bd, b_a, wi_bd, b_i, rg_lambda, g_rec)


def _attn_kernel(lq1_ref, lk1_ref, lq2_ref, lk2_ref, gsub_ref, bias_ref, q_ref, k_ref, vt_ref, o_ref,
                 m_ref, l_ref, acc_ref, aprev_ref, s0_ref, s1_ref, p0_ref, p1_ref, *, lam_init):
    tq = q_ref.shape[0]
    tk = vt_ref.shape[2]
    assert tq == tk, "the causal bias tile assumes the diagonal block is square"
    i = pl.program_id(2)

    qt = q_ref[...].astype(F32).T
    dim = lax.broadcasted_iota(I32, (LANES, tq), 0)
    qqt = jnp.concatenate([jnp.where(dim < HALF_DIM, qt, 0.0), jnp.where(dim >= HALF_DIM, qt, 0.0)],
                          axis=1).astype(BF16)

    n = (i * tq) // tk

    def scores(j):
        return _dot(k_ref[pl.ds(pl.multiple_of(j * tk, tk), tk), :], qqt)

    def softmax_update(s):
        m_prev = m_ref[...]
        m_new = jnp.maximum(m_prev, jnp.max(s, axis=0, keepdims=True))
        alpha = jnp.exp2(m_prev - m_new)
        p = jnp.exp2(s - m_new)
        l_ref[...] = alpha * l_ref[...] + jnp.sum(p, axis=0, keepdims=True)
        m_ref[...] = m_new
        return p.astype(BF16), alpha

    def pipe_step(j, s_cur, s_nxt, p_cur, p_prev):
        s_nxt[...] = scores(j + 1)
        p, alpha = softmax_update(s_cur[...])
        acc_ref[...] = aprev_ref[...] * acc_ref[...] + _dot(vt_ref[jnp.maximum(j - 1, 0)], p_prev[...])
        aprev_ref[...] = alpha
        p_cur[...] = p

    m_ref[...] = jnp.full_like(m_ref, NEG_BIG)
    l_ref[...] = jnp.zeros_like(l_ref)
    acc_ref[...] = jnp.zeros_like(acc_ref)
    aprev_ref[...] = jnp.ones_like(aprev_ref)
    odd = n % 2

    @pl.when(odd == 0)
    def _():
        p1_ref[...] = jnp.zeros_like(p1_ref)
        s0_ref[...] = scores(0)

    @pl.when(odd == 1)
    def _():
        p0_ref[...] = jnp.zeros_like(p0_ref)
        s1_ref[...] = scores(0)
        pipe_step(0, s1_ref, s0_ref, p1_ref, p0_ref)

    def pair(t, carry):
        j = 2 * t + odd
        pipe_step(j, s0_ref, s1_ref, p0_ref, p1_ref)
        pipe_step(j + 1, s1_ref, s0_ref, p1_ref, p0_ref)
        return carry

    lax.fori_loop(0, n // 2, pair, 0)

    p, alpha = softmax_update(s0_ref[...] + bias_ref[...])
    acc = aprev_ref[...] * acc_ref[...] + _dot(vt_ref[jnp.maximum(n - 1, 0)], p1_ref[...])
    acc = alpha * acc + _dot(vt_ref[n], p)

    lam = (jnp.exp(jnp.sum(lq1_ref[...] * lk1_ref[...], axis=-1, keepdims=True))
           - jnp.exp(jnp.sum(lq2_ref[...] * lk2_ref[...], axis=-1, keepdims=True)) + lam_init)
    o = acc / l_ref[...]
    o = o[:, :tq] - lam * o[:, tq:]
    o = o * lax.rsqrt(jnp.mean(o * o, axis=0, keepdims=True) + EPS) * gsub_ref[...]
    o_ref[...] = (o * (1.0 - lam_init)).T.astype(BF16)


def _attention(q, k, vt, lq1, lk1, lq2, lk2, g_sub_col, batch, seq, lam_init):
    tq = ATT_Q
    nq = seq // tq
    tk = vt.shape[2]
    nk = seq // tk
    vec = lambda b, h, i: (0, 0)
    visible = (lax.broadcasted_iota(I32, (tk, 2 * tq), 0) <= lax.broadcasted_iota(I32, (tk, 2 * tq), 1) % tq)
    bias = jnp.where(visible, 0.0, NEG_BIG).astype(F32)
    return pl.pallas_call(
        functools.partial(_attn_kernel, lam_init=lam_init),
        grid=(batch, N_HEADS, nq),
        in_specs=[
            pl.BlockSpec((1, HALF_DIM), vec),
            pl.BlockSpec((1, HALF_DIM), vec),
            pl.BlockSpec((1, HALF_DIM), vec),
            pl.BlockSpec((1, HALF_DIM), vec),
            pl.BlockSpec((V_DIM, 1), vec),
            pl.BlockSpec((tk, 2 * tq), vec),
            pl.BlockSpec((tq, LANES), lambda b, h, i: (b * nq + i, h)),
            pl.BlockSpec((seq, LANES), lambda b, h, i: (b, h)),
            pl.BlockSpec((nk, V_DIM, tk), lambda b, h, i: (b, h, 0)),
        ],
        out_specs=pl.BlockSpec((tq, V_DIM), lambda b, h, i: (b * nq + i, h)),
        out_shape=jax.ShapeDtypeStruct((batch * seq, D_ATT), BF16),
        scratch_shapes=[pltpu.VMEM((1, 2 * tq), F32), pltpu.VMEM((1, 2 * tq), F32),
                        pltpu.VMEM((V_DIM, 2 * tq), F32), pltpu.VMEM((1, 2 * tq), F32),
                        pltpu.VMEM((tk, 2 * tq), F32), pltpu.VMEM((tk, 2 * tq), F32),
                        pltpu.VMEM((tk, 2 * tq), BF16), pltpu.VMEM((tk, 2 * tq), BF16)],
        compiler_params=pltpu.CompilerParams(
            dimension_semantics=("arbitrary", "arbitrary", "arbitrary"), vmem_limit_bytes=VMEM_LIMIT),
        name="attention",
    )(lq1, lk1, lq2, lk2, g_sub_col, bias, q, k, vt)


def _sublane_total(x, op):
    return op(x, axis=0, keepdims=True)


def _route_kernel(x_ref, yrec_ref, yatt_ref, woa_ref, wob_ref, lnmoe_ref, wrt_ref, ebias_ref, tri_ref, low_ref,
                  h1_ref, hpa_ref, hpb_ref, ek_ref, wk_ref, rk_ref, cnt_ref, carry_ref):
    tm = x_ref.shape[0]
    e_n = N_EXPERTS

    @pl.when(pl.program_id(0) == 0)
    def _():
        carry_ref[...] = jnp.zeros_like(carry_ref)

    h1 = x_ref[...] + _dot(yrec_ref[...], woa_ref[...]) + _dot(yatt_ref[...], wob_ref[...])
    h1_ref[...] = h1
    hn = _rms(h1, lnmoe_ref[...])
    pa, pb = _pack_row(hn)
    hpa_ref[...] = pa
    hpb_ref[...] = pb

    logits = lax.dot_general(wrt_ref[...], hn, (((1,), (1,)), ((), ())),
                             precision=lax.Precision.HIGHEST, preferred_element_type=F32)
    scores = jax.nn.sigmoid(logits)
    sel = scores + ebias_ref[...]

    sel3 = sel.reshape(N_GROUPS, GROUP_SIZE, tm)
    idx3 = lax.broadcasted_iota(I32, (N_GROUPS, GROUP_SIZE, tm), 1)
    m1 = jnp.max(sel3, axis=1, keepdims=True)
    first = jnp.min(jnp.where(sel3 == m1, idx3, GROUP_SIZE), axis=1, keepdims=True)
    m2 = jnp.max(jnp.where(idx3 == first, -jnp.inf, sel3), axis=1, keepdims=True)
    gscore = (m1 + m2).reshape(N_GROUPS, tm)

    gidx = lax.broadcasted_iota(I32, (N_GROUPS, tm), 0)
    beaten = jnp.zeros((N_GROUPS, tm), I32)
    for g in range(N_GROUPS):
        other = gscore[g:g + 1, :]
        beats = (other > gscore) | ((other == gscore) & (g < gidx))
        beaten = beaten + beats.astype(I32)
    gkeep = beaten < TOPK_GROUPS
    keep = jnp.broadcast_to(gkeep.reshape(N_GROUPS, 1, tm), (N_GROUPS, GROUP_SIZE, tm)).reshape(e_n, tm)
    selm = jnp.where(keep, sel, -jnp.inf)

    eidx = lax.broadcasted_iota(I32, (e_n, tm), 0)
    beaten = jnp.zeros((e_n, tm), I32)
    for e in range(e_n):
        other = selm[e:e + 1, :]
        beats = (other > selm) | ((other == selm) & (e < eidx))
        beaten = beaten + beats.astype(I32)
    chosen = beaten < TOP_K
    chosen_f = chosen.astype(F32)

    wsum = _sublane_total(jnp.where(chosen, scores, 0.0), jnp.sum)
    weight = scores * (ROUTE_SCALE / wsum)

    chosen_b = chosen_f.astype(BF16)
    prefix = _dot(chosen_b, tri_ref[...])
    rank = prefix + carry_ref[...]
    carry_new = carry_ref[...] + jnp.sum(chosen_f, axis=1, keepdims=True)
    carry_ref[...] = carry_new
    cnt_ref[...] = carry_new.astype(I32)

    slot = _dot(low_ref[...], chosen_b)
    eidx_f = eidx.astype(F32)
    ek, wk, rk = [], [], []
    for kk in range(TOP_K):
        pick = chosen & (slot == float(kk))
        ek.append(_sublane_total(jnp.where(pick, eidx_f, 0.0), jnp.sum))
        wk.append(_sublane_total(jnp.where(pick, weight, 0.0), jnp.sum))
        rk.append(_sublane_total(jnp.where(pick, rank, 0.0), jnp.sum))
    ek_ref[...] = jnp.concatenate(ek, axis=0).astype(I32)
    wk_ref[...] = jnp.concatenate(wk, axis=0)
    rk_ref[...] = jnp.concatenate(rk, axis=0).astype(I32)


def _route(x2, y_rec, y_att, wo_a, wo_b, ln_moe, w_router_t, e_bias_col):
    t = x2.shape[0]
    tm = ROUTE_ROWS
    nt = t // tm
    row_map = lambda i: (i, 0)
    col_map = lambda i: (0, i)
    fixed = lambda i: (0, 0)
    tri = (lax.broadcasted_iota(I32, (tm, tm), 0) < lax.broadcasted_iota(I32, (tm, tm), 1)).astype(BF16)
    low = (lax.broadcasted_iota(I32, (N_EXPERTS, N_EXPERTS), 1)
           < lax.broadcasted_iota(I32, (N_EXPERTS, N_EXPERTS), 0)).astype(BF16)
    out_shapes = (
        jax.ShapeDtypeStruct((t, D_MODEL), F32),
        jax.ShapeDtypeStruct((t, PACK_W), U32),
        jax.ShapeDtypeStruct((t, PACK_W), U32),
        jax.ShapeDtypeStruct((TOP_K, t), I32),
        jax.ShapeDtypeStruct((TOP_K, t), F32),
        jax.ShapeDtypeStruct((TOP_K, t), I32),
        jax.ShapeDtypeStruct((N_EXPERTS, 1), I32),
    )
    return pl.pallas_call(
        _route_kernel,
        grid=(nt,),
        in_specs=[
            pl.BlockSpec((tm, D_MODEL), row_map),
            pl.BlockSpec((tm, D_REC), row_map),
            pl.BlockSpec((tm, D_ATT), row_map),
            pl.BlockSpec((D_REC, D_MODEL), fixed),
            pl.BlockSpec((D_ATT, D_MODEL), fixed),
            pl.BlockSpec((1, D_MODEL), fixed),
            pl.BlockSpec((N_EXPERTS, D_MODEL), fixed),
            pl.BlockSpec((N_EXPERTS, 1), fixed),
            pl.BlockSpec((tm, tm), fixed),
            pl.BlockSpec((N_EXPERTS, N_EXPERTS), fixed),
        ],
        out_specs=[
            pl.BlockSpec((tm, D_MODEL), row_map),
            pl.BlockSpec((tm, PACK_W), row_map),
            pl.BlockSpec((tm, PACK_W), row_map),
            pl.BlockSpec((TOP_K, tm), col_map),
            pl.BlockSpec((TOP_K, tm), col_map),
            pl.BlockSpec((TOP_K, tm), col_map),
            pl.BlockSpec((N_EXPERTS, 1), fixed),
        ],
        out_shape=out_shapes,
        scratch_shapes=[pltpu.VMEM((N_EXPERTS, 1), F32)],
        compiler_params=pltpu.CompilerParams(dimension_semantics=("arbitrary",), vmem_limit_bytes=VMEM_LIMIT),
        name="route",
    )(x2, y_rec, y_att, wo_a, wo_b, ln_moe, w_router_t, e_bias_col, tri, low)


def _plan_kernel(pad_start_ref, ek_ref, rk_ref, dest_ref):
    ek = ek_ref[...]

    def add_expert(e, base):
        return jnp.where(ek == e, pad_start_ref[e], base)

    dest_ref[...] = rk_ref[...] + lax.fori_loop(0, N_EXPERTS, add_expert, jnp.zeros_like(ek))


def _plan(pad_start, ek, rk):
    kk, t = ek.shape
    tl = min(t, 2048)
    col_map = lambda i, ps: (0, i)
    grid_spec = pltpu.PrefetchScalarGridSpec(
        num_scalar_prefetch=1,
        grid=(t // tl,),
        in_specs=[pl.BlockSpec((kk, tl), col_map), pl.BlockSpec((kk, tl), col_map)],
        out_specs=pl.BlockSpec((kk, tl), col_map),
    )
    return pl.pallas_call(
        _plan_kernel,
        grid_spec=grid_spec,
        out_shape=jax.ShapeDtypeStruct((kk, t), I32),
        compiler_params=pltpu.CompilerParams(dimension_semantics=("arbitrary",)),
        name="plan",
    )(pad_start, ek, rk)


def _sc_mesh():
    return plsc.VectorSubcoreMesh(core_axis_name="core", subcore_axis_name="subcore")


def _sc_dispatch(rows, dest, n_out):
    t, w = rows.shape
    kk = dest.shape[0]

    @pl.kernel(out_type=jax.ShapeDtypeStruct((n_out, w), rows.dtype), mesh=_sc_mesh(), scratch_types=[])
    def kern(x_hbm, i_hbm, o_hbm):
        def body(x_vmem, i_vmem):
            for k in range(kk):
                pltpu.sync_copy(x_vmem, o_hbm.at[i_vmem.at[k]])

        pltpu.emit_pipeline(
            body,
            grid=(t // SC_WINDOW,),
            in_specs=[pl.BlockSpec((SC_WINDOW, w), lambda i: (i, 0)),
                      pl.BlockSpec((kk, SC_WINDOW), lambda i: (0, i))],
            out_specs=[],
            core_axis_name=("core", "subcore"),
            dimension_semantics=(pltpu.PARALLEL,),
        )(x_hbm, i_hbm)

    return kern(rows, dest)


def _sc_combine(rows, dest):
    kk, t = dest.shape
    w = rows.shape[1]
    flat = dest.reshape(1, kk * t)

    @pl.kernel(out_type=jax.ShapeDtypeStruct((kk * t, w), rows.dtype), mesh=_sc_mesh(), scratch_types=[])
    def kern(y_hbm, i_hbm, o_hbm):
        def body(i_vmem, o_vmem):
            pltpu.sync_copy(y_hbm.at[i_vmem.at[0]], o_vmem)

        pltpu.emit_pipeline(
            body,
            grid=(kk * t // SC_WINDOW,),
            in_specs=[pl.BlockSpec((1, SC_WINDOW), lambda i: (0, i))],
            out_specs=[pl.BlockSpec((SC_WINDOW, w), lambda i: (i, 0))],
            core_axis_name=("core", "subcore"),
            dimension_semantics=(pltpu.PARALLEL,),
        )(i_hbm, o_hbm)

    return kern(rows, flat).reshape(kk, t, w)


def _experts_kernel(blk_expert_ref, n_used_ref, xa_ref, xb_ref, w1_ref, w3_ref, w2_ref, ya_ref, yb_ref,
                    w1b_ref, w3b_ref, w2b_ref):
    b = pl.program_id(0)
    prev = blk_expert_ref[jnp.maximum(b - 1, 0)]
    fresh = (b == 0) | (blk_expert_ref[b] != prev)

    @pl.when(fresh)
    def _():
        w1b_ref[...] = w1_ref[0].astype(BF16)
        w3b_ref[...] = w3_ref[0].astype(BF16)
        w2b_ref[...] = w2_ref[0].astype(BF16)

    @pl.when(b < n_used_ref[0])
    def _():
        x = _unpack_row(xa_ref[...], xb_ref[...]).astype(BF16)
        a = _dot(x, w1b_ref[...])
        g = _dot(x, w3b_ref[...])
        hmid = (jax.nn.silu(a) * g).astype(BF16)
        y = _dot(hmid, w2b_ref[...])
        pa, pb = _pack_row(y)
        ya_ref[...] = pa
        yb_ref[...] = pb


def _experts(xa, xb, w1, w3, w2, blk_expert, n_used):
    p = xa.shape[0]
    m = EXPERT_ROWS
    nblk = p // m
    row_map = lambda b, be, nu: (b, 0)
    w_map = lambda b, be, nu: (be[b], 0, 0)
    grid_spec = pltpu.PrefetchScalarGridSpec(
        num_scalar_prefetch=2,
        grid=(nblk,),
        in_specs=[
            pl.BlockSpec((m, PACK_W), row_map),
            pl.BlockSpec((m, PACK_W), row_map),
            pl.BlockSpec((1, D_MODEL, D_EXPERT), w_map),
            pl.BlockSpec((1, D_MODEL, D_EXPERT), w_map),
            pl.BlockSpec((1, D_EXPERT, D_MODEL), w_map),
        ],
        out_specs=[pl.BlockSpec((m, PACK_W), row_map), pl.BlockSpec((m, PACK_W), row_map)],
        scratch_shapes=[pltpu.VMEM((D_MODEL, D_EXPERT), BF16), pltpu.VMEM((D_MODEL, D_EXPERT), BF16),
                        pltpu.VMEM((D_EXPERT, D_MODEL), BF16)],
    )
    return pl.pallas_call(
        _experts_kernel,
        grid_spec=grid_spec,
        out_shape=(jax.ShapeDtypeStruct((p, PACK_W), U32), jax.ShapeDtypeStruct((p, PACK_W), U32)),
        compiler_params=pltpu.CompilerParams(dimension_semantics=("arbitrary",), vmem_limit_bytes=VMEM_LIMIT),
        name="experts",
    )(blk_expert, n_used, xa, xb, w1, w3, w2)


def _tail_kernel(h1_ref, ga_ref, gb_ref, wk_ref, p_ref, lnmoe_ref, ws1_ref, ws3_ref, ws2_ref, lnple_ref,
                 wpg_ref, wpp_ref, lnf_ref, o_ref):
    h1 = h1_ref[...]
    hn = _rms(h1, lnmoe_ref[...]).astype(BF16)
    shared = _dot((jax.nn.silu(_dot(hn, ws1_ref[...])) * _dot(hn, ws3_ref[...])).astype(BF16), ws2_ref[...])
    wk = wk_ref[...]
    routed = jnp.zeros_like(h1)
    for kk in range(TOP_K):
        routed = routed + wk[:, kk:kk + 1] * _unpack_row(ga_ref[kk], gb_ref[kk])
    h2 = h1 + routed + shared
    gate = jax.nn.sigmoid(_dot(_rms(h2, lnple_ref[...]).astype(BF16), wpg_ref[...]))
    h3 = h2 + gate * _dot(p_ref[...].astype(BF16), wpp_ref[...])
    o_ref[...] = _rms(h3, lnf_ref[...])


def _tail(h1, ga, gb, wk_t, p2, ln_moe, ws1, ws3, ws2, ln_ple, w_pg, w_pp, ln_f):
    t = h1.shape[0]
    tm = TAIL_ROWS
    row_map = lambda i: (i, 0)
    fixed = lambda i: (0, 0)
    g_map = lambda i: (0, i, 0)
    d_sh = ws1.shape[1]
    return pl.pallas_call(
        _tail_kernel,
        grid=(t // tm,),
        in_specs=[
            pl.BlockSpec((tm, D_MODEL), row_map),
            pl.BlockSpec((TOP_K, tm, PACK_W), g_map),
            pl.BlockSpec((TOP_K, tm, PACK_W), g_map),
            pl.BlockSpec((tm, TOP_K), row_map),
            pl.BlockSpec((tm, D_PLE), row_map),
            pl.BlockSpec((1, D_MODEL), fixed),
            pl.BlockSpec((D_MODEL, d_sh), fixed),
            pl.BlockSpec((D_MODEL, d_sh), fixed),
            pl.BlockSpec((d_sh, D_MODEL), fixed),
            pl.BlockSpec((1, D_MODEL), fixed),
            pl.BlockSpec((D_MODEL, D_MODEL), fixed),
            pl.BlockSpec((D_PLE, D_MODEL), fixed),
            pl.BlockSpec((1, D_MODEL), fixed),
        ],
        out_specs=pl.BlockSpec((tm, D_MODEL), row_map),
        out_shape=jax.ShapeDtypeStruct((t, D_MODEL), F32),
        compiler_params=pltpu.CompilerParams(dimension_semantics=("arbitrary",), vmem_limit_bytes=VMEM_LIMIT),
        name="tail",
    )(h1, ga, gb, wk_t, p2, ln_moe, ws1, ws3, ws2, ln_ple, w_pg, w_pp, ln_f)


def _rope_constants():
    half = ROPE_DIM // 2
    inv_freq = (ROPE_THETA ** (-jnp.arange(0, ROPE_DIM, 2, dtype=F32) / ROPE_DIM)).reshape(half, 1)
    f = lax.broadcasted_iota(I32, (ROPE_DIM, LANES), 0)
    l64 = lax.broadcasted_iota(I32, (ROPE_DIM, LANES), 1) % HALF_DIM
    cos_pat = ((f < half) & (l64 < ROPE_DIM) & (l64 % half == f)).astype(F32)
    sa_pat = -((f >= half) & (l64 < half) & (l64 == f - half)).astype(F32)
    sb_pat = ((f >= half) & (l64 >= half) & (l64 < ROPE_DIM) & (l64 - half == f - half)).astype(F32)
    return inv_freq, jnp.concatenate([cos_pat, sa_pat, sb_pat], axis=1)


def _block_diag_tiles(w):
    nb, bd, _ = w.shape
    per = nb // 2
    tiles = []
    for tix in range(2):
        rows = []
        for j in range(per):
            rows.append(jnp.concatenate(
                [w[tix * per + j] if c == j else jnp.zeros((bd, bd), w.dtype) for c in range(per)], axis=1))
        tiles.append(jnp.concatenate(rows, axis=0))
    return jnp.stack(tiles).astype(BF16)


def _layer(h, p_l, positions, lam_init, ln_mix, w_in, conv_w, conv_b, w_a, b_a, w_i, b_i, rg_lambda, g_rec,
           lq1, lk1, lq2, lk2, g_sub, w_out, ln_moe, w_router, e_bias, w1, w3, w2, ws1, ws3, ws2,
           ln_ple, w_ple_gate, w_ple_proj, ln_out):
    batch, seq, _ = h.shape
    t = batch * seq
    x2 = h.reshape(t, D_MODEL)
    row = lambda a: a.reshape(1, -1)
    inv_freq, rope_pat = _rope_constants()

    y_rec, q, k, vt = _mix_in(
        x2, positions.reshape(1, t), inv_freq, rope_pat, row(ln_mix), w_in.astype(BF16), conv_w, row(conv_b),
        _block_diag_tiles(w_a), row(b_a), _block_diag_tiles(w_i), row(b_i), row(rg_lambda), row(g_rec),
        batch, seq)
    y_att = _attention(q, k, vt, row(lq1), row(lk1), row(lq2), row(lk2), g_sub.reshape(-1, 1), batch, seq,
                       lam_init)

    w_out_b = w_out.astype(BF16)
    h1, hpa, hpb, ek, wk, rk, counts = _route(
        x2, y_rec, y_att, w_out_b[:D_REC], w_out_b[D_REC:], row(ln_moe), w_router.T, e_bias.reshape(-1, 1))

    m = EXPERT_ROWS
    counts = counts.reshape(-1)
    padded = (counts + m - 1) // m * m
    pad_end = jnp.cumsum(padded)
    pad_start = pad_end - padded
    n_rows = t * TOP_K + N_EXPERTS * m
    nblk = n_rows // m
    n_used = (pad_end[-1] // m).astype(I32).reshape(1)
    blk_row = jnp.minimum(jnp.arange(nblk, dtype=I32), n_used[0] - 1) * m
    blk_expert = jnp.sum((pad_end[None, :] <= blk_row[:, None]).astype(I32), axis=1)
    dest = _plan(pad_start.astype(I32), ek, rk)

    xa = _sc_dispatch(hpa, dest, n_rows)
    xb = _sc_dispatch(hpb, dest, n_rows)
    ya, yb = _experts(xa, xb, w1, w3, w2, blk_expert, n_used)
    ga = _sc_combine(ya, dest)
    gb = _sc_combine(yb, dest)

    out = _tail(h1, ga, gb, wk.T, p_l.reshape(t, D_PLE), row(ln_moe), ws1.astype(BF16), ws3.astype(BF16),
                ws2.astype(BF16), row(ln_ple), w_ple_gate.astype(BF16), w_ple_proj.astype(BF16), row(ln_out))
    return out.reshape(batch, seq, D_MODEL)


def kernel(x, p, positions, ln_mix, w_in, conv_w, conv_b, w_a, b_a, w_i, b_i, rg_lambda, g_rec, lq1, lk1, lq2,
           lk2, g_sub, w_out, ln_moe, w_router, e_bias, w1, w3, w2, ws1, ws3, ws2, ln_ple, w_ple_gate,
           w_ple_proj, ln_f):
    depth = w_in.shape[0]
    assert depth == 1, "the fused tail applies the final norm; one layer supported"
    lam_init = 0.8 - 0.6 * math.exp(-0.3 * 0)
    return _layer(x, p[0], positions, lam_init, ln_mix[0], w_in[0], conv_w[0], conv_b[0], w_a[0], b_a[0], w_i[0],
                  b_i[0], rg_lambda[0], g_rec[0], lq1[0], lk1[0], lq2[0], lk2[0], g_sub[0], w_out[0], ln_moe[0],
                  w_router[0], e_bias[0], w1[0], w3[0], w2[0], ws1[0], ws3[0], ws2[0], ln_ple[0], w_ple_gate[0],
                  w_ple_proj[0], ln_f)
```

```python
import functools
import math

import jax
import jax.numpy as jnp
from jax import lax
from jax.experimental import pallas as pl
from jax.experimental.pallas import tpu as pltpu
from jax.experimental.pallas import tpu_sc as plsc

F32 = jnp.float32
BF16 = jnp.bfloat16
U32 = jnp.uint32
I32 = jnp.int32

D_MODEL = 1024
D_REC = 512
REC_BLOCKS = 8
CONV_WIDTH = 4
RG_C = 8.0
N_HEADS = 4
HALF_DIM = 64
V_DIM = 128
D_ATT = N_HEADS * V_DIM
D_QK = N_HEADS * 2 * HALF_DIM
ROPE_DIM = 16
ROPE_THETA = 500000.0
N_EXPERTS = 64
TOP_K = 8
N_GROUPS = 8
GROUP_SIZE = N_EXPERTS // N_GROUPS
TOPK_GROUPS = 4
D_EXPERT = 256
ROUTE_SCALE = 2.5
D_PLE = 256
EPS = 1e-6

LANES = 128
SUBLANES = 8
VMEM_LIMIT = 56 * 1024 * 1024

MIX_ROWS = 512
ATT_Q = 512
ROUTE_ROWS = 512
EXPERT_ROWS = 512
TAIL_ROWS = 256
SC_WINDOW = 128
PACK_W = 256
NEG_BIG = -1e30


def _rms(x, g):
    return x * lax.rsqrt(jnp.mean(x * x, axis=-1, keepdims=True) + EPS) * g


def _dot(a, b):
    return jnp.dot(a, b, preferred_element_type=F32)


def _pack_pair(lo, hi):
    lo_bits = lax.bitcast_convert_type(lo.astype(BF16).astype(F32), U32)
    hi_bits = lax.bitcast_convert_type(hi.astype(BF16).astype(F32), U32)
    return (lo_bits >> 16) | (hi_bits & jnp.uint32(0xFFFF0000))


def _unpack_pair(p):
    lo = lax.bitcast_convert_type(p << 16, F32)
    hi = lax.bitcast_convert_type(p & jnp.uint32(0xFFFF0000), F32)
    return lo, hi


def _pack_row(x):
    w = PACK_W
    return _pack_pair(x[:, 0:w], x[:, w:2 * w]), _pack_pair(x[:, 2 * w:3 * w], x[:, 3 * w:4 * w])


def _unpack_row(pa, pb):
    c0, c1 = _unpack_pair(pa)
    c2, c3 = _unpack_pair(pb)
    return jnp.concatenate([c0, c1, c2, c3], axis=1)


def _shift_rows(a, s, fill, row):
    n, c = a.shape
    if s % SUBLANES == 0:
        return jnp.concatenate([jnp.full((s, c), fill, a.dtype), a[:n - s]], axis=0)
    return jnp.where(row >= s, pltpu.roll(a, s, 0), fill)


def _mix_in_kernel(x_ref, pos_ref, invf_ref, pat_ref, lnm_ref, win_ref, cw_ref, cb_ref, wa_ref, ba_ref,
                   wi_ref, bi_ref, lam_ref, grec_ref,
                   yrec_ref, q_ref, k_ref, vt_ref, tail_ref, hcarry_ref):
    tm = x_ref.shape[0]

    @pl.when(pl.program_id(1) == 0)
    def _():
        tail_ref[...] = jnp.zeros_like(tail_ref)
        hcarry_ref[...] = jnp.zeros_like(hcarry_ref)

    hn = _rms(x_ref[...], lnm_ref[...]).astype(BF16)
    xr = _dot(hn, win_ref[:, 0:D_REC])

    row = lax.broadcasted_iota(I32, (tm, D_REC), 0)
    row8 = lax.broadcasted_iota(I32, (SUBLANES, D_REC), 0)
    tail = tail_ref[...]
    xc = cb_ref[...] + cw_ref[CONV_WIDTH - 1:CONV_WIDTH, :] * xr
    for j in range(1, CONV_WIDTH):
        rolled = pltpu.roll(xr, j, 0)
        head = jnp.where(row8 < j, pltpu.roll(tail, j, 0), rolled[:SUBLANES])
        shifted = jnp.concatenate([head, rolled[SUBLANES:]], axis=0)
        xc = xc + cw_ref[CONV_WIDTH - 1 - j:CONV_WIDTH - j, :] * shifted
    tail_ref[...] = xr[tm - SUBLANES:, :]

    xcb = xc.astype(BF16)
    half = D_REC // 2
    ra = jnp.concatenate([_dot(xcb[:, :half], wa_ref[0]), _dot(xcb[:, half:], wa_ref[1])], axis=1)
    ri = jnp.concatenate([_dot(xcb[:, :half], wi_ref[0]), _dot(xcb[:, half:], wi_ref[1])], axis=1)
    r = jax.nn.sigmoid(ra + ba_ref[...])
    ig = jax.nn.sigmoid(ri + bi_ref[...])
    lam = lam_ref[...]
    softplus_neg = jnp.maximum(-lam, 0.0) + jnp.log(1.0 + jnp.exp(-jnp.abs(lam)))
    log_a = -RG_C * r * softplus_neg
    a = jnp.exp(log_a)
    u = jnp.sqrt(1.0 - jnp.exp(2.0 * log_a)) * ig * xc

    s = 1
    while s < tm:
        u = u + a * _shift_rows(u, s, 0.0, row)
        a = a * _shift_rows(a, s, 1.0, row)
        s *= 2
    h = u + a * hcarry_ref[...]
    hcarry_ref[...] = h[tm - 1:tm, :]

    gate = _dot(hn, win_ref[:, D_REC:2 * D_REC])
    y = h * jax.nn.gelu(gate)
    yrec_ref[...] = _rms(y, grec_ref[...]).astype(BF16)

    ang = invf_ref[...] * pos_ref[...].astype(F32)
    cs = jnp.concatenate([jnp.cos(ang), jnp.sin(ang)], axis=0)
    tabs = lax.dot_general(cs, pat_ref[...], (((0,), (0,)), ((), ())),
                           precision=lax.Precision.HIGHEST, preferred_element_type=F32)
    lane64 = lax.broadcasted_iota(I32, (1, LANES), 1) % HALF_DIM
    cosf = tabs[:, 0:LANES] + (lane64 >= ROPE_DIM).astype(F32)
    sa, sb = tabs[:, LANES:2 * LANES], tabs[:, 2 * LANES:3 * LANES]
    scale = HALF_DIM ** -0.5 * math.log2(math.e)
    for name, out_ref, off, mul in (("q", q_ref, 2 * D_REC, scale), ("k", k_ref, 2 * D_REC + D_QK, 1.0)):
        for c in range(D_QK // LANES):
            zc = _dot(hn, win_ref[:, off + c * LANES: off + (c + 1) * LANES])
            rot = zc * cosf + pltpu.roll(zc, LANES - ROPE_DIM // 2, 1) * sa + pltpu.roll(zc, ROPE_DIM // 2, 1) * sb
            out_ref[:, c * LANES:(c + 1) * LANES] = (rot * mul).astype(BF16)
    vt_ref[0] = _dot(hn, win_ref[:, 2 * D_REC + 2 * D_QK:]).T.astype(BF16)


def _mix_in(x2, pos_row, inv_freq, rope_pat, ln_mix, w_in, conv_w, conv_b, wa_bd, b_a, wi_bd, b_i, rg_lambda,
            g_rec, batch, seq):
    tm = MIX_ROWS
    nt = seq // tm
    d_in = w_in.shape[1]
    row_map = lambda b, i: (b * nt + i, 0)
    fixed2 = lambda b, i: (0, 0)
    fixed3 = lambda b, i: (0, 0, 0)
    t = batch * seq
    out_shapes = (
        jax.ShapeDtypeStruct((t, D_REC), BF16),
        jax.ShapeDtypeStruct((t, D_QK), BF16),
        jax.ShapeDtypeStruct((t, D_QK), BF16),
        jax.ShapeDtypeStruct((t // tm, D_ATT, tm), BF16),
    )
    return pl.pallas_call(
        _mix_in_kernel,
        grid=(batch, nt),
        in_specs=[
            pl.BlockSpec((tm, D_MODEL), row_map),
            pl.BlockSpec((1, tm), lambda b, i: (0, b * nt + i)),
            pl.BlockSpec((ROPE_DIM // 2, 1), fixed2),
            pl.BlockSpec((ROPE_DIM, 3 * LANES), fixed2),
            pl.BlockSpec((1, D_MODEL), fixed2),
            pl.BlockSpec((D_MODEL, d_in), fixed2),
            pl.BlockSpec((CONV_WIDTH, D_REC), fixed2),
            pl.BlockSpec((1, D_REC), fixed2),
            pl.BlockSpec((2, D_REC // 2, D_REC // 2), fixed3),
            pl.BlockSpec((1, D_REC), fixed2),
            pl.BlockSpec((2, D_REC // 2, D_REC // 2), fixed3),
            pl.BlockSpec((1, D_REC), fixed2),
            pl.BlockSpec((1, D_REC), fixed2),
            pl.BlockSpec((1, D_REC), fixed2),
        ],
        out_specs=[
            pl.BlockSpec((tm, D_REC), row_map),
            pl.BlockSpec((tm, D_QK), row_map),
            pl.BlockSpec((tm, D_QK), row_map),
            pl.BlockSpec((1, D_ATT, tm), lambda b, i: (b * nt + i, 0, 0)),
        ],
        out_shape=out_shapes,
        scratch_shapes=[pltpu.VMEM((SUBLANES, D_REC), F32), pltpu.VMEM((1, D_REC), F32)],
        compiler_params=pltpu.CompilerParams(
            dimension_semantics=("arbitrary", "arbitrary"), vmem_limit_bytes=VMEM_LIMIT),
        name="mix_in",
    )(x2, pos_row, inv_freq, rope_pat, ln_mix, w_in, conv_w, conv_b, wa_bd, b_a, wi_bd, b_i, rg_lambda, g_rec)


def _attn_kernel(lq1_ref, lk1_ref, lq2_ref, lk2_ref, gsub_ref, bias_ref, q_ref, k_ref, vt_ref, o_ref,
                 m_ref, l_ref, acc_ref, aprev_ref, s0_ref, s1_ref, p0_ref, p1_ref, *, lam_init):
    tq = q_ref.shape[0]
    tk = vt_ref.shape[2]
    assert tq == tk, "the causal bias tile assumes the diagonal block is square"
    i = pl.program_id(2)

    qt = q_ref[...].astype(F32).T
    dim = lax.broadcasted_iota(I32, (LANES, tq), 0)
    qqt = jnp.concatenate([jnp.where(dim < HALF_DIM, qt, 0.0), jnp.where(dim >= HALF_DIM, qt, 0.0)],
                          axis=1).astype(BF16)

    n = (i * tq) // tk

    def scores(j):
        return _dot(k_ref[pl.ds(pl.multiple_of(j * tk, tk), tk), :], qqt)

    def softmax_update(s):
        m_prev = m_ref[...]
        m_new = jnp.maximum(m_prev, jnp.max(s, axis=0, keepdims=True))
        alpha = jnp.exp2(m_prev - m_new)
        p = jnp.exp2(s - m_new)
        l_ref[...] = alpha * l_ref[...] + jnp.sum(p, axis=0, keepdims=True)
        m_ref[...] = m_new
        return p.astype(BF16), alpha

    def pipe_step(j, s_cur, s_nxt, p_cur, p_prev):
        s_nxt[...] = scores(j + 1)
        p, alpha = softmax_update(s_cur[...])
        acc_ref[...] = aprev_ref[...] * acc_ref[...] + _dot(vt_ref[jnp.maximum(j - 1, 0)], p_prev[...])
        aprev_ref[...] = alpha
        p_cur[...] = p

    m_ref[...] = jnp.full_like(m_ref, NEG_BIG)
    l_ref[...] = jnp.zeros_like(l_ref)
    acc_ref[...] = jnp.zeros_like(acc_ref)
    aprev_ref[...] = jnp.ones_like(aprev_ref)
    odd = n % 2

    @pl.when(odd == 0)
    def _():
        p1_ref[...] = jnp.zeros_like(p1_ref)
        s0_ref[...] = scores(0)

    @pl.when(odd == 1)
    def _():
        p0_ref[...] = jnp.zeros_like(p0_ref)
        s1_ref[...] = scores(0)
        pipe_step(0, s1_ref, s0_ref, p1_ref, p0_ref)

    def pair(t, carry):
        j = 2 * t + odd
        pipe_step(j, s0_ref, s1_ref, p0_ref, p1_ref)
        pipe_step(j + 1, s1_ref, s0_ref, p1_ref, p0_ref)
        return carry

    lax.fori_loop(0, n // 2, pair, 0)

    p, alpha = softmax_update(s0_ref[...] + bias_ref[...])
    acc = aprev_ref[...] * acc_ref[...] + _dot(vt_ref[jnp.maximum(n - 1, 0)], p1_ref[...])
    acc = alpha * acc + _dot(vt_ref[n], p)

    lam = (jnp.exp(jnp.sum(lq1_ref[...] * lk1_ref[...], axis=-1, keepdims=True))
           - jnp.exp(jnp.sum(lq2_ref[...] * lk2_ref[...], axis=-1, keepdims=True)) + lam_init)
    o = acc / l_ref[...]
    o = o[:, :tq] - lam * o[:, tq:]
    o = o * lax.rsqrt(jnp.mean(o * o, axis=0, keepdims=True) + EPS) * gsub_ref[...]
    o_ref[...] = (o * (1.0 - lam_init)).T.astype(BF16)


def _attention(q, k, vt, lq1, lk1, lq2, lk2, g_sub_col, batch, seq, lam_init):
    tq = ATT_Q
    nq = seq // tq
    tk = vt.shape[2]
    nk = seq // tk
    vec = lambda b, h, i: (0, 0)
    visible = (lax.broadcasted_iota(I32, (tk, 2 * tq), 0) <= lax.broadcasted_iota(I32, (tk, 2 * tq), 1) % tq)
    bias = jnp.where(visible, 0.0, NEG_BIG).astype(F32)
    return pl.pallas_call(
        functools.partial(_attn_kernel, lam_init=lam_init),
        grid=(batch, N_HEADS, nq),
        in_specs=[
            pl.BlockSpec((1, HALF_DIM), vec),
            pl.BlockSpec((1, HALF_DIM), vec),
            pl.BlockSpec((1, HALF_DIM), vec),
            pl.BlockSpec((1, HALF_DIM), vec),
            pl.BlockSpec((V_DIM, 1), vec),
            pl.BlockSpec((tk, 2 * tq), vec),
            pl.BlockSpec((tq, LANES), lambda b, h, i: (b * nq + i, h)),
            pl.BlockSpec((seq, LANES), lambda b, h, i: (b, h)),
            pl.BlockSpec((nk, V_DIM, tk), lambda b, h, i: (b, h, 0)),
        ],
        out_specs=pl.BlockSpec((tq, V_DIM), lambda b, h, i: (b * nq + i, h)),
        out_shape=jax.ShapeDtypeStruct((batch * seq, D_ATT), BF16),
        scratch_shapes=[pltpu.VMEM((1, 2 * tq), F32), pltpu.VMEM((1, 2 * tq), F32),
                        pltpu.VMEM((V_DIM, 2 * tq), F32), pltpu.VMEM((1, 2 * tq), F32),
                        pltpu.VMEM((tk, 2 * tq), F32), pltpu.VMEM((tk, 2 * tq), F32),
                        pltpu.VMEM((tk, 2 * tq), BF16), pltpu.VMEM((tk, 2 * tq), BF16)],
        compiler_params=pltpu.CompilerParams(
            dimension_semantics=("arbitrary", "arbitrary", "arbitrary"), vmem_limit_bytes=VMEM_LIMIT),
        name="attention",
    )(lq1, lk1, lq2, lk2, g_sub_col, bias, q, k, vt)


def _sublane_total(x, op):
    return op(x, axis=0, keepdims=True)


def _route_kernel(x_ref, yrec_ref, yatt_ref, woa_ref, wob_ref, lnmoe_ref, wrt_ref, ebias_ref, tri_ref, low_ref,
                  h1_ref, hpa_ref, hpb_ref, ek_ref, wk_ref, rk_ref, cnt_ref, carry_ref):
    tm = x_ref.shape[0]
    e_n = N_EXPERTS

    @pl.when(pl.program_id(0) == 0)
    def _():
        carry_ref[...] = jnp.zeros_like(carry_ref)

    h1 = x_ref[...] + _dot(yrec_ref[...], woa_ref[...]) + _dot(yatt_ref[...], wob_ref[...])
    h1_ref[...] = h1
    hn = _rms(h1, lnmoe_ref[...])
    pa, pb = _pack_row(hn)
    hpa_ref[...] = pa
    hpb_ref[...] = pb

    logits = lax.dot_general(wrt_ref[...], hn, (((1,), (1,)), ((), ())),
                             precision=lax.Precision.HIGHEST, preferred_element_type=F32)
    scores = jax.nn.sigmoid(logits)
    sel = scores + ebias_ref[...]

    sel3 = sel.reshape(N_GROUPS, GROUP_SIZE, tm)
    idx3 = lax.broadcasted_iota(I32, (N_GROUPS, GROUP_SIZE, tm), 1)
    m1 = jnp.max(sel3, axis=1, keepdims=True)
    first = jnp.min(jnp.where(sel3 == m1, idx3, GROUP_SIZE), axis=1, keepdims=True)
    m2 = jnp.max(jnp.where(idx3 == first, -jnp.inf, sel3), axis=1, keepdims=True)
    gscore = (m1 + m2).reshape(N_GROUPS, tm)

    gidx = lax.broadcasted_iota(I32, (N_GROUPS, tm), 0)
    beaten = jnp.zeros((N_GROUPS, tm), I32)
    for g in range(N_GROUPS):
        other = gscore[g:g + 1, :]
        beats = (other > gscore) | ((other == gscore) & (g < gidx))
        beaten = beaten + beats.astype(I32)
    gkeep = beaten < TOPK_GROUPS
    keep = jnp.broadcast_to(gkeep.reshape(N_GROUPS, 1, tm), (N_GROUPS, GROUP_SIZE, tm)).reshape(e_n, tm)
    selm = jnp.where(keep, sel, -jnp.inf)

    eidx = lax.broadcasted_iota(I32, (e_n, tm), 0)
    beaten = jnp.zeros((e_n, tm), I32)
    for e in range(e_n):
        other = selm[e:e + 1, :]
        beats = (other > selm) | ((other == selm) & (e < eidx))
        beaten = beaten + beats.astype(I32)
    chosen = beaten < TOP_K
    chosen_f = chosen.astype(F32)

    wsum = _sublane_total(jnp.where(chosen, scores, 0.0), jnp.sum)
    weight = scores * (ROUTE_SCALE / wsum)

    chosen_b = chosen_f.astype(BF16)
    prefix = _dot(chosen_b, tri_ref[...])
    rank = prefix + carry_ref[...]
    carry_new = carry_ref[...] + jnp.sum(chosen_f, axis=1, keepdims=True)
    carry_ref[...] = carry_new
    cnt_ref[...] = carry_new.astype(I32)

    slot = _dot(low_ref[...], chosen_b)
    eidx_f = eidx.astype(F32)
    ek, wk, rk = [], [], []
    for kk in range(TOP_K):
        pick = chosen & (slot == float(kk))
        ek.append(_sublane_total(jnp.where(pick, eidx_f, 0.0), jnp.sum))
        wk.append(_sublane_total(jnp.where(pick, weight, 0.0), jnp.sum))
        rk.append(_sublane_total(jnp.where(pick, rank, 0.0), jnp.sum))
    ek_ref[...] = jnp.concatenate(ek, axis=0).astype(I32)
    wk_ref[...] = jnp.concatenate(wk, axis=0)
    rk_ref[...] = jnp.concatenate(rk, axis=0).astype(I32)


def _route(x2, y_rec, y_att, wo_a, wo_b, ln_moe, w_router_t, e_bias_col):
    t = x2.shape[0]
    tm = ROUTE_ROWS
    nt = t // tm
    row_map = lambda i: (i, 0)
    col_map = lambda i: (0, i)
    fixed = lambda i: (0, 0)
    tri = (lax.broadcasted_iota(I32, (tm, tm), 0) < lax.broadcasted_iota(I32, (tm, tm), 1)).astype(BF16)
    low = (lax.broadcasted_iota(I32, (N_EXPERTS, N_EXPERTS), 1)
           < lax.broadcasted_iota(I32, (N_EXPERTS, N_EXPERTS), 0)).astype(BF16)
    out_shapes = (
        jax.ShapeDtypeStruct((t, D_MODEL), F32),
        jax.ShapeDtypeStruct((t, PACK_W), U32),
        jax.ShapeDtypeStruct((t, PACK_W), U32),
        jax.ShapeDtypeStruct((TOP_K, t), I32),
        jax.ShapeDtypeStruct((TOP_K, t), F32),
        jax.ShapeDtypeStruct((TOP_K, t), I32),
        jax.ShapeDtypeStruct((N_EXPERTS, 1), I32),
    )
    return pl.pallas_call(
        _route_kernel,
        grid=(nt,),
        in_specs=[
            pl.BlockSpec((tm, D_MODEL), row_map),
            pl.BlockSpec((tm, D_REC), row_map),
            pl.BlockSpec((tm, D_ATT), row_map),
            pl.BlockSpec((D_REC, D_MODEL), fixed),
            pl.BlockSpec((D_ATT, D_MODEL), fixed),
            pl.BlockSpec((1, D_MODEL), fixed),
            pl.BlockSpec((N_EXPERTS, D_MODEL), fixed),
            pl.BlockSpec((N_EXPERTS, 1), fixed),
            pl.BlockSpec((tm, tm), fixed),
            pl.BlockSpec((N_EXPERTS, N_EXPERTS), fixed),
        ],
        out_specs=[
            pl.BlockSpec((tm, D_MODEL), row_map),
            pl.BlockSpec((tm, PACK_W), row_map),
            pl.BlockSpec((tm, PACK_W), row_map),
            pl.BlockSpec((TOP_K, tm), col_map),
            pl.BlockSpec((TOP_K, tm), col_map),
            pl.BlockSpec((TOP_K, tm), col_map),
            pl.BlockSpec((N_EXPERTS, 1), fixed),
        ],
        out_shape=out_shapes,
        scratch_shapes=[pltpu.VMEM((N_EXPERTS, 1), F32)],
        compiler_params=pltpu.CompilerParams(dimension_semantics=("arbitrary",), vmem_limit_bytes=VMEM_LIMIT),
        name="route",
    )(x2, y_rec, y_att, wo_a, wo_b, ln_moe, w_router_t, e_bias_col, tri, low)


def _plan_kernel(pad_start_ref, ek_ref, rk_ref, dest_ref):
    ek = ek_ref[...]

    def add_expert(e, base):
        return jnp.where(ek == e, pad_start_ref[e], base)

    dest_ref[...] = rk_ref[...] + lax.fori_loop(0, N_EXPERTS, add_expert, jnp.zeros_like(ek))


def _plan(pad_start, ek, rk):
    kk, t = ek.shape
    tl = min(t, 2048)
    col_map = lambda i, ps: (0, i)
    grid_spec = pltpu.PrefetchScalarGridSpec(
        num_scalar_prefetch=1,
        grid=(t // tl,),
        in_specs=[pl.BlockSpec((kk, tl), col_map), pl.BlockSpec((kk, tl), col_map)],
        out_specs=pl.BlockSpec((kk, tl), col_map),
    )
    return pl.pallas_call(
        _plan_kernel,
        grid_spec=grid_spec,
        out_shape=jax.ShapeDtypeStruct((kk, t), I32),
        compiler_params=pltpu.CompilerParams(dimension_semantics=("arbitrary",)),
        name="plan",
    )(pad_start, ek, rk)


def _sc_mesh():
    return plsc.VectorSubcoreMesh(core_axis_name="core", subcore_axis_name="subcore")


def _sc_dispatch(rows, dest, n_out):
    t, w = rows.shape
    kk = dest.shape[0]

    @pl.kernel(out_type=jax.ShapeDtypeStruct((n_out, w), rows.dtype), mesh=_sc_mesh(), scratch_types=[])
    def kern(x_hbm, i_hbm, o_hbm):
        def body(x_vmem, i_vmem):
            for k in range(kk):
                pltpu.sync_copy(x_vmem, o_hbm.at[i_vmem.at[k]])

        pltpu.emit_pipeline(
            body,
            grid=(t // SC_WINDOW,),
            in_specs=[pl.BlockSpec((SC_WINDOW, w), lambda i: (i, 0)),
                      pl.BlockSpec((kk, SC_WINDOW), lambda i: (0, i))],
            out_specs=[],
            core_axis_name=("core", "subcore"),
            dimension_semantics=(pltpu.PARALLEL,),
        )(x_hbm, i_hbm)

    return kern(rows, dest)


def _sc_combine(rows, dest):
    kk, t = dest.shape
    w = rows.shape[1]
    flat = dest.reshape(1, kk * t)

    @pl.kernel(out_type=jax.ShapeDtypeStruct((kk * t, w), rows.dtype), mesh=_sc_mesh(), scratch_types=[])
    def kern(y_hbm, i_hbm, o_hbm):
        def body(i_vmem, o_vmem):
            pltpu.sync_copy(y_hbm.at[i_vmem.at[0]], o_vmem)

        pltpu.emit_pipeline(
            body,
            grid=(kk * t // SC_WINDOW,),
            in_specs=[pl.BlockSpec((1, SC_WINDOW), lambda i: (0, i))],
            out_specs=[pl.BlockSpec((SC_WINDOW, w), lambda i: (i, 0))],
            core_axis_name=("core", "subcore"),
            dimension_semantics=(pltpu.PARALLEL,),
        )(i_hbm, o_hbm)

    return kern(rows, flat).reshape(kk, t, w)


def _experts_kernel(blk_expert_ref, n_used_ref, first_ref, slot_ref, next_ref, xa_ref, xb_ref,
                    w1_hbm, w3_hbm, w2_hbm, ya_ref, yb_ref,
                    w1f_ref, w3f_ref, w2f_ref, w1b_ref, w3b_ref, w2b_ref, sem):
    b = pl.program_id(0)

    def weight_copies(e, s):
        return (pltpu.make_async_copy(w1_hbm.at[e], w1f_ref.at[s], sem.at[s, 0]),
                pltpu.make_async_copy(w3_hbm.at[e], w3f_ref.at[s], sem.at[s, 1]),
                pltpu.make_async_copy(w2_hbm.at[e], w2f_ref.at[s], sem.at[s, 2]))

    @pl.when(b == 0)
    def _():
        for copy in weight_copies(blk_expert_ref[0], 0):
            copy.start()

    @pl.when(first_ref[b] == 1)
    def _():
        s = slot_ref[b]
        for copy in weight_copies(blk_expert_ref[b], s):
            copy.wait()

        @pl.when(next_ref[b] >= 0)
        def _():
            for copy in weight_copies(next_ref[b], 1 - s):
                copy.start()

        w1b_ref[...] = w1f_ref[s].astype(BF16)
        w3b_ref[...] = w3f_ref[s].astype(BF16)
        w2b_ref[...] = w2f_ref[s].astype(BF16)

    @pl.when(b < n_used_ref[0])
    def _():
        x = _unpack_row(xa_ref[...], xb_ref[...]).astype(BF16)
        a = _dot(x, w1b_ref[...])
        g = _dot(x, w3b_ref[...])
        hmid = (jax.nn.silu(a) * g).astype(BF16)
        y = _dot(hmid, w2b_ref[...])
        pa, pb = _pack_row(y)
        ya_ref[...] = pa
        yb_ref[...] = pb


def _experts(xa, xb, w1, w3, w2, blk_expert, n_used, seg_first, seg_slot, seg_next):
    p = xa.shape[0]
    m = EXPERT_ROWS
    nblk = p // m
    row_map = lambda b, *_: (b, 0)
    hbm = pl.BlockSpec(memory_space=pl.ANY)
    grid_spec = pltpu.PrefetchScalarGridSpec(
        num_scalar_prefetch=5,
        grid=(nblk,),
        in_specs=[pl.BlockSpec((m, PACK_W), row_map), pl.BlockSpec((m, PACK_W), row_map), hbm, hbm, hbm],
        out_specs=[pl.BlockSpec((m, PACK_W), row_map), pl.BlockSpec((m, PACK_W), row_map)],
        scratch_shapes=[
            pltpu.VMEM((2, D_MODEL, D_EXPERT), F32), pltpu.VMEM((2, D_MODEL, D_EXPERT), F32),
            pltpu.VMEM((2, D_EXPERT, D_MODEL), F32),
            pltpu.VMEM((D_MODEL, D_EXPERT), BF16), pltpu.VMEM((D_MODEL, D_EXPERT), BF16),
            pltpu.VMEM((D_EXPERT, D_MODEL), BF16),
            pltpu.SemaphoreType.DMA((2, 3)),
        ],
    )
    return pl.pallas_call(
        _experts_kernel,
        grid_spec=grid_spec,
        out_shape=(jax.ShapeDtypeStruct((p, PACK_W), U32), jax.ShapeDtypeStruct((p, PACK_W), U32)),
        compiler_params=pltpu.CompilerParams(dimension_semantics=("arbitrary",), vmem_limit_bytes=VMEM_LIMIT),
        name="experts",
    )(blk_expert, n_used, seg_first, seg_slot, seg_next, xa, xb, w1, w3, w2)


def _tail_kernel(h1_ref, ga_ref, gb_ref, wk_ref, p_ref, lnmoe_ref, ws1_ref, ws3_ref, ws2_ref, lnple_ref,
                 wpg_ref, wpp_ref, lnf_ref, o_ref):
    h1 = h1_ref[...]
    hn = _rms(h1, lnmoe_ref[...]).astype(BF16)
    shared = _dot((jax.nn.silu(_dot(hn, ws1_ref[...])) * _dot(hn, ws3_ref[...])).astype(BF16), ws2_ref[...])
    wk = wk_ref[...]
    routed = jnp.zeros_like(h1)
    for kk in range(TOP_K):
        routed = routed + wk[:, kk:kk + 1] * _unpack_row(ga_ref[kk], gb_ref[kk])
    h2 = h1 + routed + shared
    gate = jax.nn.sigmoid(_dot(_rms(h2, lnple_ref[...]).astype(BF16), wpg_ref[...]))
    h3 = h2 + gate * _dot(p_ref[...].astype(BF16), wpp_ref[...])
    o_ref[...] = _rms(h3, lnf_ref[...])


def _tail(h1, ga, gb, wk_t, p2, ln_moe, ws1, ws3, ws2, ln_ple, w_pg, w_pp, ln_f):
    t = h1.shape[0]
    tm = TAIL_ROWS
    row_map = lambda i: (i, 0)
    fixed = lambda i: (0, 0)
    g_map = lambda i: (0, i, 0)
    d_sh = ws1.shape[1]
    return pl.pallas_call(
        _tail_kernel,
        grid=(t // tm,),
        in_specs=[
            pl.BlockSpec((tm, D_MODEL), row_map),
            pl.BlockSpec((TOP_K, tm, PACK_W), g_map),
            pl.BlockSpec((TOP_K, tm, PACK_W), g_map),
            pl.BlockSpec((tm, TOP_K), row_map),
            pl.BlockSpec((tm, D_PLE), row_map),
            pl.BlockSpec((1, D_MODEL), fixed),
            pl.BlockSpec((D_MODEL, d_sh), fixed),
            pl.BlockSpec((D_MODEL, d_sh), fixed),
            pl.BlockSpec((d_sh, D_MODEL), fixed),
            pl.BlockSpec((1, D_MODEL), fixed),
            pl.BlockSpec((D_MODEL, D_MODEL), fixed),
            pl.BlockSpec((D_PLE, D_MODEL), fixed),
            pl.BlockSpec((1, D_MODEL), fixed),
        ],
        out_specs=pl.BlockSpec((tm, D_MODEL), row_map),
        out_shape=jax.ShapeDtypeStruct((t, D_MODEL), F32),
        compiler_params=pltpu.CompilerParams(dimension_semantics=("arbitrary",), vmem_limit_bytes=VMEM_LIMIT),
        name="tail",
    )(h1, ga, gb, wk_t, p2, ln_moe, ws1, ws3, ws2, ln_ple, w_pg, w_pp, ln_f)


def _rope_constants():
    half = ROPE_DIM // 2
    inv_freq = (ROPE_THETA ** (-jnp.arange(0, ROPE_DIM, 2, dtype=F32) / ROPE_DIM)).reshape(half, 1)
    f = lax.broadcasted_iota(I32, (ROPE_DIM, LANES), 0)
    l64 = lax.broadcasted_iota(I32, (ROPE_DIM, LANES), 1) % HALF_DIM
    cos_pat = ((f < half) & (l64 < ROPE_DIM) & (l64 % half == f)).astype(F32)
    sa_pat = -((f >= half) & (l64 < half) & (l64 == f - half)).astype(F32)
    sb_pat = ((f >= half) & (l64 >= half) & (l64 < ROPE_DIM) & (l64 - half == f - half)).astype(F32)
    return inv_freq, jnp.concatenate([cos_pat, sa_pat, sb_pat], axis=1)


def _block_diag_tiles(w):
    nb, bd, _ = w.shape
    per = nb // 2
    tiles = []
    for tix in range(2):
        rows = []
        for j in range(per):
            rows.append(jnp.concatenate(
                [w[tix * per + j] if c == j else jnp.zeros((bd, bd), w.dtype) for c in range(per)], axis=1))
        tiles.append(jnp.concatenate(rows, axis=0))
    return jnp.stack(tiles).astype(BF16)


def _layer(h, p_l, positions, lam_init, ln_mix, w_in, conv_w, conv_b, w_a, b_a, w_i, b_i, rg_lambda, g_rec,
           lq1, lk1, lq2, lk2, g_sub, w_out, ln_moe, w_router, e_bias, w1, w3, w2, ws1, ws3, ws2,
           ln_ple, w_ple_gate, w_ple_proj, ln_out):
    batch, seq, _ = h.shape
    t = batch * seq
    x2 = h.reshape(t, D_MODEL)
    row = lambda a: a.reshape(1, -1)
    inv_freq, rope_pat = _rope_constants()

    y_rec, q, k, vt = _mix_in(
        x2, positions.reshape(1, t), inv_freq, rope_pat, row(ln_mix), w_in.astype(BF16), conv_w, row(conv_b),
        _block_diag_tiles(w_a), row(b_a), _block_diag_tiles(w_i), row(b_i), row(rg_lambda), row(g_rec),
        batch, seq)
    y_att = _attention(q, k, vt, row(lq1), row(lk1), row(lq2), row(lk2), g_sub.reshape(-1, 1), batch, seq,
                       lam_init)

    w_out_b = w_out.astype(BF16)
    h1, hpa, hpb, ek, wk, rk, counts = _route(
        x2, y_rec, y_att, w_out_b[:D_REC], w_out_b[D_REC:], row(ln_moe), w_router.T, e_bias.reshape(-1, 1))

    m = EXPERT_ROWS
    counts = counts.reshape(-1)
    padded = (counts + m - 1) // m * m
    pad_end = jnp.cumsum(padded)
    pad_start = pad_end - padded
    n_rows = t * TOP_K + N_EXPERTS * m
    nblk = n_rows // m
    n_used = (pad_end[-1] // m).astype(I32).reshape(1)
    blk_row = jnp.minimum(jnp.arange(nblk, dtype=I32), n_used[0] - 1) * m
    blk_expert = jnp.sum((pad_end[None, :] <= blk_row[:, None]).astype(I32), axis=1)
    blk = jnp.arange(nblk, dtype=I32)
    prev_expert = jnp.concatenate([jnp.full((1,), -1, I32), blk_expert[:-1]])
    seg_first = ((blk < n_used[0]) & (blk_expert != prev_expert)).astype(I32)
    seg_slot = ((jnp.cumsum(seg_first) - 1) % 2).astype(I32)
    eid = jnp.arange(N_EXPERTS, dtype=I32)
    later = (padded[None, :] > 0) & (eid[None, :] > eid[:, None])
    next_expert = jnp.min(jnp.where(later, eid[None, :], N_EXPERTS), axis=1)
    next_expert = jnp.where(next_expert == N_EXPERTS, -1, next_expert).astype(I32)
    seg_next = jnp.sum(jnp.where(blk_expert[:, None] == eid[None, :], next_expert[None, :], 0), axis=1)
    dest = _plan(pad_start.astype(I32), ek, rk)

    xa = _sc_dispatch(hpa, dest, n_rows)
    xb = _sc_dispatch(hpb, dest, n_rows)
    ya, yb = _experts(xa, xb, w1, w3, w2, blk_expert, n_used, seg_first, seg_slot, seg_next)
    ga = _sc_combine(ya, dest)
    gb = _sc_combine(yb, dest)

    out = _tail(h1, ga, gb, wk.T, p_l.reshape(t, D_PLE), row(ln_moe), ws1.astype(BF16), ws3.astype(BF16),
                ws2.astype(BF16), row(ln_ple), w_ple_gate.astype(BF16), w_ple_proj.astype(BF16), row(ln_out))
    return out.reshape(batch, seq, D_MODEL)


def kernel(x, p, positions, ln_mix, w_in, conv_w, conv_b, w_a, b_a, w_i, b_i, rg_lambda, g_rec, lq1, lk1, lq2,
           lk2, g_sub, w_out, ln_moe, w_router, e_bias, w1, w3, w2, ws1, ws3, ws2, ln_ple, w_ple_gate,
           w_ple_proj, ln_f):
    depth = w_in.shape[0]
    assert depth == 1, "the fused tail applies the final norm; one layer supported"
    lam_init = 0.8 - 0.6 * math.exp(-0.3 * 0)
    return _layer(x, p[0], positions, lam_init, ln_mix[0], w_in[0], conv_w[0], conv_b[0], w_a[0], b_a[0], w_i[0],
                  b_i[0], rg_lambda[0], g_rec[0], lq1[0], lk1[0], lq2[0], lk2[0], g_sub[0], w_out[0], ln_moe[0],
                  w_router[0], e_bias[0], w1[0], w3[0], w2[0], ws1[0], ws3[0], ws2[0], ln_ple[0], w_ple_gate[0],
                  w_ple_proj[0], ln_f)
```

```python
import functools
import math

import jax
import jax.numpy as jnp
from jax import lax
from jax.experimental import pallas as pl
from jax.experimental.pallas import tpu as pltpu
from jax.experimental.pallas import tpu_sc as plsc

F32 = jnp.float32
BF16 = jnp.bfloat16
U32 = jnp.uint32
I32 = jnp.int32

D_MODEL = 1024
D_REC = 512
REC_BLOCKS = 8
CONV_WIDTH = 4
RG_C = 8.0
N_HEADS = 4
HALF_DIM = 64
V_DIM = 128
D_ATT = N_HEADS * V_DIM
D_QK = N_HEADS * 2 * HALF_DIM
ROPE_DIM = 16
ROPE_THETA = 500000.0
N_EXPERTS = 64
TOP_K = 8
N_GROUPS = 8
GROUP_SIZE = N_EXPERTS // N_GROUPS
TOPK_GROUPS = 4
D_EXPERT = 256
ROUTE_SCALE = 2.5
D_PLE = 256
EPS = 1e-6

LANES = 128
SUBLANES = 8
VMEM_LIMIT = 56 * 1024 * 1024

MIX_ROWS = 512
ATT_Q = 512
ATT_TAIL_LANES = 256
ONES_ROWS = 16
V_EXT = V_DIM + ONES_ROWS
ROUTE_ROWS = 512
EXPERT_ROWS = 512
MOE_PARTS = 2
TAIL_ROWS = 256
SC_WINDOW = 128
PACK_W = 256
NEG_BIG = -1e30


def _rms(x, g):
    return x * lax.rsqrt(jnp.mean(x * x, axis=-1, keepdims=True) + EPS) * g


def _dot(a, b):
    return jnp.dot(a, b, preferred_element_type=F32)


def _pack_pair(lo, hi):
    lo_bits = lax.bitcast_convert_type(lo.astype(BF16).astype(F32), U32)
    hi_bits = lax.bitcast_convert_type(hi.astype(BF16).astype(F32), U32)
    return (lo_bits >> 16) | (hi_bits & jnp.uint32(0xFFFF0000))


def _unpack_pair(p):
    lo = lax.bitcast_convert_type(p << 16, F32)
    hi = lax.bitcast_convert_type(p & jnp.uint32(0xFFFF0000), F32)
    return lo, hi


def _pack_row(x):
    w = PACK_W
    return _pack_pair(x[:, 0:w], x[:, w:2 * w]), _pack_pair(x[:, 2 * w:3 * w], x[:, 3 * w:4 * w])


def _unpack_row(pa, pb):
    c0, c1 = _unpack_pair(pa)
    c2, c3 = _unpack_pair(pb)
    return jnp.concatenate([c0, c1, c2, c3], axis=1)


def _shift_rows(a, s, fill, row):
    n, c = a.shape
    if s % SUBLANES == 0:
        return jnp.concatenate([jnp.full((s, c), fill, a.dtype), a[:n - s]], axis=0)
    return jnp.where(row >= s, pltpu.roll(a, s, 0), fill)


def _mix_in_kernel(x_ref, pos_ref, invf_ref, pat_ref, lnm_ref, win_ref, cw_ref, cb_ref, wa_ref, ba_ref,
                   wi_ref, bi_ref, lam_ref, grec_ref,
                   yrec_ref, q_ref, k_ref, vt_ref, tail_ref, hcarry_ref):
    tm = x_ref.shape[0]

    @pl.when(pl.program_id(1) == 0)
    def _():
        tail_ref[...] = jnp.zeros_like(tail_ref)
        hcarry_ref[...] = jnp.zeros_like(hcarry_ref)

    hn = _rms(x_ref[...], lnm_ref[...]).astype(BF16)
    xr = _dot(hn, win_ref[:, 0:D_REC])

    row = lax.broadcasted_iota(I32, (tm, D_REC), 0)
    row8 = lax.broadcasted_iota(I32, (SUBLANES, D_REC), 0)
    tail = tail_ref[...]
    xc = cb_ref[...] + cw_ref[CONV_WIDTH - 1:CONV_WIDTH, :] * xr
    for j in range(1, CONV_WIDTH):
        rolled = pltpu.roll(xr, j, 0)
        head = jnp.where(row8 < j, pltpu.roll(tail, j, 0), rolled[:SUBLANES])
        shifted = jnp.concatenate([head, rolled[SUBLANES:]], axis=0)
        xc = xc + cw_ref[CONV_WIDTH - 1 - j:CONV_WIDTH - j, :] * shifted
    tail_ref[...] = xr[tm - SUBLANES:, :]

    xcb = xc.astype(BF16)
    half = D_REC // 2
    ra = jnp.concatenate([_dot(xcb[:, :half], wa_ref[0]), _dot(xcb[:, half:], wa_ref[1])], axis=1)
    ri = jnp.concatenate([_dot(xcb[:, :half], wi_ref[0]), _dot(xcb[:, half:], wi_ref[1])], axis=1)
    r = jax.nn.sigmoid(ra + ba_ref[...])
    ig = jax.nn.sigmoid(ri + bi_ref[...])
    lam = lam_ref[...]
    softplus_neg = jnp.maximum(-lam, 0.0) + jnp.log(1.0 + jnp.exp(-jnp.abs(lam)))
    log_a = -RG_C * r * softplus_neg
    a = jnp.exp(log_a)
    u = jnp.sqrt(1.0 - jnp.exp(2.0 * log_a)) * ig * xc

    s = 1
    while s < tm:
        u = u + a * _shift_rows(u, s, 0.0, row)
        a = a * _shift_rows(a, s, 1.0, row)
        s *= 2
    h = u + a * hcarry_ref[...]
    hcarry_ref[...] = h[tm - 1:tm, :]

    gate = _dot(hn, win_ref[:, D_REC:2 * D_REC])
    y = h * jax.nn.gelu(gate)
    yrec_ref[...] = _rms(y, grec_ref[...]).astype(BF16)

    ang = invf_ref[...] * pos_ref[...].astype(F32)
    cs = jnp.concatenate([jnp.cos(ang), jnp.sin(ang)], axis=0)
    tabs = lax.dot_general(cs, pat_ref[...], (((0,), (0,)), ((), ())),
                           precision=lax.Precision.HIGHEST, preferred_element_type=F32)
    lane64 = lax.broadcasted_iota(I32, (1, LANES), 1) % HALF_DIM
    cosf = tabs[:, 0:LANES] + (lane64 >= ROPE_DIM).astype(F32)
    sa, sb = tabs[:, LANES:2 * LANES], tabs[:, 2 * LANES:3 * LANES]
    scale = HALF_DIM ** -0.5 * math.log2(math.e)
    for name, out_ref, off, mul in (("q", q_ref, 2 * D_REC, scale), ("k", k_ref, 2 * D_REC + D_QK, 1.0)):
        for c in range(D_QK // LANES):
            zc = _dot(hn, win_ref[:, off + c * LANES: off + (c + 1) * LANES])
            rot = zc * cosf + pltpu.roll(zc, LANES - ROPE_DIM // 2, 1) * sa + pltpu.roll(zc, ROPE_DIM // 2, 1) * sb
            out_ref[:, c * LANES:(c + 1) * LANES] = (rot * mul).astype(BF16)
    vt = _dot(hn, win_ref[:, 2 * D_REC + 2 * D_QK:]).T.astype(BF16)
    for hd in range(N_HEADS):
        vt_ref[0, hd * V_EXT:hd * V_EXT + V_DIM, :] = vt[hd * V_DIM:(hd + 1) * V_DIM]
        vt_ref[0, hd * V_EXT + V_DIM:(hd + 1) * V_EXT, :] = jnp.ones((ONES_ROWS, tm), BF16)


def _mix_in(x2, pos_row, inv_freq, rope_pat, ln_mix, w_in, conv_w, conv_b, wa_bd, b_a, wi_bd, b_i, rg_lambda,
            g_rec, batch, seq):
    tm = MIX_ROWS
    nt = seq // tm
    d_in = w_in.shape[1]
    row_map = lambda b, i: (b * nt + i, 0)
    fixed2 = lambda b, i: (0, 0)
    fixed3 = lambda b, i: (0, 0, 0)
    t = batch * seq
    out_shapes = (
        jax.ShapeDtypeStruct((t, D_REC), BF16),
        jax.ShapeDtypeStruct((t, D_QK), BF16),
        jax.ShapeDtypeStruct((t, D_QK), BF16),
        jax.ShapeDtypeStruct((t // tm, N_HEADS * V_EXT, tm), BF16),
    )
    return pl.pallas_call(
        _mix_in_kernel,
        grid=(batch, nt),
        in_specs=[
            pl.BlockSpec((tm, D_MODEL), row_map),
            pl.BlockSpec((1, tm), lambda b, i: (0, b * nt + i)),
            pl.BlockSpec((ROPE_DIM // 2, 1), fixed2),
            pl.BlockSpec((ROPE_DIM, 3 * LANES), fixed2),
            pl.BlockSpec((1, D_MODEL), fixed2),
            pl.BlockSpec((D_MODEL, d_in), fixed2),
            pl.BlockSpec((CONV_WIDTH, D_REC), fixed2),
            pl.BlockSpec((1, D_REC), fixed2),
            pl.BlockSpec((2, D_REC // 2, D_REC // 2), fixed3),
            pl.BlockSpec((1, D_REC), fixed2),
            pl.BlockSpec((2, D_REC // 2, D_REC // 2), fixed3),
            pl.BlockSpec((1, D_REC), fixed2),
            pl.BlockSpec((1, D_REC), fixed2),
            pl.BlockSpec((1, D_REC), fixed2),
        ],
        out_specs=[
            pl.BlockSpec((tm, D_REC), row_map),
            pl.BlockSpec((tm, D_QK), row_map),
            pl.BlockSpec((tm, D_QK), row_map),
            pl.BlockSpec((1, N_HEADS * V_EXT, tm), lambda b, i: (b * nt + i, 0, 0)),
        ],
        out_shape=out_shapes,
        scratch_shapes=[pltpu.VMEM((SUBLANES, D_REC), F32), pltpu.VMEM((1, D_REC), F32)],
        compiler_params=pltpu.CompilerParams(
            dimension_semantics=("arbitrary", "arbitrary"), vmem_limit_bytes=VMEM_LIMIT),
        name="mix_in",
    )(x2, pos_row, inv_freq, rope_pat, ln_mix, w_in, conv_w, conv_b, wa_bd, b_a, wi_bd, b_i, rg_lambda, g_rec)


def _attn_kernel(lq1_ref, lk1_ref, lq2_ref, lk2_ref, gsub_ref, bias_ref, q_ref, k_ref, vt_ref, o_ref,
                 m_ref, acc_ref, aprev_ref, s0_ref, s1_ref, mb0_ref, mb1_ref, p0_ref, p1_ref, *, lam_init):
    tq = q_ref.shape[0]
    tk = vt_ref.shape[2]
    assert tq == tk, "the causal bias tile assumes the diagonal block is square"
    i = pl.program_id(2)

    qt = q_ref[...].astype(F32).T
    dim = lax.broadcasted_iota(I32, (LANES, tq), 0)
    qqt = jnp.concatenate([jnp.where(dim < HALF_DIM, qt, 0.0), jnp.where(dim >= HALF_DIM, qt, 0.0)],
                          axis=1).astype(BF16)

    n = (i * tq) // tk

    def scores(j):
        return _dot(k_ref[pl.ds(pl.multiple_of(j * tk, tk), tk), :], qqt)

    def probabilities(s, m_prev):
        m_new = jnp.maximum(m_prev, jnp.max(s, axis=0, keepdims=True))
        alpha = jnp.exp2(m_prev - m_new)
        p = jnp.exp2((s - m_new).astype(BF16))
        return p, alpha, m_new

    def store_scores(j, s_buf, mb_buf):
        s = scores(j)
        s_buf[...] = s
        mb_buf[...] = jnp.max(s, axis=0, keepdims=True)

    def pipe_step(j, cur, nxt, p_cur, p_prev):
        store_scores(j + 1, *nxt)
        s_cur, mb_cur = cur
        m_prev = m_ref[...]
        m_new = jnp.maximum(m_prev, mb_cur[...])
        p_cur[...] = jnp.exp2((s_cur[...] - m_new).astype(BF16))
        m_ref[...] = m_new
        acc_ref[...] = aprev_ref[...] * acc_ref[...] + _dot(vt_ref[jnp.maximum(j - 1, 0)], p_prev[...])
        aprev_ref[...] = jnp.exp2(m_prev - m_new)

    buf0, buf1 = (s0_ref, mb0_ref), (s1_ref, mb1_ref)
    m_ref[...] = jnp.full_like(m_ref, NEG_BIG)
    acc_ref[...] = jnp.zeros_like(acc_ref)
    aprev_ref[...] = jnp.ones_like(aprev_ref)
    odd = n % 2

    @pl.when(odd == 0)
    def _():
        p1_ref[...] = jnp.zeros_like(p1_ref)
        store_scores(0, *buf0)

    @pl.when(odd == 1)
    def _():
        p0_ref[...] = jnp.zeros_like(p0_ref)
        store_scores(0, *buf1)
        pipe_step(0, buf1, buf0, p1_ref, p0_ref)

    def pair(t, carry):
        j = 2 * t + odd
        pipe_step(j, buf0, buf1, p0_ref, p1_ref)
        pipe_step(j + 1, buf1, buf0, p1_ref, p0_ref)
        return carry

    lax.fori_loop(0, n // 2, pair, 0)

    parts = []
    for c0 in range(0, 2 * tq, ATT_TAIL_LANES):
        cols = slice(c0, c0 + ATT_TAIL_LANES)
        p, alpha, _ = probabilities(s0_ref[:, cols] + bias_ref[:, cols], m_ref[:, cols])
        part = aprev_ref[:, cols] * acc_ref[:, cols] + _dot(vt_ref[jnp.maximum(n - 1, 0)], p1_ref[:, cols])
        parts.append(alpha * part + _dot(vt_ref[n], p))
    acc = jnp.concatenate(parts, axis=1)

    lam = (jnp.exp(jnp.sum(lq1_ref[...] * lk1_ref[...], axis=-1, keepdims=True))
           - jnp.exp(jnp.sum(lq2_ref[...] * lk2_ref[...], axis=-1, keepdims=True)) + lam_init)
    o = acc[:V_DIM] / acc[V_DIM:V_DIM + 1]
    o = o[:, :tq] - lam * o[:, tq:]
    o = o * lax.rsqrt(jnp.mean(o * o, axis=0, keepdims=True) + EPS) * gsub_ref[...]
    o_ref[...] = (o * (1.0 - lam_init)).T.astype(BF16)


def _attention(q, k, vt, lq1, lk1, lq2, lk2, g_sub_col, batch, seq, lam_init):
    tq = ATT_Q
    nq = seq // tq
    tk = vt.shape[2]
    nk = seq // tk
    vec = lambda b, h, i: (0, 0)
    visible = (lax.broadcasted_iota(I32, (tk, 2 * tq), 0) <= lax.broadcasted_iota(I32, (tk, 2 * tq), 1) % tq)
    bias = jnp.where(visible, 0.0, NEG_BIG).astype(F32)
    return pl.pallas_call(
        functools.partial(_attn_kernel, lam_init=lam_init),
        grid=(batch, N_HEADS, nq),
        in_specs=[
            pl.BlockSpec((1, HALF_DIM), vec),
            pl.BlockSpec((1, HALF_DIM), vec),
            pl.BlockSpec((1, HALF_DIM), vec),
            pl.BlockSpec((1, HALF_DIM), vec),
            pl.BlockSpec((V_DIM, 1), vec),
            pl.BlockSpec((tk, 2 * tq), vec),
            pl.BlockSpec((tq, LANES), lambda b, h, i: (b * nq + i, h)),
            pl.BlockSpec((seq, LANES), lambda b, h, i: (b, h)),
            pl.BlockSpec((nk, V_EXT, tk), lambda b, h, i: (b, h, 0)),
        ],
        out_specs=pl.BlockSpec((tq, V_DIM), lambda b, h, i: (b * nq + i, h)),
        out_shape=jax.ShapeDtypeStruct((batch * seq, D_ATT), BF16),
        scratch_shapes=[pltpu.VMEM((1, 2 * tq), F32),
                        pltpu.VMEM((V_EXT, 2 * tq), F32), pltpu.VMEM((1, 2 * tq), F32),
                        pltpu.VMEM((tk, 2 * tq), F32), pltpu.VMEM((tk, 2 * tq), F32),
                        pltpu.VMEM((1, 2 * tq), F32), pltpu.VMEM((1, 2 * tq), F32),
                        pltpu.VMEM((tk, 2 * tq), BF16), pltpu.VMEM((tk, 2 * tq), BF16)],
        compiler_params=pltpu.CompilerParams(
            dimension_semantics=("arbitrary", "arbitrary", "arbitrary"), vmem_limit_bytes=VMEM_LIMIT),
        name="attention",
    )(lq1, lk1, lq2, lk2, g_sub_col, bias, q, k, vt)


def _sublane_total(x, op):
    return op(x, axis=0, keepdims=True)


def _route_kernel(x_ref, yrec_ref, yatt_ref, woa_ref, wob_ref, lnmoe_ref, wrt_ref, ebias_ref, tri_ref, low_ref,
                  h1_ref, hpa_ref, hpb_ref, ek_ref, wk_ref, rk_ref, cnt_ref, carry_ref):
    tm = x_ref.shape[0]
    e_n = N_EXPERTS

    @pl.when(pl.program_id(0) % (pl.num_programs(0) // MOE_PARTS) == 0)
    def _():
        carry_ref[...] = jnp.zeros_like(carry_ref)

    h1 = x_ref[...] + _dot(yrec_ref[...], woa_ref[...]) + _dot(yatt_ref[...], wob_ref[...])
    h1_ref[...] = h1
    hn = _rms(h1, lnmoe_ref[...])
    pa, pb = _pack_row(hn)
    hpa_ref[...] = pa
    hpb_ref[...] = pb

    logits = lax.dot_general(wrt_ref[...], hn, (((1,), (1,)), ((), ())),
                             precision=lax.Precision.HIGHEST, preferred_element_type=F32)
    scores = jax.nn.sigmoid(logits)
    sel = scores + ebias_ref[...]

    sel3 = sel.reshape(N_GROUPS, GROUP_SIZE, tm)
    idx3 = lax.broadcasted_iota(I32, (N_GROUPS, GROUP_SIZE, tm), 1)
    m1 = jnp.max(sel3, axis=1, keepdims=True)
    first = jnp.min(jnp.where(sel3 == m1, idx3, GROUP_SIZE), axis=1, keepdims=True)
    m2 = jnp.max(jnp.where(idx3 == first, -jnp.inf, sel3), axis=1, keepdims=True)
    gscore = (m1 + m2).reshape(N_GROUPS, tm)

    gidx = lax.broadcasted_iota(I32, (N_GROUPS, tm), 0)
    beaten = jnp.zeros((N_GROUPS, tm), I32)
    for g in range(N_GROUPS):
        other = gscore[g:g + 1, :]
        beats = (other > gscore) | ((other == gscore) & (g < gidx))
        beaten = beaten + beats.astype(I32)
    gkeep = beaten < TOPK_GROUPS
    keep = jnp.broadcast_to(gkeep.reshape(N_GROUPS, 1, tm), (N_GROUPS, GROUP_SIZE, tm)).reshape(e_n, tm)
    selm = jnp.where(keep, sel, -jnp.inf)

    eidx = lax.broadcasted_iota(I32, (e_n, tm), 0)
    beaten = jnp.zeros((e_n, tm), I32)
    for e in range(e_n):
        other = selm[e:e + 1, :]
        beats = (other > selm) | ((other == selm) & (e < eidx))
        beaten = beaten + beats.astype(I32)
    chosen = beaten < TOP_K
    chosen_f = chosen.astype(F32)

    wsum = _sublane_total(jnp.where(chosen, scores, 0.0), jnp.sum)
    weight = scores * (ROUTE_SCALE / wsum)

    chosen_b = chosen_f.astype(BF16)
    prefix = _dot(chosen_b, tri_ref[...])
    rank = prefix + carry_ref[...]
    carry_new = carry_ref[...] + jnp.sum(chosen_f, axis=1, keepdims=True)
    carry_ref[...] = carry_new
    cnt_ref[0] = carry_new.astype(I32)

    slot = _dot(low_ref[...], chosen_b)
    eidx_f = eidx.astype(F32)
    ek, wk, rk = [], [], []
    for kk in range(TOP_K):
        pick = chosen & (slot == float(kk))
        ek.append(_sublane_total(jnp.where(pick, eidx_f, 0.0), jnp.sum))
        wk.append(_sublane_total(jnp.where(pick, weight, 0.0), jnp.sum))
        rk.append(_sublane_total(jnp.where(pick, rank, 0.0), jnp.sum))
    ek_ref[...] = jnp.concatenate(ek, axis=0).astype(I32)
    wk_ref[...] = jnp.concatenate(wk, axis=0)
    rk_ref[...] = jnp.concatenate(rk, axis=0).astype(I32)


def _route(x2, y_rec, y_att, wo_a, wo_b, ln_moe, w_router_t, e_bias_col):
    t = x2.shape[0]
    tm = ROUTE_ROWS
    nt = t // tm
    row_map = lambda i: (i, 0)
    col_map = lambda i: (0, i)
    fixed = lambda i: (0, 0)
    tri = (lax.broadcasted_iota(I32, (tm, tm), 0) < lax.broadcasted_iota(I32, (tm, tm), 1)).astype(BF16)
    low = (lax.broadcasted_iota(I32, (N_EXPERTS, N_EXPERTS), 1)
           < lax.broadcasted_iota(I32, (N_EXPERTS, N_EXPERTS), 0)).astype(BF16)
    out_shapes = (
        jax.ShapeDtypeStruct((t, D_MODEL), F32),
        jax.ShapeDtypeStruct((t, PACK_W), U32),
        jax.ShapeDtypeStruct((t, PACK_W), U32),
        jax.ShapeDtypeStruct((TOP_K, t), I32),
        jax.ShapeDtypeStruct((TOP_K, t), F32),
        jax.ShapeDtypeStruct((TOP_K, t), I32),
        jax.ShapeDtypeStruct((MOE_PARTS, N_EXPERTS, 1), I32),
    )
    assert nt % MOE_PARTS == 0
    return pl.pallas_call(
        _route_kernel,
        grid=(nt,),
        in_specs=[
            pl.BlockSpec((tm, D_MODEL), row_map),
            pl.BlockSpec((tm, D_REC), row_map),
            pl.BlockSpec((tm, D_ATT), row_map),
            pl.BlockSpec((D_REC, D_MODEL), fixed),
            pl.BlockSpec((D_ATT, D_MODEL), fixed),
            pl.BlockSpec((1, D_MODEL), fixed),
            pl.BlockSpec((N_EXPERTS, D_MODEL), fixed),
            pl.BlockSpec((N_EXPERTS, 1), fixed),
            pl.BlockSpec((tm, tm), fixed),
            pl.BlockSpec((N_EXPERTS, N_EXPERTS), fixed),
        ],
        out_specs=[
            pl.BlockSpec((tm, D_MODEL), row_map),
            pl.BlockSpec((tm, PACK_W), row_map),
            pl.BlockSpec((tm, PACK_W), row_map),
            pl.BlockSpec((TOP_K, tm), col_map),
            pl.BlockSpec((TOP_K, tm), col_map),
            pl.BlockSpec((TOP_K, tm), col_map),
            pl.BlockSpec((1, N_EXPERTS, 1), lambda i: (i // (nt // MOE_PARTS), 0, 0)),
        ],
        out_shape=out_shapes,
        scratch_shapes=[pltpu.VMEM((N_EXPERTS, 1), F32)],
        compiler_params=pltpu.CompilerParams(dimension_semantics=("arbitrary",), vmem_limit_bytes=VMEM_LIMIT),
        name="route",
    )(x2, y_rec, y_att, wo_a, wo_b, ln_moe, w_router_t, e_bias_col, tri, low)


def _plan_kernel(pad_start_ref, ek_ref, rk_ref, dest_ref):
    ek = ek_ref[...]
    part = pl.program_id(0) // (pl.num_programs(0) // MOE_PARTS)

    def add_expert(e, base):
        return jnp.where(ek == e, pad_start_ref[part * N_EXPERTS + e], base)

    dest_ref[...] = rk_ref[...] + lax.fori_loop(0, N_EXPERTS, add_expert, jnp.zeros_like(ek))


def _plan(pad_start, ek, rk):
    kk, t = ek.shape
    tl = min(t // MOE_PARTS, 2048)
    col_map = lambda i, ps: (0, i)
    grid_spec = pltpu.PrefetchScalarGridSpec(
        num_scalar_prefetch=1,
        grid=(t // tl,),
        in_specs=[pl.BlockSpec((kk, tl), col_map), pl.BlockSpec((kk, tl), col_map)],
        out_specs=pl.BlockSpec((kk, tl), col_map),
    )
    return pl.pallas_call(
        _plan_kernel,
        grid_spec=grid_spec,
        out_shape=jax.ShapeDtypeStruct((kk, t), I32),
        compiler_params=pltpu.CompilerParams(dimension_semantics=("arbitrary",)),
        name="plan",
    )(pad_start, ek, rk)


def _sc_mesh():
    return plsc.VectorSubcoreMesh(core_axis_name="core", subcore_axis_name="subcore")


def _sc_dispatch(rows, dest, n_out, part):
    t, w = rows.shape
    kk = dest.shape[0]
    steps = t // MOE_PARTS // SC_WINDOW
    first = part * steps

    @pl.kernel(out_type=jax.ShapeDtypeStruct((n_out, w), rows.dtype), mesh=_sc_mesh(), scratch_types=[])
    def kern(x_hbm, i_hbm, o_hbm):
        def body(x_vmem, i_vmem):
            for k in range(kk):
                pltpu.sync_copy(x_vmem, o_hbm.at[i_vmem.at[k]])

        pltpu.emit_pipeline(
            body,
            grid=(steps,),
            in_specs=[pl.BlockSpec((SC_WINDOW, w), lambda i: (first + i, 0)),
                      pl.BlockSpec((kk, SC_WINDOW), lambda i: (0, first + i))],
            out_specs=[],
            core_axis_name=("core", "subcore"),
            dimension_semantics=(pltpu.PARALLEL,),
        )(x_hbm, i_hbm)

    return kern(rows, dest)


def _sc_combine(rows, dest, part):
    kk, t = dest.shape
    w = rows.shape[1]
    tp = t // MOE_PARTS
    flat = dest.reshape(1, kk * t)
    windows = tp // SC_WINDOW

    def index_block(i):
        return (0, (i // windows) * (t // SC_WINDOW) + part * windows + i % windows)

    @pl.kernel(out_type=jax.ShapeDtypeStruct((kk * tp, w), rows.dtype), mesh=_sc_mesh(), scratch_types=[])
    def kern(y_hbm, i_hbm, o_hbm):
        def body(i_vmem, o_vmem):
            pltpu.sync_copy(y_hbm.at[i_vmem.at[0]], o_vmem)

        pltpu.emit_pipeline(
            body,
            grid=(kk * windows,),
            in_specs=[pl.BlockSpec((1, SC_WINDOW), index_block)],
            out_specs=[pl.BlockSpec((SC_WINDOW, w), lambda i: (i, 0))],
            core_axis_name=("core", "subcore"),
            dimension_semantics=(pltpu.PARALLEL,),
        )(i_hbm, o_hbm)

    return kern(rows, flat).reshape(kk, tp, w)


def _experts_kernel(blk_expert_ref, n_used_ref, first_ref, slot_ref, next_ref, xa_ref, xb_ref,
                    w1_hbm, w3_hbm, w2_hbm, ya_ref, yb_ref,
                    w1f_ref, w3f_ref, w2f_ref, w1b_ref, w3b_ref, w2b_ref, sem):
    b = pl.program_id(0)

    def weight_copies(e, s):
        return (pltpu.make_async_copy(w1_hbm.at[e], w1f_ref.at[s], sem.at[s, 0]),
                pltpu.make_async_copy(w3_hbm.at[e], w3f_ref.at[s], sem.at[s, 1]),
                pltpu.make_async_copy(w2_hbm.at[e], w2f_ref.at[s], sem.at[s, 2]))

    @pl.when(b == 0)
    def _():
        for copy in weight_copies(blk_expert_ref[0], 0):
            copy.start()

    @pl.when(first_ref[b] == 1)
    def _():
        s = slot_ref[b]
        for copy in weight_copies(blk_expert_ref[b], s):
            copy.wait()

        @pl.when(next_ref[b] >= 0)
        def _():
            for copy in weight_copies(next_ref[b], 1 - s):
                copy.start()

        w1b_ref[...] = w1f_ref[s].astype(BF16)
        w3b_ref[...] = w3f_ref[s].astype(BF16)
        w2b_ref[...] = w2f_ref[s].astype(BF16)

    @pl.when(b < n_used_ref[0])
    def _():
        x = _unpack_row(xa_ref[...], xb_ref[...]).astype(BF16)
        a = _dot(x, w1b_ref[...])
        g = _dot(x, w3b_ref[...])
        hmid = (jax.nn.silu(a) * g).astype(BF16)
        y = _dot(hmid, w2b_ref[...])
        pa, pb = _pack_row(y)
        ya_ref[...] = pa
        yb_ref[...] = pb


def _experts(xa, xb, w1, w3, w2, blk_expert, n_used, seg_first, seg_slot, seg_next):
    p = xa.shape[0]
    m = EXPERT_ROWS
    nblk = p // m
    row_map = lambda b, *_: (b, 0)
    hbm = pl.BlockSpec(memory_space=pl.ANY)
    grid_spec = pltpu.PrefetchScalarGridSpec(
        num_scalar_prefetch=5,
        grid=(nblk,),
        in_specs=[pl.BlockSpec((m, PACK_W), row_map), pl.BlockSpec((m, PACK_W), row_map), hbm, hbm, hbm],
        out_specs=[pl.BlockSpec((m, PACK_W), row_map), pl.BlockSpec((m, PACK_W), row_map)],
        scratch_shapes=[
            pltpu.VMEM((2, D_MODEL, D_EXPERT), F32), pltpu.VMEM((2, D_MODEL, D_EXPERT), F32),
            pltpu.VMEM((2, D_EXPERT, D_MODEL), F32),
            pltpu.VMEM((D_MODEL, D_EXPERT), BF16), pltpu.VMEM((D_MODEL, D_EXPERT), BF16),
            pltpu.VMEM((D_EXPERT, D_MODEL), BF16),
            pltpu.SemaphoreType.DMA((2, 3)),
        ],
    )
    return pl.pallas_call(
        _experts_kernel,
        grid_spec=grid_spec,
        out_shape=(jax.ShapeDtypeStruct((p, PACK_W), U32), jax.ShapeDtypeStruct((p, PACK_W), U32)),
        compiler_params=pltpu.CompilerParams(dimension_semantics=("arbitrary",), vmem_limit_bytes=VMEM_LIMIT),
        name="experts",
    )(blk_expert, n_used, seg_first, seg_slot, seg_next, xa, xb, w1, w3, w2)


def _tail_kernel(h1_ref, ga_ref, gb_ref, wk_ref, p_ref, lnmoe_ref, ws1_ref, ws3_ref, ws2_ref, lnple_ref,
                 wpg_ref, wpp_ref, lnf_ref, o_ref):
    h1 = h1_ref[...]
    hn = _rms(h1, lnmoe_ref[...]).astype(BF16)
    shared = _dot((jax.nn.silu(_dot(hn, ws1_ref[...])) * _dot(hn, ws3_ref[...])).astype(BF16), ws2_ref[...])
    wk = wk_ref[...]
    routed = jnp.zeros_like(h1)
    for kk in range(TOP_K):
        routed = routed + wk[:, kk:kk + 1] * _unpack_row(ga_ref[kk], gb_ref[kk])
    h2 = h1 + routed + shared
    gate = jax.nn.sigmoid(_dot(_rms(h2, lnple_ref[...]).astype(BF16), wpg_ref[...]))
    h3 = h2 + gate * _dot(p_ref[...].astype(BF16), wpp_ref[...])
    o_ref[...] = _rms(h3, lnf_ref[...])


def _tail_into_kernel(out_so_far_ref, *refs):
    del out_so_far_ref
    _tail_kernel(*refs)


def _tail(out_so_far, part, h1, ga, gb, wk_t, p2, ln_moe, ws1, ws3, ws2, ln_ple, w_pg, w_pp, ln_f):
    t = h1.shape[0]
    tm = TAIL_ROWS
    steps = t // MOE_PARTS // tm
    row_map = lambda i: (part * steps + i, 0)
    fixed = lambda i: (0, 0)
    g_map = lambda i: (0, i, 0)
    d_sh = ws1.shape[1]
    carried = () if out_so_far is None else (out_so_far,)
    return pl.pallas_call(
        _tail_kernel if out_so_far is None else _tail_into_kernel,
        grid=(steps,),
        input_output_aliases={} if out_so_far is None else {0: 0},
        in_specs=[pl.BlockSpec(memory_space=pl.ANY)] * len(carried) + [
            pl.BlockSpec((tm, D_MODEL), row_map),
            pl.BlockSpec((TOP_K, tm, PACK_W), g_map),
            pl.BlockSpec((TOP_K, tm, PACK_W), g_map),
            pl.BlockSpec((tm, TOP_K), row_map),
            pl.BlockSpec((tm, D_PLE), row_map),
            pl.BlockSpec((1, D_MODEL), fixed),
            pl.BlockSpec((D_MODEL, d_sh), fixed),
            pl.BlockSpec((D_MODEL, d_sh), fixed),
            pl.BlockSpec((d_sh, D_MODEL), fixed),
            pl.BlockSpec((1, D_MODEL), fixed),
            pl.BlockSpec((D_MODEL, D_MODEL), fixed),
            pl.BlockSpec((D_PLE, D_MODEL), fixed),
            pl.BlockSpec((1, D_MODEL), fixed),
        ],
        out_specs=pl.BlockSpec((tm, D_MODEL), row_map),
        out_shape=jax.ShapeDtypeStruct((t, D_MODEL), F32),
        compiler_params=pltpu.CompilerParams(dimension_semantics=("arbitrary",), vmem_limit_bytes=VMEM_LIMIT),
        name="tail",
    )(*carried, h1, ga, gb, wk_t, p2, ln_moe, ws1, ws3, ws2, ln_ple, w_pg, w_pp, ln_f)


def _rope_constants():
    half = ROPE_DIM // 2
    inv_freq = (ROPE_THETA ** (-jnp.arange(0, ROPE_DIM, 2, dtype=F32) / ROPE_DIM)).reshape(half, 1)
    f = lax.broadcasted_iota(I32, (ROPE_DIM, LANES), 0)
    l64 = lax.broadcasted_iota(I32, (ROPE_DIM, LANES), 1) % HALF_DIM
    cos_pat = ((f < half) & (l64 < ROPE_DIM) & (l64 % half == f)).astype(F32)
    sa_pat = -((f >= half) & (l64 < half) & (l64 == f - half)).astype(F32)
    sb_pat = ((f >= half) & (l64 >= half) & (l64 < ROPE_DIM) & (l64 - half == f - half)).astype(F32)
    return inv_freq, jnp.concatenate([cos_pat, sa_pat, sb_pat], axis=1)


def _block_diag_tiles(w):
    nb, bd, _ = w.shape
    per = nb // 2
    tiles = []
    for tix in range(2):
        rows = []
        for j in range(per):
            rows.append(jnp.concatenate(
                [w[tix * per + j] if c == j else jnp.zeros((bd, bd), w.dtype) for c in range(per)], axis=1))
        tiles.append(jnp.concatenate(rows, axis=0))
    return jnp.stack(tiles).astype(BF16)


def _layer(h, p_l, positions, lam_init, ln_mix, w_in, conv_w, conv_b, w_a, b_a, w_i, b_i, rg_lambda, g_rec,
           lq1, lk1, lq2, lk2, g_sub, w_out, ln_moe, w_router, e_bias, w1, w3, w2, ws1, ws3, ws2,
           ln_ple, w_ple_gate, w_ple_proj, ln_out):
    batch, seq, _ = h.shape
    t = batch * seq
    x2 = h.reshape(t, D_MODEL)
    row = lambda a: a.reshape(1, -1)
    inv_freq, rope_pat = _rope_constants()

    y_rec, q, k, vt = _mix_in(
        x2, positions.reshape(1, t), inv_freq, rope_pat, row(ln_mix), w_in.astype(BF16), conv_w, row(conv_b),
        _block_diag_tiles(w_a), row(b_a), _block_diag_tiles(w_i), row(b_i), row(rg_lambda), row(g_rec),
        batch, seq)
    y_att = _attention(q, k, vt, row(lq1), row(lk1), row(lq2), row(lk2), g_sub.reshape(-1, 1), batch, seq,
                       lam_init)

    w_out_b = w_out.astype(BF16)
    h1, hpa, hpb, ek, wk, rk, counts = _route(
        x2, y_rec, y_att, w_out_b[:D_REC], w_out_b[D_REC:], row(ln_moe), w_router.T, e_bias.reshape(-1, 1))

    m = EXPERT_ROWS
    counts = counts.reshape(MOE_PARTS, N_EXPERTS)
    padded = (counts + m - 1) // m * m
    pad_end = jnp.cumsum(padded, axis=1)
    pad_start = pad_end - padded
    n_rows = t // MOE_PARTS * TOP_K + N_EXPERTS * m
    nblk = n_rows // m
    n_used = (pad_end[:, -1] // m).astype(I32)
    blk = jnp.arange(nblk, dtype=I32)
    blk_row = jnp.minimum(blk[None, :], n_used[:, None] - 1) * m
    blk_expert = jnp.sum((pad_end[:, None, :] <= blk_row[:, :, None]).astype(I32), axis=2)
    prev_expert = jnp.concatenate([jnp.full((MOE_PARTS, 1), -1, I32), blk_expert[:, :-1]], axis=1)
    seg_first = ((blk[None, :] < n_used[:, None]) & (blk_expert != prev_expert)).astype(I32)
    seg_slot = ((jnp.cumsum(seg_first, axis=1) - 1) % 2).astype(I32)
    eid = jnp.arange(N_EXPERTS, dtype=I32)
    later = (padded[:, None, :] > 0) & (eid[None, None, :] > eid[None, :, None])
    next_expert = jnp.min(jnp.where(later, eid[None, None, :], N_EXPERTS), axis=2)
    next_expert = jnp.where(next_expert == N_EXPERTS, -1, next_expert).astype(I32)
    seg_next = jnp.sum(jnp.where(blk_expert[:, :, None] == eid[None, None, :], next_expert[:, None, :], 0), axis=2)
    dest = _plan(pad_start.astype(I32).reshape(-1), ek, rk)

    wk_t = wk.T
    p2 = p_l.reshape(t, D_PLE)
    tail_weights = (row(ln_moe), ws1.astype(BF16), ws3.astype(BF16), ws2.astype(BF16), row(ln_ple),
                    w_ple_gate.astype(BF16), w_ple_proj.astype(BF16), row(ln_out))
    gathered = []
    for part in range(MOE_PARTS):
        xa = _sc_dispatch(hpa, dest, n_rows, part)
        xb = _sc_dispatch(hpb, dest, n_rows, part)
        ya, yb = _experts(xa, xb, w1, w3, w2, blk_expert[part], n_used[part:part + 1], seg_first[part],
                          seg_slot[part], seg_next[part])
        gathered.append((_sc_combine(ya, dest, part), _sc_combine(yb, dest, part)))
    out = None
    for part, (ga, gb) in enumerate(gathered):
        out = _tail(out, part, h1, ga, gb, wk_t, p2, *tail_weights)
    return out.reshape(batch, seq, D_MODEL)


def kernel(x, p, positions, ln_mix, w_in, conv_w, conv_b, w_a, b_a, w_i, b_i, rg_lambda, g_rec, lq1, lk1, lq2,
           lk2, g_sub, w_out, ln_moe, w_router, e_bias, w1, w3, w2, ws1, ws3, ws2, ln_ple, w_ple_gate,
           w_ple_proj, ln_f):
    depth = w_in.shape[0]
    assert depth == 1, "the fused tail applies the final norm; one layer supported"
    lam_init = 0.8 - 0.6 * math.exp(-0.3 * 0)
    return _layer(x, p[0], positions, lam_init, ln_mix[0], w_in[0], conv_w[0], conv_b[0], w_a[0], b_a[0], w_i[0],
                  b_i[0], rg_lambda[0], g_rec[0], lq1[0], lk1[0], lq2[0], lk2[0], g_sub[0], w_out[0], ln_moe[0],
                  w_router[0], e_bias[0], w1[0], w3[0], w2[0], ws1[0], ws3[0], ws2[0], ln_ple[0], w_ple_gate[0],
                  w_ple_proj[0], ln_f)
```

```python
import functools
import math

import jax
import jax.numpy as jnp
from jax import lax
from jax.experimental import pallas as pl
from jax.experimental.pallas import tpu as pltpu
from jax.experimental.pallas import tpu_sc as plsc

F32 = jnp.float32
BF16 = jnp.bfloat16
U32 = jnp.uint32
I32 = jnp.int32

D_MODEL = 1024
D_REC = 512
REC_BLOCKS = 8
CONV_WIDTH = 4
RG_C = 8.0
N_HEADS = 4
HALF_DIM = 64
V_DIM = 128
D_ATT = N_HEADS * V_DIM
D_QK = N_HEADS * 2 * HALF_DIM
ROPE_DIM = 16
ROPE_THETA = 500000.0
N_EXPERTS = 64
TOP_K = 8
N_GROUPS = 8
GROUP_SIZE = N_EXPERTS // N_GROUPS
TOPK_GROUPS = 4
D_EXPERT = 256
ROUTE_SCALE = 2.5
D_PLE = 256
EPS = 1e-6

LANES = 128
SUBLANES = 8
VMEM_LIMIT = 56 * 1024 * 1024

MIX_ROWS = 512
ATT_Q = 512
ATT_TAIL_LANES = 256
ONES_ROWS = 16
V_EXT = V_DIM + ONES_ROWS
ROUTE_ROWS = 512
EXPERT_ROWS = 512
MOE_PARTS = 1
TAIL_ROWS = 256
SC_WINDOW = 128
PACK_W = 256
NEG_BIG = -1e30


def _rms(x, g):
    return x * lax.rsqrt(jnp.mean(x * x, axis=-1, keepdims=True) + EPS) * g


def _dot(a, b):
    return jnp.dot(a, b, preferred_element_type=F32)


def _pack_pair(lo, hi):
    lo_bits = lax.bitcast_convert_type(lo.astype(BF16).astype(F32), U32)
    hi_bits = lax.bitcast_convert_type(hi.astype(BF16).astype(F32), U32)
    return (lo_bits >> 16) | (hi_bits & jnp.uint32(0xFFFF0000))


def _unpack_pair(p):
    lo = lax.bitcast_convert_type(p << 16, F32)
    hi = lax.bitcast_convert_type(p & jnp.uint32(0xFFFF0000), F32)
    return lo, hi


def _pack_row(x):
    w = PACK_W
    return _pack_pair(x[:, 0:w], x[:, w:2 * w]), _pack_pair(x[:, 2 * w:3 * w], x[:, 3 * w:4 * w])


def _unpack_row(pa, pb):
    c0, c1 = _unpack_pair(pa)
    c2, c3 = _unpack_pair(pb)
    return jnp.concatenate([c0, c1, c2, c3], axis=1)


def _shift_rows(a, s, fill, row):
    n, c = a.shape
    if s % SUBLANES == 0:
        return jnp.concatenate([jnp.full((s, c), fill, a.dtype), a[:n - s]], axis=0)
    return jnp.where(row >= s, pltpu.roll(a, s, 0), fill)


def _mix_in_kernel(x_ref, pos_ref, invf_ref, pat_ref, lnm_ref, win_ref, cw_ref, cb_ref, wa_ref, ba_ref,
                   wi_ref, bi_ref, lam_ref, grec_ref,
                   yrec_ref, q_ref, k_ref, vt_ref, tail_ref, hcarry_ref, buf_a, buf_b):
    tm = x_ref.shape[0]
    groups = tm // SUBLANES
    chunks = D_REC // LANES

    def stage(ref, v):
        for c in range(chunks):
            ref[c] = v[:, c * LANES:(c + 1) * LANES]

    def slab(ref, r):
        return jnp.concatenate([ref[c, pl.ds(r, groups, stride=SUBLANES), :] for c in range(chunks)], axis=1)

    @pl.when(pl.program_id(1) == 0)
    def _():
        tail_ref[...] = jnp.zeros_like(tail_ref)
        hcarry_ref[...] = jnp.zeros_like(hcarry_ref)

    hn = _rms(x_ref[...], lnm_ref[...]).astype(BF16)

    ang = invf_ref[...] * pos_ref[...].astype(F32)
    cs = jnp.concatenate([jnp.cos(ang), jnp.sin(ang)], axis=0)
    tabs = lax.dot_general(cs, pat_ref[...], (((0,), (0,)), ((), ())),
                           precision=lax.Precision.HIGHEST, preferred_element_type=F32)
    lane64 = lax.broadcasted_iota(I32, (1, LANES), 1) % HALF_DIM
    cosf = tabs[:, 0:LANES] + (lane64 >= ROPE_DIM).astype(F32)
    sa, sb = tabs[:, LANES:2 * LANES], tabs[:, 2 * LANES:3 * LANES]

    def project_rotary(out_ref, off, mul):
        for c in range(0, D_QK // LANES, 2):
            z2 = _dot(hn, win_ref[:, off + c * LANES: off + (c + 2) * LANES])
            for cc in range(2):
                zc = z2[:, cc * LANES:(cc + 1) * LANES]
                rot = (zc * cosf + pltpu.roll(zc, LANES - ROPE_DIM // 2, 1) * sa
                       + pltpu.roll(zc, ROPE_DIM // 2, 1) * sb)
                out_ref[:, (c + cc) * LANES:(c + cc + 1) * LANES] = (rot * mul).astype(BF16)

    def project_v():
        vt = _dot(hn, win_ref[:, 2 * D_REC + 2 * D_QK:]).T.astype(BF16)
        for hd in range(N_HEADS):
            vt_ref[0, hd * V_EXT:hd * V_EXT + V_DIM, :] = vt[hd * V_DIM:(hd + 1) * V_DIM]
            vt_ref[0, hd * V_EXT + V_DIM:(hd + 1) * V_EXT, :] = jnp.ones((ONES_ROWS, tm), BF16)

    xr = _dot(hn, win_ref[:, 0:D_REC])
    stage(buf_a, xr)
    stage(buf_b, _dot(hn, win_ref[:, D_REC:2 * D_REC]))
    tail = tail_ref[...]
    tail_ref[...] = xr[tm - SUBLANES:, :]

    grow = lax.broadcasted_iota(I32, (groups, D_REC), 0)

    def down_one(a, first_row):
        return jnp.where(grow == 0, first_row, pltpu.roll(a, 1, 0))

    xs = [slab(buf_a, r) for r in range(SUBLANES)]
    wrapped = {r: down_one(xs[r], tail[r:r + 1, :]) for r in range(SUBLANES - CONV_WIDTH + 1, SUBLANES)}
    xc = []
    for r in range(SUBLANES):
        acc = cb_ref[...] + cw_ref[CONV_WIDTH - 1:CONV_WIDTH, :] * xs[r]
        for d in range(1, CONV_WIDTH):
            prev = xs[r - d] if r >= d else wrapped[r - d + SUBLANES]
            acc = acc + cw_ref[CONV_WIDTH - 1 - d:CONV_WIDTH - d, :] * prev
        xc.append(acc)
    xc = jnp.concatenate(xc, axis=0)

    xcb = xc.astype(BF16)
    half = D_REC // 2
    ra = jnp.concatenate([_dot(xcb[:, :half], wa_ref[0]), _dot(xcb[:, half:], wa_ref[1])], axis=1)
    ri = jnp.concatenate([_dot(xcb[:, :half], wi_ref[0]), _dot(xcb[:, half:], wi_ref[1])], axis=1)
    r_gate = jax.nn.sigmoid(ra + ba_ref[...])
    i_gate = jax.nn.sigmoid(ri + bi_ref[...])
    lam = lam_ref[...]
    softplus_neg = jnp.maximum(-lam, 0.0) + jnp.log(1.0 + jnp.exp(-jnp.abs(lam)))
    log_a = -RG_C * r_gate * softplus_neg
    a = jnp.exp(log_a)
    u = jnp.sqrt(1.0 - jnp.exp(2.0 * log_a)) * i_gate * xc

    rows = lambda v, r: v[r * groups:(r + 1) * groups]
    hs, ps = [rows(u, 0)], [rows(a, 0)]
    for r in range(1, SUBLANES):
        hs.append(rows(a, r) * hs[-1] + rows(u, r))
        ps.append(rows(a, r) * ps[-1])
    tot_a, tot_h = ps[-1], hs[-1]
    s = 1
    while s < groups:
        tot_h = tot_h + tot_a * _shift_rows(tot_h, s, 0.0, grow)
        tot_a = tot_a * _shift_rows(tot_a, s, 1.0, grow)
        s *= 2
    h_in = hcarry_ref[...]
    group_end = tot_h + tot_a * h_in
    hcarry_ref[...] = group_end[groups - 1:groups, :]
    group_in = down_one(group_end, h_in)

    for r in range(SUBLANES):
        h = hs[r] + ps[r] * group_in
        y = h * jax.nn.gelu(slab(buf_b, r))
        yn = _rms(y, grec_ref[...])
        for c in range(chunks):
            buf_a[c, pl.ds(r, groups, stride=SUBLANES), :] = yn[:, c * LANES:(c + 1) * LANES]
    for c in range(chunks):
        yrec_ref[:, c * LANES:(c + 1) * LANES] = buf_a[c].astype(BF16)

    project_rotary(q_ref, 2 * D_REC, HALF_DIM ** -0.5 * math.log2(math.e))
    project_rotary(k_ref, 2 * D_REC + D_QK, 1.0)
    project_v()


def _mix_in(x2, pos_row, inv_freq, rope_pat, ln_mix, w_in, conv_w, conv_b, wa_bd, b_a, wi_bd, b_i, rg_lambda,
            g_rec, batch, seq):
    tm = MIX_ROWS
    nt = seq // tm
    d_in = w_in.shape[1]
    row_map = lambda b, i: (b * nt + i, 0)
    fixed2 = lambda b, i: (0, 0)
    fixed3 = lambda b, i: (0, 0, 0)
    t = batch * seq
    out_shapes = (
        jax.ShapeDtypeStruct((t, D_REC), BF16),
        jax.ShapeDtypeStruct((t, D_QK), BF16),
        jax.ShapeDtypeStruct((t, D_QK), BF16),
        jax.ShapeDtypeStruct((t // tm, N_HEADS * V_EXT, tm), BF16),
    )
    return pl.pallas_call(
        _mix_in_kernel,
        grid=(batch, nt),
        in_specs=[
            pl.BlockSpec((tm, D_MODEL), row_map),
            pl.BlockSpec((1, tm), lambda b, i: (0, b * nt + i)),
            pl.BlockSpec((ROPE_DIM // 2, 1), fixed2),
            pl.BlockSpec((ROPE_DIM, 3 * LANES), fixed2),
            pl.BlockSpec((1, D_MODEL), fixed2),
            pl.BlockSpec((D_MODEL, d_in), fixed2),
            pl.BlockSpec((CONV_WIDTH, D_REC), fixed2),
            pl.BlockSpec((1, D_REC), fixed2),
            pl.BlockSpec((2, D_REC // 2, D_REC // 2), fixed3),
            pl.BlockSpec((1, D_REC), fixed2),
            pl.BlockSpec((2, D_REC // 2, D_REC // 2), fixed3),
            pl.BlockSpec((1, D_REC), fixed2),
            pl.BlockSpec((1, D_REC), fixed2),
            pl.BlockSpec((1, D_REC), fixed2),
        ],
        out_specs=[
            pl.BlockSpec((tm, D_REC), row_map),
            pl.BlockSpec((tm, D_QK), row_map),
            pl.BlockSpec((tm, D_QK), row_map),
            pl.BlockSpec((1, N_HEADS * V_EXT, tm), lambda b, i: (b * nt + i, 0, 0)),
        ],
        out_shape=out_shapes,
        scratch_shapes=[pltpu.VMEM((SUBLANES, D_REC), F32), pltpu.VMEM((1, D_REC), F32),
                        pltpu.VMEM((D_REC // LANES, tm, LANES), F32),
                        pltpu.VMEM((D_REC // LANES, tm, LANES), F32)],
        compiler_params=pltpu.CompilerParams(
            dimension_semantics=("arbitrary", "arbitrary"), vmem_limit_bytes=VMEM_LIMIT),
        name="mix_in",
    )(x2, pos_row, inv_freq, rope_pat, ln_mix, w_in, conv_w, conv_b, wa_bd, b_a, wi_bd, b_i, rg_lambda, g_rec)


def _attn_kernel(lq1_ref, lk1_ref, lq2_ref, lk2_ref, gsub_ref, bias_ref, q_ref, k_ref, vt_ref, o_ref,
                 m_ref, acc_ref, aprev_ref, s0_ref, s1_ref, mb0_ref, mb1_ref, p0_ref, p1_ref, *, lam_init):
    tq = q_ref.shape[0]
    tk = vt_ref.shape[2]
    assert tq == tk, "the causal bias tile assumes the diagonal block is square"
    i = pl.program_id(2)

    qt = q_ref[...].astype(F32).T
    dim = lax.broadcasted_iota(I32, (LANES, tq), 0)
    qqt = jnp.concatenate([jnp.where(dim < HALF_DIM, qt, 0.0), jnp.where(dim >= HALF_DIM, qt, 0.0)],
                          axis=1).astype(BF16)

    n = (i * tq) // tk

    def scores(j):
        return _dot(k_ref[pl.ds(pl.multiple_of(j * tk, tk), tk), :], qqt)

    def probabilities(s, m_prev):
        m_new = jnp.maximum(m_prev, jnp.max(s, axis=0, keepdims=True))
        alpha = jnp.exp2(m_prev - m_new)
        p = jnp.exp2((s - m_new).astype(BF16))
        return p, alpha, m_new

    def store_scores(j, s_buf, mb_buf):
        s = scores(j)
        s_buf[...] = s
        mb_buf[...] = jnp.max(s, axis=0, keepdims=True)

    def pipe_step(j, cur, nxt, p_cur, p_prev):
        store_scores(j + 1, *nxt)
        s_cur, mb_cur = cur
        m_prev = m_ref[...]
        m_new = jnp.maximum(m_prev, mb_cur[...])
        p_cur[...] = jnp.exp2((s_cur[...] - m_new).astype(BF16))
        m_ref[...] = m_new
        acc_ref[...] = aprev_ref[...] * acc_ref[...] + _dot(vt_ref[jnp.maximum(j - 1, 0)], p_prev[...])
        aprev_ref[...] = jnp.exp2(m_prev - m_new)

    buf0, buf1 = (s0_ref, mb0_ref), (s1_ref, mb1_ref)
    m_ref[...] = jnp.full_like(m_ref, NEG_BIG)
    acc_ref[...] = jnp.zeros_like(acc_ref)
    aprev_ref[...] = jnp.ones_like(aprev_ref)
    odd = n % 2

    @pl.when(odd == 0)
    def _():
        p1_ref[...] = jnp.zeros_like(p1_ref)
        store_scores(0, *buf0)

    @pl.when(odd == 1)
    def _():
        p0_ref[...] = jnp.zeros_like(p0_ref)
        store_scores(0, *buf1)
        pipe_step(0, buf1, buf0, p1_ref, p0_ref)

    def pair(t, carry):
        j = 2 * t + odd
        pipe_step(j, buf0, buf1, p0_ref, p1_ref)
        pipe_step(j + 1, buf1, buf0, p1_ref, p0_ref)
        return carry

    lax.fori_loop(0, n // 2, pair, 0)

    parts = []
    for c0 in range(0, 2 * tq, ATT_TAIL_LANES):
        cols = slice(c0, c0 + ATT_TAIL_LANES)
        p, alpha, _ = probabilities(s0_ref[:, cols] + bias_ref[:, cols], m_ref[:, cols])
        part = aprev_ref[:, cols] * acc_ref[:, cols] + _dot(vt_ref[jnp.maximum(n - 1, 0)], p1_ref[:, cols])
        parts.append(alpha * part + _dot(vt_ref[n], p))
    acc = jnp.concatenate(parts, axis=1)

    lam = (jnp.exp(jnp.sum(lq1_ref[...] * lk1_ref[...], axis=-1, keepdims=True))
           - jnp.exp(jnp.sum(lq2_ref[...] * lk2_ref[...], axis=-1, keepdims=True)) + lam_init)
    o = acc[:V_DIM] / acc[V_DIM:V_DIM + 1]
    o = o[:, :tq] - lam * o[:, tq:]
    o = o * lax.rsqrt(jnp.mean(o * o, axis=0, keepdims=True) + EPS) * gsub_ref[...]
    o_ref[...] = (o * (1.0 - lam_init)).T.astype(BF16)


def _attention(q, k, vt, lq1, lk1, lq2, lk2, g_sub_col, batch, seq, lam_init):
    tq = ATT_Q
    nq = seq // tq
    tk = vt.shape[2]
    nk = seq // tk
    vec = lambda b, h, i: (0, 0)
    visible = (lax.broadcasted_iota(I32, (tk, 2 * tq), 0) <= lax.broadcasted_iota(I32, (tk, 2 * tq), 1) % tq)
    bias = jnp.where(visible, 0.0, NEG_BIG).astype(F32)
    return pl.pallas_call(
        functools.partial(_attn_kernel, lam_init=lam_init),
        grid=(batch, N_HEADS, nq),
        in_specs=[
            pl.BlockSpec((1, HALF_DIM), vec),
            pl.BlockSpec((1, HALF_DIM), vec),
            pl.BlockSpec((1, HALF_DIM), vec),
            pl.BlockSpec((1, HALF_DIM), vec),
            pl.BlockSpec((V_DIM, 1), vec),
            pl.BlockSpec((tk, 2 * tq), vec),
            pl.BlockSpec((tq, LANES), lambda b, h, i: (b * nq + i, h)),
            pl.BlockSpec((seq, LANES), lambda b, h, i: (b, h)),
            pl.BlockSpec((nk, V_EXT, tk), lambda b, h, i: (b, h, 0)),
        ],
        out_specs=pl.BlockSpec((tq, V_DIM), lambda b, h, i: (b * nq + i, h)),
        out_shape=jax.ShapeDtypeStruct((batch * seq, D_ATT), BF16),
        scratch_shapes=[pltpu.VMEM((1, 2 * tq), F32),
                        pltpu.VMEM((V_EXT, 2 * tq), F32), pltpu.VMEM((1, 2 * tq), F32),
                        pltpu.VMEM((tk, 2 * tq), F32), pltpu.VMEM((tk, 2 * tq), F32),
                        pltpu.VMEM((1, 2 * tq), F32), pltpu.VMEM((1, 2 * tq), F32),
                        pltpu.VMEM((tk, 2 * tq), BF16), pltpu.VMEM((tk, 2 * tq), BF16)],
        compiler_params=pltpu.CompilerParams(
            dimension_semantics=("arbitrary", "arbitrary", "arbitrary"), vmem_limit_bytes=VMEM_LIMIT),
        name="attention",
    )(lq1, lk1, lq2, lk2, g_sub_col, bias, q, k, vt)


def _sublane_total(x, op):
    return op(x, axis=0, keepdims=True)


def _route_kernel(x_ref, yrec_ref, yatt_ref, woa_ref, wob_ref, lnmoe_ref, wrt_ref, ebias_ref, tri_ref,
                  h1_ref, hpa_ref, hpb_ref, ek_ref, wk_ref, rk_ref, cnt_ref, carry_ref):
    tm = x_ref.shape[0]
    e_n = N_EXPERTS

    @pl.when(pl.program_id(0) % (pl.num_programs(0) // MOE_PARTS) == 0)
    def _():
        carry_ref[...] = jnp.zeros_like(carry_ref)

    h1 = x_ref[...] + _dot(yrec_ref[...], woa_ref[...]) + _dot(yatt_ref[...], wob_ref[...])
    h1_ref[...] = h1
    hn = _rms(h1, lnmoe_ref[...])
    pa, pb = _pack_row(hn)
    hpa_ref[...] = pa
    hpb_ref[...] = pb

    logits = lax.dot_general(wrt_ref[...], hn, (((1,), (1,)), ((), ())),
                             precision=lax.Precision.HIGHEST, preferred_element_type=F32)
    scores = jax.nn.sigmoid(logits)
    sel = scores + ebias_ref[...]

    sel3 = sel.reshape(N_GROUPS, GROUP_SIZE, tm)
    idx3 = lax.broadcasted_iota(I32, (N_GROUPS, GROUP_SIZE, tm), 1)
    m1 = jnp.max(sel3, axis=1, keepdims=True)
    first = jnp.min(jnp.where(sel3 == m1, idx3, GROUP_SIZE), axis=1, keepdims=True)
    m2 = jnp.max(jnp.where(idx3 == first, -jnp.inf, sel3), axis=1, keepdims=True)
    gscore = (m1 + m2).reshape(N_GROUPS, tm)

    gidx = lax.broadcasted_iota(I32, (N_GROUPS, tm), 0)
    beaten = jnp.zeros((N_GROUPS, tm), I32)
    for g in range(N_GROUPS):
        other = gscore[g:g + 1, :]
        beats = (other > gscore) | ((other == gscore) & (g < gidx))
        beaten = beaten + beats.astype(I32)
    gkeep = beaten < TOPK_GROUPS
    keep = jnp.broadcast_to(gkeep.reshape(N_GROUPS, 1, tm), (N_GROUPS, GROUP_SIZE, tm)).reshape(e_n, tm)
    selm = jnp.where(keep, sel, -jnp.inf)

    eidx = lax.broadcasted_iota(I32, (e_n, tm), 0)
    remaining = selm
    picks, ek, sk = [], [], []
    for _ in range(TOP_K):
        best = jnp.max(remaining, axis=0, keepdims=True)
        first = jnp.min(jnp.where(remaining == best, eidx, e_n), axis=0, keepdims=True)
        pick = eidx == first
        picks.append(pick)
        ek.append(first)
        sk.append(_sublane_total(jnp.where(pick, scores, 0.0), jnp.sum))
        remaining = jnp.where(pick, -jnp.inf, remaining)
    chosen_f = (remaining != selm).astype(F32)
    wsum = functools.reduce(lambda a, b: a + b, sk)
    ek_ref[...] = jnp.concatenate(ek, axis=0)
    wk_ref[...] = jnp.concatenate(sk, axis=0) * (ROUTE_SCALE / wsum)

    prefix = _dot(chosen_f.astype(BF16), tri_ref[...])
    rank = prefix + carry_ref[...]
    carry_new = carry_ref[...] + jnp.sum(chosen_f, axis=1, keepdims=True)
    carry_ref[...] = carry_new
    cnt_ref[0] = carry_new.astype(I32)
    rk = [_sublane_total(jnp.where(pick, rank, 0.0), jnp.sum) for pick in picks]
    rk_ref[...] = jnp.concatenate(rk, axis=0).astype(I32)


def _route(x2, y_rec, y_att, wo_a, wo_b, ln_moe, w_router_t, e_bias_col):
    t = x2.shape[0]
    tm = ROUTE_ROWS
    nt = t // tm
    row_map = lambda i: (i, 0)
    col_map = lambda i: (0, i)
    fixed = lambda i: (0, 0)
    tri = (lax.broadcasted_iota(I32, (tm, tm), 0) < lax.broadcasted_iota(I32, (tm, tm), 1)).astype(BF16)
    out_shapes = (
        jax.ShapeDtypeStruct((t, D_MODEL), F32),
        jax.ShapeDtypeStruct((t, PACK_W), U32),
        jax.ShapeDtypeStruct((t, PACK_W), U32),
        jax.ShapeDtypeStruct((TOP_K, t), I32),
        jax.ShapeDtypeStruct((TOP_K, t), F32),
        jax.ShapeDtypeStruct((TOP_K, t), I32),
        jax.ShapeDtypeStruct((MOE_PARTS, N_EXPERTS, 1), I32),
    )
    assert nt % MOE_PARTS == 0
    return pl.pallas_call(
        _route_kernel,
        grid=(nt,),
        in_specs=[
            pl.BlockSpec((tm, D_MODEL), row_map),
            pl.BlockSpec((tm, D_REC), row_map),
            pl.BlockSpec((tm, D_ATT), row_map),
            pl.BlockSpec((D_REC, D_MODEL), fixed),
            pl.BlockSpec((D_ATT, D_MODEL), fixed),
            pl.BlockSpec((1, D_MODEL), fixed),
            pl.BlockSpec((N_EXPERTS, D_MODEL), fixed),
            pl.BlockSpec((N_EXPERTS, 1), fixed),
            pl.BlockSpec((tm, tm), fixed),
        ],
        out_specs=[
            pl.BlockSpec((tm, D_MODEL), row_map),
            pl.BlockSpec((tm, PACK_W), row_map),
            pl.BlockSpec((tm, PACK_W), row_map),
            pl.BlockSpec((TOP_K, tm), col_map),
            pl.BlockSpec((TOP_K, tm), col_map),
            pl.BlockSpec((TOP_K, tm), col_map),
            pl.BlockSpec((1, N_EXPERTS, 1), lambda i: (i // (nt // MOE_PARTS), 0, 0)),
        ],
        out_shape=out_shapes,
        scratch_shapes=[pltpu.VMEM((N_EXPERTS, 1), F32)],
        compiler_params=pltpu.CompilerParams(dimension_semantics=("arbitrary",), vmem_limit_bytes=VMEM_LIMIT),
        name="route",
    )(x2, y_rec, y_att, wo_a, wo_b, ln_moe, w_router_t, e_bias_col, tri)


def _plan_kernel(pad_start_ref, ek_ref, rk_ref, dest_ref):
    ek = ek_ref[...]
    part = pl.program_id(0) // (pl.num_programs(0) // MOE_PARTS)

    def add_expert(e, base):
        return jnp.where(ek == e, pad_start_ref[part * N_EXPERTS + e], base)

    dest_ref[...] = rk_ref[...] + lax.fori_loop(0, N_EXPERTS, add_expert, jnp.zeros_like(ek))


def _plan(pad_start, ek, rk):
    kk, t = ek.shape
    tl = min(t // MOE_PARTS, 2048)
    col_map = lambda i, ps: (0, i)
    grid_spec = pltpu.PrefetchScalarGridSpec(
        num_scalar_prefetch=1,
        grid=(t // tl,),
        in_specs=[pl.BlockSpec((kk, tl), col_map), pl.BlockSpec((kk, tl), col_map)],
        out_specs=pl.BlockSpec((kk, tl), col_map),
    )
    return pl.pallas_call(
        _plan_kernel,
        grid_spec=grid_spec,
        out_shape=jax.ShapeDtypeStruct((kk, t), I32),
        compiler_params=pltpu.CompilerParams(dimension_semantics=("arbitrary",)),
        name="plan",
    )(pad_start, ek, rk)


def _sc_mesh():
    return plsc.VectorSubcoreMesh(core_axis_name="core", subcore_axis_name="subcore")


def _sc_dispatch(rows, dest, n_out, part):
    t, w = rows.shape
    kk = dest.shape[0]
    steps = t // MOE_PARTS // SC_WINDOW
    first = part * steps

    @pl.kernel(out_type=jax.ShapeDtypeStruct((n_out, w), rows.dtype), mesh=_sc_mesh(), scratch_types=[])
    def kern(x_hbm, i_hbm, o_hbm):
        def body(x_vmem, i_vmem):
            for k in range(kk):
                pltpu.sync_copy(x_vmem, o_hbm.at[i_vmem.at[k]])

        pltpu.emit_pipeline(
            body,
            grid=(steps,),
            in_specs=[pl.BlockSpec((SC_WINDOW, w), lambda i: (first + i, 0)),
                      pl.BlockSpec((kk, SC_WINDOW), lambda i: (0, first + i))],
            out_specs=[],
            core_axis_name=("core", "subcore"),
            dimension_semantics=(pltpu.PARALLEL,),
        )(x_hbm, i_hbm)

    return kern(rows, dest)


def _sc_combine(rows, dest, part):
    kk, t = dest.shape
    w = rows.shape[1]
    tp = t // MOE_PARTS
    flat = dest.reshape(1, kk * t)
    windows = tp // SC_WINDOW

    def index_block(i):
        return (0, (i // windows) * (t // SC_WINDOW) + part * windows + i % windows)

    @pl.kernel(out_type=jax.ShapeDtypeStruct((kk * tp, w), rows.dtype), mesh=_sc_mesh(), scratch_types=[])
    def kern(y_hbm, i_hbm, o_hbm):
        def body(i_vmem, o_vmem):
            pltpu.sync_copy(y_hbm.at[i_vmem.at[0]], o_vmem)

        pltpu.emit_pipeline(
            body,
            grid=(kk * windows,),
            in_specs=[pl.BlockSpec((1, SC_WINDOW), index_block)],
            out_specs=[pl.BlockSpec((SC_WINDOW, w), lambda i: (i, 0))],
            core_axis_name=("core", "subcore"),
            dimension_semantics=(pltpu.PARALLEL,),
        )(i_hbm, o_hbm)

    return kern(rows, flat).reshape(kk, tp, w)


def _experts_kernel(blk_expert_ref, n_used_ref, first_ref, slot_ref, next_ref, xa_ref, xb_ref,
                    w1_hbm, w3_hbm, w2_hbm, ya_ref, yb_ref,
                    w1f_ref, w3f_ref, w2f_ref, w1b_ref, w3b_ref, w2b_ref, sem):
    b = pl.program_id(0)

    def weight_copies(e, s):
        return (pltpu.make_async_copy(w1_hbm.at[e], w1f_ref.at[s], sem.at[s, 0]),
                pltpu.make_async_copy(w3_hbm.at[e], w3f_ref.at[s], sem.at[s, 1]),
                pltpu.make_async_copy(w2_hbm.at[e], w2f_ref.at[s], sem.at[s, 2]))

    @pl.when(b == 0)
    def _():
        for copy in weight_copies(blk_expert_ref[0], 0):
            copy.start()

    @pl.when(first_ref[b] == 1)
    def _():
        s = slot_ref[b]
        for copy in weight_copies(blk_expert_ref[b], s):
            copy.wait()

        @pl.when(next_ref[b] >= 0)
        def _():
            for copy in weight_copies(next_ref[b], 1 - s):
                copy.start()

        w1b_ref[...] = w1f_ref[s].astype(BF16)
        w3b_ref[...] = w3f_ref[s].astype(BF16)
        w2b_ref[...] = w2f_ref[s].astype(BF16)

    @pl.when(b < n_used_ref[0])
    def _():
        x = _unpack_row(xa_ref[...], xb_ref[...]).astype(BF16)
        a = _dot(x, w1b_ref[...])
        g = _dot(x, w3b_ref[...])
        hmid = (jax.nn.silu(a) * g).astype(BF16)
        y = _dot(hmid, w2b_ref[...])
        pa, pb = _pack_row(y)
        ya_ref[...] = pa
        yb_ref[...] = pb


def _experts(xa, xb, w1, w3, w2, blk_expert, n_used, seg_first, seg_slot, seg_next):
    p = xa.shape[0]
    m = EXPERT_ROWS
    nblk = p // m
    row_map = lambda b, *_: (b, 0)
    hbm = pl.BlockSpec(memory_space=pl.ANY)
    grid_spec = pltpu.PrefetchScalarGridSpec(
        num_scalar_prefetch=5,
        grid=(nblk,),
        in_specs=[pl.BlockSpec((m, PACK_W), row_map), pl.BlockSpec((m, PACK_W), row_map), hbm, hbm, hbm],
        out_specs=[pl.BlockSpec((m, PACK_W), row_map), pl.BlockSpec((m, PACK_W), row_map)],
        scratch_shapes=[
            pltpu.VMEM((2, D_MODEL, D_EXPERT), F32), pltpu.VMEM((2, D_MODEL, D_EXPERT), F32),
            pltpu.VMEM((2, D_EXPERT, D_MODEL), F32),
            pltpu.VMEM((D_MODEL, D_EXPERT), BF16), pltpu.VMEM((D_MODEL, D_EXPERT), BF16),
            pltpu.VMEM((D_EXPERT, D_MODEL), BF16),
            pltpu.SemaphoreType.DMA((2, 3)),
        ],
    )
    return pl.pallas_call(
        _experts_kernel,
        grid_spec=grid_spec,
        out_shape=(jax.ShapeDtypeStruct((p, PACK_W), U32), jax.ShapeDtypeStruct((p, PACK_W), U32)),
        compiler_params=pltpu.CompilerParams(dimension_semantics=("arbitrary",), vmem_limit_bytes=VMEM_LIMIT),
        name="experts",
    )(blk_expert, n_used, seg_first, seg_slot, seg_next, xa, xb, w1, w3, w2)


def _tail_kernel(h1_ref, ga_ref, gb_ref, wk_ref, p_ref, lnmoe_ref, ws1_ref, ws3_ref, ws2_ref, lnple_ref,
                 wpg_ref, wpp_ref, lnf_ref, o_ref):
    h1 = h1_ref[...]
    hn = _rms(h1, lnmoe_ref[...]).astype(BF16)
    shared = _dot((jax.nn.silu(_dot(hn, ws1_ref[...])) * _dot(hn, ws3_ref[...])).astype(BF16), ws2_ref[...])
    wk = wk_ref[...]
    routed = jnp.zeros_like(h1)
    for kk in range(TOP_K):
        routed = routed + wk[:, kk:kk + 1] * _unpack_row(ga_ref[kk], gb_ref[kk])
    h2 = h1 + routed + shared
    gate = jax.nn.sigmoid(_dot(_rms(h2, lnple_ref[...]).astype(BF16), wpg_ref[...]))
    h3 = h2 + gate * _dot(p_ref[...].astype(BF16), wpp_ref[...])
    o_ref[...] = _rms(h3, lnf_ref[...])


def _tail_into_kernel(out_so_far_ref, *refs):
    del out_so_far_ref
    _tail_kernel(*refs)


def _tail(out_so_far, part, h1, ga, gb, wk_t, p2, ln_moe, ws1, ws3, ws2, ln_ple, w_pg, w_pp, ln_f):
    t = h1.shape[0]
    tm = TAIL_ROWS
    steps = t // MOE_PARTS // tm
    row_map = lambda i: (part * steps + i, 0)
    fixed = lambda i: (0, 0)
    g_map = lambda i: (0, i, 0)
    d_sh = ws1.shape[1]
    carried = () if out_so_far is None else (out_so_far,)
    return pl.pallas_call(
        _tail_kernel if out_so_far is None else _tail_into_kernel,
        grid=(steps,),
        input_output_aliases={} if out_so_far is None else {0: 0},
        in_specs=[pl.BlockSpec(memory_space=pl.ANY)] * len(carried) + [
            pl.BlockSpec((tm, D_MODEL), row_map),
            pl.BlockSpec((TOP_K, tm, PACK_W), g_map),
            pl.BlockSpec((TOP_K, tm, PACK_W), g_map),
            pl.BlockSpec((tm, TOP_K), row_map),
            pl.BlockSpec((tm, D_PLE), row_map),
            pl.BlockSpec((1, D_MODEL), fixed),
            pl.BlockSpec((D_MODEL, d_sh), fixed),
            pl.BlockSpec((D_MODEL, d_sh), fixed),
            pl.BlockSpec((d_sh, D_MODEL), fixed),
            pl.BlockSpec((1, D_MODEL), fixed),
            pl.BlockSpec((D_MODEL, D_MODEL), fixed),
            pl.BlockSpec((D_PLE, D_MODEL), fixed),
            pl.BlockSpec((1, D_MODEL), fixed),
        ],
        out_specs=pl.BlockSpec((tm, D_MODEL), row_map),
        out_shape=jax.ShapeDtypeStruct((t, D_MODEL), F32),
        compiler_params=pltpu.CompilerParams(dimension_semantics=("arbitrary",), vmem_limit_bytes=VMEM_LIMIT),
        name="tail",
    )(*carried, h1, ga, gb, wk_t, p2, ln_moe, ws1, ws3, ws2, ln_ple, w_pg, w_pp, ln_f)


def _rope_constants():
    half = ROPE_DIM // 2
    inv_freq = (ROPE_THETA ** (-jnp.arange(0, ROPE_DIM, 2, dtype=F32) / ROPE_DIM)).reshape(half, 1)
    f = lax.broadcasted_iota(I32, (ROPE_DIM, LANES), 0)
    l64 = lax.broadcasted_iota(I32, (ROPE_DIM, LANES), 1) % HALF_DIM
    cos_pat = ((f < half) & (l64 < ROPE_DIM) & (l64 % half == f)).astype(F32)
    sa_pat = -((f >= half) & (l64 < half) & (l64 == f - half)).astype(F32)
    sb_pat = ((f >= half) & (l64 >= half) & (l64 < ROPE_DIM) & (l64 - half == f - half)).astype(F32)
    return inv_freq, jnp.concatenate([cos_pat, sa_pat, sb_pat], axis=1)


def _block_diag_tiles(w):
    nb, bd, _ = w.shape
    per = nb // 2
    tiles = []
    for tix in range(2):
        rows = []
        for j in range(per):
            rows.append(jnp.concatenate(
                [w[tix * per + j] if c == j else jnp.zeros((bd, bd), w.dtype) for c in range(per)], axis=1))
        tiles.append(jnp.concatenate(rows, axis=0))
    return jnp.stack(tiles).astype(BF16)


def _layer(h, p_l, positions, lam_init, ln_mix, w_in, conv_w, conv_b, w_a, b_a, w_i, b_i, rg_lambda, g_rec,
           lq1, lk1, lq2, lk2, g_sub, w_out, ln_moe, w_router, e_bias, w1, w3, w2, ws1, ws3, ws2,
           ln_ple, w_ple_gate, w_ple_proj, ln_out):
    batch, seq, _ = h.shape
    t = batch * seq
    x2 = h.reshape(t, D_MODEL)
    row = lambda a: a.reshape(1, -1)
    inv_freq, rope_pat = _rope_constants()

    y_rec, q, k, vt = _mix_in(
        x2, positions.reshape(1, t), inv_freq, rope_pat, row(ln_mix), w_in.astype(BF16), conv_w, row(conv_b),
        _block_diag_tiles(w_a), row(b_a), _block_diag_tiles(w_i), row(b_i), row(rg_lambda), row(g_rec),
        batch, seq)
    y_att = _attention(q, k, vt, row(lq1), row(lk1), row(lq2), row(lk2), g_sub.reshape(-1, 1), batch, seq,
                       lam_init)

    w_out_b = w_out.astype(BF16)
    h1, hpa, hpb, ek, wk, rk, counts = _route(
        x2, y_rec, y_att, w_out_b[:D_REC], w_out_b[D_REC:], row(ln_moe), w_router.T, e_bias.reshape(-1, 1))

    m = EXPERT_ROWS
    counts = counts.reshape(MOE_PARTS, N_EXPERTS)
    padded = (counts + m - 1) // m * m
    pad_end = jnp.cumsum(padded, axis=1)
    pad_start = pad_end - padded
    n_rows = t // MOE_PARTS * TOP_K + N_EXPERTS * m
    nblk = n_rows // m
    n_used = (pad_end[:, -1] // m).astype(I32)
    blk = jnp.arange(nblk, dtype=I32)
    blk_row = jnp.minimum(blk[None, :], n_used[:, None] - 1) * m
    blk_expert = jnp.sum((pad_end[:, None, :] <= blk_row[:, :, None]).astype(I32), axis=2)
    prev_expert = jnp.concatenate([jnp.full((MOE_PARTS, 1), -1, I32), blk_expert[:, :-1]], axis=1)
    seg_first = ((blk[None, :] < n_used[:, None]) & (blk_expert != prev_expert)).astype(I32)
    seg_slot = ((jnp.cumsum(seg_first, axis=1) - 1) % 2).astype(I32)
    eid = jnp.arange(N_EXPERTS, dtype=I32)
    later = (padded[:, None, :] > 0) & (eid[None, None, :] > eid[None, :, None])
    next_expert = jnp.min(jnp.where(later, eid[None, None, :], N_EXPERTS), axis=2)
    next_expert = jnp.where(next_expert == N_EXPERTS, -1, next_expert).astype(I32)
    seg_next = jnp.sum(jnp.where(blk_expert[:, :, None] == eid[None, None, :], next_expert[:, None, :], 0), axis=2)
    dest = _plan(pad_start.astype(I32).reshape(-1), ek, rk)

    wk_t = wk.T
    p2 = p_l.reshape(t, D_PLE)
    tail_weights = (row(ln_moe), ws1.astype(BF16), ws3.astype(BF16), ws2.astype(BF16), row(ln_ple),
                    w_ple_gate.astype(BF16), w_ple_proj.astype(BF16), row(ln_out))
    gathered = []
    for part in range(MOE_PARTS):
        xa = _sc_dispatch(hpa, dest, n_rows, part)
        xb = _sc_dispatch(hpb, dest, n_rows, part)
        ya, yb = _experts(xa, xb, w1, w3, w2, blk_expert[part], n_used[part:part + 1], seg_first[part],
                          seg_slot[part], seg_next[part])
        gathered.append((_sc_combine(ya, dest, part), _sc_combine(yb, dest, part)))
    out = None
    for part, (ga, gb) in enumerate(gathered):
        out = _tail(out, part, h1, ga, gb, wk_t, p2, *tail_weights)
    return out.reshape(batch, seq, D_MODEL)


def kernel(x, p, positions, ln_mix, w_in, conv_w, conv_b, w_a, b_a, w_i, b_i, rg_lambda, g_rec, lq1, lk1, lq2,
           lk2, g_sub, w_out, ln_moe, w_router, e_bias, w1, w3, w2, ws1, ws3, ws2, ln_ple, w_ple_gate,
           w_ple_proj, ln_f):
    depth = w_in.shape[0]
    assert depth == 1, "the fused tail applies the final norm; one layer supported"
    lam_init = 0.8 - 0.6 * math.exp(-0.3 * 0)
    return _layer(x, p[0], positions, lam_init, ln_mix[0], w_in[0], conv_w[0], conv_b[0], w_a[0], b_a[0], w_i[0],
                  b_i[0], rg_lambda[0], g_rec[0], lq1[0], lk1[0], lq2[0], lk2[0], g_sub[0], w_out[0], ln_moe[0],
                  w_router[0], e_bias[0], w1[0], w3[0], w2[0], ws1[0], ws3[0], ws2[0], ln_ple[0], w_ple_gate[0],
                  w_ple_proj[0], ln_f)
```

```python
import functools
import math

import jax
import jax.numpy as jnp
from jax import lax
from jax.experimental import pallas as pl
from jax.experimental.pallas import tpu as pltpu
from jax.experimental.pallas import tpu_sc as plsc

F32 = jnp.float32
BF16 = jnp.bfloat16
U32 = jnp.uint32
I32 = jnp.int32

D_MODEL = 1024
D_REC = 512
REC_BLOCKS = 8
CONV_WIDTH = 4
RG_C = 8.0
N_HEADS = 4
HALF_DIM = 64
V_DIM = 128
D_ATT = N_HEADS * V_DIM
D_QK = N_HEADS * 2 * HALF_DIM
ROPE_DIM = 16
ROPE_THETA = 500000.0
N_EXPERTS = 64
TOP_K = 8
N_GROUPS = 8
GROUP_SIZE = N_EXPERTS // N_GROUPS
TOPK_GROUPS = 4
D_EXPERT = 256
ROUTE_SCALE = 2.5
D_PLE = 256
EPS = 1e-6

LANES = 128
SUBLANES = 8
VMEM_LIMIT = 56 * 1024 * 1024

MIX_ROWS = 512
ATT_Q = 512
ATT_TAIL_LANES = 256
ONES_ROWS = 16
V_EXT = V_DIM + ONES_ROWS
ROUTE_ROWS = 512
EXPERT_ROWS = 512
MOE_PARTS = 1
TAIL_ROWS = 256
SC_WINDOW = 128
PACK_W = 256
NEG_BIG = -1e30


def _rms(x, g):
    return x * lax.rsqrt(jnp.mean(x * x, axis=-1, keepdims=True) + EPS) * g


def _dot(a, b):
    return jnp.dot(a, b, preferred_element_type=F32)


def _pack_pair(lo, hi):
    lo_bits = lax.bitcast_convert_type(lo.astype(BF16).astype(F32), U32)
    hi_bits = lax.bitcast_convert_type(hi.astype(BF16).astype(F32), U32)
    return (lo_bits >> 16) | (hi_bits & jnp.uint32(0xFFFF0000))


def _unpack_pair(p):
    lo = lax.bitcast_convert_type(p << 16, F32)
    hi = lax.bitcast_convert_type(p & jnp.uint32(0xFFFF0000), F32)
    return lo, hi


def _pack_row(x):
    w = PACK_W
    return _pack_pair(x[:, 0:w], x[:, w:2 * w]), _pack_pair(x[:, 2 * w:3 * w], x[:, 3 * w:4 * w])


def _unpack_row(pa, pb):
    c0, c1 = _unpack_pair(pa)
    c2, c3 = _unpack_pair(pb)
    return jnp.concatenate([c0, c1, c2, c3], axis=1)


def _shift_rows(a, s, fill, row):
    n, c = a.shape
    if s % SUBLANES == 0:
        return jnp.concatenate([jnp.full((s, c), fill, a.dtype), a[:n - s]], axis=0)
    return jnp.where(row >= s, pltpu.roll(a, s, 0), fill)


def _mix_in_kernel(x_ref, pos_ref, invf_ref, pat_ref, lnm_ref, win_ref, cw_ref, cb_ref, wa_ref, ba_ref,
                   wi_ref, bi_ref, lam_ref, grec_ref,
                   yrec_ref, q_ref, k_ref, vt_ref, tail_ref, hcarry_ref, buf_a, buf_b):
    tm = x_ref.shape[0]
    groups = tm // SUBLANES
    chunks = D_REC // LANES

    def stage(ref, v):
        for c in range(chunks):
            ref[c] = v[:, c * LANES:(c + 1) * LANES]

    def slab(ref, r):
        return jnp.concatenate([ref[c, pl.ds(r, groups, stride=SUBLANES), :] for c in range(chunks)], axis=1)

    @pl.when(pl.program_id(1) == 0)
    def _():
        tail_ref[...] = jnp.zeros_like(tail_ref)
        hcarry_ref[...] = jnp.zeros_like(hcarry_ref)

    hn = _rms(x_ref[...], lnm_ref[...]).astype(BF16)

    ang = invf_ref[...] * pos_ref[...].astype(F32)
    cs = jnp.concatenate([jnp.cos(ang), jnp.sin(ang)], axis=0)
    tabs = lax.dot_general(cs, pat_ref[...], (((0,), (0,)), ((), ())),
                           precision=lax.Precision.HIGHEST, preferred_element_type=F32)
    lane64 = lax.broadcasted_iota(I32, (1, LANES), 1) % HALF_DIM
    cosf = tabs[:, 0:LANES] + (lane64 >= ROPE_DIM).astype(F32)
    sa, sb = tabs[:, LANES:2 * LANES], tabs[:, 2 * LANES:3 * LANES]

    def project_rotary(out_ref, off, mul):
        for c in range(0, D_QK // LANES, 2):
            z2 = _dot(hn, win_ref[:, off + c * LANES: off + (c + 2) * LANES])
            for cc in range(2):
                zc = z2[:, cc * LANES:(cc + 1) * LANES]
                rot = (zc * cosf + pltpu.roll(zc, LANES - ROPE_DIM // 2, 1) * sa
                       + pltpu.roll(zc, ROPE_DIM // 2, 1) * sb)
                out_ref[:, (c + cc) * LANES:(c + cc + 1) * LANES] = (rot * mul).astype(BF16)

    def project_v():
        vt = _dot(hn, win_ref[:, 2 * D_REC + 2 * D_QK:]).T.astype(BF16)
        for hd in range(N_HEADS):
            vt_ref[0, hd * V_EXT:hd * V_EXT + V_DIM, :] = vt[hd * V_DIM:(hd + 1) * V_DIM]
            vt_ref[0, hd * V_EXT + V_DIM:(hd + 1) * V_EXT, :] = jnp.ones((ONES_ROWS, tm), BF16)

    xr = _dot(hn, win_ref[:, 0:D_REC])
    stage(buf_a, xr)
    stage(buf_b, _dot(hn, win_ref[:, D_REC:2 * D_REC]))
    tail = tail_ref[...]
    tail_ref[...] = xr[tm - SUBLANES:, :]

    grow = lax.broadcasted_iota(I32, (groups, D_REC), 0)

    def down_one(a, first_row):
        return jnp.where(grow == 0, first_row, pltpu.roll(a, 1, 0))

    xs = [slab(buf_a, r) for r in range(SUBLANES)]
    wrapped = {r: down_one(xs[r], tail[r:r + 1, :]) for r in range(SUBLANES - CONV_WIDTH + 1, SUBLANES)}
    xc = []
    for r in range(SUBLANES):
        acc = cb_ref[...] + cw_ref[CONV_WIDTH - 1:CONV_WIDTH, :] * xs[r]
        for d in range(1, CONV_WIDTH):
            prev = xs[r - d] if r >= d else wrapped[r - d + SUBLANES]
            acc = acc + cw_ref[CONV_WIDTH - 1 - d:CONV_WIDTH - d, :] * prev
        xc.append(acc)
    xc = jnp.concatenate(xc, axis=0)

    xcb = xc.astype(BF16)
    half = D_REC // 2
    ra = jnp.concatenate([_dot(xcb[:, :half], wa_ref[0]), _dot(xcb[:, half:], wa_ref[1])], axis=1)
    ri = jnp.concatenate([_dot(xcb[:, :half], wi_ref[0]), _dot(xcb[:, half:], wi_ref[1])], axis=1)
    r_gate = jax.nn.sigmoid(ra + ba_ref[...])
    i_gate = jax.nn.sigmoid(ri + bi_ref[...])
    lam = lam_ref[...]
    softplus_neg = jnp.maximum(-lam, 0.0) + jnp.log(1.0 + jnp.exp(-jnp.abs(lam)))
    log_a = -RG_C * r_gate * softplus_neg
    a = jnp.exp(log_a)
    u = jnp.sqrt(1.0 - jnp.exp(2.0 * log_a)) * i_gate * xc

    rows = lambda v, r: v[r * groups:(r + 1) * groups]
    hs, ps = [rows(u, 0)], [rows(a, 0)]
    for r in range(1, SUBLANES):
        hs.append(rows(a, r) * hs[-1] + rows(u, r))
        ps.append(rows(a, r) * ps[-1])
    tot_a, tot_h = ps[-1], hs[-1]
    s = 1
    while s < groups:
        tot_h = tot_h + tot_a * _shift_rows(tot_h, s, 0.0, grow)
        tot_a = tot_a * _shift_rows(tot_a, s, 1.0, grow)
        s *= 2
    h_in = hcarry_ref[...]
    group_end = tot_h + tot_a * h_in
    hcarry_ref[...] = group_end[groups - 1:groups, :]
    group_in = down_one(group_end, h_in)

    for r in range(SUBLANES):
        h = hs[r] + ps[r] * group_in
        y = h * jax.nn.gelu(slab(buf_b, r))
        yn = _rms(y, grec_ref[...])
        for c in range(chunks):
            buf_a[c, pl.ds(r, groups, stride=SUBLANES), :] = yn[:, c * LANES:(c + 1) * LANES]
    for c in range(chunks):
        yrec_ref[:, c * LANES:(c + 1) * LANES] = buf_a[c].astype(BF16)

    project_rotary(q_ref, 2 * D_REC, HALF_DIM ** -0.5 * math.log2(math.e))
    project_rotary(k_ref, 2 * D_REC + D_QK, 1.0)
    project_v()


def _mix_in(x2, pos_row, inv_freq, rope_pat, ln_mix, w_in, conv_w, conv_b, wa_bd, b_a, wi_bd, b_i, rg_lambda,
            g_rec, batch, seq):
    tm = MIX_ROWS
    nt = seq // tm
    d_in = w_in.shape[1]
    row_map = lambda b, i: (b * nt + i, 0)
    fixed2 = lambda b, i: (0, 0)
    fixed3 = lambda b, i: (0, 0, 0)
    t = batch * seq
    out_shapes = (
        jax.ShapeDtypeStruct((t, D_REC), BF16),
        jax.ShapeDtypeStruct((t, D_QK), BF16),
        jax.ShapeDtypeStruct((t, D_QK), BF16),
        jax.ShapeDtypeStruct((t // tm, N_HEADS * V_EXT, tm), BF16),
    )
    return pl.pallas_call(
        _mix_in_kernel,
        grid=(batch, nt),
        in_specs=[
            pl.BlockSpec((tm, D_MODEL), row_map),
            pl.BlockSpec((1, tm), lambda b, i: (0, b * nt + i)),
            pl.BlockSpec((ROPE_DIM // 2, 1), fixed2),
            pl.BlockSpec((ROPE_DIM, 3 * LANES), fixed2),
            pl.BlockSpec((1, D_MODEL), fixed2),
            pl.BlockSpec((D_MODEL, d_in), fixed2),
            pl.BlockSpec((CONV_WIDTH, D_REC), fixed2),
            pl.BlockSpec((1, D_REC), fixed2),
            pl.BlockSpec((2, D_REC // 2, D_REC // 2), fixed3),
            pl.BlockSpec((1, D_REC), fixed2),
            pl.BlockSpec((2, D_REC // 2, D_REC // 2), fixed3),
            pl.BlockSpec((1, D_REC), fixed2),
            pl.BlockSpec((1, D_REC), fixed2),
            pl.BlockSpec((1, D_REC), fixed2),
        ],
        out_specs=[
            pl.BlockSpec((tm, D_REC), row_map),
            pl.BlockSpec((tm, D_QK), row_map),
            pl.BlockSpec((tm, D_QK), row_map),
            pl.BlockSpec((1, N_HEADS * V_EXT, tm), lambda b, i: (b * nt + i, 0, 0)),
        ],
        out_shape=out_shapes,
        scratch_shapes=[pltpu.VMEM((SUBLANES, D_REC), F32), pltpu.VMEM((1, D_REC), F32),
                        pltpu.VMEM((D_REC // LANES, tm, LANES), F32),
                        pltpu.VMEM((D_REC // LANES, tm, LANES), F32)],
        compiler_params=pltpu.CompilerParams(
            dimension_semantics=("arbitrary", "arbitrary"), vmem_limit_bytes=VMEM_LIMIT),
        name="mix_in",
    )(x2, pos_row, inv_freq, rope_pat, ln_mix, w_in, conv_w, conv_b, wa_bd, b_a, wi_bd, b_i, rg_lambda, g_rec)


def _attn_kernel(lq1_ref, lk1_ref, lq2_ref, lk2_ref, gsub_ref, bias_ref, q_ref, k_ref, vt_ref, o_ref,
                 m_ref, acc_ref, aprev_ref, s0_ref, s1_ref, mb0_ref, mb1_ref, p0_ref, p1_ref, *, lam_init):
    tq = q_ref.shape[0]
    tk = vt_ref.shape[2]
    assert tq == tk, "the causal bias tile assumes the diagonal block is square"
    i = pl.program_id(2)

    qt = q_ref[...].astype(F32).T
    dim = lax.broadcasted_iota(I32, (LANES, tq), 0)
    qqt = jnp.concatenate([jnp.where(dim < HALF_DIM, qt, 0.0), jnp.where(dim >= HALF_DIM, qt, 0.0)],
                          axis=1).astype(BF16)

    n = (i * tq) // tk

    def scores(j):
        return _dot(k_ref[pl.ds(pl.multiple_of(j * tk, tk), tk), :], qqt)

    def probabilities(s, m_prev):
        m_new = jnp.maximum(m_prev, jnp.max(s, axis=0, keepdims=True))
        alpha = jnp.exp2(m_prev - m_new)
        p = jnp.exp2((s - m_new).astype(BF16))
        return p, alpha, m_new

    def store_scores(j, s_buf, mb_buf):
        s = scores(j)
        s_buf[...] = s
        mb_buf[...] = jnp.max(s, axis=0, keepdims=True)

    def pipe_step(j, cur, nxt, p_cur, p_prev):
        store_scores(j + 1, *nxt)
        s_cur, mb_cur = cur
        m_prev = m_ref[...]
        m_new = jnp.maximum(m_prev, mb_cur[...])
        p_cur[...] = jnp.exp2((s_cur[...] - m_new).astype(BF16))
        m_ref[...] = m_new
        acc_ref[...] = aprev_ref[...] * acc_ref[...] + _dot(vt_ref[jnp.maximum(j - 1, 0)], p_prev[...])
        aprev_ref[...] = jnp.exp2(m_prev - m_new)

    buf0, buf1 = (s0_ref, mb0_ref), (s1_ref, mb1_ref)
    m_ref[...] = jnp.full_like(m_ref, NEG_BIG)
    acc_ref[...] = jnp.zeros_like(acc_ref)
    aprev_ref[...] = jnp.ones_like(aprev_ref)
    odd = n % 2

    @pl.when(odd == 0)
    def _():
        p1_ref[...] = jnp.zeros_like(p1_ref)
        store_scores(0, *buf0)

    @pl.when(odd == 1)
    def _():
        p0_ref[...] = jnp.zeros_like(p0_ref)
        store_scores(0, *buf1)
        pipe_step(0, buf1, buf0, p1_ref, p0_ref)

    def pair(t, carry):
        j = 2 * t + odd
        pipe_step(j, buf0, buf1, p0_ref, p1_ref)
        pipe_step(j + 1, buf1, buf0, p1_ref, p0_ref)
        return carry

    lax.fori_loop(0, n // 2, pair, 0)

    parts = []
    for c0 in range(0, 2 * tq, ATT_TAIL_LANES):
        cols = slice(c0, c0 + ATT_TAIL_LANES)
        p, alpha, _ = probabilities(s0_ref[:, cols] + bias_ref[:, cols], m_ref[:, cols])
        part = aprev_ref[:, cols] * acc_ref[:, cols] + _dot(vt_ref[jnp.maximum(n - 1, 0)], p1_ref[:, cols])
        parts.append(alpha * part + _dot(vt_ref[n], p))
    acc = jnp.concatenate(parts, axis=1)

    lam = (jnp.exp(jnp.sum(lq1_ref[...] * lk1_ref[...], axis=-1, keepdims=True))
           - jnp.exp(jnp.sum(lq2_ref[...] * lk2_ref[...], axis=-1, keepdims=True)) + lam_init)
    o = acc[:V_DIM] / acc[V_DIM:V_DIM + 1]
    o = o[:, :tq] - lam * o[:, tq:]
    o = o * lax.rsqrt(jnp.mean(o * o, axis=0, keepdims=True) + EPS) * gsub_ref[...]
    o_ref[...] = (o * (1.0 - lam_init)).T.astype(BF16)


def _attention(q, k, vt, lq1, lk1, lq2, lk2, g_sub_col, batch, seq, lam_init):
    tq = ATT_Q
    nq = seq // tq
    tk = vt.shape[2]
    nk = seq // tk
    vec = lambda b, h, i: (0, 0)
    visible = (lax.broadcasted_iota(I32, (tk, 2 * tq), 0) <= lax.broadcasted_iota(I32, (tk, 2 * tq), 1) % tq)
    bias = jnp.where(visible, 0.0, NEG_BIG).astype(F32)
    return pl.pallas_call(
        functools.partial(_attn_kernel, lam_init=lam_init),
        grid=(batch, N_HEADS, nq),
        in_specs=[
            pl.BlockSpec((1, HALF_DIM), vec),
            pl.BlockSpec((1, HALF_DIM), vec),
            pl.BlockSpec((1, HALF_DIM), vec),
            pl.BlockSpec((1, HALF_DIM), vec),
            pl.BlockSpec((V_DIM, 1), vec),
            pl.BlockSpec((tk, 2 * tq), vec),
            pl.BlockSpec((tq, LANES), lambda b, h, i: (b * nq + i, h)),
            pl.BlockSpec((seq, LANES), lambda b, h, i: (b, h)),
            pl.BlockSpec((nk, V_EXT, tk), lambda b, h, i: (b, h, 0)),
        ],
        out_specs=pl.BlockSpec((tq, V_DIM), lambda b, h, i: (b * nq + i, h)),
        out_shape=jax.ShapeDtypeStruct((batch * seq, D_ATT), BF16),
        scratch_shapes=[pltpu.VMEM((1, 2 * tq), F32),
                        pltpu.VMEM((V_EXT, 2 * tq), F32), pltpu.VMEM((1, 2 * tq), F32),
                        pltpu.VMEM((tk, 2 * tq), F32), pltpu.VMEM((tk, 2 * tq), F32),
                        pltpu.VMEM((1, 2 * tq), F32), pltpu.VMEM((1, 2 * tq), F32),
                        pltpu.VMEM((tk, 2 * tq), BF16), pltpu.VMEM((tk, 2 * tq), BF16)],
        compiler_params=pltpu.CompilerParams(
            dimension_semantics=("arbitrary", "arbitrary", "arbitrary"), vmem_limit_bytes=VMEM_LIMIT),
        name="attention",
    )(lq1, lk1, lq2, lk2, g_sub_col, bias, q, k, vt)


def _sublane_total(x, op):
    return op(x, axis=0, keepdims=True)


def _route_kernel(x_ref, yrec_ref, yatt_ref, woa_ref, wob_ref, lnmoe_ref, wrt_ref, ebias_ref, tri_ref,
                  h1_ref, hpa_ref, hpb_ref, ek_ref, wk_ref, rk_ref, cnt_ref, carry_ref):
    tm = x_ref.shape[0]
    e_n = N_EXPERTS

    @pl.when(pl.program_id(0) % (pl.num_programs(0) // MOE_PARTS) == 0)
    def _():
        carry_ref[...] = jnp.zeros_like(carry_ref)

    h1 = x_ref[...] + _dot(yrec_ref[...], woa_ref[...]) + _dot(yatt_ref[...], wob_ref[...])
    h1_ref[...] = h1
    hn = _rms(h1, lnmoe_ref[...])
    pa, pb = _pack_row(hn)
    hpa_ref[...] = pa
    hpb_ref[...] = pb

    logits = lax.dot_general(wrt_ref[...], hn, (((1,), (1,)), ((), ())),
                             precision=lax.Precision.HIGHEST, preferred_element_type=F32)
    scores = jax.nn.sigmoid(logits)
    sel = scores + ebias_ref[...]

    sel3 = sel.reshape(N_GROUPS, GROUP_SIZE, tm)
    idx3 = lax.broadcasted_iota(I32, (N_GROUPS, GROUP_SIZE, tm), 1)
    m1 = jnp.max(sel3, axis=1, keepdims=True)
    first = jnp.min(jnp.where(sel3 == m1, idx3, GROUP_SIZE), axis=1, keepdims=True)
    m2 = jnp.max(jnp.where(idx3 == first, -jnp.inf, sel3), axis=1, keepdims=True)
    gscore = (m1 + m2).reshape(N_GROUPS, tm)

    gidx = lax.broadcasted_iota(I32, (N_GROUPS, tm), 0)
    beaten = jnp.zeros((N_GROUPS, tm), I32)
    for g in range(N_GROUPS):
        other = gscore[g:g + 1, :]
        beats = (other > gscore) | ((other == gscore) & (g < gidx))
        beaten = beaten + beats.astype(I32)
    gkeep = beaten < TOPK_GROUPS
    keep = jnp.broadcast_to(gkeep.reshape(N_GROUPS, 1, tm), (N_GROUPS, GROUP_SIZE, tm)).reshape(e_n, tm)
    selm = jnp.where(keep, sel, -jnp.inf)

    eidx = lax.broadcasted_iota(I32, (e_n, tm), 0)
    remaining = selm
    picks, ek, sk = [], [], []
    for _ in range(TOP_K):
        best = jnp.max(remaining, axis=0, keepdims=True)
        first = jnp.min(jnp.where(remaining == best, eidx, e_n), axis=0, keepdims=True)
        pick = eidx == first
        picks.append(pick)
        ek.append(first)
        sk.append(_sublane_total(jnp.where(pick, scores, 0.0), jnp.sum))
        remaining = jnp.where(pick, -jnp.inf, remaining)
    chosen_f = (remaining != selm).astype(F32)
    wsum = functools.reduce(lambda a, b: a + b, sk)
    ek_ref[...] = jnp.concatenate(ek, axis=0)
    wk_ref[...] = jnp.concatenate(sk, axis=0) * (ROUTE_SCALE / wsum)

    prefix = _dot(chosen_f.astype(BF16), tri_ref[...])
    rank = prefix + carry_ref[...]
    carry_new = carry_ref[...] + jnp.sum(chosen_f, axis=1, keepdims=True)
    carry_ref[...] = carry_new
    cnt_ref[0] = carry_new.astype(I32)
    rk = [_sublane_total(jnp.where(pick, rank, 0.0), jnp.sum) for pick in picks]
    rk_ref[...] = jnp.concatenate(rk, axis=0).astype(I32)


def _route(x2, y_rec, y_att, wo_a, wo_b, ln_moe, w_router_t, e_bias_col):
    t = x2.shape[0]
    tm = ROUTE_ROWS
    nt = t // tm
    row_map = lambda i: (i, 0)
    col_map = lambda i: (0, i)
    fixed = lambda i: (0, 0)
    tri = (lax.broadcasted_iota(I32, (tm, tm), 0) < lax.broadcasted_iota(I32, (tm, tm), 1)).astype(BF16)
    out_shapes = (
        jax.ShapeDtypeStruct((t, D_MODEL), F32),
        jax.ShapeDtypeStruct((t, PACK_W), U32),
        jax.ShapeDtypeStruct((t, PACK_W), U32),
        jax.ShapeDtypeStruct((TOP_K, t), I32),
        jax.ShapeDtypeStruct((TOP_K, t), F32),
        jax.ShapeDtypeStruct((TOP_K, t), I32),
        jax.ShapeDtypeStruct((MOE_PARTS, N_EXPERTS, 1), I32),
    )
    assert nt % MOE_PARTS == 0
    return pl.pallas_call(
        _route_kernel,
        grid=(nt,),
        in_specs=[
            pl.BlockSpec((tm, D_MODEL), row_map),
            pl.BlockSpec((tm, D_REC), row_map),
            pl.BlockSpec((tm, D_ATT), row_map),
            pl.BlockSpec((D_REC, D_MODEL), fixed),
            pl.BlockSpec((D_ATT, D_MODEL), fixed),
            pl.BlockSpec((1, D_MODEL), fixed),
            pl.BlockSpec((N_EXPERTS, D_MODEL), fixed),
            pl.BlockSpec((N_EXPERTS, 1), fixed),
            pl.BlockSpec((tm, tm), fixed),
        ],
        out_specs=[
            pl.BlockSpec((tm, D_MODEL), row_map),
            pl.BlockSpec((tm, PACK_W), row_map),
            pl.BlockSpec((tm, PACK_W), row_map),
            pl.BlockSpec((TOP_K, tm), col_map),
            pl.BlockSpec((TOP_K, tm), col_map),
            pl.BlockSpec((TOP_K, tm), col_map),
            pl.BlockSpec((1, N_EXPERTS, 1), lambda i: (i // (nt // MOE_PARTS), 0, 0)),
        ],
        out_shape=out_shapes,
        scratch_shapes=[pltpu.VMEM((N_EXPERTS, 1), F32)],
        compiler_params=pltpu.CompilerParams(dimension_semantics=("arbitrary",), vmem_limit_bytes=VMEM_LIMIT),
        name="route",
    )(x2, y_rec, y_att, wo_a, wo_b, ln_moe, w_router_t, e_bias_col, tri)


def _plan_kernel(pad_start_ref, ek_ref, rk_ref, dest_ref):
    ek = ek_ref[...]
    part = pl.program_id(0) // (pl.num_programs(0) // MOE_PARTS)

    def add_expert(e, base):
        return jnp.where(ek == e, pad_start_ref[part * N_EXPERTS + e], base)

    dest_ref[...] = rk_ref[...] + lax.fori_loop(0, N_EXPERTS, add_expert, jnp.zeros_like(ek))


def _plan(pad_start, ek, rk):
    kk, t = ek.shape
    tl = min(t // MOE_PARTS, 2048)
    col_map = lambda i, ps: (0, i)
    grid_spec = pltpu.PrefetchScalarGridSpec(
        num_scalar_prefetch=1,
        grid=(t // tl,),
        in_specs=[pl.BlockSpec((kk, tl), col_map), pl.BlockSpec((kk, tl), col_map)],
        out_specs=pl.BlockSpec((kk, tl), col_map),
    )
    return pl.pallas_call(
        _plan_kernel,
        grid_spec=grid_spec,
        out_shape=jax.ShapeDtypeStruct((kk, t), I32),
        compiler_params=pltpu.CompilerParams(dimension_semantics=("arbitrary",)),
        name="plan",
    )(pad_start, ek, rk)


def _sc_mesh():
    return plsc.VectorSubcoreMesh(core_axis_name="core", subcore_axis_name="subcore")


def _sc_dispatch(rows, dest, n_out, part):
    t, w = rows.shape
    kk = dest.shape[0]
    steps = t // MOE_PARTS // SC_WINDOW
    first = part * steps

    @pl.kernel(out_type=jax.ShapeDtypeStruct((n_out, w), rows.dtype), mesh=_sc_mesh(), scratch_types=[])
    def kern(x_hbm, i_hbm, o_hbm):
        def body(x_vmem, i_vmem):
            for k in range(kk):
                pltpu.sync_copy(x_vmem, o_hbm.at[i_vmem.at[k]])

        pltpu.emit_pipeline(
            body,
            grid=(steps,),
            in_specs=[pl.BlockSpec((SC_WINDOW, w), lambda i: (first + i, 0)),
                      pl.BlockSpec((kk, SC_WINDOW), lambda i: (0, first + i))],
            out_specs=[],
            core_axis_name=("core", "subcore"),
            dimension_semantics=(pltpu.PARALLEL,),
        )(x_hbm, i_hbm)

    return kern(rows, dest)


def _sc_combine(rows, dest, part):
    kk, t = dest.shape
    w = rows.shape[1]
    tp = t // MOE_PARTS
    flat = dest.reshape(1, kk * t)
    windows = tp // SC_WINDOW

    def index_block(i):
        return (0, (i // windows) * (t // SC_WINDOW) + part * windows + i % windows)

    @pl.kernel(out_type=jax.ShapeDtypeStruct((kk * tp, w), rows.dtype), mesh=_sc_mesh(), scratch_types=[])
    def kern(y_hbm, i_hbm, o_hbm):
        def body(i_vmem, o_vmem):
            pltpu.sync_copy(y_hbm.at[i_vmem.at[0]], o_vmem)

        pltpu.emit_pipeline(
            body,
            grid=(kk * windows,),
            in_specs=[pl.BlockSpec((1, SC_WINDOW), index_block)],
            out_specs=[pl.BlockSpec((SC_WINDOW, w), lambda i: (i, 0))],
            core_axis_name=("core", "subcore"),
            dimension_semantics=(pltpu.PARALLEL,),
        )(i_hbm, o_hbm)

    return kern(rows, flat).reshape(kk, tp, w)


X_SLOTS = 3
Y_SLOTS = 2


def _experts_kernel(blk_expert_ref, n_used_ref, first_ref, slot_ref, next_ref,
                    xa_hbm, xb_hbm, w1_hbm, w3_hbm, w2_hbm, ya_hbm, yb_hbm,
                    xa_buf, xb_buf, ya_buf, yb_buf, w1f_ref, w3f_ref, w2f_ref, w1b_ref, w3b_ref, w2b_ref,
                    wsem, xsem, ysem):
    m = EXPERT_ROWS
    n_used = n_used_ref[0]

    def weight_copies(e, s):
        return (pltpu.make_async_copy(w1_hbm.at[e], w1f_ref.at[s], wsem.at[s, 0]),
                pltpu.make_async_copy(w3_hbm.at[e], w3f_ref.at[s], wsem.at[s, 1]),
                pltpu.make_async_copy(w2_hbm.at[e], w2f_ref.at[s], wsem.at[s, 2]))

    def x_copies(b):
        rows, s = pl.ds(pl.multiple_of(b * m, m), m), b % X_SLOTS
        return (pltpu.make_async_copy(xa_hbm.at[rows], xa_buf.at[s], xsem.at[s, 0]),
                pltpu.make_async_copy(xb_hbm.at[rows], xb_buf.at[s], xsem.at[s, 1]))

    def y_copies(b):
        rows, s = pl.ds(pl.multiple_of(b * m, m), m), b % Y_SLOTS
        return (pltpu.make_async_copy(ya_buf.at[s], ya_hbm.at[rows], ysem.at[s, 0]),
                pltpu.make_async_copy(yb_buf.at[s], yb_hbm.at[rows], ysem.at[s, 1]))

    def start(copies):
        for copy in copies:
            copy.start()

    def wait(copies):
        for copy in copies:
            copy.wait()

    start(weight_copies(blk_expert_ref[0], 0))
    start(x_copies(0))

    @pl.when(n_used > 1)
    def _():
        start(x_copies(1))

    def block(b, carry):
        @pl.when(b + 2 < n_used)
        def _():
            start(x_copies(b + 2))

        @pl.when(first_ref[b] == 1)
        def _():
            s = slot_ref[b]
            wait(weight_copies(blk_expert_ref[b], s))

            @pl.when(next_ref[b] >= 0)
            def _():
                start(weight_copies(next_ref[b], 1 - s))

            w1b_ref[...] = w1f_ref[s].astype(BF16)
            w3b_ref[...] = w3f_ref[s].astype(BF16)
            w2b_ref[...] = w2f_ref[s].astype(BF16)

        wait(x_copies(b))

        @pl.when(b >= Y_SLOTS)
        def _():
            wait(y_copies(b - Y_SLOTS))

        x = _unpack_row(xa_buf[b % X_SLOTS], xb_buf[b % X_SLOTS]).astype(BF16)
        a = _dot(x, w1b_ref[...])
        g = _dot(x, w3b_ref[...])
        hmid = (jax.nn.silu(a) * g).astype(BF16)
        y = _dot(hmid, w2b_ref[...])
        pa, pb = _pack_row(y)
        ya_buf[b % Y_SLOTS] = pa
        yb_buf[b % Y_SLOTS] = pb
        start(y_copies(b))
        return carry

    lax.fori_loop(0, n_used, block, 0)

    for back in range(Y_SLOTS, 0, -1):
        @pl.when(n_used - back >= 0)
        def _(back=back):
            wait(y_copies(n_used - back))


def _experts(xa, xb, w1, w3, w2, blk_expert, n_used, seg_first, seg_slot, seg_next):
    p = xa.shape[0]
    m = EXPERT_ROWS
    hbm = pl.BlockSpec(memory_space=pl.ANY)
    grid_spec = pltpu.PrefetchScalarGridSpec(
        num_scalar_prefetch=5,
        grid=(1,),
        in_specs=[hbm] * 5,
        out_specs=[hbm, hbm],
        scratch_shapes=[
            pltpu.VMEM((X_SLOTS, m, PACK_W), U32), pltpu.VMEM((X_SLOTS, m, PACK_W), U32),
            pltpu.VMEM((Y_SLOTS, m, PACK_W), U32), pltpu.VMEM((Y_SLOTS, m, PACK_W), U32),
            pltpu.VMEM((2, D_MODEL, D_EXPERT), F32), pltpu.VMEM((2, D_MODEL, D_EXPERT), F32),
            pltpu.VMEM((2, D_EXPERT, D_MODEL), F32),
            pltpu.VMEM((D_MODEL, D_EXPERT), BF16), pltpu.VMEM((D_MODEL, D_EXPERT), BF16),
            pltpu.VMEM((D_EXPERT, D_MODEL), BF16),
            pltpu.SemaphoreType.DMA((2, 3)), pltpu.SemaphoreType.DMA((X_SLOTS, 2)),
            pltpu.SemaphoreType.DMA((Y_SLOTS, 2)),
        ],
    )
    return pl.pallas_call(
        _experts_kernel,
        grid_spec=grid_spec,
        out_shape=(jax.ShapeDtypeStruct((p, PACK_W), U32), jax.ShapeDtypeStruct((p, PACK_W), U32)),
        compiler_params=pltpu.CompilerParams(dimension_semantics=("arbitrary",), vmem_limit_bytes=VMEM_LIMIT),
        name="experts",
    )(blk_expert, n_used, seg_first, seg_slot, seg_next, xa, xb, w1, w3, w2)


def _tail_kernel(h1_ref, ga_ref, gb_ref, wk_ref, p_ref, lnmoe_ref, ws1_ref, ws3_ref, ws2_ref, lnple_ref,
                 wpg_ref, wpp_ref, lnf_ref, o_ref):
    h1 = h1_ref[...]
    hn = _rms(h1, lnmoe_ref[...]).astype(BF16)
    shared = _dot((jax.nn.silu(_dot(hn, ws1_ref[...])) * _dot(hn, ws3_ref[...])).astype(BF16), ws2_ref[...])
    wk = wk_ref[...]
    routed = jnp.zeros_like(h1)
    for kk in range(TOP_K):
        routed = routed + wk[:, kk:kk + 1] * _unpack_row(ga_ref[kk], gb_ref[kk])
    h2 = h1 + routed + shared
    gate = jax.nn.sigmoid(_dot(_rms(h2, lnple_ref[...]).astype(BF16), wpg_ref[...]))
    h3 = h2 + gate * _dot(p_ref[...].astype(BF16), wpp_ref[...])
    o_ref[...] = _rms(h3, lnf_ref[...])


def _tail_into_kernel(out_so_far_ref, *refs):
    del out_so_far_ref
    _tail_kernel(*refs)


def _tail(out_so_far, part, h1, ga, gb, wk_t, p2, ln_moe, ws1, ws3, ws2, ln_ple, w_pg, w_pp, ln_f):
    t = h1.shape[0]
    tm = TAIL_ROWS
    steps = t // MOE_PARTS // tm
    row_map = lambda i: (part * steps + i, 0)
    fixed = lambda i: (0, 0)
    g_map = lambda i: (0, i, 0)
    d_sh = ws1.shape[1]
    carried = () if out_so_far is None else (out_so_far,)
    return pl.pallas_call(
        _tail_kernel if out_so_far is None else _tail_into_kernel,
        grid=(steps,),
        input_output_aliases={} if out_so_far is None else {0: 0},
        in_specs=[pl.BlockSpec(memory_space=pl.ANY)] * len(carried) + [
            pl.BlockSpec((tm, D_MODEL), row_map),
            pl.BlockSpec((TOP_K, tm, PACK_W), g_map),
            pl.BlockSpec((TOP_K, tm, PACK_W), g_map),
            pl.BlockSpec((tm, TOP_K), row_map),
            pl.BlockSpec((tm, D_PLE), row_map),
            pl.BlockSpec((1, D_MODEL), fixed),
            pl.BlockSpec((D_MODEL, d_sh), fixed),
            pl.BlockSpec((D_MODEL, d_sh), fixed),
            pl.BlockSpec((d_sh, D_MODEL), fixed),
            pl.BlockSpec((1, D_MODEL), fixed),
            pl.BlockSpec((D_MODEL, D_MODEL), fixed),
            pl.BlockSpec((D_PLE, D_MODEL), fixed),
            pl.BlockSpec((1, D_MODEL), fixed),
        ],
        out_specs=pl.BlockSpec((tm, D_MODEL), row_map),
        out_shape=jax.ShapeDtypeStruct((t, D_MODEL), F32),
        compiler_params=pltpu.CompilerParams(dimension_semantics=("arbitrary",), vmem_limit_bytes=VMEM_LIMIT),
        name="tail",
    )(*carried, h1, ga, gb, wk_t, p2, ln_moe, ws1, ws3, ws2, ln_ple, w_pg, w_pp, ln_f)


def _rope_constants():
    half = ROPE_DIM // 2
    inv_freq = (ROPE_THETA ** (-jnp.arange(0, ROPE_DIM, 2, dtype=F32) / ROPE_DIM)).reshape(half, 1)
    f = lax.broadcasted_iota(I32, (ROPE_DIM, LANES), 0)
    l64 = lax.broadcasted_iota(I32, (ROPE_DIM, LANES), 1) % HALF_DIM
    cos_pat = ((f < half) & (l64 < ROPE_DIM) & (l64 % half == f)).astype(F32)
    sa_pat = -((f >= half) & (l64 < half) & (l64 == f - half)).astype(F32)
    sb_pat = ((f >= half) & (l64 >= half) & (l64 < ROPE_DIM) & (l64 - half == f - half)).astype(F32)
    return inv_freq, jnp.concatenate([cos_pat, sa_pat, sb_pat], axis=1)


def _block_diag_tiles(w):
    nb, bd, _ = w.shape
    per = nb // 2
    tiles = []
    for tix in range(2):
        rows = []
        for j in range(per):
            rows.append(jnp.concatenate(
                [w[tix * per + j] if c == j else jnp.zeros((bd, bd), w.dtype) for c in range(per)], axis=1))
        tiles.append(jnp.concatenate(rows, axis=0))
    return jnp.stack(tiles).astype(BF16)


def _layer(h, p_l, positions, lam_init, ln_mix, w_in, conv_w, conv_b, w_a, b_a, w_i, b_i, rg_lambda, g_rec,
           lq1, lk1, lq2, lk2, g_sub, w_out, ln_moe, w_router, e_bias, w1, w3, w2, ws1, ws3, ws2,
           ln_ple, w_ple_gate, w_ple_proj, ln_out):
    batch, seq, _ = h.shape
    t = batch * seq
    x2 = h.reshape(t, D_MODEL)
    row = lambda a: a.reshape(1, -1)
    inv_freq, rope_pat = _rope_constants()

    y_rec, q, k, vt = _mix_in(
        x2, positions.reshape(1, t), inv_freq, rope_pat, row(ln_mix), w_in.astype(BF16), conv_w, row(conv_b),
        _block_diag_tiles(w_a), row(b_a), _block_diag_tiles(w_i), row(b_i), row(rg_lambda), row(g_rec),
        batch, seq)
    y_att = _attention(q, k, vt, row(lq1), row(lk1), row(lq2), row(lk2), g_sub.reshape(-1, 1), batch, seq,
                       lam_init)

    w_out_b = w_out.astype(BF16)
    h1, hpa, hpb, ek, wk, rk, counts = _route(
        x2, y_rec, y_att, w_out_b[:D_REC], w_out_b[D_REC:], row(ln_moe), w_router.T, e_bias.reshape(-1, 1))

    m = EXPERT_ROWS
    counts = counts.reshape(MOE_PARTS, N_EXPERTS)
    padded = (counts + m - 1) // m * m
    pad_end = jnp.cumsum(padded, axis=1)
    pad_start = pad_end - padded
    n_rows = t // MOE_PARTS * TOP_K + N_EXPERTS * m
    nblk = n_rows // m
    n_used = (pad_end[:, -1] // m).astype(I32)
    blk = jnp.arange(nblk, dtype=I32)
    blk_row = jnp.minimum(blk[None, :], n_used[:, None] - 1) * m
    blk_expert = jnp.sum((pad_end[:, None, :] <= blk_row[:, :, None]).astype(I32), axis=2)
    prev_expert = jnp.concatenate([jnp.full((MOE_PARTS, 1), -1, I32), blk_expert[:, :-1]], axis=1)
    seg_first = ((blk[None, :] < n_used[:, None]) & (blk_expert != prev_expert)).astype(I32)
    seg_slot = ((jnp.cumsum(seg_first, axis=1) - 1) % 2).astype(I32)
    eid = jnp.arange(N_EXPERTS, dtype=I32)
    later = (padded[:, None, :] > 0) & (eid[None, None, :] > eid[None, :, None])
    next_expert = jnp.min(jnp.where(later, eid[None, None, :], N_EXPERTS), axis=2)
    next_expert = jnp.where(next_expert == N_EXPERTS, -1, next_expert).astype(I32)
    seg_next = jnp.sum(jnp.where(blk_expert[:, :, None] == eid[None, None, :], next_expert[:, None, :], 0), axis=2)
    dest = _plan(pad_start.astype(I32).reshape(-1), ek, rk)

    wk_t = wk.T
    p2 = p_l.reshape(t, D_PLE)
    tail_weights = (row(ln_moe), ws1.astype(BF16), ws3.astype(BF16), ws2.astype(BF16), row(ln_ple),
                    w_ple_gate.astype(BF16), w_ple_proj.astype(BF16), row(ln_out))
    gathered = []
    for part in range(MOE_PARTS):
        xa = _sc_dispatch(hpa, dest, n_rows, part)
        xb = _sc_dispatch(hpb, dest, n_rows, part)
        ya, yb = _experts(xa, xb, w1, w3, w2, blk_expert[part], n_used[part:part + 1], seg_first[part],
                          seg_slot[part], seg_next[part])
        gathered.append((_sc_combine(ya, dest, part), _sc_combine(yb, dest, part)))
    out = None
    for part, (ga, gb) in enumerate(gathered):
        out = _tail(out, part, h1, ga, gb, wk_t, p2, *tail_weights)
    return out.reshape(batch, seq, D_MODEL)


def kernel(x, p, positions, ln_mix, w_in, conv_w, conv_b, w_a, b_a, w_i, b_i, rg_lambda, g_rec, lq1, lk1, lq2,
           lk2, g_sub, w_out, ln_moe, w_router, e_bias, w1, w3, w2, ws1, ws3, ws2, ln_ple, w_ple_gate,
           w_ple_proj, ln_f):
    depth = w_in.shape[0]
    assert depth == 1, "the fused tail applies the final norm; one layer supported"
    lam_init = 0.8 - 0.6 * math.exp(-0.3 * 0)
    return _layer(x, p[0], positions, lam_init, ln_mix[0], w_in[0], conv_w[0], conv_b[0], w_a[0], b_a[0], w_i[0],
                  b_i[0], rg_lambda[0], g_rec[0], lq1[0], lk1[0], lq2[0], lk2[0], g_sub[0], w_out[0], ln_moe[0],
                  w_router[0], e_bias[0], w1[0], w3[0], w2[0], ws1[0], ws3[0], ws2[0], ln_ple[0], w_ple_gate[0],
                  w_ple_proj[0], ln_f)
```

```python
import functools
import math

import jax
import jax.numpy as jnp
from jax import lax
from jax.experimental import pallas as pl
from jax.experimental.pallas import tpu as pltpu
from jax.experimental.pallas import tpu_sc as plsc

F32 = jnp.float32
BF16 = jnp.bfloat16
U32 = jnp.uint32
I32 = jnp.int32

D_MODEL = 1024
D_REC = 512
REC_BLOCKS = 8
CONV_WIDTH = 4
RG_C = 8.0
N_HEADS = 4
HALF_DIM = 64
V_DIM = 128
D_ATT = N_HEADS * V_DIM
D_QK = N_HEADS * 2 * HALF_DIM
ROPE_DIM = 16
ROPE_THETA = 500000.0
N_EXPERTS = 64
TOP_K = 8
N_GROUPS = 8
GROUP_SIZE = N_EXPERTS // N_GROUPS
TOPK_GROUPS = 4
D_EXPERT = 256
ROUTE_SCALE = 2.5
D_PLE = 256
EPS = 1e-6

LANES = 128
SUBLANES = 8
VMEM_LIMIT = 56 * 1024 * 1024

MIX_ROWS = 512
ATT_Q = 512
ATT_TAIL_LANES = 256
ONES_ROWS = 16
V_EXT = V_DIM + ONES_ROWS
ROUTE_ROWS = 1024
EXPERT_ROWS = 512
MOE_PARTS = 1
TAIL_ROWS = 512
SC_WINDOW = 128
PACK_W = 256
NEG_BIG = -1e30


def _rms(x, g):
    return x * lax.rsqrt(jnp.mean(x * x, axis=-1, keepdims=True) + EPS) * g


def _dot(a, b):
    return jnp.dot(a, b, preferred_element_type=F32)


def _pack_pair(lo, hi):
    lo_bits = lax.bitcast_convert_type(lo.astype(BF16).astype(F32), U32)
    hi_bits = lax.bitcast_convert_type(hi.astype(BF16).astype(F32), U32)
    return (lo_bits >> 16) | (hi_bits & jnp.uint32(0xFFFF0000))


def _unpack_pair(p):
    lo = lax.bitcast_convert_type(p << 16, F32)
    hi = lax.bitcast_convert_type(p & jnp.uint32(0xFFFF0000), F32)
    return lo, hi


def _pack_row(x):
    w = PACK_W
    return _pack_pair(x[:, 0:w], x[:, w:2 * w]), _pack_pair(x[:, 2 * w:3 * w], x[:, 3 * w:4 * w])


def _unpack_row(pa, pb):
    c0, c1 = _unpack_pair(pa)
    c2, c3 = _unpack_pair(pb)
    return jnp.concatenate([c0, c1, c2, c3], axis=1)


def _shift_rows(a, s, fill, row):
    n, c = a.shape
    if s % SUBLANES == 0:
        return jnp.concatenate([jnp.full((s, c), fill, a.dtype), a[:n - s]], axis=0)
    return jnp.where(row >= s, pltpu.roll(a, s, 0), fill)


def _mix_in_kernel(x_ref, pos_ref, invf_ref, pat_ref, lnm_ref, win32_ref, cw_ref, cb_ref, wa_ref, ba_ref,
                   wi_ref, bi_ref, lam_ref, grec_ref,
                   yrec_ref, q_ref, k_ref, vt_ref, tail_ref, hcarry_ref, buf_a, buf_b, win_ref):
    tm = x_ref.shape[0]
    groups = tm // SUBLANES
    chunks = D_REC // LANES

    def stage(ref, v):
        for c in range(chunks):
            ref[c] = v[:, c * LANES:(c + 1) * LANES]

    def slab(ref, r):
        return jnp.concatenate([ref[c, pl.ds(r, groups, stride=SUBLANES), :] for c in range(chunks)], axis=1)

    @pl.when((pl.program_id(0) == 0) & (pl.program_id(1) == 0))
    def _():
        win_ref[...] = win32_ref[...].astype(BF16)

    @pl.when(pl.program_id(1) == 0)
    def _():
        tail_ref[...] = jnp.zeros_like(tail_ref)
        hcarry_ref[...] = jnp.zeros_like(hcarry_ref)

    hn = _rms(x_ref[...], lnm_ref[...]).astype(BF16)

    ang = invf_ref[...] * pos_ref[...].astype(F32)
    cs = jnp.concatenate([jnp.cos(ang), jnp.sin(ang)], axis=0)
    tabs = lax.dot_general(cs, pat_ref[...], (((0,), (0,)), ((), ())),
                           precision=lax.Precision.HIGHEST, preferred_element_type=F32)
    lane64 = lax.broadcasted_iota(I32, (1, LANES), 1) % HALF_DIM
    cosf = tabs[:, 0:LANES] + (lane64 >= ROPE_DIM).astype(F32)
    sa, sb = tabs[:, LANES:2 * LANES], tabs[:, 2 * LANES:3 * LANES]

    def project_rotary(out_ref, off, mul):
        for c in range(0, D_QK // LANES, 2):
            z2 = _dot(hn, win_ref[:, off + c * LANES: off + (c + 2) * LANES])
            for cc in range(2):
                zc = z2[:, cc * LANES:(cc + 1) * LANES]
                rot = (zc * cosf + pltpu.roll(zc, LANES - ROPE_DIM // 2, 1) * sa
                       + pltpu.roll(zc, ROPE_DIM // 2, 1) * sb)
                out_ref[:, (c + cc) * LANES:(c + cc + 1) * LANES] = (rot * mul).astype(BF16)

    def project_v():
        vt = _dot(hn, win_ref[:, 2 * D_REC + 2 * D_QK:]).T.astype(BF16)
        for hd in range(N_HEADS):
            vt_ref[0, hd * V_EXT:hd * V_EXT + V_DIM, :] = vt[hd * V_DIM:(hd + 1) * V_DIM]
            vt_ref[0, hd * V_EXT + V_DIM:(hd + 1) * V_EXT, :] = jnp.ones((ONES_ROWS, tm), BF16)

    xr = _dot(hn, win_ref[:, 0:D_REC])
    stage(buf_a, xr)
    stage(buf_b, _dot(hn, win_ref[:, D_REC:2 * D_REC]))
    tail = tail_ref[...]
    tail_ref[...] = xr[tm - SUBLANES:, :]

    grow = lax.broadcasted_iota(I32, (groups, D_REC), 0)

    def down_one(a, first_row):
        return jnp.where(grow == 0, first_row, pltpu.roll(a, 1, 0))

    xs = [slab(buf_a, r) for r in range(SUBLANES)]
    wrapped = {r: down_one(xs[r], tail[r:r + 1, :]) for r in range(SUBLANES - CONV_WIDTH + 1, SUBLANES)}
    xc = []
    for r in range(SUBLANES):
        acc = cb_ref[...] + cw_ref[CONV_WIDTH - 1:CONV_WIDTH, :] * xs[r]
        for d in range(1, CONV_WIDTH):
            prev = xs[r - d] if r >= d else wrapped[r - d + SUBLANES]
            acc = acc + cw_ref[CONV_WIDTH - 1 - d:CONV_WIDTH - d, :] * prev
        xc.append(acc)
    xc = jnp.concatenate(xc, axis=0)

    xcb = xc.astype(BF16)
    half = D_REC // 2
    ra = jnp.concatenate([_dot(xcb[:, :half], wa_ref[0]), _dot(xcb[:, half:], wa_ref[1])], axis=1)
    ri = jnp.concatenate([_dot(xcb[:, :half], wi_ref[0]), _dot(xcb[:, half:], wi_ref[1])], axis=1)
    r_gate = jax.nn.sigmoid(ra + ba_ref[...])
    i_gate = jax.nn.sigmoid(ri + bi_ref[...])
    lam = lam_ref[...]
    softplus_neg = jnp.maximum(-lam, 0.0) + jnp.log(1.0 + jnp.exp(-jnp.abs(lam)))
    log_a = -RG_C * r_gate * softplus_neg
    a = jnp.exp(log_a)
    u = jnp.sqrt(1.0 - jnp.exp(2.0 * log_a)) * i_gate * xc

    rows = lambda v, r: v[r * groups:(r + 1) * groups]
    hs, ps = [rows(u, 0)], [rows(a, 0)]
    for r in range(1, SUBLANES):
        hs.append(rows(a, r) * hs[-1] + rows(u, r))
        ps.append(rows(a, r) * ps[-1])
    tot_a, tot_h = ps[-1], hs[-1]
    s = 1
    while s < groups:
        tot_h = tot_h + tot_a * _shift_rows(tot_h, s, 0.0, grow)
        tot_a = tot_a * _shift_rows(tot_a, s, 1.0, grow)
        s *= 2
    h_in = hcarry_ref[...]
    group_end = tot_h + tot_a * h_in
    hcarry_ref[...] = group_end[groups - 1:groups, :]
    group_in = down_one(group_end, h_in)

    for r in range(SUBLANES):
        h = hs[r] + ps[r] * group_in
        y = h * jax.nn.gelu(slab(buf_b, r))
        yn = _rms(y, grec_ref[...])
        for c in range(chunks):
            buf_a[c, pl.ds(r, groups, stride=SUBLANES), :] = yn[:, c * LANES:(c + 1) * LANES]
    for c in range(chunks):
        yrec_ref[:, c * LANES:(c + 1) * LANES] = buf_a[c].astype(BF16)

    project_rotary(q_ref, 2 * D_REC, HALF_DIM ** -0.5 * math.log2(math.e))
    project_rotary(k_ref, 2 * D_REC + D_QK, 1.0)
    project_v()


def _mix_in(x2, pos_row, inv_freq, rope_pat, ln_mix, w_in, conv_w, conv_b, wa_bd, b_a, wi_bd, b_i, rg_lambda,
            g_rec, batch, seq):
    tm = MIX_ROWS
    nt = seq // tm
    d_in = w_in.shape[1]
    row_map = lambda b, i: (b * nt + i, 0)
    fixed2 = lambda b, i: (0, 0)
    fixed3 = lambda b, i: (0, 0, 0)
    t = batch * seq
    out_shapes = (
        jax.ShapeDtypeStruct((t, D_REC), BF16),
        jax.ShapeDtypeStruct((t, D_QK), BF16),
        jax.ShapeDtypeStruct((t, D_QK), BF16),
        jax.ShapeDtypeStruct((t // tm, N_HEADS * V_EXT, tm), BF16),
    )
    return pl.pallas_call(
        _mix_in_kernel,
        grid=(batch, nt),
        in_specs=[
            pl.BlockSpec((tm, D_MODEL), row_map),
            pl.BlockSpec((1, tm), lambda b, i: (0, b * nt + i)),
            pl.BlockSpec((ROPE_DIM // 2, 1), fixed2),
            pl.BlockSpec((ROPE_DIM, 3 * LANES), fixed2),
            pl.BlockSpec((1, D_MODEL), fixed2),
            pl.BlockSpec((D_MODEL, d_in), fixed2, pipeline_mode=pl.Buffered(1)),
            pl.BlockSpec((CONV_WIDTH, D_REC), fixed2),
            pl.BlockSpec((1, D_REC), fixed2),
            pl.BlockSpec((2, D_REC // 2, D_REC // 2), fixed3),
            pl.BlockSpec((1, D_REC), fixed2),
            pl.BlockSpec((2, D_REC // 2, D_REC // 2), fixed3),
            pl.BlockSpec((1, D_REC), fixed2),
            pl.BlockSpec((1, D_REC), fixed2),
            pl.BlockSpec((1, D_REC), fixed2),
        ],
        out_specs=[
            pl.BlockSpec((tm, D_REC), row_map),
            pl.BlockSpec((tm, D_QK), row_map),
            pl.BlockSpec((tm, D_QK), row_map),
            pl.BlockSpec((1, N_HEADS * V_EXT, tm), lambda b, i: (b * nt + i, 0, 0)),
        ],
        out_shape=out_shapes,
        scratch_shapes=[pltpu.VMEM((SUBLANES, D_REC), F32), pltpu.VMEM((1, D_REC), F32),
                        pltpu.VMEM((D_REC // LANES, tm, LANES), F32),
                        pltpu.VMEM((D_REC // LANES, tm, LANES), F32),
                        pltpu.VMEM((D_MODEL, d_in), BF16)],
        compiler_params=pltpu.CompilerParams(
            dimension_semantics=("arbitrary", "arbitrary"), vmem_limit_bytes=VMEM_LIMIT),
        name="mix_in",
    )(x2, pos_row, inv_freq, rope_pat, ln_mix, w_in, conv_w, conv_b, wa_bd, b_a, wi_bd, b_i, rg_lambda, g_rec)


def _attn_kernel(lq1_ref, lk1_ref, lq2_ref, lk2_ref, gsub_ref, bias_ref, q_ref, k_ref, vt_ref, o_ref,
                 m_ref, acc_ref, aprev_ref, s0_ref, s1_ref, mb0_ref, mb1_ref, p0_ref, p1_ref, *, lam_init):
    tq = q_ref.shape[0]
    tk = vt_ref.shape[2]
    assert tq == tk, "the causal bias tile assumes the diagonal block is square"
    i = pl.program_id(2)

    qt = q_ref[...].astype(F32).T
    dim = lax.broadcasted_iota(I32, (LANES, tq), 0)
    qqt = jnp.concatenate([jnp.where(dim < HALF_DIM, qt, 0.0), jnp.where(dim >= HALF_DIM, qt, 0.0)],
                          axis=1).astype(BF16)

    n = (i * tq) // tk

    def scores(j):
        return _dot(k_ref[pl.ds(pl.multiple_of(j * tk, tk), tk), :], qqt)

    def probabilities(s, m_prev):
        m_new = jnp.maximum(m_prev, jnp.max(s, axis=0, keepdims=True))
        alpha = jnp.exp2(m_prev - m_new)
        p = jnp.exp2((s - m_new).astype(BF16))
        return p, alpha, m_new

    def store_scores(j, s_buf, mb_buf):
        s = scores(j)
        s_buf[...] = s
        mb_buf[...] = jnp.max(s, axis=0, keepdims=True)

    def pipe_step(j, cur, nxt, p_cur, p_prev):
        store_scores(j + 1, *nxt)
        s_cur, mb_cur = cur
        m_prev = m_ref[...]
        m_new = jnp.maximum(m_prev, mb_cur[...])
        p_cur[...] = jnp.exp2((s_cur[...] - m_new).astype(BF16))
        m_ref[...] = m_new
        acc_ref[...] = aprev_ref[...] * acc_ref[...] + _dot(vt_ref[jnp.maximum(j - 1, 0)], p_prev[...])
        aprev_ref[...] = jnp.exp2(m_prev - m_new)

    buf0, buf1 = (s0_ref, mb0_ref), (s1_ref, mb1_ref)
    m_ref[...] = jnp.full_like(m_ref, NEG_BIG)
    acc_ref[...] = jnp.zeros_like(acc_ref)
    aprev_ref[...] = jnp.ones_like(aprev_ref)
    odd = n % 2

    @pl.when(odd == 0)
    def _():
        p1_ref[...] = jnp.zeros_like(p1_ref)
        store_scores(0, *buf0)

    @pl.when(odd == 1)
    def _():
        p0_ref[...] = jnp.zeros_like(p0_ref)
        store_scores(0, *buf1)
        pipe_step(0, buf1, buf0, p1_ref, p0_ref)

    def pair(t, carry):
        j = 2 * t + odd
        pipe_step(j, buf0, buf1, p0_ref, p1_ref)
        pipe_step(j + 1, buf1, buf0, p1_ref, p0_ref)
        return carry

    lax.fori_loop(0, n // 2, pair, 0)

    parts = []
    for c0 in range(0, 2 * tq, ATT_TAIL_LANES):
        cols = slice(c0, c0 + ATT_TAIL_LANES)
        p, alpha, _ = probabilities(s0_ref[:, cols] + bias_ref[:, cols], m_ref[:, cols])
        part = aprev_ref[:, cols] * acc_ref[:, cols] + _dot(vt_ref[jnp.maximum(n - 1, 0)], p1_ref[:, cols])
        parts.append(alpha * part + _dot(vt_ref[n], p))
    acc = jnp.concatenate(parts, axis=1)

    lam = (jnp.exp(jnp.sum(lq1_ref[...] * lk1_ref[...], axis=-1, keepdims=True))
           - jnp.exp(jnp.sum(lq2_ref[...] * lk2_ref[...], axis=-1, keepdims=True)) + lam_init)
    o = acc[:V_DIM] / acc[V_DIM:V_DIM + 1]
    o = o[:, :tq] - lam * o[:, tq:]
    o = o * lax.rsqrt(jnp.mean(o * o, axis=0, keepdims=True) + EPS) * gsub_ref[...]
    o_ref[...] = (o * (1.0 - lam_init)).T.astype(BF16)


def _attention(q, k, vt, lq1, lk1, lq2, lk2, g_sub_col, batch, seq, lam_init):
    tq = ATT_Q
    nq = seq // tq
    tk = vt.shape[2]
    nk = seq // tk
    vec = lambda b, h, i: (0, 0)
    visible = (lax.broadcasted_iota(I32, (tk, 2 * tq), 0) <= lax.broadcasted_iota(I32, (tk, 2 * tq), 1) % tq)
    bias = jnp.where(visible, 0.0, NEG_BIG).astype(F32)
    return pl.pallas_call(
        functools.partial(_attn_kernel, lam_init=lam_init),
        grid=(batch, N_HEADS, nq),
        in_specs=[
            pl.BlockSpec((1, HALF_DIM), vec),
            pl.BlockSpec((1, HALF_DIM), vec),
            pl.BlockSpec((1, HALF_DIM), vec),
            pl.BlockSpec((1, HALF_DIM), vec),
            pl.BlockSpec((V_DIM, 1), vec),
            pl.BlockSpec((tk, 2 * tq), vec),
            pl.BlockSpec((tq, LANES), lambda b, h, i: (b * nq + i, h)),
            pl.BlockSpec((seq, LANES), lambda b, h, i: (b, h)),
            pl.BlockSpec((nk, V_EXT, tk), lambda b, h, i: (b, h, 0)),
        ],
        out_specs=pl.BlockSpec((tq, V_DIM), lambda b, h, i: (b * nq + i, h)),
        out_shape=jax.ShapeDtypeStruct((batch * seq, D_ATT), BF16),
        scratch_shapes=[pltpu.VMEM((1, 2 * tq), F32),
                        pltpu.VMEM((V_EXT, 2 * tq), F32), pltpu.VMEM((1, 2 * tq), F32),
                        pltpu.VMEM((tk, 2 * tq), F32), pltpu.VMEM((tk, 2 * tq), F32),
                        pltpu.VMEM((1, 2 * tq), F32), pltpu.VMEM((1, 2 * tq), F32),
                        pltpu.VMEM((tk, 2 * tq), BF16), pltpu.VMEM((tk, 2 * tq), BF16)],
        compiler_params=pltpu.CompilerParams(
            dimension_semantics=("arbitrary", "arbitrary", "arbitrary"), vmem_limit_bytes=VMEM_LIMIT),
        name="attention",
    )(lq1, lk1, lq2, lk2, g_sub_col, bias, q, k, vt)


def _sublane_total(x, op):
    return op(x, axis=0, keepdims=True)


def _route_kernel(x_ref, yrec_ref, yatt_ref, wo32_ref, lnmoe_ref, wrt_ref, ebias_ref, tri_ref,
                  h1_ref, hpa_ref, hpb_ref, ek_ref, wk_ref, rk_ref, cnt_ref, carry_ref, wo_ref):
    tm = x_ref.shape[0]
    e_n = N_EXPERTS

    @pl.when(pl.program_id(0) == 0)
    def _():
        wo_ref[...] = wo32_ref[...].astype(BF16)

    @pl.when(pl.program_id(0) % (pl.num_programs(0) // MOE_PARTS) == 0)
    def _():
        carry_ref[...] = jnp.zeros_like(carry_ref)

    h1 = x_ref[...] + _dot(yrec_ref[...], wo_ref[:D_REC, :]) + _dot(yatt_ref[...], wo_ref[D_REC:, :])
    h1_ref[...] = h1
    hn = _rms(h1, lnmoe_ref[...])
    pa, pb = _pack_row(hn)
    hpa_ref[...] = pa
    hpb_ref[...] = pb

    logits = lax.dot_general(wrt_ref[...], hn, (((1,), (1,)), ((), ())),
                             precision=lax.Precision.HIGHEST, preferred_element_type=F32)
    scores = jax.nn.sigmoid(logits)
    sel = scores + ebias_ref[...]

    sel3 = sel.reshape(N_GROUPS, GROUP_SIZE, tm)
    idx3 = lax.broadcasted_iota(I32, (N_GROUPS, GROUP_SIZE, tm), 1)
    m1 = jnp.max(sel3, axis=1, keepdims=True)
    first = jnp.min(jnp.where(sel3 == m1, idx3, GROUP_SIZE), axis=1, keepdims=True)
    m2 = jnp.max(jnp.where(idx3 == first, -jnp.inf, sel3), axis=1, keepdims=True)
    gscore = (m1 + m2).reshape(N_GROUPS, tm)

    gidx = lax.broadcasted_iota(I32, (N_GROUPS, tm), 0)
    beaten = jnp.zeros((N_GROUPS, tm), I32)
    for g in range(N_GROUPS):
        other = gscore[g:g + 1, :]
        beats = (other > gscore) | ((other == gscore) & (g < gidx))
        beaten = beaten + beats.astype(I32)
    gkeep = beaten < TOPK_GROUPS
    keep = jnp.broadcast_to(gkeep.reshape(N_GROUPS, 1, tm), (N_GROUPS, GROUP_SIZE, tm)).reshape(e_n, tm)
    selm = jnp.where(keep, sel, -jnp.inf)

    eidx = lax.broadcasted_iota(I32, (e_n, tm), 0)
    remaining = selm
    picks, ek, sk = [], [], []
    for _ in range(TOP_K):
        best = jnp.max(remaining, axis=0, keepdims=True)
        first = jnp.min(jnp.where(remaining == best, eidx, e_n), axis=0, keepdims=True)
        pick = eidx == first
        picks.append(pick)
        ek.append(first)
        sk.append(_sublane_total(jnp.where(pick, scores, 0.0), jnp.sum))
        remaining = jnp.where(pick, -jnp.inf, remaining)
    chosen_f = (remaining != selm).astype(F32)
    wsum = functools.reduce(lambda a, b: a + b, sk)
    ek_ref[...] = jnp.concatenate(ek, axis=0)
    wk_ref[...] = (jnp.concatenate(sk, axis=0) * (ROUTE_SCALE / wsum)).T

    prefix = _dot(chosen_f.astype(BF16), tri_ref[...])
    rank = prefix + carry_ref[...]
    carry_new = carry_ref[...] + jnp.sum(chosen_f, axis=1, keepdims=True)
    carry_ref[...] = carry_new
    cnt_ref[0] = carry_new.astype(I32)
    rk = [_sublane_total(jnp.where(pick, rank, 0.0), jnp.sum) for pick in picks]
    rk_ref[...] = jnp.concatenate(rk, axis=0).astype(I32)


def _route(x2, y_rec, y_att, w_out, ln_moe, w_router_t, e_bias_col):
    t = x2.shape[0]
    tm = ROUTE_ROWS
    nt = t // tm
    row_map = lambda i: (i, 0)
    col_map = lambda i: (0, i)
    fixed = lambda i: (0, 0)
    tri = (lax.broadcasted_iota(I32, (tm, tm), 0) < lax.broadcasted_iota(I32, (tm, tm), 1)).astype(BF16)
    out_shapes = (
        jax.ShapeDtypeStruct((t, D_MODEL), F32),
        jax.ShapeDtypeStruct((t, PACK_W), U32),
        jax.ShapeDtypeStruct((t, PACK_W), U32),
        jax.ShapeDtypeStruct((TOP_K, t), I32),
        jax.ShapeDtypeStruct((t, TOP_K), F32),
        jax.ShapeDtypeStruct((TOP_K, t), I32),
        jax.ShapeDtypeStruct((MOE_PARTS, N_EXPERTS, 1), I32),
    )
    assert nt % MOE_PARTS == 0
    return pl.pallas_call(
        _route_kernel,
        grid=(nt,),
        in_specs=[
            pl.BlockSpec((tm, D_MODEL), row_map),
            pl.BlockSpec((tm, D_REC), row_map),
            pl.BlockSpec((tm, D_ATT), row_map),
            pl.BlockSpec((D_REC + D_ATT, D_MODEL), fixed, pipeline_mode=pl.Buffered(1)),
            pl.BlockSpec((1, D_MODEL), fixed),
            pl.BlockSpec((N_EXPERTS, D_MODEL), fixed),
            pl.BlockSpec((N_EXPERTS, 1), fixed),
            pl.BlockSpec((tm, tm), fixed),
        ],
        out_specs=[
            pl.BlockSpec((tm, D_MODEL), row_map),
            pl.BlockSpec((tm, PACK_W), row_map),
            pl.BlockSpec((tm, PACK_W), row_map),
            pl.BlockSpec((TOP_K, tm), col_map),
            pl.BlockSpec((tm, TOP_K), row_map),
            pl.BlockSpec((TOP_K, tm), col_map),
            pl.BlockSpec((1, N_EXPERTS, 1), lambda i: (i // (nt // MOE_PARTS), 0, 0)),
        ],
        out_shape=out_shapes,
        scratch_shapes=[pltpu.VMEM((N_EXPERTS, 1), F32), pltpu.VMEM((D_REC + D_ATT, D_MODEL), BF16)],
        compiler_params=pltpu.CompilerParams(dimension_semantics=("arbitrary",), vmem_limit_bytes=VMEM_LIMIT),
        name="route",
    )(x2, y_rec, y_att, w_out, ln_moe, w_router_t, e_bias_col, tri)


def _plan_kernel(pad_start_ref, ek_ref, rk_ref, dest_ref):
    ek = ek_ref[...]
    part = pl.program_id(0) // (pl.num_programs(0) // MOE_PARTS)

    def add_expert(e, base):
        return jnp.where(ek == e, pad_start_ref[part * N_EXPERTS + e], base)

    dest_ref[...] = rk_ref[...] + lax.fori_loop(0, N_EXPERTS, add_expert, jnp.zeros_like(ek))


def _plan(pad_start, ek, rk):
    kk, t = ek.shape
    tl = min(t // MOE_PARTS, 2048)
    col_map = lambda i, ps: (0, i)
    grid_spec = pltpu.PrefetchScalarGridSpec(
        num_scalar_prefetch=1,
        grid=(t // tl,),
        in_specs=[pl.BlockSpec((kk, tl), col_map), pl.BlockSpec((kk, tl), col_map)],
        out_specs=pl.BlockSpec((kk, tl), col_map),
    )
    return pl.pallas_call(
        _plan_kernel,
        grid_spec=grid_spec,
        out_shape=jax.ShapeDtypeStruct((kk, t), I32),
        compiler_params=pltpu.CompilerParams(dimension_semantics=("arbitrary",)),
        name="plan",
    )(pad_start, ek, rk)


def _sc_mesh():
    return plsc.VectorSubcoreMesh(core_axis_name="core", subcore_axis_name="subcore")


def _sc_dispatch(rows, dest, n_out, part):
    t, w = rows.shape
    kk = dest.shape[0]
    steps = t // MOE_PARTS // SC_WINDOW
    first = part * steps

    @pl.kernel(out_type=jax.ShapeDtypeStruct((n_out, w), rows.dtype), mesh=_sc_mesh(), scratch_types=[])
    def kern(x_hbm, i_hbm, o_hbm):
        def body(x_vmem, i_vmem):
            for k in range(kk):
                pltpu.sync_copy(x_vmem, o_hbm.at[i_vmem.at[k]])

        pltpu.emit_pipeline(
            body,
            grid=(steps,),
            in_specs=[pl.BlockSpec((SC_WINDOW, w), lambda i: (first + i, 0)),
                      pl.BlockSpec((kk, SC_WINDOW), lambda i: (0, first + i))],
            out_specs=[],
            core_axis_name=("core", "subcore"),
            dimension_semantics=(pltpu.PARALLEL,),
        )(x_hbm, i_hbm)

    return kern(rows, dest)


def _sc_combine(rows, dest, part):
    kk, t = dest.shape
    w = rows.shape[1]
    tp = t // MOE_PARTS
    flat = dest.reshape(1, kk * t)
    windows = tp // SC_WINDOW

    def index_block(i):
        return (0, (i // windows) * (t // SC_WINDOW) + part * windows + i % windows)

    @pl.kernel(out_type=jax.ShapeDtypeStruct((kk * tp, w), rows.dtype), mesh=_sc_mesh(), scratch_types=[])
    def kern(y_hbm, i_hbm, o_hbm):
        def body(i_vmem, o_vmem):
            pltpu.sync_copy(y_hbm.at[i_vmem.at[0]], o_vmem)

        pltpu.emit_pipeline(
            body,
            grid=(kk * windows,),
            in_specs=[pl.BlockSpec((1, SC_WINDOW), index_block)],
            out_specs=[pl.BlockSpec((SC_WINDOW, w), lambda i: (i, 0))],
            core_axis_name=("core", "subcore"),
            dimension_semantics=(pltpu.PARALLEL,),
        )(i_hbm, o_hbm)

    return kern(rows, flat).reshape(kk, tp, w)


X_SLOTS = 3
Y_SLOTS = 2


def _experts_kernel(blk_expert_ref, n_used_ref, first_ref, slot_ref, next_ref,
                    xa_hbm, xb_hbm, w1_hbm, w3_hbm, w2_hbm, ya_hbm, yb_hbm,
                    xa_buf, xb_buf, ya_buf, yb_buf, w1f_ref, w3f_ref, w2f_ref, w1b_ref, w3b_ref, w2b_ref,
                    wsem, xsem, ysem):
    m = EXPERT_ROWS
    n_used = n_used_ref[0]

    def weight_copies(e, s):
        return (pltpu.make_async_copy(w1_hbm.at[e], w1f_ref.at[s], wsem.at[s, 0]),
                pltpu.make_async_copy(w3_hbm.at[e], w3f_ref.at[s], wsem.at[s, 1]),
                pltpu.make_async_copy(w2_hbm.at[e], w2f_ref.at[s], wsem.at[s, 2]))

    def x_copies(b):
        rows, s = pl.ds(pl.multiple_of(b * m, m), m), b % X_SLOTS
        return (pltpu.make_async_copy(xa_hbm.at[rows], xa_buf.at[s], xsem.at[s, 0]),
                pltpu.make_async_copy(xb_hbm.at[rows], xb_buf.at[s], xsem.at[s, 1]))

    def y_copies(b):
        rows, s = pl.ds(pl.multiple_of(b * m, m), m), b % Y_SLOTS
        return (pltpu.make_async_copy(ya_buf.at[s], ya_hbm.at[rows], ysem.at[s, 0]),
                pltpu.make_async_copy(yb_buf.at[s], yb_hbm.at[rows], ysem.at[s, 1]))

    def start(copies):
        for copy in copies:
            copy.start()

    def wait(copies):
        for copy in copies:
            copy.wait()

    start(weight_copies(blk_expert_ref[0], 0))
    start(x_copies(0))

    @pl.when(n_used > 1)
    def _():
        start(x_copies(1))

    def block(b, carry):
        @pl.when(b + 2 < n_used)
        def _():
            start(x_copies(b + 2))

        @pl.when(first_ref[b] == 1)
        def _():
            s = slot_ref[b]
            wait(weight_copies(blk_expert_ref[b], s))

            @pl.when(next_ref[b] >= 0)
            def _():
                start(weight_copies(next_ref[b], 1 - s))

            w1b_ref[...] = w1f_ref[s].astype(BF16)
            w3b_ref[...] = w3f_ref[s].astype(BF16)
            w2b_ref[...] = w2f_ref[s].astype(BF16)

        wait(x_copies(b))

        @pl.when(b >= Y_SLOTS)
        def _():
            wait(y_copies(b - Y_SLOTS))

        x = _unpack_row(xa_buf[b % X_SLOTS], xb_buf[b % X_SLOTS]).astype(BF16)
        a = _dot(x, w1b_ref[...])
        g = _dot(x, w3b_ref[...])
        hmid = (jax.nn.silu(a) * g).astype(BF16)
        y = _dot(hmid, w2b_ref[...])
        pa, pb = _pack_row(y)
        ya_buf[b % Y_SLOTS] = pa
        yb_buf[b % Y_SLOTS] = pb
        start(y_copies(b))
        return carry

    lax.fori_loop(0, n_used, block, 0)

    for back in range(Y_SLOTS, 0, -1):
        @pl.when(n_used - back >= 0)
        def _(back=back):
            wait(y_copies(n_used - back))


def _experts(xa, xb, w1, w3, w2, blk_expert, n_used, seg_first, seg_slot, seg_next):
    p = xa.shape[0]
    m = EXPERT_ROWS
    hbm = pl.BlockSpec(memory_space=pl.ANY)
    grid_spec = pltpu.PrefetchScalarGridSpec(
        num_scalar_prefetch=5,
        grid=(1,),
        in_specs=[hbm] * 5,
        out_specs=[hbm, hbm],
        scratch_shapes=[
            pltpu.VMEM((X_SLOTS, m, PACK_W), U32), pltpu.VMEM((X_SLOTS, m, PACK_W), U32),
            pltpu.VMEM((Y_SLOTS, m, PACK_W), U32), pltpu.VMEM((Y_SLOTS, m, PACK_W), U32),
            pltpu.VMEM((2, D_MODEL, D_EXPERT), F32), pltpu.VMEM((2, D_MODEL, D_EXPERT), F32),
            pltpu.VMEM((2, D_EXPERT, D_MODEL), F32),
            pltpu.VMEM((D_MODEL, D_EXPERT), BF16), pltpu.VMEM((D_MODEL, D_EXPERT), BF16),
            pltpu.VMEM((D_EXPERT, D_MODEL), BF16),
            pltpu.SemaphoreType.DMA((2, 3)), pltpu.SemaphoreType.DMA((X_SLOTS, 2)),
            pltpu.SemaphoreType.DMA((Y_SLOTS, 2)),
        ],
    )
    return pl.pallas_call(
        _experts_kernel,
        grid_spec=grid_spec,
        out_shape=(jax.ShapeDtypeStruct((p, PACK_W), U32), jax.ShapeDtypeStruct((p, PACK_W), U32)),
        compiler_params=pltpu.CompilerParams(dimension_semantics=("arbitrary",), vmem_limit_bytes=VMEM_LIMIT),
        name="experts",
    )(blk_expert, n_used, seg_first, seg_slot, seg_next, xa, xb, w1, w3, w2)


def _tail_kernel(h1_ref, ga_ref, gb_ref, wk_ref, p_ref, lnmoe_ref, ws1_32, ws3_32, ws2_32, lnple_ref,
                 wpg_32, wpp_32, lnf_ref, o_ref, ws1_ref, ws3_ref, ws2_ref, wpg_ref, wpp_ref):
    @pl.when(pl.program_id(0) == 0)
    def _():
        for dst, src in ((ws1_ref, ws1_32), (ws3_ref, ws3_32), (ws2_ref, ws2_32), (wpg_ref, wpg_32),
                         (wpp_ref, wpp_32)):
            dst[...] = src[...].astype(BF16)

    h1 = h1_ref[...]
    hn = _rms(h1, lnmoe_ref[...]).astype(BF16)
    shared = _dot((jax.nn.silu(_dot(hn, ws1_ref[...])) * _dot(hn, ws3_ref[...])).astype(BF16), ws2_ref[...])
    wk = wk_ref[...]
    routed = jnp.zeros_like(h1)
    for kk in range(TOP_K):
        routed = routed + wk[:, kk:kk + 1] * _unpack_row(ga_ref[kk], gb_ref[kk])
    h2 = h1 + routed + shared
    gate = jax.nn.sigmoid(_dot(_rms(h2, lnple_ref[...]).astype(BF16), wpg_ref[...]))
    h3 = h2 + gate * _dot(p_ref[...].astype(BF16), wpp_ref[...])
    o_ref[...] = _rms(h3, lnf_ref[...])


def _tail_into_kernel(out_so_far_ref, *refs):
    del out_so_far_ref
    _tail_kernel(*refs)


def _tail(out_so_far, part, h1, ga, gb, wk_t, p2, ln_moe, ws1, ws3, ws2, ln_ple, w_pg, w_pp, ln_f):
    t = h1.shape[0]
    tm = TAIL_ROWS
    steps = t // MOE_PARTS // tm
    row_map = lambda i: (part * steps + i, 0)
    fixed = lambda i: (0, 0)
    g_map = lambda i: (0, i, 0)
    d_sh = ws1.shape[1]
    carried = () if out_so_far is None else (out_so_far,)
    return pl.pallas_call(
        _tail_kernel if out_so_far is None else _tail_into_kernel,
        grid=(steps,),
        input_output_aliases={} if out_so_far is None else {0: 0},
        in_specs=[pl.BlockSpec(memory_space=pl.ANY)] * len(carried) + [
            pl.BlockSpec((tm, D_MODEL), row_map),
            pl.BlockSpec((TOP_K, tm, PACK_W), g_map),
            pl.BlockSpec((TOP_K, tm, PACK_W), g_map),
            pl.BlockSpec((tm, TOP_K), row_map),
            pl.BlockSpec((tm, D_PLE), row_map),
            pl.BlockSpec((1, D_MODEL), fixed),
            pl.BlockSpec((D_MODEL, d_sh), fixed, pipeline_mode=pl.Buffered(1)),
            pl.BlockSpec((D_MODEL, d_sh), fixed, pipeline_mode=pl.Buffered(1)),
            pl.BlockSpec((d_sh, D_MODEL), fixed, pipeline_mode=pl.Buffered(1)),
            pl.BlockSpec((1, D_MODEL), fixed),
            pl.BlockSpec((D_MODEL, D_MODEL), fixed, pipeline_mode=pl.Buffered(1)),
            pl.BlockSpec((D_PLE, D_MODEL), fixed, pipeline_mode=pl.Buffered(1)),
            pl.BlockSpec((1, D_MODEL), fixed),
        ],
        out_specs=pl.BlockSpec((tm, D_MODEL), row_map),
        out_shape=jax.ShapeDtypeStruct((t, D_MODEL), F32),
        scratch_shapes=[pltpu.VMEM((D_MODEL, d_sh), BF16), pltpu.VMEM((D_MODEL, d_sh), BF16),
                        pltpu.VMEM((d_sh, D_MODEL), BF16), pltpu.VMEM((D_MODEL, D_MODEL), BF16),
                        pltpu.VMEM((D_PLE, D_MODEL), BF16)],
        compiler_params=pltpu.CompilerParams(dimension_semantics=("arbitrary",), vmem_limit_bytes=VMEM_LIMIT),
        name="tail",
    )(*carried, h1, ga, gb, wk_t, p2, ln_moe, ws1, ws3, ws2, ln_ple, w_pg, w_pp, ln_f)


def _rope_constants():
    half = ROPE_DIM // 2
    inv_freq = (ROPE_THETA ** (-jnp.arange(0, ROPE_DIM, 2, dtype=F32) / ROPE_DIM)).reshape(half, 1)
    f = lax.broadcasted_iota(I32, (ROPE_DIM, LANES), 0)
    l64 = lax.broadcasted_iota(I32, (ROPE_DIM, LANES), 1) % HALF_DIM
    cos_pat = ((f < half) & (l64 < ROPE_DIM) & (l64 % half == f)).astype(F32)
    sa_pat = -((f >= half) & (l64 < half) & (l64 == f - half)).astype(F32)
    sb_pat = ((f >= half) & (l64 >= half) & (l64 < ROPE_DIM) & (l64 - half == f - half)).astype(F32)
    return inv_freq, jnp.concatenate([cos_pat, sa_pat, sb_pat], axis=1)


def _block_diag_tiles(w):
    nb, bd, _ = w.shape
    per = nb // 2
    tiles = []
    for tix in range(2):
        rows = []
        for j in range(per):
            rows.append(jnp.concatenate(
                [w[tix * per + j] if c == j else jnp.zeros((bd, bd), w.dtype) for c in range(per)], axis=1))
        tiles.append(jnp.concatenate(rows, axis=0))
    return jnp.stack(tiles).astype(BF16)


def _layer(h, p_l, positions, lam_init, ln_mix, w_in, conv_w, conv_b, w_a, b_a, w_i, b_i, rg_lambda, g_rec,
           lq1, lk1, lq2, lk2, g_sub, w_out, ln_moe, w_router, e_bias, w1, w3, w2, ws1, ws3, ws2,
           ln_ple, w_ple_gate, w_ple_proj, ln_out):
    batch, seq, _ = h.shape
    t = batch * seq
    x2 = h.reshape(t, D_MODEL)
    row = lambda a: a.reshape(1, -1)
    inv_freq, rope_pat = _rope_constants()

    y_rec, q, k, vt = _mix_in(
        x2, positions.reshape(1, t), inv_freq, rope_pat, row(ln_mix), w_in, conv_w, row(conv_b),
        _block_diag_tiles(w_a), row(b_a), _block_diag_tiles(w_i), row(b_i), row(rg_lambda), row(g_rec),
        batch, seq)
    y_att = _attention(q, k, vt, row(lq1), row(lk1), row(lq2), row(lk2), g_sub.reshape(-1, 1), batch, seq,
                       lam_init)

    h1, hpa, hpb, ek, wk_t, rk, counts = _route(
        x2, y_rec, y_att, w_out, row(ln_moe), w_router.T, e_bias.reshape(-1, 1))

    m = EXPERT_ROWS
    counts = counts.reshape(MOE_PARTS, N_EXPERTS)
    padded = (counts + m - 1) // m * m
    pad_end = jnp.cumsum(padded, axis=1)
    pad_start = pad_end - padded
    n_rows = t // MOE_PARTS * TOP_K + N_EXPERTS * m
    nblk = n_rows // m
    n_used = (pad_end[:, -1] // m).astype(I32)
    blk = jnp.arange(nblk, dtype=I32)
    blk_row = jnp.minimum(blk[None, :], n_used[:, None] - 1) * m
    blk_expert = jnp.sum((pad_end[:, None, :] <= blk_row[:, :, None]).astype(I32), axis=2)
    prev_expert = jnp.concatenate([jnp.full((MOE_PARTS, 1), -1, I32), blk_expert[:, :-1]], axis=1)
    seg_first = ((blk[None, :] < n_used[:, None]) & (blk_expert != prev_expert)).astype(I32)
    seg_slot = ((jnp.cumsum(seg_first, axis=1) - 1) % 2).astype(I32)
    eid = jnp.arange(N_EXPERTS, dtype=I32)
    later = (padded[:, None, :] > 0) & (eid[None, None, :] > eid[None, :, None])
    next_expert = jnp.min(jnp.where(later, eid[None, None, :], N_EXPERTS), axis=2)
    next_expert = jnp.where(next_expert == N_EXPERTS, -1, next_expert).astype(I32)
    seg_next = jnp.sum(jnp.where(blk_expert[:, :, None] == eid[None, None, :], next_expert[:, None, :], 0), axis=2)
    dest = _plan(pad_start.astype(I32).reshape(-1), ek, rk)

    p2 = p_l.reshape(t, D_PLE)
    tail_weights = (row(ln_moe), ws1, ws3, ws2, row(ln_ple), w_ple_gate, w_ple_proj, row(ln_out))
    gathered = []
    for part in range(MOE_PARTS):
        xa = _sc_dispatch(hpa, dest, n_rows, part)
        xb = _sc_dispatch(hpb, dest, n_rows, part)
        ya, yb = _experts(xa, xb, w1, w3, w2, blk_expert[part], n_used[part:part + 1], seg_first[part],
                          seg_slot[part], seg_next[part])
        gathered.append((_sc_combine(ya, dest, part), _sc_combine(yb, dest, part)))
    out = None
    for part, (ga, gb) in enumerate(gathered):
        out = _tail(out, part, h1, ga, gb, wk_t, p2, *tail_weights)
    return out.reshape(batch, seq, D_MODEL)


def kernel(x, p, positions, ln_mix, w_in, conv_w, conv_b, w_a, b_a, w_i, b_i, rg_lambda, g_rec, lq1, lk1, lq2,
           lk2, g_sub, w_out, ln_moe, w_router, e_bias, w1, w3, w2, ws1, ws3, ws2, ln_ple, w_ple_gate,
           w_ple_proj, ln_f):
    depth = w_in.shape[0]
    assert depth == 1, "the fused tail applies the final norm; one layer supported"
    lam_init = 0.8 - 0.6 * math.exp(-0.3 * 0)
    return _layer(x, p[0], positions, lam_init, ln_mix[0], w_in[0], conv_w[0], conv_b[0], w_a[0], b_a[0], w_i[0],
                  b_i[0], rg_lambda[0], g_rec[0], lq1[0], lk1[0], lq2[0], lk2[0], g_sub[0], w_out[0], ln_moe[0],
                  w_router[0], e_bias[0], w1[0], w3[0], w2[0], ws1[0], ws3[0], ws2[0], ln_ple[0], w_ple_gate[0],
                  w_ple_proj[0], ln_f)
```

```python
import functools
import math

import jax
import jax.numpy as jnp
from jax import lax
from jax.experimental import pallas as pl
from jax.experimental.pallas import tpu as pltpu
from jax.experimental.pallas import tpu_sc as plsc

F32 = jnp.float32
BF16 = jnp.bfloat16
U32 = jnp.uint32
I32 = jnp.int32

D_MODEL = 1024
D_REC = 512
REC_BLOCKS = 8
CONV_WIDTH = 4
RG_C = 8.0
N_HEADS = 4
HALF_DIM = 64
V_DIM = 128
D_ATT = N_HEADS * V_DIM
D_QK = N_HEADS * 2 * HALF_DIM
ROPE_DIM = 16
ROPE_THETA = 500000.0
N_EXPERTS = 64
TOP_K = 8
N_GROUPS = 8
GROUP_SIZE = N_EXPERTS // N_GROUPS
TOPK_GROUPS = 4
D_EXPERT = 256
ROUTE_SCALE = 2.5
D_PLE = 256
EPS = 1e-6

LANES = 128
SUBLANES = 8
VMEM_LIMIT = 56 * 1024 * 1024

MIX_ROWS = 512
ATT_Q = 512
ATT_TAIL_LANES = 256
ONES_ROWS = 16
V_EXT = V_DIM + ONES_ROWS
ROUTE_ROWS = 1024
EXPERT_ROWS = 512
MOE_PARTS = 1
TAIL_ROWS = 512
SC_WINDOW = 128
PACK_W = 256
NEG_BIG = -1e30


def _rms(x, g):
    return x * lax.rsqrt(jnp.mean(x * x, axis=-1, keepdims=True) + EPS) * g


def _dot(a, b):
    return jnp.dot(a, b, preferred_element_type=F32)


def _pack_pair(lo, hi):
    lo_bits = lax.bitcast_convert_type(lo.astype(BF16).astype(F32), U32)
    hi_bits = lax.bitcast_convert_type(hi.astype(BF16).astype(F32), U32)
    return (lo_bits >> 16) | (hi_bits & jnp.uint32(0xFFFF0000))


def _unpack_pair(p):
    lo = lax.bitcast_convert_type(p << 16, F32)
    hi = lax.bitcast_convert_type(p & jnp.uint32(0xFFFF0000), F32)
    return lo, hi


def _pack_row(x):
    w = PACK_W
    return _pack_pair(x[:, 0:w], x[:, w:2 * w]), _pack_pair(x[:, 2 * w:3 * w], x[:, 3 * w:4 * w])


def _unpack_row(pa, pb):
    c0, c1 = _unpack_pair(pa)
    c2, c3 = _unpack_pair(pb)
    return jnp.concatenate([c0, c1, c2, c3], axis=1)


def _shift_rows(a, s, fill, row):
    n, c = a.shape
    if s % SUBLANES == 0:
        return jnp.concatenate([jnp.full((s, c), fill, a.dtype), a[:n - s]], axis=0)
    return jnp.where(row >= s, pltpu.roll(a, s, 0), fill)


def _mix_in_kernel(x_ref, pos_ref, invf_ref, pat_ref, lnm_ref, win32_ref, cw_ref, cb_ref, wa_ref, ba_ref,
                   wi_ref, bi_ref, lam_ref, grec_ref,
                   yrec_ref, q_ref, k_ref, vt_ref, tail_ref, hcarry_ref, buf_a, buf_b, win_ref):
    tm = x_ref.shape[0]
    groups = tm // SUBLANES
    chunks = D_REC // LANES

    def stage(ref, v):
        for c in range(chunks):
            ref[c] = v[:, c * LANES:(c + 1) * LANES]

    def slab(ref, r):
        return jnp.concatenate([ref[c, pl.ds(r, groups, stride=SUBLANES), :] for c in range(chunks)], axis=1)

    @pl.when((pl.program_id(0) == 0) & (pl.program_id(1) == 0))
    def _():
        win_ref[...] = win32_ref[...].astype(BF16)

    @pl.when(pl.program_id(1) == 0)
    def _():
        tail_ref[...] = jnp.zeros_like(tail_ref)
        hcarry_ref[...] = jnp.zeros_like(hcarry_ref)

    hn = _rms(x_ref[...], lnm_ref[...]).astype(BF16)

    ang = invf_ref[...] * pos_ref[...].astype(F32)
    cs = jnp.concatenate([jnp.cos(ang), jnp.sin(ang)], axis=0)
    tabs = lax.dot_general(cs, pat_ref[...], (((0,), (0,)), ((), ())),
                           precision=lax.Precision.HIGHEST, preferred_element_type=F32)
    lane64 = lax.broadcasted_iota(I32, (1, LANES), 1) % HALF_DIM
    cosf = tabs[:, 0:LANES] + (lane64 >= ROPE_DIM).astype(F32)
    sa, sb = tabs[:, LANES:2 * LANES], tabs[:, 2 * LANES:3 * LANES]

    def project_rotary(out_ref, off, mul):
        for c in range(0, D_QK // LANES, 2):
            z2 = _dot(hn, win_ref[:, off + c * LANES: off + (c + 2) * LANES])
            for cc in range(2):
                zc = z2[:, cc * LANES:(cc + 1) * LANES]
                rot = (zc * cosf + pltpu.roll(zc, LANES - ROPE_DIM // 2, 1) * sa
                       + pltpu.roll(zc, ROPE_DIM // 2, 1) * sb)
                out_ref[:, (c + cc) * LANES:(c + cc + 1) * LANES] = (rot * mul).astype(BF16)

    def project_v():
        vt = _dot(hn, win_ref[:, 2 * D_REC + 2 * D_QK:]).T.astype(BF16)
        for hd in range(N_HEADS):
            vt_ref[0, hd * V_EXT:hd * V_EXT + V_DIM, :] = vt[hd * V_DIM:(hd + 1) * V_DIM]
            vt_ref[0, hd * V_EXT + V_DIM:(hd + 1) * V_EXT, :] = jnp.ones((ONES_ROWS, tm), BF16)

    xr = _dot(hn, win_ref[:, 0:D_REC])
    stage(buf_a, xr)
    stage(buf_b, _dot(hn, win_ref[:, D_REC:2 * D_REC]))
    tail = tail_ref[...]
    tail_ref[...] = xr[tm - SUBLANES:, :]

    grow = lax.broadcasted_iota(I32, (groups, D_REC), 0)

    def down_one(a, first_row):
        return jnp.where(grow == 0, first_row, pltpu.roll(a, 1, 0))

    xs = [slab(buf_a, r) for r in range(SUBLANES)]
    wrapped = {r: down_one(xs[r], tail[r:r + 1, :]) for r in range(SUBLANES - CONV_WIDTH + 1, SUBLANES)}
    xc = []
    for r in range(SUBLANES):
        acc = cb_ref[...] + cw_ref[CONV_WIDTH - 1:CONV_WIDTH, :] * xs[r]
        for d in range(1, CONV_WIDTH):
            prev = xs[r - d] if r >= d else wrapped[r - d + SUBLANES]
            acc = acc + cw_ref[CONV_WIDTH - 1 - d:CONV_WIDTH - d, :] * prev
        xc.append(acc)
    xc = jnp.concatenate(xc, axis=0)

    xcb = xc.astype(BF16)
    half = D_REC // 2
    ra = jnp.concatenate([_dot(xcb[:, :half], wa_ref[0]), _dot(xcb[:, half:], wa_ref[1])], axis=1)
    ri = jnp.concatenate([_dot(xcb[:, :half], wi_ref[0]), _dot(xcb[:, half:], wi_ref[1])], axis=1)
    r_gate = jax.nn.sigmoid(ra + ba_ref[...])
    i_gate = jax.nn.sigmoid(ri + bi_ref[...])
    lam = lam_ref[...]
    softplus_neg = jnp.maximum(-lam, 0.0) + jnp.log(1.0 + jnp.exp(-jnp.abs(lam)))
    log_a = -RG_C * r_gate * softplus_neg
    a = jnp.exp(log_a)
    u = jnp.sqrt(1.0 - jnp.exp(2.0 * log_a)) * i_gate * xc

    rows = lambda v, r: v[r * groups:(r + 1) * groups]
    hs, ps = [rows(u, 0)], [rows(a, 0)]
    for r in range(1, SUBLANES):
        hs.append(rows(a, r) * hs[-1] + rows(u, r))
        ps.append(rows(a, r) * ps[-1])
    tot_a, tot_h = ps[-1], hs[-1]
    s = 1
    while s < groups:
        tot_h = tot_h + tot_a * _shift_rows(tot_h, s, 0.0, grow)
        tot_a = tot_a * _shift_rows(tot_a, s, 1.0, grow)
        s *= 2
    h_in = hcarry_ref[...]
    group_end = tot_h + tot_a * h_in
    hcarry_ref[...] = group_end[groups - 1:groups, :]
    group_in = down_one(group_end, h_in)

    for r in range(SUBLANES):
        h = hs[r] + ps[r] * group_in
        y = h * jax.nn.gelu(slab(buf_b, r))
        yn = _rms(y, grec_ref[...])
        for c in range(chunks):
            buf_a[c, pl.ds(r, groups, stride=SUBLANES), :] = yn[:, c * LANES:(c + 1) * LANES]
    for c in range(chunks):
        yrec_ref[:, c * LANES:(c + 1) * LANES] = buf_a[c].astype(BF16)

    project_rotary(q_ref, 2 * D_REC, HALF_DIM ** -0.5 * math.log2(math.e))
    project_rotary(k_ref, 2 * D_REC + D_QK, 1.0)
    project_v()


def _mix_in(x2, pos_row, inv_freq, rope_pat, ln_mix, w_in, conv_w, conv_b, wa_bd, b_a, wi_bd, b_i, rg_lambda,
            g_rec, batch, seq):
    tm = MIX_ROWS
    nt = seq // tm
    d_in = w_in.shape[1]
    row_map = lambda b, i: (b * nt + i, 0)
    fixed2 = lambda b, i: (0, 0)
    fixed3 = lambda b, i: (0, 0, 0)
    t = batch * seq
    out_shapes = (
        jax.ShapeDtypeStruct((t, D_REC), BF16),
        jax.ShapeDtypeStruct((t, D_QK), BF16),
        jax.ShapeDtypeStruct((t, D_QK), BF16),
        jax.ShapeDtypeStruct((t // tm, N_HEADS * V_EXT, tm), BF16),
    )
    return pl.pallas_call(
        _mix_in_kernel,
        grid=(batch, nt),
        in_specs=[
            pl.BlockSpec((tm, D_MODEL), row_map),
            pl.BlockSpec((1, tm), lambda b, i: (0, b * nt + i)),
            pl.BlockSpec((ROPE_DIM // 2, 1), fixed2),
            pl.BlockSpec((ROPE_DIM, 3 * LANES), fixed2),
            pl.BlockSpec((1, D_MODEL), fixed2),
            pl.BlockSpec((D_MODEL, d_in), fixed2, pipeline_mode=pl.Buffered(1)),
            pl.BlockSpec((CONV_WIDTH, D_REC), fixed2),
            pl.BlockSpec((1, D_REC), fixed2),
            pl.BlockSpec((2, D_REC // 2, D_REC // 2), fixed3),
            pl.BlockSpec((1, D_REC), fixed2),
            pl.BlockSpec((2, D_REC // 2, D_REC // 2), fixed3),
            pl.BlockSpec((1, D_REC), fixed2),
            pl.BlockSpec((1, D_REC), fixed2),
            pl.BlockSpec((1, D_REC), fixed2),
        ],
        out_specs=[
            pl.BlockSpec((tm, D_REC), row_map),
            pl.BlockSpec((tm, D_QK), row_map),
            pl.BlockSpec((tm, D_QK), row_map),
            pl.BlockSpec((1, N_HEADS * V_EXT, tm), lambda b, i: (b * nt + i, 0, 0)),
        ],
        out_shape=out_shapes,
        scratch_shapes=[pltpu.VMEM((SUBLANES, D_REC), F32), pltpu.VMEM((1, D_REC), F32),
                        pltpu.VMEM((D_REC // LANES, tm, LANES), F32),
                        pltpu.VMEM((D_REC // LANES, tm, LANES), F32),
                        pltpu.VMEM((D_MODEL, d_in), BF16)],
        compiler_params=pltpu.CompilerParams(
            dimension_semantics=("arbitrary", "arbitrary"), vmem_limit_bytes=VMEM_LIMIT),
        name="mix_in",
    )(x2, pos_row, inv_freq, rope_pat, ln_mix, w_in, conv_w, conv_b, wa_bd, b_a, wi_bd, b_i, rg_lambda, g_rec)


def _attn_kernel(lq1_ref, lk1_ref, lq2_ref, lk2_ref, gsub_ref, bias_ref, q_ref, k_ref, vt_ref, o_ref,
                 m_ref, acc_ref, aprev_ref, s0_ref, s1_ref, mb0_ref, mb1_ref, p0_ref, p1_ref, *, lam_init):
    tq = ATT_Q
    tk = vt_ref.shape[2]
    assert tq == tk, "the causal bias tile assumes the diagonal block is square"
    lam = (jnp.exp(jnp.sum(lq1_ref[...] * lk1_ref[...], axis=-1, keepdims=True))
           - jnp.exp(jnp.sum(lq2_ref[...] * lk2_ref[...], axis=-1, keepdims=True)) + lam_init)

    def query_tile(i, carry):
        _attn_query_tile(i, lam, gsub_ref, bias_ref, q_ref, k_ref, vt_ref, o_ref, m_ref, acc_ref, aprev_ref,
                         s0_ref, s1_ref, mb0_ref, mb1_ref, p0_ref, p1_ref, lam_init=lam_init)
        return carry

    lax.fori_loop(0, q_ref.shape[0] // tq, query_tile, 0)


def _attn_query_tile(i, lam, gsub_ref, bias_ref, q_ref, k_ref, vt_ref, o_ref, m_ref, acc_ref, aprev_ref,
                     s0_ref, s1_ref, mb0_ref, mb1_ref, p0_ref, p1_ref, *, lam_init):
    tq = ATT_Q
    tk = vt_ref.shape[2]
    q_rows = pl.ds(pl.multiple_of(i * tq, tq), tq)

    qt = q_ref[q_rows, :].astype(F32).T
    dim = lax.broadcasted_iota(I32, (LANES, tq), 0)
    qqt = jnp.concatenate([jnp.where(dim < HALF_DIM, qt, 0.0), jnp.where(dim >= HALF_DIM, qt, 0.0)],
                          axis=1).astype(BF16)

    n = (i * tq) // tk

    def scores(j):
        return _dot(k_ref[pl.ds(pl.multiple_of(j * tk, tk), tk), :], qqt)

    def probabilities(s, m_prev):
        m_new = jnp.maximum(m_prev, jnp.max(s, axis=0, keepdims=True))
        alpha = jnp.exp2(m_prev - m_new)
        p = jnp.exp2((s - m_new).astype(BF16))
        return p, alpha, m_new

    def store_scores(j, s_buf, mb_buf):
        s = scores(j)
        s_buf[...] = s
        mb_buf[...] = jnp.max(s, axis=0, keepdims=True)

    def pipe_step(j, cur, nxt, p_cur, p_prev):
        store_scores(j + 1, *nxt)
        s_cur, mb_cur = cur
        m_prev = m_ref[...]
        m_new = jnp.maximum(m_prev, mb_cur[...])
        p_cur[...] = jnp.exp2((s_cur[...] - m_new).astype(BF16))
        m_ref[...] = m_new
        acc_ref[...] = aprev_ref[...] * acc_ref[...] + _dot(vt_ref[jnp.maximum(j - 1, 0)], p_prev[...])
        aprev_ref[...] = jnp.exp2(m_prev - m_new)

    buf0, buf1 = (s0_ref, mb0_ref), (s1_ref, mb1_ref)
    m_ref[...] = jnp.full_like(m_ref, NEG_BIG)
    acc_ref[...] = jnp.zeros_like(acc_ref)
    aprev_ref[...] = jnp.ones_like(aprev_ref)
    odd = n % 2

    @pl.when(odd == 0)
    def _():
        p1_ref[...] = jnp.zeros_like(p1_ref)
        store_scores(0, *buf0)

    @pl.when(odd == 1)
    def _():
        p0_ref[...] = jnp.zeros_like(p0_ref)
        store_scores(0, *buf1)
        pipe_step(0, buf1, buf0, p1_ref, p0_ref)

    def pair(t, carry):
        j = 2 * t + odd
        pipe_step(j, buf0, buf1, p0_ref, p1_ref)
        pipe_step(j + 1, buf1, buf0, p1_ref, p0_ref)
        return carry

    lax.fori_loop(0, n // 2, pair, 0)

    parts = []
    for c0 in range(0, 2 * tq, ATT_TAIL_LANES):
        cols = slice(c0, c0 + ATT_TAIL_LANES)
        p, alpha, _ = probabilities(s0_ref[:, cols] + bias_ref[:, cols], m_ref[:, cols])
        part = aprev_ref[:, cols] * acc_ref[:, cols] + _dot(vt_ref[jnp.maximum(n - 1, 0)], p1_ref[:, cols])
        parts.append(alpha * part + _dot(vt_ref[n], p))
    acc = jnp.concatenate(parts, axis=1)

    o = acc[:V_DIM] / acc[V_DIM:V_DIM + 1]
    o = o[:, :tq] - lam * o[:, tq:]
    o = o * lax.rsqrt(jnp.mean(o * o, axis=0, keepdims=True) + EPS) * gsub_ref[...]
    o_ref[q_rows, :] = (o * (1.0 - lam_init)).T.astype(BF16)


def _attention(q, k, vt, lq1, lk1, lq2, lk2, g_sub_col, batch, seq, lam_init):
    tq = ATT_Q
    nq = seq // tq
    tk = vt.shape[2]
    nk = seq // tk
    vec = lambda b, h: (0, 0)
    per_head = lambda b, h: (b, h)
    visible = (lax.broadcasted_iota(I32, (tk, 2 * tq), 0) <= lax.broadcasted_iota(I32, (tk, 2 * tq), 1) % tq)
    bias = jnp.where(visible, 0.0, NEG_BIG).astype(F32)
    return pl.pallas_call(
        functools.partial(_attn_kernel, lam_init=lam_init),
        grid=(batch, N_HEADS),
        in_specs=[
            pl.BlockSpec((1, HALF_DIM), vec),
            pl.BlockSpec((1, HALF_DIM), vec),
            pl.BlockSpec((1, HALF_DIM), vec),
            pl.BlockSpec((1, HALF_DIM), vec),
            pl.BlockSpec((V_DIM, 1), vec),
            pl.BlockSpec((tk, 2 * tq), vec),
            pl.BlockSpec((seq, LANES), per_head),
            pl.BlockSpec((seq, LANES), per_head),
            pl.BlockSpec((nk, V_EXT, tk), lambda b, h: (b, h, 0)),
        ],
        out_specs=pl.BlockSpec((seq, V_DIM), per_head),
        out_shape=jax.ShapeDtypeStruct((batch * seq, D_ATT), BF16),
        scratch_shapes=[pltpu.VMEM((1, 2 * tq), F32),
                        pltpu.VMEM((V_EXT, 2 * tq), F32), pltpu.VMEM((1, 2 * tq), F32),
                        pltpu.VMEM((tk, 2 * tq), F32), pltpu.VMEM((tk, 2 * tq), F32),
                        pltpu.VMEM((1, 2 * tq), F32), pltpu.VMEM((1, 2 * tq), F32),
                        pltpu.VMEM((tk, 2 * tq), BF16), pltpu.VMEM((tk, 2 * tq), BF16)],
        compiler_params=pltpu.CompilerParams(
            dimension_semantics=("arbitrary", "arbitrary"), vmem_limit_bytes=VMEM_LIMIT),
        name="attention",
    )(lq1, lk1, lq2, lk2, g_sub_col, bias, q, k, vt)


def _sublane_total(x, op):
    return op(x, axis=0, keepdims=True)


def _route_kernel(x_ref, yrec_ref, yatt_ref, wo32_ref, lnmoe_ref, wrt_ref, ebias_ref, tri_ref,
                  h1_ref, hpa_ref, hpb_ref, ek_ref, wk_ref, rk_ref, cnt_ref, carry_ref, wo_ref):
    tm = x_ref.shape[0]
    e_n = N_EXPERTS

    @pl.when(pl.program_id(0) == 0)
    def _():
        wo_ref[...] = wo32_ref[...].astype(BF16)

    @pl.when(pl.program_id(0) % (pl.num_programs(0) // MOE_PARTS) == 0)
    def _():
        carry_ref[...] = jnp.zeros_like(carry_ref)

    h1 = x_ref[...] + _dot(yrec_ref[...], wo_ref[:D_REC, :]) + _dot(yatt_ref[...], wo_ref[D_REC:, :])
    h1_ref[...] = h1
    hn = _rms(h1, lnmoe_ref[...])
    pa, pb = _pack_row(hn)
    hpa_ref[...] = pa
    hpb_ref[...] = pb

    logits = lax.dot_general(wrt_ref[...], hn, (((1,), (1,)), ((), ())),
                             precision=lax.Precision.HIGHEST, preferred_element_type=F32)
    scores = jax.nn.sigmoid(logits)
    sel = scores + ebias_ref[...]

    sel3 = sel.reshape(N_GROUPS, GROUP_SIZE, tm)
    idx3 = lax.broadcasted_iota(I32, (N_GROUPS, GROUP_SIZE, tm), 1)
    m1 = jnp.max(sel3, axis=1, keepdims=True)
    first = jnp.min(jnp.where(sel3 == m1, idx3, GROUP_SIZE), axis=1, keepdims=True)
    m2 = jnp.max(jnp.where(idx3 == first, -jnp.inf, sel3), axis=1, keepdims=True)
    gscore = (m1 + m2).reshape(N_GROUPS, tm)

    gidx = lax.broadcasted_iota(I32, (N_GROUPS, tm), 0)
    beaten = jnp.zeros((N_GROUPS, tm), I32)
    for g in range(N_GROUPS):
        other = gscore[g:g + 1, :]
        beats = (other > gscore) | ((other == gscore) & (g < gidx))
        beaten = beaten + beats.astype(I32)
    gkeep = beaten < TOPK_GROUPS
    keep = jnp.broadcast_to(gkeep.reshape(N_GROUPS, 1, tm), (N_GROUPS, GROUP_SIZE, tm)).reshape(e_n, tm)
    selm = jnp.where(keep, sel, -jnp.inf)

    eidx = lax.broadcasted_iota(I32, (e_n, tm), 0)
    remaining = selm
    picks, ek, sk = [], [], []
    for _ in range(TOP_K):
        best = jnp.max(remaining, axis=0, keepdims=True)
        first = jnp.min(jnp.where(remaining == best, eidx, e_n), axis=0, keepdims=True)
        pick = eidx == first
        picks.append(pick)
        ek.append(first)
        sk.append(_sublane_total(jnp.where(pick, scores, 0.0), jnp.sum))
        remaining = jnp.where(pick, -jnp.inf, remaining)
    chosen_f = (remaining != selm).astype(F32)
    wsum = functools.reduce(lambda a, b: a + b, sk)
    ek_ref[...] = jnp.concatenate(ek, axis=0)
    wk_ref[...] = (jnp.concatenate(sk, axis=0) * (ROUTE_SCALE / wsum)).T

    prefix = _dot(chosen_f.astype(BF16), tri_ref[...])
    rank = prefix + carry_ref[...]
    carry_new = carry_ref[...] + jnp.sum(chosen_f, axis=1, keepdims=True)
    carry_ref[...] = carry_new
    cnt_ref[0] = carry_new.astype(I32)
    rk = [_sublane_total(jnp.where(pick, rank, 0.0), jnp.sum) for pick in picks]
    rk_ref[...] = jnp.concatenate(rk, axis=0).astype(I32)


def _route(x2, y_rec, y_att, w_out, ln_moe, w_router_t, e_bias_col):
    t = x2.shape[0]
    tm = ROUTE_ROWS
    nt = t // tm
    row_map = lambda i: (i, 0)
    col_map = lambda i: (0, i)
    fixed = lambda i: (0, 0)
    tri = (lax.broadcasted_iota(I32, (tm, tm), 0) < lax.broadcasted_iota(I32, (tm, tm), 1)).astype(BF16)
    out_shapes = (
        jax.ShapeDtypeStruct((t, D_MODEL), F32),
        jax.ShapeDtypeStruct((t, PACK_W), U32),
        jax.ShapeDtypeStruct((t, PACK_W), U32),
        jax.ShapeDtypeStruct((TOP_K, t), I32),
        jax.ShapeDtypeStruct((t, TOP_K), F32),
        jax.ShapeDtypeStruct((TOP_K, t), I32),
        jax.ShapeDtypeStruct((MOE_PARTS, N_EXPERTS, 1), I32),
    )
    assert nt % MOE_PARTS == 0
    return pl.pallas_call(
        _route_kernel,
        grid=(nt,),
        in_specs=[
            pl.BlockSpec((tm, D_MODEL), row_map),
            pl.BlockSpec((tm, D_REC), row_map),
            pl.BlockSpec((tm, D_ATT), row_map),
            pl.BlockSpec((D_REC + D_ATT, D_MODEL), fixed, pipeline_mode=pl.Buffered(1)),
            pl.BlockSpec((1, D_MODEL), fixed),
            pl.BlockSpec((N_EXPERTS, D_MODEL), fixed),
            pl.BlockSpec((N_EXPERTS, 1), fixed),
            pl.BlockSpec((tm, tm), fixed),
        ],
        out_specs=[
            pl.BlockSpec((tm, D_MODEL), row_map),
            pl.BlockSpec((tm, PACK_W), row_map),
            pl.BlockSpec((tm, PACK_W), row_map),
            pl.BlockSpec((TOP_K, tm), col_map),
            pl.BlockSpec((tm, TOP_K), row_map),
            pl.BlockSpec((TOP_K, tm), col_map),
            pl.BlockSpec((1, N_EXPERTS, 1), lambda i: (i // (nt // MOE_PARTS), 0, 0)),
        ],
        out_shape=out_shapes,
        scratch_shapes=[pltpu.VMEM((N_EXPERTS, 1), F32), pltpu.VMEM((D_REC + D_ATT, D_MODEL), BF16)],
        compiler_params=pltpu.CompilerParams(dimension_semantics=("arbitrary",), vmem_limit_bytes=VMEM_LIMIT),
        name="route",
    )(x2, y_rec, y_att, w_out, ln_moe, w_router_t, e_bias_col, tri)


def _plan_kernel(pad_start_ref, ek_ref, rk_ref, dest_ref):
    ek = ek_ref[...]
    part = pl.program_id(0) // (pl.num_programs(0) // MOE_PARTS)

    def add_expert(e, base):
        return jnp.where(ek == e, pad_start_ref[part * N_EXPERTS + e], base)

    dest_ref[...] = rk_ref[...] + lax.fori_loop(0, N_EXPERTS, add_expert, jnp.zeros_like(ek))


def _plan(pad_start, ek, rk):
    kk, t = ek.shape
    tl = min(t // MOE_PARTS, 2048)
    col_map = lambda i, ps: (0, i)
    grid_spec = pltpu.PrefetchScalarGridSpec(
        num_scalar_prefetch=1,
        grid=(t // tl,),
        in_specs=[pl.BlockSpec((kk, tl), col_map), pl.BlockSpec((kk, tl), col_map)],
        out_specs=pl.BlockSpec((kk, tl), col_map),
    )
    return pl.pallas_call(
        _plan_kernel,
        grid_spec=grid_spec,
        out_shape=jax.ShapeDtypeStruct((kk, t), I32),
        compiler_params=pltpu.CompilerParams(dimension_semantics=("arbitrary",)),
        name="plan",
    )(pad_start, ek, rk)


def _sc_mesh():
    return plsc.VectorSubcoreMesh(core_axis_name="core", subcore_axis_name="subcore")


def _sc_dispatch(rows, dest, n_out, part):
    t, w = rows.shape
    kk = dest.shape[0]
    steps = t // MOE_PARTS // SC_WINDOW
    first = part * steps

    @pl.kernel(out_type=jax.ShapeDtypeStruct((n_out, w), rows.dtype), mesh=_sc_mesh(), scratch_types=[])
    def kern(x_hbm, i_hbm, o_hbm):
        def body(x_vmem, i_vmem):
            for k in range(kk):
                pltpu.sync_copy(x_vmem, o_hbm.at[i_vmem.at[k]])

        pltpu.emit_pipeline(
            body,
            grid=(steps,),
            in_specs=[pl.BlockSpec((SC_WINDOW, w), lambda i: (first + i, 0)),
                      pl.BlockSpec((kk, SC_WINDOW), lambda i: (0, first + i))],
            out_specs=[],
            core_axis_name=("core", "subcore"),
            dimension_semantics=(pltpu.PARALLEL,),
        )(x_hbm, i_hbm)

    return kern(rows, dest)


def _sc_combine(rows, dest, part):
    kk, t = dest.shape
    w = rows.shape[1]
    tp = t // MOE_PARTS
    flat = dest.reshape(1, kk * t)
    windows = tp // SC_WINDOW

    def index_block(i):
        return (0, (i // windows) * (t // SC_WINDOW) + part * windows + i % windows)

    @pl.kernel(out_type=jax.ShapeDtypeStruct((kk * tp, w), rows.dtype), mesh=_sc_mesh(), scratch_types=[])
    def kern(y_hbm, i_hbm, o_hbm):
        def body(i_vmem, o_vmem):
            pltpu.sync_copy(y_hbm.at[i_vmem.at[0]], o_vmem)

        pltpu.emit_pipeline(
            body,
            grid=(kk * windows,),
            in_specs=[pl.BlockSpec((1, SC_WINDOW), index_block)],
            out_specs=[pl.BlockSpec((SC_WINDOW, w), lambda i: (i, 0))],
            core_axis_name=("core", "subcore"),
            dimension_semantics=(pltpu.PARALLEL,),
        )(i_hbm, o_hbm)

    return kern(rows, flat).reshape(kk, tp, w)


X_SLOTS = 3
Y_SLOTS = 2


def _experts_kernel(blk_expert_ref, n_used_ref, first_ref, slot_ref, next_ref,
                    xa_hbm, xb_hbm, w1_hbm, w3_hbm, w2_hbm, ya_hbm, yb_hbm,
                    xa_buf, xb_buf, ya_buf, yb_buf, w1f_ref, w3f_ref, w2f_ref, w1b_ref, w3b_ref, w2b_ref,
                    wsem, xsem, ysem):
    m = EXPERT_ROWS
    n_used = n_used_ref[0]

    def weight_copies(e, s):
        return (pltpu.make_async_copy(w1_hbm.at[e], w1f_ref.at[s], wsem.at[s, 0]),
                pltpu.make_async_copy(w3_hbm.at[e], w3f_ref.at[s], wsem.at[s, 1]),
                pltpu.make_async_copy(w2_hbm.at[e], w2f_ref.at[s], wsem.at[s, 2]))

    def x_copies(b):
        rows, s = pl.ds(pl.multiple_of(b * m, m), m), b % X_SLOTS
        return (pltpu.make_async_copy(xa_hbm.at[rows], xa_buf.at[s], xsem.at[s, 0]),
                pltpu.make_async_copy(xb_hbm.at[rows], xb_buf.at[s], xsem.at[s, 1]))

    def y_copies(b):
        rows, s = pl.ds(pl.multiple_of(b * m, m), m), b % Y_SLOTS
        return (pltpu.make_async_copy(ya_buf.at[s], ya_hbm.at[rows], ysem.at[s, 0]),
                pltpu.make_async_copy(yb_buf.at[s], yb_hbm.at[rows], ysem.at[s, 1]))

    def start(copies):
        for copy in copies:
            copy.start()

    def wait(copies):
        for copy in copies:
            copy.wait()

    start(weight_copies(blk_expert_ref[0], 0))
    start(x_copies(0))

    @pl.when(n_used > 1)
    def _():
        start(x_copies(1))

    def block(b, carry):
        @pl.when(b + 2 < n_used)
        def _():
            start(x_copies(b + 2))

        @pl.when(first_ref[b] == 1)
        def _():
            s = slot_ref[b]
            wait(weight_copies(blk_expert_ref[b], s))

            @pl.when(next_ref[b] >= 0)
            def _():
                start(weight_copies(next_ref[b], 1 - s))

            w1b_ref[...] = w1f_ref[s].astype(BF16)
            w3b_ref[...] = w3f_ref[s].astype(BF16)
            w2b_ref[...] = w2f_ref[s].astype(BF16)

        wait(x_copies(b))

        @pl.when(b >= Y_SLOTS)
        def _():
            wait(y_copies(b - Y_SLOTS))

        x = _unpack_row(xa_buf[b % X_SLOTS], xb_buf[b % X_SLOTS]).astype(BF16)
        a = _dot(x, w1b_ref[...])
        g = _dot(x, w3b_ref[...])
        hmid = (jax.nn.silu(a) * g).astype(BF16)
        y = _dot(hmid, w2b_ref[...])
        pa, pb = _pack_row(y)
        ya_buf[b % Y_SLOTS] = pa
        yb_buf[b % Y_SLOTS] = pb
        start(y_copies(b))
        return carry

    lax.fori_loop(0, n_used, block, 0)

    for back in range(Y_SLOTS, 0, -1):
        @pl.when(n_used - back >= 0)
        def _(back=back):
            wait(y_copies(n_used - back))


def _experts(xa, xb, w1, w3, w2, blk_expert, n_used, seg_first, seg_slot, seg_next):
    p = xa.shape[0]
    m = EXPERT_ROWS
    hbm = pl.BlockSpec(memory_space=pl.ANY)
    grid_spec = pltpu.PrefetchScalarGridSpec(
        num_scalar_prefetch=5,
        grid=(1,),
        in_specs=[hbm] * 5,
        out_specs=[hbm, hbm],
        scratch_shapes=[
            pltpu.VMEM((X_SLOTS, m, PACK_W), U32), pltpu.VMEM((X_SLOTS, m, PACK_W), U32),
            pltpu.VMEM((Y_SLOTS, m, PACK_W), U32), pltpu.VMEM((Y_SLOTS, m, PACK_W), U32),
            pltpu.VMEM((2, D_MODEL, D_EXPERT), F32), pltpu.VMEM((2, D_MODEL, D_EXPERT), F32),
            pltpu.VMEM((2, D_EXPERT, D_MODEL), F32),
            pltpu.VMEM((D_MODEL, D_EXPERT), BF16), pltpu.VMEM((D_MODEL, D_EXPERT), BF16),
            pltpu.VMEM((D_EXPERT, D_MODEL), BF16),
            pltpu.SemaphoreType.DMA((2, 3)), pltpu.SemaphoreType.DMA((X_SLOTS, 2)),
            pltpu.SemaphoreType.DMA((Y_SLOTS, 2)),
        ],
    )
    return pl.pallas_call(
        _experts_kernel,
        grid_spec=grid_spec,
        out_shape=(jax.ShapeDtypeStruct((p, PACK_W), U32), jax.ShapeDtypeStruct((p, PACK_W), U32)),
        compiler_params=pltpu.CompilerParams(dimension_semantics=("arbitrary",), vmem_limit_bytes=VMEM_LIMIT),
        name="experts",
    )(blk_expert, n_used, seg_first, seg_slot, seg_next, xa, xb, w1, w3, w2)


def _tail_kernel(h1_ref, ga_ref, gb_ref, wk_ref, p_ref, lnmoe_ref, ws1_32, ws3_32, ws2_32, lnple_ref,
                 wpg_32, wpp_32, lnf_ref, o_ref, ws1_ref, ws3_ref, ws2_ref, wpg_ref, wpp_ref):
    @pl.when(pl.program_id(0) == 0)
    def _():
        for dst, src in ((ws1_ref, ws1_32), (ws3_ref, ws3_32), (ws2_ref, ws2_32), (wpg_ref, wpg_32),
                         (wpp_ref, wpp_32)):
            dst[...] = src[...].astype(BF16)

    h1 = h1_ref[...]
    hn = _rms(h1, lnmoe_ref[...]).astype(BF16)
    shared = _dot((jax.nn.silu(_dot(hn, ws1_ref[...])) * _dot(hn, ws3_ref[...])).astype(BF16), ws2_ref[...])
    wk = wk_ref[...]
    routed = jnp.zeros_like(h1)
    for kk in range(TOP_K):
        routed = routed + wk[:, kk:kk + 1] * _unpack_row(ga_ref[kk], gb_ref[kk])
    h2 = h1 + routed + shared
    gate = jax.nn.sigmoid(_dot(_rms(h2, lnple_ref[...]).astype(BF16), wpg_ref[...]))
    h3 = h2 + gate * _dot(p_ref[...].astype(BF16), wpp_ref[...])
    o_ref[...] = _rms(h3, lnf_ref[...])


def _tail_into_kernel(out_so_far_ref, *refs):
    del out_so_far_ref
    _tail_kernel(*refs)


def _tail(out_so_far, part, h1, ga, gb, wk_t, p2, ln_moe, ws1, ws3, ws2, ln_ple, w_pg, w_pp, ln_f):
    t = h1.shape[0]
    tm = TAIL_ROWS
    steps = t // MOE_PARTS // tm
    row_map = lambda i: (part * steps + i, 0)
    fixed = lambda i: (0, 0)
    g_map = lambda i: (0, i, 0)
    d_sh = ws1.shape[1]
    carried = () if out_so_far is None else (out_so_far,)
    return pl.pallas_call(
        _tail_kernel if out_so_far is None else _tail_into_kernel,
        grid=(steps,),
        input_output_aliases={} if out_so_far is None else {0: 0},
        in_specs=[pl.BlockSpec(memory_space=pl.ANY)] * len(carried) + [
            pl.BlockSpec((tm, D_MODEL), row_map),
            pl.BlockSpec((TOP_K, tm, PACK_W), g_map),
            pl.BlockSpec((TOP_K, tm, PACK_W), g_map),
            pl.BlockSpec((tm, TOP_K), row_map),
            pl.BlockSpec((tm, D_PLE), row_map),
            pl.BlockSpec((1, D_MODEL), fixed),
            pl.BlockSpec((D_MODEL, d_sh), fixed, pipeline_mode=pl.Buffered(1)),
            pl.BlockSpec((D_MODEL, d_sh), fixed, pipeline_mode=pl.Buffered(1)),
            pl.BlockSpec((d_sh, D_MODEL), fixed, pipeline_mode=pl.Buffered(1)),
            pl.BlockSpec((1, D_MODEL), fixed),
            pl.BlockSpec((D_MODEL, D_MODEL), fixed, pipeline_mode=pl.Buffered(1)),
            pl.BlockSpec((D_PLE, D_MODEL), fixed, pipeline_mode=pl.Buffered(1)),
            pl.BlockSpec((1, D_MODEL), fixed),
        ],
        out_specs=pl.BlockSpec((tm, D_MODEL), row_map),
        out_shape=jax.ShapeDtypeStruct((t, D_MODEL), F32),
        scratch_shapes=[pltpu.VMEM((D_MODEL, d_sh), BF16), pltpu.VMEM((D_MODEL, d_sh), BF16),
                        pltpu.VMEM((d_sh, D_MODEL), BF16), pltpu.VMEM((D_MODEL, D_MODEL), BF16),
                        pltpu.VMEM((D_PLE, D_MODEL), BF16)],
        compiler_params=pltpu.CompilerParams(dimension_semantics=("arbitrary",), vmem_limit_bytes=VMEM_LIMIT),
        name="tail",
    )(*carried, h1, ga, gb, wk_t, p2, ln_moe, ws1, ws3, ws2, ln_ple, w_pg, w_pp, ln_f)


def _rope_constants():
    half = ROPE_DIM // 2
    inv_freq = (ROPE_THETA ** (-jnp.arange(0, ROPE_DIM, 2, dtype=F32) / ROPE_DIM)).reshape(half, 1)
    f = lax.broadcasted_iota(I32, (ROPE_DIM, LANES), 0)
    l64 = lax.broadcasted_iota(I32, (ROPE_DIM, LANES), 1) % HALF_DIM
    cos_pat = ((f < half) & (l64 < ROPE_DIM) & (l64 % half == f)).astype(F32)
    sa_pat = -((f >= half) & (l64 < half) & (l64 == f - half)).astype(F32)
    sb_pat = ((f >= half) & (l64 >= half) & (l64 < ROPE_DIM) & (l64 - half == f - half)).astype(F32)
    return inv_freq, jnp.concatenate([cos_pat, sa_pat, sb_pat], axis=1)


def _block_diag_tiles(w):
    nb, bd, _ = w.shape
    per = nb // 2
    tiles = []
    for tix in range(2):
        rows = []
        for j in range(per):
            rows.append(jnp.concatenate(
                [w[tix * per + j] if c == j else jnp.zeros((bd, bd), w.dtype) for c in range(per)], axis=1))
        tiles.append(jnp.concatenate(rows, axis=0))
    return jnp.stack(tiles).astype(BF16)


def _layer(h, p_l, positions, lam_init, ln_mix, w_in, conv_w, conv_b, w_a, b_a, w_i, b_i, rg_lambda, g_rec,
           lq1, lk1, lq2, lk2, g_sub, w_out, ln_moe, w_router, e_bias, w1, w3, w2, ws1, ws3, ws2,
           ln_ple, w_ple_gate, w_ple_proj, ln_out):
    batch, seq, _ = h.shape
    t = batch * seq
    x2 = h.reshape(t, D_MODEL)
    row = lambda a: a.reshape(1, -1)
    inv_freq, rope_pat = _rope_constants()

    y_rec, q, k, vt = _mix_in(
        x2, positions.reshape(1, t), inv_freq, rope_pat, row(ln_mix), w_in, conv_w, row(conv_b),
        _block_diag_tiles(w_a), row(b_a), _block_diag_tiles(w_i), row(b_i), row(rg_lambda), row(g_rec),
        batch, seq)
    y_att = _attention(q, k, vt, row(lq1), row(lk1), row(lq2), row(lk2), g_sub.reshape(-1, 1), batch, seq,
                       lam_init)

    h1, hpa, hpb, ek, wk_t, rk, counts = _route(
        x2, y_rec, y_att, w_out, row(ln_moe), w_router.T, e_bias.reshape(-1, 1))

    m = EXPERT_ROWS
    counts = counts.reshape(MOE_PARTS, N_EXPERTS)
    padded = (counts + m - 1) // m * m
    pad_end = jnp.cumsum(padded, axis=1)
    pad_start = pad_end - padded
    n_rows = t // MOE_PARTS * TOP_K + N_EXPERTS * m
    nblk = n_rows // m
    n_used = (pad_end[:, -1] // m).astype(I32)
    blk = jnp.arange(nblk, dtype=I32)
    blk_row = jnp.minimum(blk[None, :], n_used[:, None] - 1) * m
    blk_expert = jnp.sum((pad_end[:, None, :] <= blk_row[:, :, None]).astype(I32), axis=2)
    prev_expert = jnp.concatenate([jnp.full((MOE_PARTS, 1), -1, I32), blk_expert[:, :-1]], axis=1)
    seg_first = ((blk[None, :] < n_used[:, None]) & (blk_expert != prev_expert)).astype(I32)
    seg_slot = ((jnp.cumsum(seg_first, axis=1) - 1) % 2).astype(I32)
    eid = jnp.arange(N_EXPERTS, dtype=I32)
    later = (padded[:, None, :] > 0) & (eid[None, None, :] > eid[None, :, None])
    next_expert = jnp.min(jnp.where(later, eid[None, None, :], N_EXPERTS), axis=2)
    next_expert = jnp.where(next_expert == N_EXPERTS, -1, next_expert).astype(I32)
    seg_next = jnp.sum(jnp.where(blk_expert[:, :, None] == eid[None, None, :], next_expert[:, None, :], 0), axis=2)
    dest = _plan(pad_start.astype(I32).reshape(-1), ek, rk)

    p2 = p_l.reshape(t, D_PLE)
    tail_weights = (row(ln_moe), ws1, ws3, ws2, row(ln_ple), w_ple_gate, w_ple_proj, row(ln_out))
    gathered = []
    for part in range(MOE_PARTS):
        xa = _sc_dispatch(hpa, dest, n_rows, part)
        xb = _sc_dispatch(hpb, dest, n_rows, part)
        ya, yb = _experts(xa, xb, w1, w3, w2, blk_expert[part], n_used[part:part + 1], seg_first[part],
                          seg_slot[part], seg_next[part])
        gathered.append((_sc_combine(ya, dest, part), _sc_combine(yb, dest, part)))
    out = None
    for part, (ga, gb) in enumerate(gathered):
        out = _tail(out, part, h1, ga, gb, wk_t, p2, *tail_weights)
    return out.reshape(batch, seq, D_MODEL)


def kernel(x, p, positions, ln_mix, w_in, conv_w, conv_b, w_a, b_a, w_i, b_i, rg_lambda, g_rec, lq1, lk1, lq2,
           lk2, g_sub, w_out, ln_moe, w_router, e_bias, w1, w3, w2, ws1, ws3, ws2, ln_ple, w_ple_gate,
           w_ple_proj, ln_f):
    depth = w_in.shape[0]
    assert depth == 1, "the fused tail applies the final norm; one layer supported"
    lam_init = 0.8 - 0.6 * math.exp(-0.3 * 0)
    return _layer(x, p[0], positions, lam_init, ln_mix[0], w_in[0], conv_w[0], conv_b[0], w_a[0], b_a[0], w_i[0],
                  b_i[0], rg_lambda[0], g_rec[0], lq1[0], lk1[0], lq2[0], lk2[0], g_sub[0], w_out[0], ln_moe[0],
                  w_router[0], e_bias[0], w1[0], w3[0], w2[0], ws1[0], ws3[0], ws2[0], ln_ple[0], w_ple_gate[0],
                  w_ple_proj[0], ln_f)
```

```python
import functools
import math

import jax
import jax.numpy as jnp
from jax import lax
from jax.experimental import pallas as pl
from jax.experimental.pallas import tpu as pltpu
from jax.experimental.pallas import tpu_sc as plsc

F32 = jnp.float32
BF16 = jnp.bfloat16
U32 = jnp.uint32
I32 = jnp.int32

D_MODEL = 1024
D_REC = 512
REC_BLOCKS = 8
CONV_WIDTH = 4
RG_C = 8.0
N_HEADS = 4
HALF_DIM = 64
V_DIM = 128
D_ATT = N_HEADS * V_DIM
D_QK = N_HEADS * 2 * HALF_DIM
ROPE_DIM = 16
ROPE_THETA = 500000.0
N_EXPERTS = 64
TOP_K = 8
N_GROUPS = 8
GROUP_SIZE = N_EXPERTS // N_GROUPS
TOPK_GROUPS = 4
D_EXPERT = 256
ROUTE_SCALE = 2.5
D_PLE = 256
EPS = 1e-6

LANES = 128
SUBLANES = 8
VMEM_LIMIT = 56 * 1024 * 1024

MIX_ROWS = 512
ATT_Q = 512
ATT_TAIL_LANES = 256
ONES_ROWS = 16
V_EXT = V_DIM + ONES_ROWS
ROUTE_ROWS = 1024
EXPERT_ROWS = 512
MOE_PARTS = 1
TAIL_ROWS = 512
SC_WINDOW = 128
PACK_W = 256
NEG_BIG = -1e30


def _rms(x, g):
    return x * lax.rsqrt(jnp.mean(x * x, axis=-1, keepdims=True) + EPS) * g


def _dot(a, b):
    return jnp.dot(a, b, preferred_element_type=F32)


def _pack_pair(lo, hi):
    lo_bits = lax.bitcast_convert_type(lo.astype(BF16).astype(F32), U32)
    hi_bits = lax.bitcast_convert_type(hi.astype(BF16).astype(F32), U32)
    return (lo_bits >> 16) | (hi_bits & jnp.uint32(0xFFFF0000))


def _unpack_pair(p):
    lo = lax.bitcast_convert_type(p << 16, F32)
    hi = lax.bitcast_convert_type(p & jnp.uint32(0xFFFF0000), F32)
    return lo, hi


def _pack_row(x):
    w = PACK_W
    return _pack_pair(x[:, 0:w], x[:, w:2 * w]), _pack_pair(x[:, 2 * w:3 * w], x[:, 3 * w:4 * w])


def _unpack_row(pa, pb):
    c0, c1 = _unpack_pair(pa)
    c2, c3 = _unpack_pair(pb)
    return jnp.concatenate([c0, c1, c2, c3], axis=1)


def _shift_rows(a, s, fill, row):
    n, c = a.shape
    if s % SUBLANES == 0:
        return jnp.concatenate([jnp.full((s, c), fill, a.dtype), a[:n - s]], axis=0)
    return jnp.where(row >= s, pltpu.roll(a, s, 0), fill)


def _mix_in_kernel(x_ref, pos_ref, invf_ref, pat_ref, lnm_ref, win32_ref, cw_ref, cb_ref, wa_ref, ba_ref,
                   wi_ref, bi_ref, lam_ref, grec_ref,
                   yrec_ref, q_ref, k_ref, vt_ref, tail_ref, hcarry_ref, buf_a, buf_b, win_ref):
    tm = x_ref.shape[0]
    groups = tm // SUBLANES
    chunks = D_REC // LANES

    def stage(ref, v):
        for c in range(chunks):
            ref[c] = v[:, c * LANES:(c + 1) * LANES]

    def slab(ref, r):
        return jnp.concatenate([ref[c, pl.ds(r, groups, stride=SUBLANES), :] for c in range(chunks)], axis=1)

    @pl.when((pl.program_id(0) == 0) & (pl.program_id(1) == 0))
    def _():
        win_ref[...] = win32_ref[...].astype(BF16)

    @pl.when(pl.program_id(1) == 0)
    def _():
        tail_ref[...] = jnp.zeros_like(tail_ref)
        hcarry_ref[...] = jnp.zeros_like(hcarry_ref)

    hn = _rms(x_ref[...], lnm_ref[...]).astype(BF16)

    ang = invf_ref[...] * pos_ref[...].astype(F32)
    cs = jnp.concatenate([jnp.cos(ang), jnp.sin(ang)], axis=0)
    tabs = lax.dot_general(cs, pat_ref[...], (((0,), (0,)), ((), ())),
                           precision=lax.Precision.HIGHEST, preferred_element_type=F32)
    lane64 = lax.broadcasted_iota(I32, (1, LANES), 1) % HALF_DIM
    cosf = tabs[:, 0:LANES] + (lane64 >= ROPE_DIM).astype(F32)
    sa, sb = tabs[:, LANES:2 * LANES], tabs[:, 2 * LANES:3 * LANES]

    def project_rotary(out_ref, off, mul):
        for c in range(0, D_QK // LANES, 2):
            z2 = _dot(hn, win_ref[:, off + c * LANES: off + (c + 2) * LANES])
            for cc in range(2):
                zc = z2[:, cc * LANES:(cc + 1) * LANES]
                rot = (zc * cosf + pltpu.roll(zc, LANES - ROPE_DIM // 2, 1) * sa
                       + pltpu.roll(zc, ROPE_DIM // 2, 1) * sb)
                out_ref[:, (c + cc) * LANES:(c + cc + 1) * LANES] = (rot * mul).astype(BF16)

    def project_v():
        vt = _dot(hn, win_ref[:, 2 * D_REC + 2 * D_QK:]).T.astype(BF16)
        for hd in range(N_HEADS):
            vt_ref[0, hd * V_EXT:hd * V_EXT + V_DIM, :] = vt[hd * V_DIM:(hd + 1) * V_DIM]
            vt_ref[0, hd * V_EXT + V_DIM:(hd + 1) * V_EXT, :] = jnp.ones((ONES_ROWS, tm), BF16)

    xr = _dot(hn, win_ref[:, 0:D_REC])
    stage(buf_a, xr)
    stage(buf_b, _dot(hn, win_ref[:, D_REC:2 * D_REC]))
    tail = tail_ref[...]
    tail_ref[...] = xr[tm - SUBLANES:, :]

    grow = lax.broadcasted_iota(I32, (groups, D_REC), 0)

    def down_one(a, first_row):
        return jnp.where(grow == 0, first_row, pltpu.roll(a, 1, 0))

    xs = [slab(buf_a, r) for r in range(SUBLANES)]
    wrapped = {r: down_one(xs[r], tail[r:r + 1, :]) for r in range(SUBLANES - CONV_WIDTH + 1, SUBLANES)}
    xc = []
    for r in range(SUBLANES):
        acc = cb_ref[...] + cw_ref[CONV_WIDTH - 1:CONV_WIDTH, :] * xs[r]
        for d in range(1, CONV_WIDTH):
            prev = xs[r - d] if r >= d else wrapped[r - d + SUBLANES]
            acc = acc + cw_ref[CONV_WIDTH - 1 - d:CONV_WIDTH - d, :] * prev
        xc.append(acc)
    xc = jnp.concatenate(xc, axis=0)

    xcb = xc.astype(BF16)
    half = D_REC // 2
    ra = jnp.concatenate([_dot(xcb[:, :half], wa_ref[0]), _dot(xcb[:, half:], wa_ref[1])], axis=1)
    ri = jnp.concatenate([_dot(xcb[:, :half], wi_ref[0]), _dot(xcb[:, half:], wi_ref[1])], axis=1)
    r_gate = jax.nn.sigmoid(ra + ba_ref[...])
    i_gate = jax.nn.sigmoid(ri + bi_ref[...])
    lam = lam_ref[...]
    softplus_neg = jnp.maximum(-lam, 0.0) + jnp.log(1.0 + jnp.exp(-jnp.abs(lam)))
    log_a = -RG_C * r_gate * softplus_neg
    a = jnp.exp(log_a)
    u = jnp.sqrt(1.0 - jnp.exp(2.0 * log_a)) * i_gate * xc

    rows = lambda v, r: v[r * groups:(r + 1) * groups]
    hs, ps = [rows(u, 0)], [rows(a, 0)]
    for r in range(1, SUBLANES):
        hs.append(rows(a, r) * hs[-1] + rows(u, r))
        ps.append(rows(a, r) * ps[-1])
    tot_a, tot_h = ps[-1], hs[-1]
    s = 1
    while s < groups:
        tot_h = tot_h + tot_a * _shift_rows(tot_h, s, 0.0, grow)
        tot_a = tot_a * _shift_rows(tot_a, s, 1.0, grow)
        s *= 2
    h_in = hcarry_ref[...]
    group_end = tot_h + tot_a * h_in
    hcarry_ref[...] = group_end[groups - 1:groups, :]
    group_in = down_one(group_end, h_in)

    for r in range(SUBLANES):
        h = hs[r] + ps[r] * group_in
        y = h * jax.nn.gelu(slab(buf_b, r))
        yn = _rms(y, grec_ref[...])
        for c in range(chunks):
            buf_a[c, pl.ds(r, groups, stride=SUBLANES), :] = yn[:, c * LANES:(c + 1) * LANES]
    for c in range(chunks):
        yrec_ref[:, c * LANES:(c + 1) * LANES] = buf_a[c].astype(BF16)

    project_rotary(q_ref, 2 * D_REC, HALF_DIM ** -0.5 * math.log2(math.e))
    project_rotary(k_ref, 2 * D_REC + D_QK, 1.0)
    project_v()


def _mix_in(x2, pos_row, inv_freq, rope_pat, ln_mix, w_in, conv_w, conv_b, wa_bd, b_a, wi_bd, b_i, rg_lambda,
            g_rec, batch, seq):
    tm = MIX_ROWS
    nt = seq // tm
    d_in = w_in.shape[1]
    row_map = lambda b, i: (b * nt + i, 0)
    fixed2 = lambda b, i: (0, 0)
    fixed3 = lambda b, i: (0, 0, 0)
    t = batch * seq
    out_shapes = (
        jax.ShapeDtypeStruct((t, D_REC), BF16),
        jax.ShapeDtypeStruct((t, D_QK), BF16),
        jax.ShapeDtypeStruct((t, D_QK), BF16),
        jax.ShapeDtypeStruct((t // tm, N_HEADS * V_EXT, tm), BF16),
    )
    return pl.pallas_call(
        _mix_in_kernel,
        grid=(batch, nt),
        in_specs=[
            pl.BlockSpec((tm, D_MODEL), row_map),
            pl.BlockSpec((1, tm), lambda b, i: (0, b * nt + i)),
            pl.BlockSpec((ROPE_DIM // 2, 1), fixed2),
            pl.BlockSpec((ROPE_DIM, 3 * LANES), fixed2),
            pl.BlockSpec((1, D_MODEL), fixed2),
            pl.BlockSpec((D_MODEL, d_in), fixed2, pipeline_mode=pl.Buffered(1)),
            pl.BlockSpec((CONV_WIDTH, D_REC), fixed2),
            pl.BlockSpec((1, D_REC), fixed2),
            pl.BlockSpec((2, D_REC // 2, D_REC // 2), fixed3),
            pl.BlockSpec((1, D_REC), fixed2),
            pl.BlockSpec((2, D_REC // 2, D_REC // 2), fixed3),
            pl.BlockSpec((1, D_REC), fixed2),
            pl.BlockSpec((1, D_REC), fixed2),
            pl.BlockSpec((1, D_REC), fixed2),
        ],
        out_specs=[
            pl.BlockSpec((tm, D_REC), row_map),
            pl.BlockSpec((tm, D_QK), row_map),
            pl.BlockSpec((tm, D_QK), row_map),
            pl.BlockSpec((1, N_HEADS * V_EXT, tm), lambda b, i: (b * nt + i, 0, 0)),
        ],
        out_shape=out_shapes,
        scratch_shapes=[pltpu.VMEM((SUBLANES, D_REC), F32), pltpu.VMEM((1, D_REC), F32),
                        pltpu.VMEM((D_REC // LANES, tm, LANES), F32),
                        pltpu.VMEM((D_REC // LANES, tm, LANES), F32),
                        pltpu.VMEM((D_MODEL, d_in), BF16)],
        compiler_params=pltpu.CompilerParams(
            dimension_semantics=("arbitrary", "arbitrary"), vmem_limit_bytes=VMEM_LIMIT),
        name="mix_in",
    )(x2, pos_row, inv_freq, rope_pat, ln_mix, w_in, conv_w, conv_b, wa_bd, b_a, wi_bd, b_i, rg_lambda, g_rec)


def _attn_kernel(lq1_ref, lk1_ref, lq2_ref, lk2_ref, gsub_ref, bias_ref, q_ref, k_ref, vt_ref, o_ref,
                 m_ref, acc_ref, aprev_ref, s0_ref, s1_ref, mb0_ref, mb1_ref, p0_ref, p1_ref, *, lam_init):
    tq = ATT_Q
    tk = vt_ref.shape[2]
    assert tq == tk, "the causal bias tile assumes the diagonal block is square"
    lam = (jnp.exp(jnp.sum(lq1_ref[...] * lk1_ref[...], axis=-1, keepdims=True))
           - jnp.exp(jnp.sum(lq2_ref[...] * lk2_ref[...], axis=-1, keepdims=True)) + lam_init)

    def query_tile(i, carry):
        _attn_query_tile(i, lam, gsub_ref, bias_ref, q_ref, k_ref, vt_ref, o_ref, m_ref, acc_ref, aprev_ref,
                         s0_ref, s1_ref, mb0_ref, mb1_ref, p0_ref, p1_ref, lam_init=lam_init)
        return carry

    lax.fori_loop(0, q_ref.shape[0] // tq, query_tile, 0)


def _attn_query_tile(i, lam, gsub_ref, bias_ref, q_ref, k_ref, vt_ref, o_ref, m_ref, acc_ref, aprev_ref,
                     s0_ref, s1_ref, mb0_ref, mb1_ref, p0_ref, p1_ref, *, lam_init):
    tq = ATT_Q
    tk = vt_ref.shape[2]
    q_rows = pl.ds(pl.multiple_of(i * tq, tq), tq)

    qt = q_ref[q_rows, :].astype(F32).T
    dim = lax.broadcasted_iota(I32, (LANES, tq), 0)
    qqt = jnp.concatenate([jnp.where(dim < HALF_DIM, qt, 0.0), jnp.where(dim >= HALF_DIM, qt, 0.0)],
                          axis=1).astype(BF16)

    n = (i * tq) // tk

    def scores(j):
        return _dot(k_ref[pl.ds(pl.multiple_of(j * tk, tk), tk), :], qqt)

    def probabilities(s, m_prev):
        m_new = jnp.maximum(m_prev, jnp.max(s, axis=0, keepdims=True))
        alpha = jnp.exp2(m_prev - m_new)
        p = jnp.exp2((s - m_new).astype(BF16))
        return p, alpha, m_new

    def store_scores(j, s_buf, mb_buf):
        s = scores(j)
        s_buf[...] = s
        mb_buf[...] = jnp.max(s, axis=0, keepdims=True)

    def pipe_step(j, cur, nxt, p_cur, p_prev):
        store_scores(j + 1, *nxt)
        s_cur, mb_cur = cur
        m_prev = m_ref[...]
        m_new = jnp.maximum(m_prev, mb_cur[...])
        p_cur[...] = jnp.exp2((s_cur[...] - m_new).astype(BF16))
        m_ref[...] = m_new
        acc_ref[...] = aprev_ref[...] * acc_ref[...] + _dot(vt_ref[jnp.maximum(j - 1, 0)], p_prev[...])
        aprev_ref[...] = jnp.exp2(m_prev - m_new)

    buf0, buf1 = (s0_ref, mb0_ref), (s1_ref, mb1_ref)
    m_ref[...] = jnp.full_like(m_ref, NEG_BIG)
    acc_ref[...] = jnp.zeros_like(acc_ref)
    aprev_ref[...] = jnp.ones_like(aprev_ref)
    odd = n % 2

    @pl.when(odd == 0)
    def _():
        p1_ref[...] = jnp.zeros_like(p1_ref)
        store_scores(0, *buf0)

    @pl.when(odd == 1)
    def _():
        p0_ref[...] = jnp.zeros_like(p0_ref)
        store_scores(0, *buf1)
        pipe_step(0, buf1, buf0, p1_ref, p0_ref)

    def pair(t, carry):
        j = 2 * t + odd
        pipe_step(j, buf0, buf1, p0_ref, p1_ref)
        pipe_step(j + 1, buf1, buf0, p1_ref, p0_ref)
        return carry

    lax.fori_loop(0, n // 2, pair, 0)

    parts = []
    for c0 in range(0, 2 * tq, ATT_TAIL_LANES):
        cols = slice(c0, c0 + ATT_TAIL_LANES)
        keys = c0 % tq + ATT_TAIL_LANES
        p, alpha, _ = probabilities(s0_ref[:keys, cols] + bias_ref[:keys, cols], m_ref[:, cols])
        part = aprev_ref[:, cols] * acc_ref[:, cols] + _dot(vt_ref[jnp.maximum(n - 1, 0)], p1_ref[:, cols])
        parts.append(alpha * part + _dot(vt_ref[n][:, :keys], p))
    acc = jnp.concatenate(parts, axis=1)

    o = acc[:V_DIM] / acc[V_DIM:V_DIM + 1]
    o = o[:, :tq] - lam * o[:, tq:]
    o = o * lax.rsqrt(jnp.mean(o * o, axis=0, keepdims=True) + EPS) * gsub_ref[...]
    o_ref[q_rows, :] = (o * (1.0 - lam_init)).T.astype(BF16)


def _attention(q, k, vt, lq1, lk1, lq2, lk2, g_sub_col, batch, seq, lam_init):
    tq = ATT_Q
    nq = seq // tq
    tk = vt.shape[2]
    nk = seq // tk
    vec = lambda b, h: (0, 0)
    per_head = lambda b, h: (b, h)
    visible = (lax.broadcasted_iota(I32, (tk, 2 * tq), 0) <= lax.broadcasted_iota(I32, (tk, 2 * tq), 1) % tq)
    bias = jnp.where(visible, 0.0, NEG_BIG).astype(F32)
    return pl.pallas_call(
        functools.partial(_attn_kernel, lam_init=lam_init),
        grid=(batch, N_HEADS),
        in_specs=[
            pl.BlockSpec((1, HALF_DIM), vec),
            pl.BlockSpec((1, HALF_DIM), vec),
            pl.BlockSpec((1, HALF_DIM), vec),
            pl.BlockSpec((1, HALF_DIM), vec),
            pl.BlockSpec((V_DIM, 1), vec),
            pl.BlockSpec((tk, 2 * tq), vec),
            pl.BlockSpec((seq, LANES), per_head),
            pl.BlockSpec((seq, LANES), per_head),
            pl.BlockSpec((nk, V_EXT, tk), lambda b, h: (b, h, 0)),
        ],
        out_specs=pl.BlockSpec((seq, V_DIM), per_head),
        out_shape=jax.ShapeDtypeStruct((batch * seq, D_ATT), BF16),
        scratch_shapes=[pltpu.VMEM((1, 2 * tq), F32),
                        pltpu.VMEM((V_EXT, 2 * tq), F32), pltpu.VMEM((1, 2 * tq), F32),
                        pltpu.VMEM((tk, 2 * tq), F32), pltpu.VMEM((tk, 2 * tq), F32),
                        pltpu.VMEM((1, 2 * tq), F32), pltpu.VMEM((1, 2 * tq), F32),
                        pltpu.VMEM((tk, 2 * tq), BF16), pltpu.VMEM((tk, 2 * tq), BF16)],
        compiler_params=pltpu.CompilerParams(
            dimension_semantics=("arbitrary", "arbitrary"), vmem_limit_bytes=VMEM_LIMIT),
        name="attention",
    )(lq1, lk1, lq2, lk2, g_sub_col, bias, q, k, vt)


def _sublane_total(x, op):
    return op(x, axis=0, keepdims=True)


def _route_kernel(x_ref, yrec_ref, yatt_ref, wo32_ref, lnmoe_ref, wrt_ref, ebias_ref, tri_ref,
                  h1_ref, hpa_ref, hpb_ref, ek_ref, wk_ref, rk_ref, cnt_ref, carry_ref, wo_ref):
    tm = x_ref.shape[0]
    e_n = N_EXPERTS

    @pl.when(pl.program_id(0) == 0)
    def _():
        wo_ref[...] = wo32_ref[...].astype(BF16)

    @pl.when(pl.program_id(0) % (pl.num_programs(0) // MOE_PARTS) == 0)
    def _():
        carry_ref[...] = jnp.zeros_like(carry_ref)

    h1 = x_ref[...] + _dot(yrec_ref[...], wo_ref[:D_REC, :]) + _dot(yatt_ref[...], wo_ref[D_REC:, :])
    h1_ref[...] = h1
    hn = _rms(h1, lnmoe_ref[...])
    pa, pb = _pack_row(hn)
    hpa_ref[...] = pa
    hpb_ref[...] = pb

    logits = lax.dot_general(wrt_ref[...], hn, (((1,), (1,)), ((), ())),
                             precision=lax.Precision.HIGHEST, preferred_element_type=F32)
    scores = jax.nn.sigmoid(logits)
    sel = scores + ebias_ref[...]

    sel3 = sel.reshape(N_GROUPS, GROUP_SIZE, tm)
    idx3 = lax.broadcasted_iota(I32, (N_GROUPS, GROUP_SIZE, tm), 1)
    m1 = jnp.max(sel3, axis=1, keepdims=True)
    first = jnp.min(jnp.where(sel3 == m1, idx3, GROUP_SIZE), axis=1, keepdims=True)
    m2 = jnp.max(jnp.where(idx3 == first, -jnp.inf, sel3), axis=1, keepdims=True)
    gscore = (m1 + m2).reshape(N_GROUPS, tm)

    gidx = lax.broadcasted_iota(I32, (N_GROUPS, tm), 0)
    beaten = jnp.zeros((N_GROUPS, tm), I32)
    for g in range(N_GROUPS):
        other = gscore[g:g + 1, :]
        beats = (other > gscore) | ((other == gscore) & (g < gidx))
        beaten = beaten + beats.astype(I32)
    gkeep = beaten < TOPK_GROUPS
    keep = jnp.broadcast_to(gkeep.reshape(N_GROUPS, 1, tm), (N_GROUPS, GROUP_SIZE, tm)).reshape(e_n, tm)
    selm = jnp.where(keep, sel, -jnp.inf)

    eidx = lax.broadcasted_iota(I32, (e_n, tm), 0)
    remaining = selm
    picks, ek, sk = [], [], []
    for _ in range(TOP_K):
        best = jnp.max(remaining, axis=0, keepdims=True)
        first = jnp.min(jnp.where(remaining == best, eidx, e_n), axis=0, keepdims=True)
        pick = eidx == first
        picks.append(pick)
        ek.append(first)
        sk.append(_sublane_total(jnp.where(pick, scores, 0.0), jnp.sum))
        remaining = jnp.where(pick, -jnp.inf, remaining)
    chosen_f = (remaining != selm).astype(F32)
    wsum = functools.reduce(lambda a, b: a + b, sk)
    ek_ref[...] = jnp.concatenate(ek, axis=0)
    wk_ref[...] = (jnp.concatenate(sk, axis=0) * (ROUTE_SCALE / wsum)).T

    prefix = _dot(chosen_f.astype(BF16), tri_ref[...])
    rank = prefix + carry_ref[...]
    carry_new = carry_ref[...] + jnp.sum(chosen_f, axis=1, keepdims=True)
    carry_ref[...] = carry_new
    cnt_ref[0] = carry_new.astype(I32)
    rk = [_sublane_total(jnp.where(pick, rank, 0.0), jnp.sum) for pick in picks]
    rk_ref[...] = jnp.concatenate(rk, axis=0).astype(I32)


def _route(x2, y_rec, y_att, w_out, ln_moe, w_router_t, e_bias_col):
    t = x2.shape[0]
    tm = ROUTE_ROWS
    nt = t // tm
    row_map = lambda i: (i, 0)
    col_map = lambda i: (0, i)
    fixed = lambda i: (0, 0)
    tri = (lax.broadcasted_iota(I32, (tm, tm), 0) < lax.broadcasted_iota(I32, (tm, tm), 1)).astype(BF16)
    out_shapes = (
        jax.ShapeDtypeStruct((t, D_MODEL), F32),
        jax.ShapeDtypeStruct((t, PACK_W), U32),
        jax.ShapeDtypeStruct((t, PACK_W), U32),
        jax.ShapeDtypeStruct((TOP_K, t), I32),
        jax.ShapeDtypeStruct((t, TOP_K), F32),
        jax.ShapeDtypeStruct((TOP_K, t), I32),
        jax.ShapeDtypeStruct((MOE_PARTS, N_EXPERTS, 1), I32),
    )
    assert nt % MOE_PARTS == 0
    return pl.pallas_call(
        _route_kernel,
        grid=(nt,),
        in_specs=[
            pl.BlockSpec((tm, D_MODEL), row_map),
            pl.BlockSpec((tm, D_REC), row_map),
            pl.BlockSpec((tm, D_ATT), row_map),
            pl.BlockSpec((D_REC + D_ATT, D_MODEL), fixed, pipeline_mode=pl.Buffered(1)),
            pl.BlockSpec((1, D_MODEL), fixed),
            pl.BlockSpec((N_EXPERTS, D_MODEL), fixed),
            pl.BlockSpec((N_EXPERTS, 1), fixed),
            pl.BlockSpec((tm, tm), fixed),
        ],
        out_specs=[
            pl.BlockSpec((tm, D_MODEL), row_map),
            pl.BlockSpec((tm, PACK_W), row_map),
            pl.BlockSpec((tm, PACK_W), row_map),
            pl.BlockSpec((TOP_K, tm), col_map),
            pl.BlockSpec((tm, TOP_K), row_map),
            pl.BlockSpec((TOP_K, tm), col_map),
            pl.BlockSpec((1, N_EXPERTS, 1), lambda i: (i // (nt // MOE_PARTS), 0, 0)),
        ],
        out_shape=out_shapes,
        scratch_shapes=[pltpu.VMEM((N_EXPERTS, 1), F32), pltpu.VMEM((D_REC + D_ATT, D_MODEL), BF16)],
        compiler_params=pltpu.CompilerParams(dimension_semantics=("arbitrary",), vmem_limit_bytes=VMEM_LIMIT),
        name="route",
    )(x2, y_rec, y_att, w_out, ln_moe, w_router_t, e_bias_col, tri)


def _plan_kernel(pad_start_ref, ek_ref, rk_ref, dest_ref):
    ek = ek_ref[...]
    part = pl.program_id(0) // (pl.num_programs(0) // MOE_PARTS)

    def add_expert(e, base):
        return jnp.where(ek == e, pad_start_ref[part * N_EXPERTS + e], base)

    dest_ref[...] = rk_ref[...] + lax.fori_loop(0, N_EXPERTS, add_expert, jnp.zeros_like(ek))


def _plan(pad_start, ek, rk):
    kk, t = ek.shape
    tl = min(t // MOE_PARTS, 2048)
    col_map = lambda i, ps: (0, i)
    grid_spec = pltpu.PrefetchScalarGridSpec(
        num_scalar_prefetch=1,
        grid=(t // tl,),
        in_specs=[pl.BlockSpec((kk, tl), col_map), pl.BlockSpec((kk, tl), col_map)],
        out_specs=pl.BlockSpec((kk, tl), col_map),
    )
    return pl.pallas_call(
        _plan_kernel,
        grid_spec=grid_spec,
        out_shape=jax.ShapeDtypeStruct((kk, t), I32),
        compiler_params=pltpu.CompilerParams(dimension_semantics=("arbitrary",)),
        name="plan",
    )(pad_start, ek, rk)


def _sc_mesh():
    return plsc.VectorSubcoreMesh(core_axis_name="core", subcore_axis_name="subcore")


def _sc_dispatch(rows, dest, n_out, part):
    t, w = rows.shape
    kk = dest.shape[0]
    steps = t // MOE_PARTS // SC_WINDOW
    first = part * steps

    @pl.kernel(out_type=jax.ShapeDtypeStruct((n_out, w), rows.dtype), mesh=_sc_mesh(), scratch_types=[])
    def kern(x_hbm, i_hbm, o_hbm):
        def body(x_vmem, i_vmem):
            for k in range(kk):
                pltpu.sync_copy(x_vmem, o_hbm.at[i_vmem.at[k]])

        pltpu.emit_pipeline(
            body,
            grid=(steps,),
            in_specs=[pl.BlockSpec((SC_WINDOW, w), lambda i: (first + i, 0)),
                      pl.BlockSpec((kk, SC_WINDOW), lambda i: (0, first + i))],
            out_specs=[],
            core_axis_name=("core", "subcore"),
            dimension_semantics=(pltpu.PARALLEL,),
        )(x_hbm, i_hbm)

    return kern(rows, dest)


def _sc_combine(rows, dest, part):
    kk, t = dest.shape
    w = rows.shape[1]
    tp = t // MOE_PARTS
    flat = dest.reshape(1, kk * t)
    windows = tp // SC_WINDOW

    def index_block(i):
        return (0, (i // windows) * (t // SC_WINDOW) + part * windows + i % windows)

    @pl.kernel(out_type=jax.ShapeDtypeStruct((kk * tp, w), rows.dtype), mesh=_sc_mesh(), scratch_types=[])
    def kern(y_hbm, i_hbm, o_hbm):
        def body(i_vmem, o_vmem):
            pltpu.sync_copy(y_hbm.at[i_vmem.at[0]], o_vmem)

        pltpu.emit_pipeline(
            body,
            grid=(kk * windows,),
            in_specs=[pl.BlockSpec((1, SC_WINDOW), index_block)],
            out_specs=[pl.BlockSpec((SC_WINDOW, w), lambda i: (i, 0))],
            core_axis_name=("core", "subcore"),
            dimension_semantics=(pltpu.PARALLEL,),
        )(i_hbm, o_hbm)

    return kern(rows, flat).reshape(kk, tp, w)


X_SLOTS = 3
Y_SLOTS = 2


def _experts_kernel(blk_expert_ref, n_used_ref, first_ref, slot_ref, next_ref,
                    xa_hbm, xb_hbm, w1_hbm, w3_hbm, w2_hbm, ya_hbm, yb_hbm,
                    xa_buf, xb_buf, ya_buf, yb_buf, w1f_ref, w3f_ref, w2f_ref, w1b_ref, w3b_ref, w2b_ref,
                    wsem, xsem, ysem):
    m = EXPERT_ROWS
    n_used = n_used_ref[0]

    def weight_copies(e, s):
        return (pltpu.make_async_copy(w1_hbm.at[e], w1f_ref.at[s], wsem.at[s, 0]),
                pltpu.make_async_copy(w3_hbm.at[e], w3f_ref.at[s], wsem.at[s, 1]),
                pltpu.make_async_copy(w2_hbm.at[e], w2f_ref.at[s], wsem.at[s, 2]))

    def x_copies(b):
        rows, s = pl.ds(pl.multiple_of(b * m, m), m), b % X_SLOTS
        return (pltpu.make_async_copy(xa_hbm.at[rows], xa_buf.at[s], xsem.at[s, 0]),
                pltpu.make_async_copy(xb_hbm.at[rows], xb_buf.at[s], xsem.at[s, 1]))

    def y_copies(b):
        rows, s = pl.ds(pl.multiple_of(b * m, m), m), b % Y_SLOTS
        return (pltpu.make_async_copy(ya_buf.at[s], ya_hbm.at[rows], ysem.at[s, 0]),
                pltpu.make_async_copy(yb_buf.at[s], yb_hbm.at[rows], ysem.at[s, 1]))

    def start(copies):
        for copy in copies:
            copy.start()

    def wait(copies):
        for copy in copies:
            copy.wait()

    start(weight_copies(blk_expert_ref[0], 0))
    start(x_copies(0))

    @pl.when(n_used > 1)
    def _():
        start(x_copies(1))

    def block(b, carry):
        @pl.when(b + 2 < n_used)
        def _():
            start(x_copies(b + 2))

        @pl.when(first_ref[b] == 1)
        def _():
            s = slot_ref[b]
            wait(weight_copies(blk_expert_ref[b], s))

            @pl.when(next_ref[b] >= 0)
            def _():
                start(weight_copies(next_ref[b], 1 - s))

            w1b_ref[...] = w1f_ref[s].astype(BF16)
            w3b_ref[...] = w3f_ref[s].astype(BF16)
            w2b_ref[...] = w2f_ref[s].astype(BF16)

        wait(x_copies(b))

        @pl.when(b >= Y_SLOTS)
        def _():
            wait(y_copies(b - Y_SLOTS))

        x = _unpack_row(xa_buf[b % X_SLOTS], xb_buf[b % X_SLOTS]).astype(BF16)
        a = _dot(x, w1b_ref[...])
        g = _dot(x, w3b_ref[...])
        hmid = (jax.nn.silu(a) * g).astype(BF16)
        y = _dot(hmid, w2b_ref[...])
        pa, pb = _pack_row(y)
        ya_buf[b % Y_SLOTS] = pa
        yb_buf[b % Y_SLOTS] = pb
        start(y_copies(b))
        return carry

    lax.fori_loop(0, n_used, block, 0)

    for back in range(Y_SLOTS, 0, -1):
        @pl.when(n_used - back >= 0)
        def _(back=back):
            wait(y_copies(n_used - back))


def _experts(xa, xb, w1, w3, w2, blk_expert, n_used, seg_first, seg_slot, seg_next):
    p = xa.shape[0]
    m = EXPERT_ROWS
    hbm = pl.BlockSpec(memory_space=pl.ANY)
    grid_spec = pltpu.PrefetchScalarGridSpec(
        num_scalar_prefetch=5,
        grid=(1,),
        in_specs=[hbm] * 5,
        out_specs=[hbm, hbm],
        scratch_shapes=[
            pltpu.VMEM((X_SLOTS, m, PACK_W), U32), pltpu.VMEM((X_SLOTS, m, PACK_W), U32),
            pltpu.VMEM((Y_SLOTS, m, PACK_W), U32), pltpu.VMEM((Y_SLOTS, m, PACK_W), U32),
            pltpu.VMEM((2, D_MODEL, D_EXPERT), F32), pltpu.VMEM((2, D_MODEL, D_EXPERT), F32),
            pltpu.VMEM((2, D_EXPERT, D_MODEL), F32),
            pltpu.VMEM((D_MODEL, D_EXPERT), BF16), pltpu.VMEM((D_MODEL, D_EXPERT), BF16),
            pltpu.VMEM((D_EXPERT, D_MODEL), BF16),
            pltpu.SemaphoreType.DMA((2, 3)), pltpu.SemaphoreType.DMA((X_SLOTS, 2)),
            pltpu.SemaphoreType.DMA((Y_SLOTS, 2)),
        ],
    )
    return pl.pallas_call(
        _experts_kernel,
        grid_spec=grid_spec,
        out_shape=(jax.ShapeDtypeStruct((p, PACK_W), U32), jax.ShapeDtypeStruct((p, PACK_W), U32)),
        compiler_params=pltpu.CompilerParams(dimension_semantics=("arbitrary",), vmem_limit_bytes=VMEM_LIMIT),
        name="experts",
    )(blk_expert, n_used, seg_first, seg_slot, seg_next, xa, xb, w1, w3, w2)


def _tail_kernel(h1_ref, ga_ref, gb_ref, wk_ref, p_ref, lnmoe_ref, ws1_32, ws3_32, ws2_32, lnple_ref,
                 wpg_32, wpp_32, lnf_ref, o_ref, ws1_ref, ws3_ref, ws2_ref, wpg_ref, wpp_ref):
    @pl.when(pl.program_id(0) == 0)
    def _():
        for dst, src in ((ws1_ref, ws1_32), (ws3_ref, ws3_32), (ws2_ref, ws2_32), (wpg_ref, wpg_32),
                         (wpp_ref, wpp_32)):
            dst[...] = src[...].astype(BF16)

    h1 = h1_ref[...]
    hn = _rms(h1, lnmoe_ref[...]).astype(BF16)
    shared = _dot((jax.nn.silu(_dot(hn, ws1_ref[...])) * _dot(hn, ws3_ref[...])).astype(BF16), ws2_ref[...])
    wk = wk_ref[...]
    routed = jnp.zeros_like(h1)
    for kk in range(TOP_K):
        routed = routed + wk[:, kk:kk + 1] * _unpack_row(ga_ref[kk], gb_ref[kk])
    h2 = h1 + routed + shared
    gate = jax.nn.sigmoid(_dot(_rms(h2, lnple_ref[...]).astype(BF16), wpg_ref[...]))
    h3 = h2 + gate * _dot(p_ref[...].astype(BF16), wpp_ref[...])
    o_ref[...] = _rms(h3, lnf_ref[...])


def _tail_into_kernel(out_so_far_ref, *refs):
    del out_so_far_ref
    _tail_kernel(*refs)


def _tail(out_so_far, part, h1, ga, gb, wk_t, p2, ln_moe, ws1, ws3, ws2, ln_ple, w_pg, w_pp, ln_f):
    t = h1.shape[0]
    tm = TAIL_ROWS
    steps = t // MOE_PARTS // tm
    row_map = lambda i: (part * steps + i, 0)
    fixed = lambda i: (0, 0)
    g_map = lambda i: (0, i, 0)
    d_sh = ws1.shape[1]
    carried = () if out_so_far is None else (out_so_far,)
    return pl.pallas_call(
        _tail_kernel if out_so_far is None else _tail_into_kernel,
        grid=(steps,),
        input_output_aliases={} if out_so_far is None else {0: 0},
        in_specs=[pl.BlockSpec(memory_space=pl.ANY)] * len(carried) + [
            pl.BlockSpec((tm, D_MODEL), row_map),
            pl.BlockSpec((TOP_K, tm, PACK_W), g_map),
            pl.BlockSpec((TOP_K, tm, PACK_W), g_map),
            pl.BlockSpec((tm, TOP_K), row_map),
            pl.BlockSpec((tm, D_PLE), row_map),
            pl.BlockSpec((1, D_MODEL), fixed),
            pl.BlockSpec((D_MODEL, d_sh), fixed, pipeline_mode=pl.Buffered(1)),
            pl.BlockSpec((D_MODEL, d_sh), fixed, pipeline_mode=pl.Buffered(1)),
            pl.BlockSpec((d_sh, D_MODEL), fixed, pipeline_mode=pl.Buffered(1)),
            pl.BlockSpec((1, D_MODEL), fixed),
            pl.BlockSpec((D_MODEL, D_MODEL), fixed, pipeline_mode=pl.Buffered(1)),
            pl.BlockSpec((D_PLE, D_MODEL), fixed, pipeline_mode=pl.Buffered(1)),
            pl.BlockSpec((1, D_MODEL), fixed),
        ],
        out_specs=pl.BlockSpec((tm, D_MODEL), row_map),
        out_shape=jax.ShapeDtypeStruct((t, D_MODEL), F32),
        scratch_shapes=[pltpu.VMEM((D_MODEL, d_sh), BF16), pltpu.VMEM((D_MODEL, d_sh), BF16),
                        pltpu.VMEM((d_sh, D_MODEL), BF16), pltpu.VMEM((D_MODEL, D_MODEL), BF16),
                        pltpu.VMEM((D_PLE, D_MODEL), BF16)],
        compiler_params=pltpu.CompilerParams(dimension_semantics=("arbitrary",), vmem_limit_bytes=VMEM_LIMIT),
        name="tail",
    )(*carried, h1, ga, gb, wk_t, p2, ln_moe, ws1, ws3, ws2, ln_ple, w_pg, w_pp, ln_f)


def _rope_constants():
    half = ROPE_DIM // 2
    inv_freq = (ROPE_THETA ** (-jnp.arange(0, ROPE_DIM, 2, dtype=F32) / ROPE_DIM)).reshape(half, 1)
    f = lax.broadcasted_iota(I32, (ROPE_DIM, LANES), 0)
    l64 = lax.broadcasted_iota(I32, (ROPE_DIM, LANES), 1) % HALF_DIM
    cos_pat = ((f < half) & (l64 < ROPE_DIM) & (l64 % half == f)).astype(F32)
    sa_pat = -((f >= half) & (l64 < half) & (l64 == f - half)).astype(F32)
    sb_pat = ((f >= half) & (l64 >= half) & (l64 < ROPE_DIM) & (l64 - half == f - half)).astype(F32)
    return inv_freq, jnp.concatenate([cos_pat, sa_pat, sb_pat], axis=1)


def _block_diag_tiles(w):
    nb, bd, _ = w.shape
    per = nb // 2
    tiles = []
    for tix in range(2):
        rows = []
        for j in range(per):
            rows.append(jnp.concatenate(
                [w[tix * per + j] if c == j else jnp.zeros((bd, bd), w.dtype) for c in range(per)], axis=1))
        tiles.append(jnp.concatenate(rows, axis=0))
    return jnp.stack(tiles).astype(BF16)


def _layer(h, p_l, positions, lam_init, ln_mix, w_in, conv_w, conv_b, w_a, b_a, w_i, b_i, rg_lambda, g_rec,
           lq1, lk1, lq2, lk2, g_sub, w_out, ln_moe, w_router, e_bias, w1, w3, w2, ws1, ws3, ws2,
           ln_ple, w_ple_gate, w_ple_proj, ln_out):
    batch, seq, _ = h.shape
    t = batch * seq
    x2 = h.reshape(t, D_MODEL)
    row = lambda a: a.reshape(1, -1)
    inv_freq, rope_pat = _rope_constants()

    y_rec, q, k, vt = _mix_in(
        x2, positions.reshape(1, t), inv_freq, rope_pat, row(ln_mix), w_in, conv_w, row(conv_b),
        _block_diag_tiles(w_a), row(b_a), _block_diag_tiles(w_i), row(b_i), row(rg_lambda), row(g_rec),
        batch, seq)
    y_att = _attention(q, k, vt, row(lq1), row(lk1), row(lq2), row(lk2), g_sub.reshape(-1, 1), batch, seq,
                       lam_init)

    h1, hpa, hpb, ek, wk_t, rk, counts = _route(
        x2, y_rec, y_att, w_out, row(ln_moe), w_router.T, e_bias.reshape(-1, 1))

    m = EXPERT_ROWS
    counts = counts.reshape(MOE_PARTS, N_EXPERTS)
    padded = (counts + m - 1) // m * m
    pad_end = jnp.cumsum(padded, axis=1)
    pad_start = pad_end - padded
    n_rows = t // MOE_PARTS * TOP_K + N_EXPERTS * m
    nblk = n_rows // m
    n_used = (pad_end[:, -1] // m).astype(I32)
    blk = jnp.arange(nblk, dtype=I32)
    blk_row = jnp.minimum(blk[None, :], n_used[:, None] - 1) * m
    blk_expert = jnp.sum((pad_end[:, None, :] <= blk_row[:, :, None]).astype(I32), axis=2)
    prev_expert = jnp.concatenate([jnp.full((MOE_PARTS, 1), -1, I32), blk_expert[:, :-1]], axis=1)
    seg_first = ((blk[None, :] < n_used[:, None]) & (blk_expert != prev_expert)).astype(I32)
    seg_slot = ((jnp.cumsum(seg_first, axis=1) - 1) % 2).astype(I32)
    eid = jnp.arange(N_EXPERTS, dtype=I32)
    later = (padded[:, None, :] > 0) & (eid[None, None, :] > eid[None, :, None])
    next_expert = jnp.min(jnp.where(later, eid[None, None, :], N_EXPERTS), axis=2)
    next_expert = jnp.where(next_expert == N_EXPERTS, -1, next_expert).astype(I32)
    seg_next = jnp.sum(jnp.where(blk_expert[:, :, None] == eid[None, None, :], next_expert[:, None, :], 0), axis=2)
    dest = _plan(pad_start.astype(I32).reshape(-1), ek, rk)

    p2 = p_l.reshape(t, D_PLE)
    tail_weights = (row(ln_moe), ws1, ws3, ws2, row(ln_ple), w_ple_gate, w_ple_proj, row(ln_out))
    gathered = []
    for part in range(MOE_PARTS):
        xa = _sc_dispatch(hpa, dest, n_rows, part)
        xb = _sc_dispatch(hpb, dest, n_rows, part)
        ya, yb = _experts(xa, xb, w1, w3, w2, blk_expert[part], n_used[part:part + 1], seg_first[part],
                          seg_slot[part], seg_next[part])
        gathered.append((_sc_combine(ya, dest, part), _sc_combine(yb, dest, part)))
    out = None
    for part, (ga, gb) in enumerate(gathered):
        out = _tail(out, part, h1, ga, gb, wk_t, p2, *tail_weights)
    return out.reshape(batch, seq, D_MODEL)


def kernel(x, p, positions, ln_mix, w_in, conv_w, conv_b, w_a, b_a, w_i, b_i, rg_lambda, g_rec, lq1, lk1, lq2,
           lk2, g_sub, w_out, ln_moe, w_router, e_bias, w1, w3, w2, ws1, ws3, ws2, ln_ple, w_ple_gate,
           w_ple_proj, ln_f):
    depth = w_in.shape[0]
    assert depth == 1, "the fused tail applies the final norm; one layer supported"
    lam_init = 0.8 - 0.6 * math.exp(-0.3 * 0)
    return _layer(x, p[0], positions, lam_init, ln_mix[0], w_in[0], conv_w[0], conv_b[0], w_a[0], b_a[0], w_i[0],
                  b_i[0], rg_lambda[0], g_rec[0], lq1[0], lk1[0], lq2[0], lk2[0], g_sub[0], w_out[0], ln_moe[0],
                  w_router[0], e_bias[0], w1[0], w3[0], w2[0], ws1[0], ws3[0], ws2[0], ln_ple[0], w_ple_gate[0],
                  w_ple_proj[0], ln_f)
```

```python
import functools
import math

import jax
import jax.numpy as jnp
from jax import lax
from jax.experimental import pallas as pl
from jax.experimental.pallas import tpu as pltpu
from jax.experimental.pallas import tpu_sc as plsc

F32 = jnp.float32
BF16 = jnp.bfloat16
U32 = jnp.uint32
I32 = jnp.int32

D_MODEL = 1024
D_REC = 512
REC_BLOCKS = 8
CONV_WIDTH = 4
RG_C = 8.0
N_HEADS = 4
HALF_DIM = 64
V_DIM = 128
D_ATT = N_HEADS * V_DIM
D_QK = N_HEADS * 2 * HALF_DIM
ROPE_DIM = 16
ROPE_THETA = 500000.0
N_EXPERTS = 64
TOP_K = 8
N_GROUPS = 8
GROUP_SIZE = N_EXPERTS // N_GROUPS
TOPK_GROUPS = 4
D_EXPERT = 256
ROUTE_SCALE = 2.5
D_PLE = 256
EPS = 1e-6

LANES = 128
SUBLANES = 8
VMEM_LIMIT = 56 * 1024 * 1024

MIX_ROWS = 512
ATT_Q = 512
ATT_TAIL_LANES = 256
ONES_ROWS = 16
V_EXT = V_DIM + ONES_ROWS
ROUTE_ROWS = 1024
EXPERT_ROWS = 512
MOE_PARTS = 1
BATCH_GROUPS = 2
TAIL_ROWS = 512
SC_WINDOW = 128
PACK_W = 256
NEG_BIG = -1e30


def _rms(x, g):
    return x * lax.rsqrt(jnp.mean(x * x, axis=-1, keepdims=True) + EPS) * g


def _dot(a, b):
    return jnp.dot(a, b, preferred_element_type=F32)


def _pack_pair(lo, hi):
    lo_bits = lax.bitcast_convert_type(lo.astype(BF16).astype(F32), U32)
    hi_bits = lax.bitcast_convert_type(hi.astype(BF16).astype(F32), U32)
    return (lo_bits >> 16) | (hi_bits & jnp.uint32(0xFFFF0000))


def _unpack_pair(p):
    lo = lax.bitcast_convert_type(p << 16, F32)
    hi = lax.bitcast_convert_type(p & jnp.uint32(0xFFFF0000), F32)
    return lo, hi


def _pack_row(x):
    w = PACK_W
    return _pack_pair(x[:, 0:w], x[:, w:2 * w]), _pack_pair(x[:, 2 * w:3 * w], x[:, 3 * w:4 * w])


def _unpack_row(pa, pb):
    c0, c1 = _unpack_pair(pa)
    c2, c3 = _unpack_pair(pb)
    return jnp.concatenate([c0, c1, c2, c3], axis=1)


def _shift_rows(a, s, fill, row):
    n, c = a.shape
    if s % SUBLANES == 0:
        return jnp.concatenate([jnp.full((s, c), fill, a.dtype), a[:n - s]], axis=0)
    return jnp.where(row >= s, pltpu.roll(a, s, 0), fill)


def _mix_in_kernel(x_ref, pos_ref, invf_ref, pat_ref, lnm_ref, win32_ref, cw_ref, cb_ref, wa_ref, ba_ref,
                   wi_ref, bi_ref, lam_ref, grec_ref,
                   yrec_ref, q_ref, k_ref, vt_ref, tail_ref, hcarry_ref, buf_a, buf_b, win_ref):
    tm = x_ref.shape[0]
    groups = tm // SUBLANES
    chunks = D_REC // LANES

    def stage(ref, v):
        for c in range(chunks):
            ref[c] = v[:, c * LANES:(c + 1) * LANES]

    def slab(ref, r):
        return jnp.concatenate([ref[c, pl.ds(r, groups, stride=SUBLANES), :] for c in range(chunks)], axis=1)

    @pl.when((pl.program_id(0) == 0) & (pl.program_id(1) == 0))
    def _():
        win_ref[...] = win32_ref[...].astype(BF16)

    @pl.when(pl.program_id(1) == 0)
    def _():
        tail_ref[...] = jnp.zeros_like(tail_ref)
        hcarry_ref[...] = jnp.zeros_like(hcarry_ref)

    hn = _rms(x_ref[...], lnm_ref[...]).astype(BF16)

    ang = invf_ref[...] * pos_ref[...].astype(F32)
    cs = jnp.concatenate([jnp.cos(ang), jnp.sin(ang)], axis=0)
    tabs = lax.dot_general(cs, pat_ref[...], (((0,), (0,)), ((), ())),
                           precision=lax.Precision.HIGHEST, preferred_element_type=F32)
    lane64 = lax.broadcasted_iota(I32, (1, LANES), 1) % HALF_DIM
    cosf = tabs[:, 0:LANES] + (lane64 >= ROPE_DIM).astype(F32)
    sa, sb = tabs[:, LANES:2 * LANES], tabs[:, 2 * LANES:3 * LANES]

    def project_rotary(out_ref, off, mul):
        for c in range(0, D_QK // LANES, 2):
            z2 = _dot(hn, win_ref[:, off + c * LANES: off + (c + 2) * LANES])
            for cc in range(2):
                zc = z2[:, cc * LANES:(cc + 1) * LANES]
                rot = (zc * cosf + pltpu.roll(zc, LANES - ROPE_DIM // 2, 1) * sa
                       + pltpu.roll(zc, ROPE_DIM // 2, 1) * sb)
                out_ref[:, (c + cc) * LANES:(c + cc + 1) * LANES] = (rot * mul).astype(BF16)

    def project_v():
        vt = _dot(hn, win_ref[:, 2 * D_REC + 2 * D_QK:]).T.astype(BF16)
        for hd in range(N_HEADS):
            vt_ref[0, hd * V_EXT:hd * V_EXT + V_DIM, :] = vt[hd * V_DIM:(hd + 1) * V_DIM]
            vt_ref[0, hd * V_EXT + V_DIM:(hd + 1) * V_EXT, :] = jnp.ones((ONES_ROWS, tm), BF16)

    xr = _dot(hn, win_ref[:, 0:D_REC])
    stage(buf_a, xr)
    stage(buf_b, _dot(hn, win_ref[:, D_REC:2 * D_REC]))
    tail = tail_ref[...]
    tail_ref[...] = xr[tm - SUBLANES:, :]

    grow = lax.broadcasted_iota(I32, (groups, D_REC), 0)

    def down_one(a, first_row):
        return jnp.where(grow == 0, first_row, pltpu.roll(a, 1, 0))

    xs = [slab(buf_a, r) for r in range(SUBLANES)]
    wrapped = {r: down_one(xs[r], tail[r:r + 1, :]) for r in range(SUBLANES - CONV_WIDTH + 1, SUBLANES)}
    xc = []
    for r in range(SUBLANES):
        acc = cb_ref[...] + cw_ref[CONV_WIDTH - 1:CONV_WIDTH, :] * xs[r]
        for d in range(1, CONV_WIDTH):
            prev = xs[r - d] if r >= d else wrapped[r - d + SUBLANES]
            acc = acc + cw_ref[CONV_WIDTH - 1 - d:CONV_WIDTH - d, :] * prev
        xc.append(acc)
    xc = jnp.concatenate(xc, axis=0)

    xcb = xc.astype(BF16)
    half = D_REC // 2
    ra = jnp.concatenate([_dot(xcb[:, :half], wa_ref[0]), _dot(xcb[:, half:], wa_ref[1])], axis=1)
    ri = jnp.concatenate([_dot(xcb[:, :half], wi_ref[0]), _dot(xcb[:, half:], wi_ref[1])], axis=1)
    r_gate = jax.nn.sigmoid(ra + ba_ref[...])
    i_gate = jax.nn.sigmoid(ri + bi_ref[...])
    lam = lam_ref[...]
    softplus_neg = jnp.maximum(-lam, 0.0) + jnp.log(1.0 + jnp.exp(-jnp.abs(lam)))
    log_a = -RG_C * r_gate * softplus_neg
    a = jnp.exp(log_a)
    u = jnp.sqrt(1.0 - jnp.exp(2.0 * log_a)) * i_gate * xc

    rows = lambda v, r: v[r * groups:(r + 1) * groups]
    hs, ps = [rows(u, 0)], [rows(a, 0)]
    for r in range(1, SUBLANES):
        hs.append(rows(a, r) * hs[-1] + rows(u, r))
        ps.append(rows(a, r) * ps[-1])
    tot_a, tot_h = ps[-1], hs[-1]
    s = 1
    while s < groups:
        tot_h = tot_h + tot_a * _shift_rows(tot_h, s, 0.0, grow)
        tot_a = tot_a * _shift_rows(tot_a, s, 1.0, grow)
        s *= 2
    h_in = hcarry_ref[...]
    group_end = tot_h + tot_a * h_in
    hcarry_ref[...] = group_end[groups - 1:groups, :]
    group_in = down_one(group_end, h_in)

    for r in range(SUBLANES):
        h = hs[r] + ps[r] * group_in
        y = h * jax.nn.gelu(slab(buf_b, r))
        yn = _rms(y, grec_ref[...])
        for c in range(chunks):
            buf_a[c, pl.ds(r, groups, stride=SUBLANES), :] = yn[:, c * LANES:(c + 1) * LANES]
    for c in range(chunks):
        yrec_ref[:, c * LANES:(c + 1) * LANES] = buf_a[c].astype(BF16)

    project_rotary(q_ref, 2 * D_REC, HALF_DIM ** -0.5 * math.log2(math.e))
    project_rotary(k_ref, 2 * D_REC + D_QK, 1.0)
    project_v()


def _mix_in(x2, pos_row, inv_freq, rope_pat, ln_mix, w_in, conv_w, conv_b, wa_bd, b_a, wi_bd, b_i, rg_lambda,
            g_rec, batch, seq):
    tm = MIX_ROWS
    nt = seq // tm
    d_in = w_in.shape[1]
    row_map = lambda b, i: (b * nt + i, 0)
    fixed2 = lambda b, i: (0, 0)
    fixed3 = lambda b, i: (0, 0, 0)
    t = batch * seq
    out_shapes = (
        jax.ShapeDtypeStruct((t, D_REC), BF16),
        jax.ShapeDtypeStruct((t, D_QK), BF16),
        jax.ShapeDtypeStruct((t, D_QK), BF16),
        jax.ShapeDtypeStruct((t // tm, N_HEADS * V_EXT, tm), BF16),
    )
    return pl.pallas_call(
        _mix_in_kernel,
        grid=(batch, nt),
        in_specs=[
            pl.BlockSpec((tm, D_MODEL), row_map),
            pl.BlockSpec((1, tm), lambda b, i: (0, b * nt + i)),
            pl.BlockSpec((ROPE_DIM // 2, 1), fixed2),
            pl.BlockSpec((ROPE_DIM, 3 * LANES), fixed2),
            pl.BlockSpec((1, D_MODEL), fixed2),
            pl.BlockSpec((D_MODEL, d_in), fixed2, pipeline_mode=pl.Buffered(1)),
            pl.BlockSpec((CONV_WIDTH, D_REC), fixed2),
            pl.BlockSpec((1, D_REC), fixed2),
            pl.BlockSpec((2, D_REC // 2, D_REC // 2), fixed3),
            pl.BlockSpec((1, D_REC), fixed2),
            pl.BlockSpec((2, D_REC // 2, D_REC // 2), fixed3),
            pl.BlockSpec((1, D_REC), fixed2),
            pl.BlockSpec((1, D_REC), fixed2),
            pl.BlockSpec((1, D_REC), fixed2),
        ],
        out_specs=[
            pl.BlockSpec((tm, D_REC), row_map),
            pl.BlockSpec((tm, D_QK), row_map),
            pl.BlockSpec((tm, D_QK), row_map),
            pl.BlockSpec((1, N_HEADS * V_EXT, tm), lambda b, i: (b * nt + i, 0, 0)),
        ],
        out_shape=out_shapes,
        scratch_shapes=[pltpu.VMEM((SUBLANES, D_REC), F32), pltpu.VMEM((1, D_REC), F32),
                        pltpu.VMEM((D_REC // LANES, tm, LANES), F32),
                        pltpu.VMEM((D_REC // LANES, tm, LANES), F32),
                        pltpu.VMEM((D_MODEL, d_in), BF16)],
        compiler_params=pltpu.CompilerParams(
            dimension_semantics=("arbitrary", "arbitrary"), vmem_limit_bytes=VMEM_LIMIT),
        name="mix_in",
    )(x2, pos_row, inv_freq, rope_pat, ln_mix, w_in, conv_w, conv_b, wa_bd, b_a, wi_bd, b_i, rg_lambda, g_rec)


def _attn_kernel(lq1_ref, lk1_ref, lq2_ref, lk2_ref, gsub_ref, bias_ref, q_ref, k_ref, vt_ref, o_ref,
                 m_ref, acc_ref, aprev_ref, s0_ref, s1_ref, mb0_ref, mb1_ref, p0_ref, p1_ref, *, lam_init):
    tq = ATT_Q
    tk = vt_ref.shape[2]
    assert tq == tk, "the causal bias tile assumes the diagonal block is square"
    lam = (jnp.exp(jnp.sum(lq1_ref[...] * lk1_ref[...], axis=-1, keepdims=True))
           - jnp.exp(jnp.sum(lq2_ref[...] * lk2_ref[...], axis=-1, keepdims=True)) + lam_init)

    def query_tile(i, carry):
        _attn_query_tile(i, lam, gsub_ref, bias_ref, q_ref, k_ref, vt_ref, o_ref, m_ref, acc_ref, aprev_ref,
                         s0_ref, s1_ref, mb0_ref, mb1_ref, p0_ref, p1_ref, lam_init=lam_init)
        return carry

    lax.fori_loop(0, q_ref.shape[0] // tq, query_tile, 0)


def _attn_query_tile(i, lam, gsub_ref, bias_ref, q_ref, k_ref, vt_ref, o_ref, m_ref, acc_ref, aprev_ref,
                     s0_ref, s1_ref, mb0_ref, mb1_ref, p0_ref, p1_ref, *, lam_init):
    tq = ATT_Q
    tk = vt_ref.shape[2]
    q_rows = pl.ds(pl.multiple_of(i * tq, tq), tq)

    qt = q_ref[q_rows, :].astype(F32).T
    dim = lax.broadcasted_iota(I32, (LANES, tq), 0)
    qqt = jnp.concatenate([jnp.where(dim < HALF_DIM, qt, 0.0), jnp.where(dim >= HALF_DIM, qt, 0.0)],
                          axis=1).astype(BF16)

    n = (i * tq) // tk

    def scores(j):
        return _dot(k_ref[pl.ds(pl.multiple_of(j * tk, tk), tk), :], qqt)

    def probabilities(s, m_prev):
        m_new = jnp.maximum(m_prev, jnp.max(s, axis=0, keepdims=True))
        alpha = jnp.exp2(m_prev - m_new)
        p = jnp.exp2((s - m_new).astype(BF16))
        return p, alpha, m_new

    def store_scores(j, s_buf, mb_buf):
        s = scores(j)
        s_buf[...] = s
        mb_buf[...] = jnp.max(s, axis=0, keepdims=True)

    def pipe_step(j, cur, nxt, p_cur, p_prev):
        store_scores(j + 1, *nxt)
        s_cur, mb_cur = cur
        m_prev = m_ref[...]
        m_new = jnp.maximum(m_prev, mb_cur[...])
        p_cur[...] = jnp.exp2((s_cur[...] - m_new).astype(BF16))
        m_ref[...] = m_new
        acc_ref[...] = aprev_ref[...] * acc_ref[...] + _dot(vt_ref[jnp.maximum(j - 1, 0)], p_prev[...])
        aprev_ref[...] = jnp.exp2(m_prev - m_new)

    buf0, buf1 = (s0_ref, mb0_ref), (s1_ref, mb1_ref)
    m_ref[...] = jnp.full_like(m_ref, NEG_BIG)
    acc_ref[...] = jnp.zeros_like(acc_ref)
    aprev_ref[...] = jnp.ones_like(aprev_ref)
    odd = n % 2

    @pl.when(odd == 0)
    def _():
        p1_ref[...] = jnp.zeros_like(p1_ref)
        store_scores(0, *buf0)

    @pl.when(odd == 1)
    def _():
        p0_ref[...] = jnp.zeros_like(p0_ref)
        store_scores(0, *buf1)
        pipe_step(0, buf1, buf0, p1_ref, p0_ref)

    def pair(t, carry):
        j = 2 * t + odd
        pipe_step(j, buf0, buf1, p0_ref, p1_ref)
        pipe_step(j + 1, buf1, buf0, p1_ref, p0_ref)
        return carry

    lax.fori_loop(0, n // 2, pair, 0)

    parts = []
    for c0 in range(0, 2 * tq, ATT_TAIL_LANES):
        cols = slice(c0, c0 + ATT_TAIL_LANES)
        keys = c0 % tq + ATT_TAIL_LANES
        p, alpha, _ = probabilities(s0_ref[:keys, cols] + bias_ref[:keys, cols], m_ref[:, cols])
        part = aprev_ref[:, cols] * acc_ref[:, cols] + _dot(vt_ref[jnp.maximum(n - 1, 0)], p1_ref[:, cols])
        parts.append(alpha * part + _dot(vt_ref[n][:, :keys], p))
    acc = jnp.concatenate(parts, axis=1)

    o = acc[:V_DIM] / acc[V_DIM:V_DIM + 1]
    o = o[:, :tq] - lam * o[:, tq:]
    o = o * lax.rsqrt(jnp.mean(o * o, axis=0, keepdims=True) + EPS) * gsub_ref[...]
    o_ref[q_rows, :] = (o * (1.0 - lam_init)).T.astype(BF16)


def _attention(q, k, vt, lq1, lk1, lq2, lk2, g_sub_col, batch, seq, lam_init):
    tq = ATT_Q
    nq = seq // tq
    tk = vt.shape[2]
    nk = seq // tk
    vec = lambda b, h: (0, 0)
    per_head = lambda b, h: (b, h)
    visible = (lax.broadcasted_iota(I32, (tk, 2 * tq), 0) <= lax.broadcasted_iota(I32, (tk, 2 * tq), 1) % tq)
    bias = jnp.where(visible, 0.0, NEG_BIG).astype(F32)
    return pl.pallas_call(
        functools.partial(_attn_kernel, lam_init=lam_init),
        grid=(batch, N_HEADS),
        in_specs=[
            pl.BlockSpec((1, HALF_DIM), vec),
            pl.BlockSpec((1, HALF_DIM), vec),
            pl.BlockSpec((1, HALF_DIM), vec),
            pl.BlockSpec((1, HALF_DIM), vec),
            pl.BlockSpec((V_DIM, 1), vec),
            pl.BlockSpec((tk, 2 * tq), vec),
            pl.BlockSpec((seq, LANES), per_head),
            pl.BlockSpec((seq, LANES), per_head),
            pl.BlockSpec((nk, V_EXT, tk), lambda b, h: (b, h, 0)),
        ],
        out_specs=pl.BlockSpec((seq, V_DIM), per_head),
        out_shape=jax.ShapeDtypeStruct((batch * seq, D_ATT), BF16),
        scratch_shapes=[pltpu.VMEM((1, 2 * tq), F32),
                        pltpu.VMEM((V_EXT, 2 * tq), F32), pltpu.VMEM((1, 2 * tq), F32),
                        pltpu.VMEM((tk, 2 * tq), F32), pltpu.VMEM((tk, 2 * tq), F32),
                        pltpu.VMEM((1, 2 * tq), F32), pltpu.VMEM((1, 2 * tq), F32),
                        pltpu.VMEM((tk, 2 * tq), BF16), pltpu.VMEM((tk, 2 * tq), BF16)],
        compiler_params=pltpu.CompilerParams(
            dimension_semantics=("arbitrary", "arbitrary"), vmem_limit_bytes=VMEM_LIMIT),
        name="attention",
    )(lq1, lk1, lq2, lk2, g_sub_col, bias, q, k, vt)


def _sublane_total(x, op):
    return op(x, axis=0, keepdims=True)


def _route_kernel(x_ref, yrec_ref, yatt_ref, wo32_ref, lnmoe_ref, wrt_ref, ebias_ref, tri_ref,
                  h1_ref, hpa_ref, hpb_ref, ek_ref, wk_ref, rk_ref, cnt_ref, carry_ref, wo_ref):
    tm = x_ref.shape[0]
    e_n = N_EXPERTS

    @pl.when(pl.program_id(0) == 0)
    def _():
        wo_ref[...] = wo32_ref[...].astype(BF16)

    @pl.when(pl.program_id(0) % (pl.num_programs(0) // MOE_PARTS) == 0)
    def _():
        carry_ref[...] = jnp.zeros_like(carry_ref)

    h1 = x_ref[...] + _dot(yrec_ref[...], wo_ref[:D_REC, :]) + _dot(yatt_ref[...], wo_ref[D_REC:, :])
    h1_ref[...] = h1
    hn = _rms(h1, lnmoe_ref[...])
    pa, pb = _pack_row(hn)
    hpa_ref[...] = pa
    hpb_ref[...] = pb

    logits = lax.dot_general(wrt_ref[...], hn, (((1,), (1,)), ((), ())),
                             precision=lax.Precision.HIGHEST, preferred_element_type=F32)
    scores = jax.nn.sigmoid(logits)
    sel = scores + ebias_ref[...]

    sel3 = sel.reshape(N_GROUPS, GROUP_SIZE, tm)
    idx3 = lax.broadcasted_iota(I32, (N_GROUPS, GROUP_SIZE, tm), 1)
    m1 = jnp.max(sel3, axis=1, keepdims=True)
    first = jnp.min(jnp.where(sel3 == m1, idx3, GROUP_SIZE), axis=1, keepdims=True)
    m2 = jnp.max(jnp.where(idx3 == first, -jnp.inf, sel3), axis=1, keepdims=True)
    gscore = (m1 + m2).reshape(N_GROUPS, tm)

    gidx = lax.broadcasted_iota(I32, (N_GROUPS, tm), 0)
    beaten = jnp.zeros((N_GROUPS, tm), I32)
    for g in range(N_GROUPS):
        other = gscore[g:g + 1, :]
        beats = (other > gscore) | ((other == gscore) & (g < gidx))
        beaten = beaten + beats.astype(I32)
    gkeep = beaten < TOPK_GROUPS
    keep = jnp.broadcast_to(gkeep.reshape(N_GROUPS, 1, tm), (N_GROUPS, GROUP_SIZE, tm)).reshape(e_n, tm)
    selm = jnp.where(keep, sel, -jnp.inf)

    eidx = lax.broadcasted_iota(I32, (e_n, tm), 0)
    remaining = selm
    picks, ek, sk = [], [], []
    for _ in range(TOP_K):
        best = jnp.max(remaining, axis=0, keepdims=True)
        first = jnp.min(jnp.where(remaining == best, eidx, e_n), axis=0, keepdims=True)
        pick = eidx == first
        picks.append(pick)
        ek.append(first)
        sk.append(_sublane_total(jnp.where(pick, scores, 0.0), jnp.sum))
        remaining = jnp.where(pick, -jnp.inf, remaining)
    chosen_f = (remaining != selm).astype(F32)
    wsum = functools.reduce(lambda a, b: a + b, sk)
    ek_ref[...] = jnp.concatenate(ek, axis=0)
    wk_ref[...] = (jnp.concatenate(sk, axis=0) * (ROUTE_SCALE / wsum)).T

    prefix = _dot(chosen_f.astype(BF16), tri_ref[...])
    rank = prefix + carry_ref[...]
    carry_new = carry_ref[...] + jnp.sum(chosen_f, axis=1, keepdims=True)
    carry_ref[...] = carry_new
    cnt_ref[0] = carry_new.astype(I32)
    rk = [_sublane_total(jnp.where(pick, rank, 0.0), jnp.sum) for pick in picks]
    rk_ref[...] = jnp.concatenate(rk, axis=0).astype(I32)


def _route(x2, y_rec, y_att, w_out, ln_moe, w_router_t, e_bias_col):
    t = x2.shape[0]
    tm = ROUTE_ROWS
    nt = t // tm
    row_map = lambda i: (i, 0)
    col_map = lambda i: (0, i)
    fixed = lambda i: (0, 0)
    tri = (lax.broadcasted_iota(I32, (tm, tm), 0) < lax.broadcasted_iota(I32, (tm, tm), 1)).astype(BF16)
    out_shapes = (
        jax.ShapeDtypeStruct((t, D_MODEL), F32),
        jax.ShapeDtypeStruct((t, PACK_W), U32),
        jax.ShapeDtypeStruct((t, PACK_W), U32),
        jax.ShapeDtypeStruct((TOP_K, t), I32),
        jax.ShapeDtypeStruct((t, TOP_K), F32),
        jax.ShapeDtypeStruct((TOP_K, t), I32),
        jax.ShapeDtypeStruct((MOE_PARTS, N_EXPERTS, 1), I32),
    )
    assert nt % MOE_PARTS == 0
    return pl.pallas_call(
        _route_kernel,
        grid=(nt,),
        in_specs=[
            pl.BlockSpec((tm, D_MODEL), row_map),
            pl.BlockSpec((tm, D_REC), row_map),
            pl.BlockSpec((tm, D_ATT), row_map),
            pl.BlockSpec((D_REC + D_ATT, D_MODEL), fixed, pipeline_mode=pl.Buffered(1)),
            pl.BlockSpec((1, D_MODEL), fixed),
            pl.BlockSpec((N_EXPERTS, D_MODEL), fixed),
            pl.BlockSpec((N_EXPERTS, 1), fixed),
            pl.BlockSpec((tm, tm), fixed),
        ],
        out_specs=[
            pl.BlockSpec((tm, D_MODEL), row_map),
            pl.BlockSpec((tm, PACK_W), row_map),
            pl.BlockSpec((tm, PACK_W), row_map),
            pl.BlockSpec((TOP_K, tm), col_map),
            pl.BlockSpec((tm, TOP_K), row_map),
            pl.BlockSpec((TOP_K, tm), col_map),
            pl.BlockSpec((1, N_EXPERTS, 1), lambda i: (i // (nt // MOE_PARTS), 0, 0)),
        ],
        out_shape=out_shapes,
        scratch_shapes=[pltpu.VMEM((N_EXPERTS, 1), F32), pltpu.VMEM((D_REC + D_ATT, D_MODEL), BF16)],
        compiler_params=pltpu.CompilerParams(dimension_semantics=("arbitrary",), vmem_limit_bytes=VMEM_LIMIT),
        name="route",
    )(x2, y_rec, y_att, w_out, ln_moe, w_router_t, e_bias_col, tri)


def _plan_kernel(pad_start_ref, ek_ref, rk_ref, dest_ref):
    ek = ek_ref[...]
    part = pl.program_id(0) // (pl.num_programs(0) // MOE_PARTS)

    def add_expert(e, base):
        return jnp.where(ek == e, pad_start_ref[part * N_EXPERTS + e], base)

    dest_ref[...] = rk_ref[...] + lax.fori_loop(0, N_EXPERTS, add_expert, jnp.zeros_like(ek))


def _plan(pad_start, ek, rk):
    kk, t = ek.shape
    tl = min(t // MOE_PARTS, 2048)
    col_map = lambda i, ps: (0, i)
    grid_spec = pltpu.PrefetchScalarGridSpec(
        num_scalar_prefetch=1,
        grid=(t // tl,),
        in_specs=[pl.BlockSpec((kk, tl), col_map), pl.BlockSpec((kk, tl), col_map)],
        out_specs=pl.BlockSpec((kk, tl), col_map),
    )
    return pl.pallas_call(
        _plan_kernel,
        grid_spec=grid_spec,
        out_shape=jax.ShapeDtypeStruct((kk, t), I32),
        compiler_params=pltpu.CompilerParams(dimension_semantics=("arbitrary",)),
        name="plan",
    )(pad_start, ek, rk)


def _sc_mesh():
    return plsc.VectorSubcoreMesh(core_axis_name="core", subcore_axis_name="subcore")


def _sc_dispatch(rows, dest, n_out, part):
    t, w = rows.shape
    kk = dest.shape[0]
    steps = t // MOE_PARTS // SC_WINDOW
    first = part * steps

    @pl.kernel(out_type=jax.ShapeDtypeStruct((n_out, w), rows.dtype), mesh=_sc_mesh(), scratch_types=[])
    def kern(x_hbm, i_hbm, o_hbm):
        def body(x_vmem, i_vmem):
            for k in range(kk):
                pltpu.sync_copy(x_vmem, o_hbm.at[i_vmem.at[k]])

        pltpu.emit_pipeline(
            body,
            grid=(steps,),
            in_specs=[pl.BlockSpec((SC_WINDOW, w), lambda i: (first + i, 0)),
                      pl.BlockSpec((kk, SC_WINDOW), lambda i: (0, first + i))],
            out_specs=[],
            core_axis_name=("core", "subcore"),
            dimension_semantics=(pltpu.PARALLEL,),
        )(x_hbm, i_hbm)

    return kern(rows, dest)


def _sc_combine(rows, dest, part):
    kk, t = dest.shape
    w = rows.shape[1]
    tp = t // MOE_PARTS
    flat = dest.reshape(1, kk * t)
    windows = tp // SC_WINDOW

    def index_block(i):
        return (0, (i // windows) * (t // SC_WINDOW) + part * windows + i % windows)

    @pl.kernel(out_type=jax.ShapeDtypeStruct((kk * tp, w), rows.dtype), mesh=_sc_mesh(), scratch_types=[])
    def kern(y_hbm, i_hbm, o_hbm):
        def body(i_vmem, o_vmem):
            pltpu.sync_copy(y_hbm.at[i_vmem.at[0]], o_vmem)

        pltpu.emit_pipeline(
            body,
            grid=(kk * windows,),
            in_specs=[pl.BlockSpec((1, SC_WINDOW), index_block)],
            out_specs=[pl.BlockSpec((SC_WINDOW, w), lambda i: (i, 0))],
            core_axis_name=("core", "subcore"),
            dimension_semantics=(pltpu.PARALLEL,),
        )(i_hbm, o_hbm)

    return kern(rows, flat).reshape(kk, tp, w)


X_SLOTS = 3
Y_SLOTS = 2


def _experts_kernel(blk_expert_ref, n_used_ref, first_ref, slot_ref, next_ref,
                    xa_hbm, xb_hbm, w1_hbm, w3_hbm, w2_hbm, ya_hbm, yb_hbm,
                    xa_buf, xb_buf, ya_buf, yb_buf, w1f_ref, w3f_ref, w2f_ref, w1b_ref, w3b_ref, w2b_ref,
                    wsem, xsem, ysem):
    m = EXPERT_ROWS
    n_used = n_used_ref[0]

    def weight_copies(e, s):
        return (pltpu.make_async_copy(w1_hbm.at[e], w1f_ref.at[s], wsem.at[s, 0]),
                pltpu.make_async_copy(w3_hbm.at[e], w3f_ref.at[s], wsem.at[s, 1]),
                pltpu.make_async_copy(w2_hbm.at[e], w2f_ref.at[s], wsem.at[s, 2]))

    def x_copies(b):
        rows, s = pl.ds(pl.multiple_of(b * m, m), m), b % X_SLOTS
        return (pltpu.make_async_copy(xa_hbm.at[rows], xa_buf.at[s], xsem.at[s, 0]),
                pltpu.make_async_copy(xb_hbm.at[rows], xb_buf.at[s], xsem.at[s, 1]))

    def y_copies(b):
        rows, s = pl.ds(pl.multiple_of(b * m, m), m), b % Y_SLOTS
        return (pltpu.make_async_copy(ya_buf.at[s], ya_hbm.at[rows], ysem.at[s, 0]),
                pltpu.make_async_copy(yb_buf.at[s], yb_hbm.at[rows], ysem.at[s, 1]))

    def start(copies):
        for copy in copies:
            copy.start()

    def wait(copies):
        for copy in copies:
            copy.wait()

    start(weight_copies(blk_expert_ref[0], 0))
    start(x_copies(0))

    @pl.when(n_used > 1)
    def _():
        start(x_copies(1))

    def block(b, carry):
        @pl.when(b + 2 < n_used)
        def _():
            start(x_copies(b + 2))

        @pl.when(first_ref[b] == 1)
        def _():
            s = slot_ref[b]
            wait(weight_copies(blk_expert_ref[b], s))

            @pl.when(next_ref[b] >= 0)
            def _():
                start(weight_copies(next_ref[b], 1 - s))

            w1b_ref[...] = w1f_ref[s].astype(BF16)
            w3b_ref[...] = w3f_ref[s].astype(BF16)
            w2b_ref[...] = w2f_ref[s].astype(BF16)

        wait(x_copies(b))

        @pl.when(b >= Y_SLOTS)
        def _():
            wait(y_copies(b - Y_SLOTS))

        x = _unpack_row(xa_buf[b % X_SLOTS], xb_buf[b % X_SLOTS]).astype(BF16)
        a = _dot(x, w1b_ref[...])
        g = _dot(x, w3b_ref[...])
        hmid = (jax.nn.silu(a) * g).astype(BF16)
        y = _dot(hmid, w2b_ref[...])
        pa, pb = _pack_row(y)
        ya_buf[b % Y_SLOTS] = pa
        yb_buf[b % Y_SLOTS] = pb
        start(y_copies(b))
        return carry

    lax.fori_loop(0, n_used, block, 0)

    for back in range(Y_SLOTS, 0, -1):
        @pl.when(n_used - back >= 0)
        def _(back=back):
            wait(y_copies(n_used - back))


def _experts(xa, xb, w1, w3, w2, blk_expert, n_used, seg_first, seg_slot, seg_next):
    p = xa.shape[0]
    m = EXPERT_ROWS
    hbm = pl.BlockSpec(memory_space=pl.ANY)
    grid_spec = pltpu.PrefetchScalarGridSpec(
        num_scalar_prefetch=5,
        grid=(1,),
        in_specs=[hbm] * 5,
        out_specs=[hbm, hbm],
        scratch_shapes=[
            pltpu.VMEM((X_SLOTS, m, PACK_W), U32), pltpu.VMEM((X_SLOTS, m, PACK_W), U32),
            pltpu.VMEM((Y_SLOTS, m, PACK_W), U32), pltpu.VMEM((Y_SLOTS, m, PACK_W), U32),
            pltpu.VMEM((2, D_MODEL, D_EXPERT), F32), pltpu.VMEM((2, D_MODEL, D_EXPERT), F32),
            pltpu.VMEM((2, D_EXPERT, D_MODEL), F32),
            pltpu.VMEM((D_MODEL, D_EXPERT), BF16), pltpu.VMEM((D_MODEL, D_EXPERT), BF16),
            pltpu.VMEM((D_EXPERT, D_MODEL), BF16),
            pltpu.SemaphoreType.DMA((2, 3)), pltpu.SemaphoreType.DMA((X_SLOTS, 2)),
            pltpu.SemaphoreType.DMA((Y_SLOTS, 2)),
        ],
    )
    return pl.pallas_call(
        _experts_kernel,
        grid_spec=grid_spec,
        out_shape=(jax.ShapeDtypeStruct((p, PACK_W), U32), jax.ShapeDtypeStruct((p, PACK_W), U32)),
        compiler_params=pltpu.CompilerParams(dimension_semantics=("arbitrary",), vmem_limit_bytes=VMEM_LIMIT),
        name="experts",
    )(blk_expert, n_used, seg_first, seg_slot, seg_next, xa, xb, w1, w3, w2)


def _tail_kernel(h1_ref, ga_ref, gb_ref, wk_ref, p_ref, lnmoe_ref, ws1_32, ws3_32, ws2_32, lnple_ref,
                 wpg_32, wpp_32, lnf_ref, o_ref, ws1_ref, ws3_ref, ws2_ref, wpg_ref, wpp_ref):
    @pl.when(pl.program_id(0) == 0)
    def _():
        for dst, src in ((ws1_ref, ws1_32), (ws3_ref, ws3_32), (ws2_ref, ws2_32), (wpg_ref, wpg_32),
                         (wpp_ref, wpp_32)):
            dst[...] = src[...].astype(BF16)

    h1 = h1_ref[...]
    hn = _rms(h1, lnmoe_ref[...]).astype(BF16)
    shared = _dot((jax.nn.silu(_dot(hn, ws1_ref[...])) * _dot(hn, ws3_ref[...])).astype(BF16), ws2_ref[...])
    wk = wk_ref[...]
    routed = jnp.zeros_like(h1)
    for kk in range(TOP_K):
        routed = routed + wk[:, kk:kk + 1] * _unpack_row(ga_ref[kk], gb_ref[kk])
    h2 = h1 + routed + shared
    gate = jax.nn.sigmoid(_dot(_rms(h2, lnple_ref[...]).astype(BF16), wpg_ref[...]))
    h3 = h2 + gate * _dot(p_ref[...].astype(BF16), wpp_ref[...])
    o_ref[...] = _rms(h3, lnf_ref[...])


def _tail_into_kernel(out_so_far_ref, *refs):
    del out_so_far_ref
    _tail_kernel(*refs)


def _tail(out_so_far, part, h1, ga, gb, wk_t, p2, ln_moe, ws1, ws3, ws2, ln_ple, w_pg, w_pp, ln_f):
    t = h1.shape[0]
    tm = TAIL_ROWS
    steps = t // MOE_PARTS // tm
    row_map = lambda i: (part * steps + i, 0)
    fixed = lambda i: (0, 0)
    g_map = lambda i: (0, i, 0)
    d_sh = ws1.shape[1]
    carried = () if out_so_far is None else (out_so_far,)
    return pl.pallas_call(
        _tail_kernel if out_so_far is None else _tail_into_kernel,
        grid=(steps,),
        input_output_aliases={} if out_so_far is None else {0: 0},
        in_specs=[pl.BlockSpec(memory_space=pl.ANY)] * len(carried) + [
            pl.BlockSpec((tm, D_MODEL), row_map),
            pl.BlockSpec((TOP_K, tm, PACK_W), g_map),
            pl.BlockSpec((TOP_K, tm, PACK_W), g_map),
            pl.BlockSpec((tm, TOP_K), row_map),
            pl.BlockSpec((tm, D_PLE), row_map),
            pl.BlockSpec((1, D_MODEL), fixed),
            pl.BlockSpec((D_MODEL, d_sh), fixed, pipeline_mode=pl.Buffered(1)),
            pl.BlockSpec((D_MODEL, d_sh), fixed, pipeline_mode=pl.Buffered(1)),
            pl.BlockSpec((d_sh, D_MODEL), fixed, pipeline_mode=pl.Buffered(1)),
            pl.BlockSpec((1, D_MODEL), fixed),
            pl.BlockSpec((D_MODEL, D_MODEL), fixed, pipeline_mode=pl.Buffered(1)),
            pl.BlockSpec((D_PLE, D_MODEL), fixed, pipeline_mode=pl.Buffered(1)),
            pl.BlockSpec((1, D_MODEL), fixed),
        ],
        out_specs=pl.BlockSpec((tm, D_MODEL), row_map),
        out_shape=jax.ShapeDtypeStruct((t, D_MODEL), F32),
        scratch_shapes=[pltpu.VMEM((D_MODEL, d_sh), BF16), pltpu.VMEM((D_MODEL, d_sh), BF16),
                        pltpu.VMEM((d_sh, D_MODEL), BF16), pltpu.VMEM((D_MODEL, D_MODEL), BF16),
                        pltpu.VMEM((D_PLE, D_MODEL), BF16)],
        compiler_params=pltpu.CompilerParams(dimension_semantics=("arbitrary",), vmem_limit_bytes=VMEM_LIMIT),
        name="tail",
    )(*carried, h1, ga, gb, wk_t, p2, ln_moe, ws1, ws3, ws2, ln_ple, w_pg, w_pp, ln_f)


def _rope_constants():
    half = ROPE_DIM // 2
    inv_freq = (ROPE_THETA ** (-jnp.arange(0, ROPE_DIM, 2, dtype=F32) / ROPE_DIM)).reshape(half, 1)
    f = lax.broadcasted_iota(I32, (ROPE_DIM, LANES), 0)
    l64 = lax.broadcasted_iota(I32, (ROPE_DIM, LANES), 1) % HALF_DIM
    cos_pat = ((f < half) & (l64 < ROPE_DIM) & (l64 % half == f)).astype(F32)
    sa_pat = -((f >= half) & (l64 < half) & (l64 == f - half)).astype(F32)
    sb_pat = ((f >= half) & (l64 >= half) & (l64 < ROPE_DIM) & (l64 - half == f - half)).astype(F32)
    return inv_freq, jnp.concatenate([cos_pat, sa_pat, sb_pat], axis=1)


def _block_diag_tiles(w):
    nb, bd, _ = w.shape
    per = nb // 2
    tiles = []
    for tix in range(2):
        rows = []
        for j in range(per):
            rows.append(jnp.concatenate(
                [w[tix * per + j] if c == j else jnp.zeros((bd, bd), w.dtype) for c in range(per)], axis=1))
        tiles.append(jnp.concatenate(rows, axis=0))
    return jnp.stack(tiles).astype(BF16)


def _layer(h, p_l, positions, lam_init, ln_mix, w_in, conv_w, conv_b, w_a, b_a, w_i, b_i, rg_lambda, g_rec,
           lq1, lk1, lq2, lk2, g_sub, w_out, ln_moe, w_router, e_bias, w1, w3, w2, ws1, ws3, ws2,
           ln_ple, w_ple_gate, w_ple_proj, ln_out):
    batch, seq, _ = h.shape
    t = batch * seq
    x2 = h.reshape(t, D_MODEL)
    row = lambda a: a.reshape(1, -1)
    inv_freq, rope_pat = _rope_constants()

    y_rec, q, k, vt = _mix_in(
        x2, positions.reshape(1, t), inv_freq, rope_pat, row(ln_mix), w_in, conv_w, row(conv_b),
        _block_diag_tiles(w_a), row(b_a), _block_diag_tiles(w_i), row(b_i), row(rg_lambda), row(g_rec),
        batch, seq)
    y_att = _attention(q, k, vt, row(lq1), row(lk1), row(lq2), row(lk2), g_sub.reshape(-1, 1), batch, seq,
                       lam_init)

    h1, hpa, hpb, ek, wk_t, rk, counts = _route(
        x2, y_rec, y_att, w_out, row(ln_moe), w_router.T, e_bias.reshape(-1, 1))

    m = EXPERT_ROWS
    counts = counts.reshape(MOE_PARTS, N_EXPERTS)
    padded = (counts + m - 1) // m * m
    pad_end = jnp.cumsum(padded, axis=1)
    pad_start = pad_end - padded
    n_rows = t // MOE_PARTS * TOP_K + N_EXPERTS * m
    nblk = n_rows // m
    n_used = (pad_end[:, -1] // m).astype(I32)
    blk = jnp.arange(nblk, dtype=I32)
    blk_row = jnp.minimum(blk[None, :], n_used[:, None] - 1) * m
    blk_expert = jnp.sum((pad_end[:, None, :] <= blk_row[:, :, None]).astype(I32), axis=2)
    prev_expert = jnp.concatenate([jnp.full((MOE_PARTS, 1), -1, I32), blk_expert[:, :-1]], axis=1)
    seg_first = ((blk[None, :] < n_used[:, None]) & (blk_expert != prev_expert)).astype(I32)
    seg_slot = ((jnp.cumsum(seg_first, axis=1) - 1) % 2).astype(I32)
    eid = jnp.arange(N_EXPERTS, dtype=I32)
    later = (padded[:, None, :] > 0) & (eid[None, None, :] > eid[None, :, None])
    next_expert = jnp.min(jnp.where(later, eid[None, None, :], N_EXPERTS), axis=2)
    next_expert = jnp.where(next_expert == N_EXPERTS, -1, next_expert).astype(I32)
    seg_next = jnp.sum(jnp.where(blk_expert[:, :, None] == eid[None, None, :], next_expert[:, None, :], 0), axis=2)
    dest = _plan(pad_start.astype(I32).reshape(-1), ek, rk)

    p2 = p_l.reshape(t, D_PLE)
    tail_weights = (row(ln_moe), ws1, ws3, ws2, row(ln_ple), w_ple_gate, w_ple_proj, row(ln_out))
    gathered = []
    for part in range(MOE_PARTS):
        xa = _sc_dispatch(hpa, dest, n_rows, part)
        xb = _sc_dispatch(hpb, dest, n_rows, part)
        ya, yb = _experts(xa, xb, w1, w3, w2, blk_expert[part], n_used[part:part + 1], seg_first[part],
                          seg_slot[part], seg_next[part])
        gathered.append((_sc_combine(ya, dest, part), _sc_combine(yb, dest, part)))
    out = None
    for part, (ga, gb) in enumerate(gathered):
        out = _tail(out, part, h1, ga, gb, wk_t, p2, *tail_weights)
    return out.reshape(batch, seq, D_MODEL)


def kernel(x, p, positions, ln_mix, w_in, conv_w, conv_b, w_a, b_a, w_i, b_i, rg_lambda, g_rec, lq1, lk1, lq2,
           lk2, g_sub, w_out, ln_moe, w_router, e_bias, w1, w3, w2, ws1, ws3, ws2, ln_ple, w_ple_gate,
           w_ple_proj, ln_f):
    depth = w_in.shape[0]
    assert depth == 1, "the fused tail applies the final norm; one layer supported"
    lam_init = 0.8 - 0.6 * math.exp(-0.3 * 0)
    groups = []
    step = x.shape[0] // BATCH_GROUPS
    for g in range(BATCH_GROUPS):
        rows = slice(g * step, (g + 1) * step)
        groups.append(_layer(
            x[rows], p[0][rows], positions[rows], lam_init, ln_mix[0], w_in[0], conv_w[0], conv_b[0], w_a[0],
            b_a[0], w_i[0], b_i[0], rg_lambda[0], g_rec[0], lq1[0], lk1[0], lq2[0], lk2[0], g_sub[0], w_out[0],
            ln_moe[0], w_router[0], e_bias[0], w1[0], w3[0], w2[0], ws1[0], ws3[0], ws2[0], ln_ple[0],
            w_ple_gate[0], w_ple_proj[0], ln_f))
    return jnp.concatenate(groups, axis=0)
```

```python
import functools
import math

import jax
import jax.numpy as jnp
from jax import lax
from jax.experimental import pallas as pl
from jax.experimental.pallas import tpu as pltpu
from jax.experimental.pallas import tpu_sc as plsc

F32 = jnp.float32
BF16 = jnp.bfloat16
U32 = jnp.uint32
I32 = jnp.int32

D_MODEL = 1024
D_REC = 512
REC_BLOCKS = 8
CONV_WIDTH = 4
RG_C = 8.0
N_HEADS = 4
HALF_DIM = 64
V_DIM = 128
D_ATT = N_HEADS * V_DIM
D_QK = N_HEADS * 2 * HALF_DIM
ROPE_DIM = 16
ROPE_THETA = 500000.0
N_EXPERTS = 64
TOP_K = 8
N_GROUPS = 8
GROUP_SIZE = N_EXPERTS // N_GROUPS
TOPK_GROUPS = 4
D_EXPERT = 256
ROUTE_SCALE = 2.5
D_PLE = 256
EPS = 1e-6

LANES = 128
SUBLANES = 8
VMEM_LIMIT = 56 * 1024 * 1024

MIX_ROWS = 512
ATT_Q = 512
ATT_TAIL_LANES = 256
ONES_ROWS = 16
V_EXT = V_DIM + ONES_ROWS
ROUTE_ROWS = 1024
EXPERT_ROWS = 512
TAIL_ROWS = 512
SC_WINDOW = 128
PACK_W = 256
NEG_BIG = -1e30


def _rms(x, g):
    return x * lax.rsqrt(jnp.mean(x * x, axis=-1, keepdims=True) + EPS) * g


def _dot(a, b):
    return jnp.dot(a, b, preferred_element_type=F32)


def _pack_pair(lo, hi):
    lo_bits = lax.bitcast_convert_type(lo.astype(BF16).astype(F32), U32)
    hi_bits = lax.bitcast_convert_type(hi.astype(BF16).astype(F32), U32)
    return (lo_bits >> 16) | (hi_bits & jnp.uint32(0xFFFF0000))


def _unpack_pair(p):
    lo = lax.bitcast_convert_type(p << 16, F32)
    hi = lax.bitcast_convert_type(p & jnp.uint32(0xFFFF0000), F32)
    return lo, hi


def _pack_row(x):
    w = PACK_W
    return _pack_pair(x[:, 0:w], x[:, w:2 * w]), _pack_pair(x[:, 2 * w:3 * w], x[:, 3 * w:4 * w])


def _unpack_row(pa, pb):
    c0, c1 = _unpack_pair(pa)
    c2, c3 = _unpack_pair(pb)
    return jnp.concatenate([c0, c1, c2, c3], axis=1)


def _shift_rows(a, s, fill, row):
    n, c = a.shape
    if s % SUBLANES == 0:
        return jnp.concatenate([jnp.full((s, c), fill, a.dtype), a[:n - s]], axis=0)
    return jnp.where(row >= s, pltpu.roll(a, s, 0), fill)


def _mix_in_kernel(x_ref, pos_ref, invf_ref, pat_ref, lnm_ref, win32_ref, cw_ref, cb_ref, wa_ref, ba_ref,
                   wi_ref, bi_ref, lam_ref, grec_ref,
                   yrec_ref, q_ref, k_ref, vt_ref, tail_ref, hcarry_ref, buf_a, buf_b, win_ref):
    tm = x_ref.shape[0]
    groups = tm // SUBLANES
    chunks = D_REC // LANES

    def stage(ref, v):
        for c in range(chunks):
            ref[c] = v[:, c * LANES:(c + 1) * LANES]

    def slab(ref, r):
        return jnp.concatenate([ref[c, pl.ds(r, groups, stride=SUBLANES), :] for c in range(chunks)], axis=1)

    @pl.when((pl.program_id(0) == 0) & (pl.program_id(1) == 0))
    def _():
        win_ref[...] = win32_ref[...].astype(BF16)

    @pl.when(pl.program_id(1) == 0)
    def _():
        tail_ref[...] = jnp.zeros_like(tail_ref)
        hcarry_ref[...] = jnp.zeros_like(hcarry_ref)

    hn = _rms(x_ref[...], lnm_ref[...]).astype(BF16)

    ang = invf_ref[...] * pos_ref[...].astype(F32)
    cs = jnp.concatenate([jnp.cos(ang), jnp.sin(ang)], axis=0)
    tabs = lax.dot_general(cs, pat_ref[...], (((0,), (0,)), ((), ())),
                           precision=lax.Precision.HIGHEST, preferred_element_type=F32)
    lane64 = lax.broadcasted_iota(I32, (1, LANES), 1) % HALF_DIM
    cosf = tabs[:, 0:LANES] + (lane64 >= ROPE_DIM).astype(F32)
    sa, sb = tabs[:, LANES:2 * LANES], tabs[:, 2 * LANES:3 * LANES]

    def project_rotary(out_ref, off, mul):
        for c in range(0, D_QK // LANES, 2):
            z2 = _dot(hn, win_ref[:, off + c * LANES: off + (c + 2) * LANES])
            for cc in range(2):
                zc = z2[:, cc * LANES:(cc + 1) * LANES]
                rot = (zc * cosf + pltpu.roll(zc, LANES - ROPE_DIM // 2, 1) * sa
                       + pltpu.roll(zc, ROPE_DIM // 2, 1) * sb)
                out_ref[:, (c + cc) * LANES:(c + cc + 1) * LANES] = (rot * mul).astype(BF16)

    def project_v():
        vt = _dot(hn, win_ref[:, 2 * D_REC + 2 * D_QK:]).T.astype(BF16)
        for hd in range(N_HEADS):
            vt_ref[0, hd * V_EXT:hd * V_EXT + V_DIM, :] = vt[hd * V_DIM:(hd + 1) * V_DIM]
            vt_ref[0, hd * V_EXT + V_DIM:(hd + 1) * V_EXT, :] = jnp.ones((ONES_ROWS, tm), BF16)

    xr = _dot(hn, win_ref[:, 0:D_REC])
    stage(buf_a, xr)
    stage(buf_b, _dot(hn, win_ref[:, D_REC:2 * D_REC]))
    tail = tail_ref[...]
    tail_ref[...] = xr[tm - SUBLANES:, :]

    grow = lax.broadcasted_iota(I32, (groups, D_REC), 0)

    def down_one(a, first_row):
        return jnp.where(grow == 0, first_row, pltpu.roll(a, 1, 0))

    xs = [slab(buf_a, r) for r in range(SUBLANES)]
    wrapped = {r: down_one(xs[r], tail[r:r + 1, :]) for r in range(SUBLANES - CONV_WIDTH + 1, SUBLANES)}
    xc = []
    for r in range(SUBLANES):
        acc = cb_ref[...] + cw_ref[CONV_WIDTH - 1:CONV_WIDTH, :] * xs[r]
        for d in range(1, CONV_WIDTH):
            prev = xs[r - d] if r >= d else wrapped[r - d + SUBLANES]
            acc = acc + cw_ref[CONV_WIDTH - 1 - d:CONV_WIDTH - d, :] * prev
        xc.append(acc)
    xc = jnp.concatenate(xc, axis=0)

    xcb = xc.astype(BF16)
    half = D_REC // 2
    ra = jnp.concatenate([_dot(xcb[:, :half], wa_ref[0]), _dot(xcb[:, half:], wa_ref[1])], axis=1)
    ri = jnp.concatenate([_dot(xcb[:, :half], wi_ref[0]), _dot(xcb[:, half:], wi_ref[1])], axis=1)
    r_gate = jax.nn.sigmoid(ra + ba_ref[...])
    i_gate = jax.nn.sigmoid(ri + bi_ref[...])
    lam = lam_ref[...]
    softplus_neg = jnp.maximum(-lam, 0.0) + jnp.log(1.0 + jnp.exp(-jnp.abs(lam)))
    log_a = -RG_C * r_gate * softplus_neg
    a = jnp.exp(log_a)
    u = jnp.sqrt(1.0 - jnp.exp(2.0 * log_a)) * i_gate * xc

    rows = lambda v, r: v[r * groups:(r + 1) * groups]
    hs, ps = [rows(u, 0)], [rows(a, 0)]
    for r in range(1, SUBLANES):
        hs.append(rows(a, r) * hs[-1] + rows(u, r))
        ps.append(rows(a, r) * ps[-1])
    tot_a, tot_h = ps[-1], hs[-1]
    s = 1
    while s < groups:
        tot_h = tot_h + tot_a * _shift_rows(tot_h, s, 0.0, grow)
        tot_a = tot_a * _shift_rows(tot_a, s, 1.0, grow)
        s *= 2
    h_in = hcarry_ref[...]
    group_end = tot_h + tot_a * h_in
    hcarry_ref[...] = group_end[groups - 1:groups, :]
    group_in = down_one(group_end, h_in)

    for r in range(SUBLANES):
        h = hs[r] + ps[r] * group_in
        y = h * jax.nn.gelu(slab(buf_b, r))
        yn = _rms(y, grec_ref[...])
        for c in range(chunks):
            buf_a[c, pl.ds(r, groups, stride=SUBLANES), :] = yn[:, c * LANES:(c + 1) * LANES]
    for c in range(chunks):
        yrec_ref[:, c * LANES:(c + 1) * LANES] = buf_a[c].astype(BF16)

    project_rotary(q_ref, 2 * D_REC, HALF_DIM ** -0.5 * math.log2(math.e))
    project_rotary(k_ref, 2 * D_REC + D_QK, 1.0)
    project_v()


def _mix_in(x2, pos_row, inv_freq, rope_pat, ln_mix, w_in, conv_w, conv_b, wa_bd, b_a, wi_bd, b_i, rg_lambda,
            g_rec, batch, seq):
    tm = MIX_ROWS
    nt = seq // tm
    d_in = w_in.shape[1]
    row_map = lambda b, i: (b * nt + i, 0)
    fixed2 = lambda b, i: (0, 0)
    fixed3 = lambda b, i: (0, 0, 0)
    t = batch * seq
    out_shapes = (
        jax.ShapeDtypeStruct((t, D_REC), BF16),
        jax.ShapeDtypeStruct((t, D_QK), BF16),
        jax.ShapeDtypeStruct((t, D_QK), BF16),
        jax.ShapeDtypeStruct((t // tm, N_HEADS * V_EXT, tm), BF16),
    )
    return pl.pallas_call(
        _mix_in_kernel,
        grid=(batch, nt),
        in_specs=[
            pl.BlockSpec((tm, D_MODEL), row_map),
            pl.BlockSpec((1, tm), lambda b, i: (0, b * nt + i)),
            pl.BlockSpec((ROPE_DIM // 2, 1), fixed2),
            pl.BlockSpec((ROPE_DIM, 3 * LANES), fixed2),
            pl.BlockSpec((1, D_MODEL), fixed2),
            pl.BlockSpec((D_MODEL, d_in), fixed2, pipeline_mode=pl.Buffered(1)),
            pl.BlockSpec((CONV_WIDTH, D_REC), fixed2),
            pl.BlockSpec((1, D_REC), fixed2),
            pl.BlockSpec((2, D_REC // 2, D_REC // 2), fixed3),
            pl.BlockSpec((1, D_REC), fixed2),
            pl.BlockSpec((2, D_REC // 2, D_REC // 2), fixed3),
            pl.BlockSpec((1, D_REC), fixed2),
            pl.BlockSpec((1, D_REC), fixed2),
            pl.BlockSpec((1, D_REC), fixed2),
        ],
        out_specs=[
            pl.BlockSpec((tm, D_REC), row_map),
            pl.BlockSpec((tm, D_QK), row_map),
            pl.BlockSpec((tm, D_QK), row_map),
            pl.BlockSpec((1, N_HEADS * V_EXT, tm), lambda b, i: (b * nt + i, 0, 0)),
        ],
        out_shape=out_shapes,
        scratch_shapes=[pltpu.VMEM((SUBLANES, D_REC), F32), pltpu.VMEM((1, D_REC), F32),
                        pltpu.VMEM((D_REC // LANES, tm, LANES), F32),
                        pltpu.VMEM((D_REC // LANES, tm, LANES), F32),
                        pltpu.VMEM((D_MODEL, d_in), BF16)],
        compiler_params=pltpu.CompilerParams(
            dimension_semantics=("arbitrary", "arbitrary"), vmem_limit_bytes=VMEM_LIMIT),
        name="mix_in",
    )(x2, pos_row, inv_freq, rope_pat, ln_mix, w_in, conv_w, conv_b, wa_bd, b_a, wi_bd, b_i, rg_lambda, g_rec)


def _attn_kernel(lq1_ref, lk1_ref, lq2_ref, lk2_ref, gsub_ref, bias_ref, q_ref, k_ref, vt_ref, o_ref,
                 m_ref, acc_ref, aprev_ref, s0_ref, s1_ref, mb0_ref, mb1_ref, p0_ref, p1_ref, *, lam_init):
    tq = ATT_Q
    tk = vt_ref.shape[2]
    assert tq == tk, "the causal bias tile assumes the diagonal block is square"
    lam = (jnp.exp(jnp.sum(lq1_ref[...] * lk1_ref[...], axis=-1, keepdims=True))
           - jnp.exp(jnp.sum(lq2_ref[...] * lk2_ref[...], axis=-1, keepdims=True)) + lam_init)

    def query_tile(i, carry):
        _attn_query_tile(i, lam, gsub_ref, bias_ref, q_ref, k_ref, vt_ref, o_ref, m_ref, acc_ref, aprev_ref,
                         s0_ref, s1_ref, mb0_ref, mb1_ref, p0_ref, p1_ref, lam_init=lam_init)
        return carry

    lax.fori_loop(0, q_ref.shape[0] // tq, query_tile, 0)


def _attn_query_tile(i, lam, gsub_ref, bias_ref, q_ref, k_ref, vt_ref, o_ref, m_ref, acc_ref, aprev_ref,
                     s0_ref, s1_ref, mb0_ref, mb1_ref, p0_ref, p1_ref, *, lam_init):
    tq = ATT_Q
    tk = vt_ref.shape[2]
    q_rows = pl.ds(pl.multiple_of(i * tq, tq), tq)

    qt = q_ref[q_rows, :].astype(F32).T
    dim = lax.broadcasted_iota(I32, (LANES, tq), 0)
    qqt = jnp.concatenate([jnp.where(dim < HALF_DIM, qt, 0.0), jnp.where(dim >= HALF_DIM, qt, 0.0)],
                          axis=1).astype(BF16)

    n = (i * tq) // tk

    def scores(j):
        return _dot(k_ref[pl.ds(pl.multiple_of(j * tk, tk), tk), :], qqt)

    def probabilities(s, m_prev):
        m_new = jnp.maximum(m_prev, jnp.max(s, axis=0, keepdims=True))
        alpha = jnp.exp2(m_prev - m_new)
        p = jnp.exp2((s - m_new).astype(BF16))
        return p, alpha, m_new

    def store_scores(j, s_buf, mb_buf):
        s = scores(j)
        s_buf[...] = s
        mb_buf[...] = jnp.max(s, axis=0, keepdims=True)

    def pipe_step(j, cur, nxt, p_cur, p_prev):
        store_scores(j + 1, *nxt)
        s_cur, mb_cur = cur
        m_prev = m_ref[...]
        m_new = jnp.maximum(m_prev, mb_cur[...])
        p_cur[...] = jnp.exp2((s_cur[...] - m_new).astype(BF16))
        m_ref[...] = m_new
        if p_prev is not None:
            acc_ref[...] = aprev_ref[...] * acc_ref[...] + _dot(vt_ref[jnp.maximum(j - 1, 0)], p_prev[...])
        aprev_ref[...] = jnp.exp2(m_prev - m_new)

    buf0, buf1 = (s0_ref, mb0_ref), (s1_ref, mb1_ref)
    m_ref[...] = jnp.full_like(m_ref, NEG_BIG)
    acc_ref[...] = jnp.zeros_like(acc_ref)
    aprev_ref[...] = jnp.ones_like(aprev_ref)
    odd = n % 2

    @pl.when(odd == 0)
    def _():
        p1_ref[...] = jnp.zeros_like(p1_ref)
        store_scores(0, *buf0)

    @pl.when(odd == 1)
    def _():
        store_scores(0, *buf1)
        pipe_step(0, buf1, buf0, p1_ref, None)

    def pair(t, carry):
        j = 2 * t + odd
        pipe_step(j, buf0, buf1, p0_ref, p1_ref)
        pipe_step(j + 1, buf1, buf0, p1_ref, p0_ref)
        return carry

    lax.fori_loop(0, n // 2, pair, 0)

    parts = []
    for c0 in range(0, 2 * tq, ATT_TAIL_LANES):
        cols = slice(c0, c0 + ATT_TAIL_LANES)
        keys = c0 % tq + ATT_TAIL_LANES
        p, alpha, _ = probabilities(s0_ref[:keys, cols] + bias_ref[:keys, cols], m_ref[:, cols])
        part = aprev_ref[:, cols] * acc_ref[:, cols] + _dot(vt_ref[jnp.maximum(n - 1, 0)], p1_ref[:, cols])
        parts.append(alpha * part + _dot(vt_ref[n][:, :keys], p))
    acc = jnp.concatenate(parts, axis=1)

    o = acc[:V_DIM] / acc[V_DIM:V_DIM + 1]
    o = o[:, :tq] - lam * o[:, tq:]
    o = o * lax.rsqrt(jnp.mean(o * o, axis=0, keepdims=True) + EPS) * gsub_ref[...]
    o_ref[q_rows, :] = (o * (1.0 - lam_init)).T.astype(BF16)


def _attention(q, k, vt, lq1, lk1, lq2, lk2, g_sub_col, batch, seq, lam_init):
    tq = ATT_Q
    nq = seq // tq
    tk = vt.shape[2]
    nk = seq // tk
    vec = lambda b, h: (0, 0)
    per_head = lambda b, h: (b, h)
    visible = (lax.broadcasted_iota(I32, (tk, 2 * tq), 0) <= lax.broadcasted_iota(I32, (tk, 2 * tq), 1) % tq)
    bias = jnp.where(visible, 0.0, NEG_BIG).astype(F32)
    return pl.pallas_call(
        functools.partial(_attn_kernel, lam_init=lam_init),
        grid=(batch, N_HEADS),
        in_specs=[
            pl.BlockSpec((1, HALF_DIM), vec),
            pl.BlockSpec((1, HALF_DIM), vec),
            pl.BlockSpec((1, HALF_DIM), vec),
            pl.BlockSpec((1, HALF_DIM), vec),
            pl.BlockSpec((V_DIM, 1), vec),
            pl.BlockSpec((tk, 2 * tq), vec),
            pl.BlockSpec((seq, LANES), per_head),
            pl.BlockSpec((seq, LANES), per_head),
            pl.BlockSpec((nk, V_EXT, tk), lambda b, h: (b, h, 0)),
        ],
        out_specs=pl.BlockSpec((seq, V_DIM), per_head),
        out_shape=jax.ShapeDtypeStruct((batch * seq, D_ATT), BF16),
        scratch_shapes=[pltpu.VMEM((1, 2 * tq), F32),
                        pltpu.VMEM((V_EXT, 2 * tq), F32), pltpu.VMEM((1, 2 * tq), F32),
                        pltpu.VMEM((tk, 2 * tq), F32), pltpu.VMEM((tk, 2 * tq), F32),
                        pltpu.VMEM((1, 2 * tq), F32), pltpu.VMEM((1, 2 * tq), F32),
                        pltpu.VMEM((tk, 2 * tq), BF16), pltpu.VMEM((tk, 2 * tq), BF16)],
        compiler_params=pltpu.CompilerParams(
            dimension_semantics=("arbitrary", "arbitrary"), vmem_limit_bytes=VMEM_LIMIT),
        name="attention",
    )(lq1, lk1, lq2, lk2, g_sub_col, bias, q, k, vt)


def _sublane_total(x, op):
    return op(x, axis=0, keepdims=True)


def _route_kernel(x_ref, yrec_ref, yatt_ref, wo32_ref, lnmoe_ref, wrt_ref, ebias_ref, tri_ref,
                  h1_ref, hpa_ref, hpb_ref, ek_ref, wk_ref, rk_ref, cnt_ref, carry_ref, wo_ref):
    tm = x_ref.shape[0]
    e_n = N_EXPERTS

    @pl.when(pl.program_id(0) == 0)
    def _():
        wo_ref[...] = wo32_ref[...].astype(BF16)
        carry_ref[...] = jnp.zeros_like(carry_ref)

    h1 = x_ref[...] + _dot(yrec_ref[...], wo_ref[:D_REC, :]) + _dot(yatt_ref[...], wo_ref[D_REC:, :])
    h1_ref[...] = h1
    hn = _rms(h1, lnmoe_ref[...])
    pa, pb = _pack_row(hn)
    hpa_ref[...] = pa
    hpb_ref[...] = pb

    logits = lax.dot_general(wrt_ref[...], hn, (((1,), (1,)), ((), ())),
                             precision=lax.Precision.HIGHEST, preferred_element_type=F32)
    scores = jax.nn.sigmoid(logits)
    sel = scores + ebias_ref[...]

    sel3 = sel.reshape(N_GROUPS, GROUP_SIZE, tm)
    idx3 = lax.broadcasted_iota(I32, (N_GROUPS, GROUP_SIZE, tm), 1)
    m1 = jnp.max(sel3, axis=1, keepdims=True)
    first = jnp.min(jnp.where(sel3 == m1, idx3, GROUP_SIZE), axis=1, keepdims=True)
    m2 = jnp.max(jnp.where(idx3 == first, -jnp.inf, sel3), axis=1, keepdims=True)
    gscore = (m1 + m2).reshape(N_GROUPS, tm)

    gidx = lax.broadcasted_iota(I32, (N_GROUPS, tm), 0)
    beaten = jnp.zeros((N_GROUPS, tm), I32)
    for g in range(N_GROUPS):
        other = gscore[g:g + 1, :]
        beats = (other > gscore) | ((other == gscore) & (g < gidx))
        beaten = beaten + beats.astype(I32)
    gkeep = beaten < TOPK_GROUPS
    keep = jnp.broadcast_to(gkeep.reshape(N_GROUPS, 1, tm), (N_GROUPS, GROUP_SIZE, tm)).reshape(e_n, tm)
    selm = jnp.where(keep, sel, -jnp.inf)

    eidx = lax.broadcasted_iota(I32, (e_n, tm), 0)
    remaining = selm
    picks, ek, sk = [], [], []
    for _ in range(TOP_K):
        best = jnp.max(remaining, axis=0, keepdims=True)
        first = jnp.min(jnp.where(remaining == best, eidx, e_n), axis=0, keepdims=True)
        pick = eidx == first
        picks.append(pick)
        ek.append(first)
        sk.append(_sublane_total(jnp.where(pick, scores, 0.0), jnp.sum))
        remaining = jnp.where(pick, -jnp.inf, remaining)
    chosen_f = (remaining != selm).astype(F32)
    wsum = functools.reduce(lambda a, b: a + b, sk)
    ek_ref[...] = jnp.concatenate(ek, axis=0)
    wk_ref[...] = (jnp.concatenate(sk, axis=0) * (ROUTE_SCALE / wsum)).T

    prefix = _dot(chosen_f.astype(BF16), tri_ref[...])
    rank = prefix + carry_ref[...]
    carry_new = carry_ref[...] + jnp.sum(chosen_f, axis=1, keepdims=True)
    carry_ref[...] = carry_new
    cnt_ref[...] = carry_new.astype(I32)
    rk = [_sublane_total(jnp.where(pick, rank, 0.0), jnp.sum) for pick in picks]
    rk_ref[...] = jnp.concatenate(rk, axis=0).astype(I32)


def _route(x2, y_rec, y_att, w_out, ln_moe, w_router_t, e_bias_col):
    t = x2.shape[0]
    tm = ROUTE_ROWS
    nt = t // tm
    row_map = lambda i: (i, 0)
    col_map = lambda i: (0, i)
    fixed = lambda i: (0, 0)
    tri = (lax.broadcasted_iota(I32, (tm, tm), 0) < lax.broadcasted_iota(I32, (tm, tm), 1)).astype(BF16)
    out_shapes = (
        jax.ShapeDtypeStruct((t, D_MODEL), F32),
        jax.ShapeDtypeStruct((t, PACK_W), U32),
        jax.ShapeDtypeStruct((t, PACK_W), U32),
        jax.ShapeDtypeStruct((TOP_K, t), I32),
        jax.ShapeDtypeStruct((t, TOP_K), F32),
        jax.ShapeDtypeStruct((TOP_K, t), I32),
        jax.ShapeDtypeStruct((N_EXPERTS, 1), I32),
    )
    return pl.pallas_call(
        _route_kernel,
        grid=(nt,),
        in_specs=[
            pl.BlockSpec((tm, D_MODEL), row_map),
            pl.BlockSpec((tm, D_REC), row_map),
            pl.BlockSpec((tm, D_ATT), row_map),
            pl.BlockSpec((D_REC + D_ATT, D_MODEL), fixed, pipeline_mode=pl.Buffered(1)),
            pl.BlockSpec((1, D_MODEL), fixed),
            pl.BlockSpec((N_EXPERTS, D_MODEL), fixed),
            pl.BlockSpec((N_EXPERTS, 1), fixed),
            pl.BlockSpec((tm, tm), fixed),
        ],
        out_specs=[
            pl.BlockSpec((tm, D_MODEL), row_map),
            pl.BlockSpec((tm, PACK_W), row_map),
            pl.BlockSpec((tm, PACK_W), row_map),
            pl.BlockSpec((TOP_K, tm), col_map),
            pl.BlockSpec((tm, TOP_K), row_map),
            pl.BlockSpec((TOP_K, tm), col_map),
            pl.BlockSpec((N_EXPERTS, 1), fixed),
        ],
        out_shape=out_shapes,
        scratch_shapes=[pltpu.VMEM((N_EXPERTS, 1), F32), pltpu.VMEM((D_REC + D_ATT, D_MODEL), BF16)],
        compiler_params=pltpu.CompilerParams(dimension_semantics=("arbitrary",), vmem_limit_bytes=VMEM_LIMIT),
        name="route",
    )(x2, y_rec, y_att, w_out, ln_moe, w_router_t, e_bias_col, tri)


def _plan_kernel(pad_start_ref, ek_ref, rk_ref, dest_ref):
    ek = ek_ref[...]

    def add_expert(e, base):
        return jnp.where(ek == e, pad_start_ref[e], base)

    dest_ref[...] = rk_ref[...] + lax.fori_loop(0, N_EXPERTS, add_expert, jnp.zeros_like(ek))


def _plan(pad_start, ek, rk):
    kk, t = ek.shape
    tl = min(t, 2048)
    col_map = lambda i, ps: (0, i)
    grid_spec = pltpu.PrefetchScalarGridSpec(
        num_scalar_prefetch=1,
        grid=(t // tl,),
        in_specs=[pl.BlockSpec((kk, tl), col_map), pl.BlockSpec((kk, tl), col_map)],
        out_specs=pl.BlockSpec((kk, tl), col_map),
    )
    return pl.pallas_call(
        _plan_kernel,
        grid_spec=grid_spec,
        out_shape=jax.ShapeDtypeStruct((kk, t), I32),
        compiler_params=pltpu.CompilerParams(dimension_semantics=("arbitrary",)),
        name="plan",
    )(pad_start, ek, rk)


def _sc_mesh():
    return plsc.VectorSubcoreMesh(core_axis_name="core", subcore_axis_name="subcore")


def _sc_dispatch(rows, dest, n_out):
    t, w = rows.shape
    kk = dest.shape[0]

    @pl.kernel(out_type=jax.ShapeDtypeStruct((n_out, w), rows.dtype), mesh=_sc_mesh(), scratch_types=[])
    def kern(x_hbm, i_hbm, o_hbm):
        def body(x_vmem, i_vmem):
            for k in range(kk):
                pltpu.sync_copy(x_vmem, o_hbm.at[i_vmem.at[k]])

        pltpu.emit_pipeline(
            body,
            grid=(t // SC_WINDOW,),
            in_specs=[pl.BlockSpec((SC_WINDOW, w), lambda i: (i, 0)),
                      pl.BlockSpec((kk, SC_WINDOW), lambda i: (0, i))],
            out_specs=[],
            core_axis_name=("core", "subcore"),
            dimension_semantics=(pltpu.PARALLEL,),
        )(x_hbm, i_hbm)

    return kern(rows, dest)


def _sc_combine(rows, dest):
    kk, t = dest.shape
    w = rows.shape[1]
    flat = dest.reshape(1, kk * t)

    @pl.kernel(out_type=jax.ShapeDtypeStruct((kk * t, w), rows.dtype), mesh=_sc_mesh(), scratch_types=[])
    def kern(y_hbm, i_hbm, o_hbm):
        def body(i_vmem, o_vmem):
            pltpu.sync_copy(y_hbm.at[i_vmem.at[0]], o_vmem)

        pltpu.emit_pipeline(
            body,
            grid=(kk * t // SC_WINDOW,),
            in_specs=[pl.BlockSpec((1, SC_WINDOW), lambda i: (0, i))],
            out_specs=[pl.BlockSpec((SC_WINDOW, w), lambda i: (i, 0))],
            core_axis_name=("core", "subcore"),
            dimension_semantics=(pltpu.PARALLEL,),
        )(i_hbm, o_hbm)

    return kern(rows, flat).reshape(kk, t, w)


X_SLOTS = 3
Y_SLOTS = 2


def _experts_kernel(blk_expert_ref, n_used_ref, first_ref, slot_ref, next_ref,
                    xa_hbm, xb_hbm, w1_hbm, w3_hbm, w2_hbm, ya_hbm, yb_hbm,
                    xa_buf, xb_buf, ya_buf, yb_buf, w1f_ref, w3f_ref, w2f_ref, w1b_ref, w3b_ref, w2b_ref,
                    wsem, xsem, ysem):
    m = EXPERT_ROWS
    n_used = n_used_ref[0]

    def weight_copies(e, s):
        return (pltpu.make_async_copy(w1_hbm.at[e], w1f_ref.at[s], wsem.at[s, 0]),
                pltpu.make_async_copy(w3_hbm.at[e], w3f_ref.at[s], wsem.at[s, 1]),
                pltpu.make_async_copy(w2_hbm.at[e], w2f_ref.at[s], wsem.at[s, 2]))

    def x_copies(b):
        rows, s = pl.ds(pl.multiple_of(b * m, m), m), b % X_SLOTS
        return (pltpu.make_async_copy(xa_hbm.at[rows], xa_buf.at[s], xsem.at[s, 0]),
                pltpu.make_async_copy(xb_hbm.at[rows], xb_buf.at[s], xsem.at[s, 1]))

    def y_copies(b):
        rows, s = pl.ds(pl.multiple_of(b * m, m), m), b % Y_SLOTS
        return (pltpu.make_async_copy(ya_buf.at[s], ya_hbm.at[rows], ysem.at[s, 0]),
                pltpu.make_async_copy(yb_buf.at[s], yb_hbm.at[rows], ysem.at[s, 1]))

    def start(copies):
        for copy in copies:
            copy.start()

    def wait(copies):
        for copy in copies:
            copy.wait()

    start(weight_copies(blk_expert_ref[0], 0))
    start(x_copies(0))

    @pl.when(n_used > 1)
    def _():
        start(x_copies(1))

    def block(b, carry):
        @pl.when(b + 2 < n_used)
        def _():
            start(x_copies(b + 2))

        @pl.when(first_ref[b] == 1)
        def _():
            s = slot_ref[b]
            wait(weight_copies(blk_expert_ref[b], s))

            @pl.when(next_ref[b] >= 0)
            def _():
                start(weight_copies(next_ref[b], 1 - s))

            w1b_ref[...] = w1f_ref[s].astype(BF16)
            w3b_ref[...] = w3f_ref[s].astype(BF16)
            w2b_ref[...] = w2f_ref[s].astype(BF16)

        wait(x_copies(b))

        @pl.when(b >= Y_SLOTS)
        def _():
            wait(y_copies(b - Y_SLOTS))

        x = _unpack_row(xa_buf[b % X_SLOTS], xb_buf[b % X_SLOTS]).astype(BF16)
        a = _dot(x, w1b_ref[...])
        g = _dot(x, w3b_ref[...])
        hmid = (jax.nn.silu(a) * g).astype(BF16)
        y = _dot(hmid, w2b_ref[...])
        pa, pb = _pack_row(y)
        ya_buf[b % Y_SLOTS] = pa
        yb_buf[b % Y_SLOTS] = pb
        start(y_copies(b))
        return carry

    lax.fori_loop(0, n_used, block, 0)

    for back in range(Y_SLOTS, 0, -1):
        @pl.when(n_used - back >= 0)
        def _(back=back):
            wait(y_copies(n_used - back))


def _experts(xa, xb, w1, w3, w2, blk_expert, n_used, seg_first, seg_slot, seg_next):
    p = xa.shape[0]
    m = EXPERT_ROWS
    hbm = pl.BlockSpec(memory_space=pl.ANY)
    grid_spec = pltpu.PrefetchScalarGridSpec(
        num_scalar_prefetch=5,
        grid=(1,),
        in_specs=[hbm] * 5,
        out_specs=[hbm, hbm],
        scratch_shapes=[
            pltpu.VMEM((X_SLOTS, m, PACK_W), U32), pltpu.VMEM((X_SLOTS, m, PACK_W), U32),
            pltpu.VMEM((Y_SLOTS, m, PACK_W), U32), pltpu.VMEM((Y_SLOTS, m, PACK_W), U32),
            pltpu.VMEM((2, D_MODEL, D_EXPERT), F32), pltpu.VMEM((2, D_MODEL, D_EXPERT), F32),
            pltpu.VMEM((2, D_EXPERT, D_MODEL), F32),
            pltpu.VMEM((D_MODEL, D_EXPERT), BF16), pltpu.VMEM((D_MODEL, D_EXPERT), BF16),
            pltpu.VMEM((D_EXPERT, D_MODEL), BF16),
            pltpu.SemaphoreType.DMA((2, 3)), pltpu.SemaphoreType.DMA((X_SLOTS, 2)),
            pltpu.SemaphoreType.DMA((Y_SLOTS, 2)),
        ],
    )
    return pl.pallas_call(
        _experts_kernel,
        grid_spec=grid_spec,
        out_shape=(jax.ShapeDtypeStruct((p, PACK_W), U32), jax.ShapeDtypeStruct((p, PACK_W), U32)),
        compiler_params=pltpu.CompilerParams(dimension_semantics=("arbitrary",), vmem_limit_bytes=VMEM_LIMIT),
        name="experts",
    )(blk_expert, n_used, seg_first, seg_slot, seg_next, xa, xb, w1, w3, w2)


def _tail_kernel(h1_ref, ga_ref, gb_ref, wk_ref, p_ref, lnmoe_ref, ws1_32, ws3_32, ws2_32, lnple_ref,
                 wpg_32, wpp_32, lnf_ref, o_ref, ws1_ref, ws3_ref, ws2_ref, wpg_ref, wpp_ref):
    @pl.when(pl.program_id(0) == 0)
    def _():
        for dst, src in ((ws1_ref, ws1_32), (ws3_ref, ws3_32), (ws2_ref, ws2_32), (wpg_ref, wpg_32),
                         (wpp_ref, wpp_32)):
            dst[...] = src[...].astype(BF16)

    h1 = h1_ref[...]
    hn = _rms(h1, lnmoe_ref[...]).astype(BF16)
    shared = _dot((jax.nn.silu(_dot(hn, ws1_ref[...])) * _dot(hn, ws3_ref[...])).astype(BF16), ws2_ref[...])
    wk = wk_ref[...]
    routed = jnp.zeros_like(h1)
    for kk in range(TOP_K):
        routed = routed + wk[:, kk:kk + 1] * _unpack_row(ga_ref[kk], gb_ref[kk])
    h2 = h1 + routed + shared
    gate = jax.nn.sigmoid(_dot(_rms(h2, lnple_ref[...]).astype(BF16), wpg_ref[...]))
    h3 = h2 + gate * _dot(p_ref[...].astype(BF16), wpp_ref[...])
    o_ref[...] = _rms(h3, lnf_ref[...])


def _tail(h1, ga, gb, wk_t, p2, ln_moe, ws1, ws3, ws2, ln_ple, w_pg, w_pp, ln_f):
    t = h1.shape[0]
    tm = TAIL_ROWS
    row_map = lambda i: (i, 0)
    fixed = lambda i: (0, 0)
    g_map = lambda i: (0, i, 0)
    d_sh = ws1.shape[1]
    return pl.pallas_call(
        _tail_kernel,
        grid=(t // tm,),
        in_specs=[
            pl.BlockSpec((tm, D_MODEL), row_map),
            pl.BlockSpec((TOP_K, tm, PACK_W), g_map),
            pl.BlockSpec((TOP_K, tm, PACK_W), g_map),
            pl.BlockSpec((tm, TOP_K), row_map),
            pl.BlockSpec((tm, D_PLE), row_map),
            pl.BlockSpec((1, D_MODEL), fixed),
            pl.BlockSpec((D_MODEL, d_sh), fixed, pipeline_mode=pl.Buffered(1)),
            pl.BlockSpec((D_MODEL, d_sh), fixed, pipeline_mode=pl.Buffered(1)),
            pl.BlockSpec((d_sh, D_MODEL), fixed, pipeline_mode=pl.Buffered(1)),
            pl.BlockSpec((1, D_MODEL), fixed),
            pl.BlockSpec((D_MODEL, D_MODEL), fixed, pipeline_mode=pl.Buffered(1)),
            pl.BlockSpec((D_PLE, D_MODEL), fixed, pipeline_mode=pl.Buffered(1)),
            pl.BlockSpec((1, D_MODEL), fixed),
        ],
        out_specs=pl.BlockSpec((tm, D_MODEL), row_map),
        out_shape=jax.ShapeDtypeStruct((t, D_MODEL), F32),
        scratch_shapes=[pltpu.VMEM((D_MODEL, d_sh), BF16), pltpu.VMEM((D_MODEL, d_sh), BF16),
                        pltpu.VMEM((d_sh, D_MODEL), BF16), pltpu.VMEM((D_MODEL, D_MODEL), BF16),
                        pltpu.VMEM((D_PLE, D_MODEL), BF16)],
        compiler_params=pltpu.CompilerParams(dimension_semantics=("arbitrary",), vmem_limit_bytes=VMEM_LIMIT),
        name="tail",
    )(h1, ga, gb, wk_t, p2, ln_moe, ws1, ws3, ws2, ln_ple, w_pg, w_pp, ln_f)


def _rope_constants():
    half = ROPE_DIM // 2
    inv_freq = (ROPE_THETA ** (-jnp.arange(0, ROPE_DIM, 2, dtype=F32) / ROPE_DIM)).reshape(half, 1)
    f = lax.broadcasted_iota(I32, (ROPE_DIM, LANES), 0)
    l64 = lax.broadcasted_iota(I32, (ROPE_DIM, LANES), 1) % HALF_DIM
    cos_pat = ((f < half) & (l64 < ROPE_DIM) & (l64 % half == f)).astype(F32)
    sa_pat = -((f >= half) & (l64 < half) & (l64 == f - half)).astype(F32)
    sb_pat = ((f >= half) & (l64 >= half) & (l64 < ROPE_DIM) & (l64 - half == f - half)).astype(F32)
    return inv_freq, jnp.concatenate([cos_pat, sa_pat, sb_pat], axis=1)


def _block_diag_tiles(w):
    nb, bd, _ = w.shape
    per = nb // 2
    tiles = []
    for tix in range(2):
        rows = []
        for j in range(per):
            rows.append(jnp.concatenate(
                [w[tix * per + j] if c == j else jnp.zeros((bd, bd), w.dtype) for c in range(per)], axis=1))
        tiles.append(jnp.concatenate(rows, axis=0))
    return jnp.stack(tiles).astype(BF16)


def _layer(h, p_l, positions, lam_init, ln_mix, w_in, conv_w, conv_b, w_a, b_a, w_i, b_i, rg_lambda, g_rec,
           lq1, lk1, lq2, lk2, g_sub, w_out, ln_moe, w_router, e_bias, w1, w3, w2, ws1, ws3, ws2,
           ln_ple, w_ple_gate, w_ple_proj, ln_out):
    batch, seq, _ = h.shape
    t = batch * seq
    x2 = h.reshape(t, D_MODEL)
    row = lambda a: a.reshape(1, -1)
    inv_freq, rope_pat = _rope_constants()

    y_rec, q, k, vt = _mix_in(
        x2, positions.reshape(1, t), inv_freq, rope_pat, row(ln_mix), w_in, conv_w, row(conv_b),
        _block_diag_tiles(w_a), row(b_a), _block_diag_tiles(w_i), row(b_i), row(rg_lambda), row(g_rec),
        batch, seq)
    y_att = _attention(q, k, vt, row(lq1), row(lk1), row(lq2), row(lk2), g_sub.reshape(-1, 1), batch, seq,
                       lam_init)

    h1, hpa, hpb, ek, wk_t, rk, counts = _route(
        x2, y_rec, y_att, w_out, row(ln_moe), w_router.T, e_bias.reshape(-1, 1))

    m = EXPERT_ROWS
    counts = counts.reshape(-1)
    padded = (counts + m - 1) // m * m
    pad_end = jnp.cumsum(padded)
    pad_start = pad_end - padded
    n_rows = t * TOP_K + N_EXPERTS * m
    nblk = n_rows // m
    n_used = (pad_end[-1] // m).astype(I32).reshape(1)
    blk = jnp.arange(nblk, dtype=I32)
    blk_row = jnp.minimum(blk, n_used[0] - 1) * m
    blk_expert = jnp.sum((pad_end[None, :] <= blk_row[:, None]).astype(I32), axis=1)
    prev_expert = jnp.concatenate([jnp.full((1,), -1, I32), blk_expert[:-1]])
    seg_first = ((blk < n_used[0]) & (blk_expert != prev_expert)).astype(I32)
    seg_slot = ((jnp.cumsum(seg_first) - 1) % 2).astype(I32)
    eid = jnp.arange(N_EXPERTS, dtype=I32)
    later = (padded[None, :] > 0) & (eid[None, :] > eid[:, None])
    next_expert = jnp.min(jnp.where(later, eid[None, :], N_EXPERTS), axis=1)
    next_expert = jnp.where(next_expert == N_EXPERTS, -1, next_expert).astype(I32)
    seg_next = jnp.sum(jnp.where(blk_expert[:, None] == eid[None, :], next_expert[None, :], 0), axis=1)
    dest = _plan(pad_start.astype(I32), ek, rk)

    xa = _sc_dispatch(hpa, dest, n_rows)
    xb = _sc_dispatch(hpb, dest, n_rows)
    ya, yb = _experts(xa, xb, w1, w3, w2, blk_expert, n_used, seg_first, seg_slot, seg_next)
    ga = _sc_combine(ya, dest)
    gb = _sc_combine(yb, dest)

    out = _tail(h1, ga, gb, wk_t, p_l.reshape(t, D_PLE), row(ln_moe), ws1, ws3, ws2, row(ln_ple), w_ple_gate,
                w_ple_proj, row(ln_out))
    return out.reshape(batch, seq, D_MODEL)


def kernel(x, p, positions, ln_mix, w_in, conv_w, conv_b, w_a, b_a, w_i, b_i, rg_lambda, g_rec, lq1, lk1, lq2,
           lk2, g_sub, w_out, ln_moe, w_router, e_bias, w1, w3, w2, ws1, ws3, ws2, ln_ple, w_ple_gate,
           w_ple_proj, ln_f):
    depth = w_in.shape[0]
    assert depth == 1, "the fused tail applies the final norm; one layer supported"
    lam_init = 0.8 - 0.6 * math.exp(-0.3 * 0)
    return _layer(x, p[0], positions, lam_init, ln_mix[0], w_in[0], conv_w[0], conv_b[0], w_a[0], b_a[0], w_i[0],
                  b_i[0], rg_lambda[0], g_rec[0], lq1[0], lk1[0], lq2[0], lk2[0], g_sub[0], w_out[0], ln_moe[0],
                  w_router[0], e_bias[0], w1[0], w3[0], w2[0], ws1[0], ws3[0], ws2[0], ln_ple[0], w_ple_gate[0],
                  w_ple_proj[0], ln_f)
```

```python
import functools
import math

import jax
import jax.numpy as jnp
from jax import lax
from jax.experimental import pallas as pl
from jax.experimental.pallas import tpu as pltpu
from jax.experimental.pallas import tpu_sc as plsc

F32 = jnp.float32
BF16 = jnp.bfloat16
U32 = jnp.uint32
I32 = jnp.int32

D_MODEL = 1024
D_REC = 512
REC_BLOCKS = 8
CONV_WIDTH = 4
RG_C = 8.0
N_HEADS = 4
HALF_DIM = 64
V_DIM = 128
D_ATT = N_HEADS * V_DIM
D_QK = N_HEADS * 2 * HALF_DIM
ROPE_DIM = 16
ROPE_THETA = 500000.0
N_EXPERTS = 64
TOP_K = 8
N_GROUPS = 8
GROUP_SIZE = N_EXPERTS // N_GROUPS
TOPK_GROUPS = 4
D_EXPERT = 256
ROUTE_SCALE = 2.5
D_PLE = 256
EPS = 1e-6

LANES = 128
SUBLANES = 8
VMEM_LIMIT = 56 * 1024 * 1024

MIX_ROWS = 512
ATT_Q = 512
ATT_TAIL_LANES = 256
ONES_ROWS = 16
V_EXT = V_DIM + ONES_ROWS
ROUTE_ROWS = 1024
EXPERT_ROWS = 512
TAIL_ROWS = 512
SC_WINDOW = 128
PACK_W = 256
NEG_BIG = -1e30


def _rms(x, g):
    return x * lax.rsqrt(jnp.mean(x * x, axis=-1, keepdims=True) + EPS) * g


def _dot(a, b):
    return jnp.dot(a, b, preferred_element_type=F32)


def _pack_pair(lo, hi):
    lo_bits = lax.bitcast_convert_type(lo.astype(BF16).astype(F32), U32)
    hi_bits = lax.bitcast_convert_type(hi.astype(BF16).astype(F32), U32)
    return (lo_bits >> 16) | (hi_bits & jnp.uint32(0xFFFF0000))


def _unpack_pair(p):
    lo = lax.bitcast_convert_type(p << 16, F32)
    hi = lax.bitcast_convert_type(p & jnp.uint32(0xFFFF0000), F32)
    return lo, hi


def _pack_row(x):
    w = PACK_W
    return _pack_pair(x[:, 0:w], x[:, w:2 * w]), _pack_pair(x[:, 2 * w:3 * w], x[:, 3 * w:4 * w])


def _unpack_row(pa, pb):
    c0, c1 = _unpack_pair(pa)
    c2, c3 = _unpack_pair(pb)
    return jnp.concatenate([c0, c1, c2, c3], axis=1)


def _shift_rows(a, s, fill, row):
    n, c = a.shape
    if s % SUBLANES == 0:
        return jnp.concatenate([jnp.full((s, c), fill, a.dtype), a[:n - s]], axis=0)
    return jnp.where(row >= s, pltpu.roll(a, s, 0), fill)


def _mix_in_kernel(x_ref, pos_ref, invf_ref, pat_ref, lnm_ref, win32_ref, cw_ref, cb_ref, wa_ref, ba_ref,
                   wi_ref, bi_ref, lam_ref, grec_ref,
                   yrec_ref, q_ref, k_ref, vt_ref, tail_ref, hcarry_ref, buf_a, buf_b, win_ref):
    tm = x_ref.shape[0]
    groups = tm // SUBLANES
    chunks = D_REC // LANES

    def stage(ref, v):
        for c in range(chunks):
            ref[c] = v[:, c * LANES:(c + 1) * LANES]

    def slab(ref, r):
        return jnp.concatenate([ref[c, pl.ds(r, groups, stride=SUBLANES), :] for c in range(chunks)], axis=1)

    @pl.when((pl.program_id(0) == 0) & (pl.program_id(1) == 0))
    def _():
        win_ref[...] = win32_ref[...].astype(BF16)

    @pl.when(pl.program_id(1) == 0)
    def _():
        tail_ref[...] = jnp.zeros_like(tail_ref)
        hcarry_ref[...] = jnp.zeros_like(hcarry_ref)

    hn = _rms(x_ref[...], lnm_ref[...]).astype(BF16)

    ang = invf_ref[...] * pos_ref[...].astype(F32)
    cs = jnp.concatenate([jnp.cos(ang), jnp.sin(ang)], axis=0)
    tabs = lax.dot_general(cs, pat_ref[...], (((0,), (0,)), ((), ())),
                           precision=lax.Precision.HIGHEST, preferred_element_type=F32)
    lane64 = lax.broadcasted_iota(I32, (1, LANES), 1) % HALF_DIM
    cosf = tabs[:, 0:LANES] + (lane64 >= ROPE_DIM).astype(F32)
    sa, sb = tabs[:, LANES:2 * LANES], tabs[:, 2 * LANES:3 * LANES]

    def project_rotary(out_ref, off, mul):
        for c in range(0, D_QK // LANES, 2):
            z2 = _dot(hn, win_ref[:, off + c * LANES: off + (c + 2) * LANES])
            for cc in range(2):
                zc = z2[:, cc * LANES:(cc + 1) * LANES]
                rot = (zc * cosf + pltpu.roll(zc, LANES - ROPE_DIM // 2, 1) * sa
                       + pltpu.roll(zc, ROPE_DIM // 2, 1) * sb)
                out_ref[:, (c + cc) * LANES:(c + cc + 1) * LANES] = (rot * mul).astype(BF16)

    def project_v():
        vt = _dot(hn, win_ref[:, 2 * D_REC + 2 * D_QK:]).T.astype(BF16)
        for hd in range(N_HEADS):
            vt_ref[0, hd * V_EXT:hd * V_EXT + V_DIM, :] = vt[hd * V_DIM:(hd + 1) * V_DIM]
            vt_ref[0, hd * V_EXT + V_DIM:(hd + 1) * V_EXT, :] = jnp.ones((ONES_ROWS, tm), BF16)

    xr = _dot(hn, win_ref[:, 0:D_REC])
    stage(buf_a, xr)
    stage(buf_b, _dot(hn, win_ref[:, D_REC:2 * D_REC]))
    tail = tail_ref[...]
    tail_ref[...] = xr[tm - SUBLANES:, :]

    grow = lax.broadcasted_iota(I32, (groups, D_REC), 0)

    def down_one(a, first_row):
        return jnp.where(grow == 0, first_row, pltpu.roll(a, 1, 0))

    xs = [slab(buf_a, r) for r in range(SUBLANES)]
    wrapped = {r: down_one(xs[r], tail[r:r + 1, :]) for r in range(SUBLANES - CONV_WIDTH + 1, SUBLANES)}
    xc = []
    for r in range(SUBLANES):
        acc = cb_ref[...] + cw_ref[CONV_WIDTH - 1:CONV_WIDTH, :] * xs[r]
        for d in range(1, CONV_WIDTH):
            prev = xs[r - d] if r >= d else wrapped[r - d + SUBLANES]
            acc = acc + cw_ref[CONV_WIDTH - 1 - d:CONV_WIDTH - d, :] * prev
        xc.append(acc)
    xc = jnp.concatenate(xc, axis=0)

    xcb = xc.astype(BF16)
    half = D_REC // 2
    ra = jnp.concatenate([_dot(xcb[:, :half], wa_ref[0]), _dot(xcb[:, half:], wa_ref[1])], axis=1)
    ri = jnp.concatenate([_dot(xcb[:, :half], wi_ref[0]), _dot(xcb[:, half:], wi_ref[1])], axis=1)
    r_gate = jax.nn.sigmoid(ra + ba_ref[...])
    i_gate = jax.nn.sigmoid(ri + bi_ref[...])
    lam = lam_ref[...]
    softplus_neg = jnp.maximum(-lam, 0.0) + jnp.log(1.0 + jnp.exp(-jnp.abs(lam)))
    log_a = -RG_C * r_gate * softplus_neg
    a = jnp.exp(log_a)
    u = jnp.sqrt(1.0 - jnp.exp(2.0 * log_a)) * i_gate * xc

    rows = lambda v, r: v[r * groups:(r + 1) * groups]
    hs, ps = [rows(u, 0)], [rows(a, 0)]
    for r in range(1, SUBLANES):
        hs.append(rows(a, r) * hs[-1] + rows(u, r))
        ps.append(rows(a, r) * ps[-1])
    tot_a, tot_h = ps[-1], hs[-1]
    s = 1
    while s < groups:
        tot_h = tot_h + tot_a * _shift_rows(tot_h, s, 0.0, grow)
        tot_a = tot_a * _shift_rows(tot_a, s, 1.0, grow)
        s *= 2
    h_in = hcarry_ref[...]
    group_end = tot_h + tot_a * h_in
    hcarry_ref[...] = group_end[groups - 1:groups, :]
    group_in = down_one(group_end, h_in)

    for r in range(SUBLANES):
        h = hs[r] + ps[r] * group_in
        y = h * jax.nn.gelu(slab(buf_b, r))
        yn = _rms(y, grec_ref[...])
        for c in range(chunks):
            buf_a[c, pl.ds(r, groups, stride=SUBLANES), :] = yn[:, c * LANES:(c + 1) * LANES]
    for c in range(chunks):
        yrec_ref[:, c * LANES:(c + 1) * LANES] = buf_a[c].astype(BF16)

    project_rotary(q_ref, 2 * D_REC, HALF_DIM ** -0.5 * math.log2(math.e))
    project_rotary(k_ref, 2 * D_REC + D_QK, 1.0)
    project_v()


def _mix_in(x2, pos_row, inv_freq, rope_pat, ln_mix, w_in, conv_w, conv_b, wa_bd, b_a, wi_bd, b_i, rg_lambda,
            g_rec, batch, seq):
    tm = MIX_ROWS
    nt = seq // tm
    d_in = w_in.shape[1]
    row_map = lambda b, i: (b * nt + i, 0)
    fixed2 = lambda b, i: (0, 0)
    fixed3 = lambda b, i: (0, 0, 0)
    t = batch * seq
    out_shapes = (
        jax.ShapeDtypeStruct((t, D_REC), BF16),
        jax.ShapeDtypeStruct((t, D_QK), BF16),
        jax.ShapeDtypeStruct((t, D_QK), BF16),
        jax.ShapeDtypeStruct((t // tm, N_HEADS * V_EXT, tm), BF16),
    )
    return pl.pallas_call(
        _mix_in_kernel,
        grid=(batch, nt),
        in_specs=[
            pl.BlockSpec((tm, D_MODEL), row_map),
            pl.BlockSpec((1, tm), lambda b, i: (0, b * nt + i)),
            pl.BlockSpec((ROPE_DIM // 2, 1), fixed2),
            pl.BlockSpec((ROPE_DIM, 3 * LANES), fixed2),
            pl.BlockSpec((1, D_MODEL), fixed2),
            pl.BlockSpec((D_MODEL, d_in), fixed2, pipeline_mode=pl.Buffered(1)),
            pl.BlockSpec((CONV_WIDTH, D_REC), fixed2),
            pl.BlockSpec((1, D_REC), fixed2),
            pl.BlockSpec((2, D_REC // 2, D_REC // 2), fixed3),
            pl.BlockSpec((1, D_REC), fixed2),
            pl.BlockSpec((2, D_REC // 2, D_REC // 2), fixed3),
            pl.BlockSpec((1, D_REC), fixed2),
            pl.BlockSpec((1, D_REC), fixed2),
            pl.BlockSpec((1, D_REC), fixed2),
        ],
        out_specs=[
            pl.BlockSpec((tm, D_REC), row_map),
            pl.BlockSpec((tm, D_QK), row_map),
            pl.BlockSpec((tm, D_QK), row_map),
            pl.BlockSpec((1, N_HEADS * V_EXT, tm), lambda b, i: (b * nt + i, 0, 0)),
        ],
        out_shape=out_shapes,
        scratch_shapes=[pltpu.VMEM((SUBLANES, D_REC), F32), pltpu.VMEM((1, D_REC), F32),
                        pltpu.VMEM((D_REC // LANES, tm, LANES), F32),
                        pltpu.VMEM((D_REC // LANES, tm, LANES), F32),
                        pltpu.VMEM((D_MODEL, d_in), BF16)],
        compiler_params=pltpu.CompilerParams(
            dimension_semantics=("arbitrary", "arbitrary"), vmem_limit_bytes=VMEM_LIMIT),
        name="mix_in",
    )(x2, pos_row, inv_freq, rope_pat, ln_mix, w_in, conv_w, conv_b, wa_bd, b_a, wi_bd, b_i, rg_lambda, g_rec)


def _attn_kernel(lq1_ref, lk1_ref, lq2_ref, lk2_ref, gsub_ref, bias_ref, q_ref, k_ref, vt_ref, o_ref,
                 m_ref, acc_ref, aprev_ref, s0_ref, s1_ref, mb0_ref, mb1_ref, p0_ref, p1_ref, *, lam_init):
    tq = ATT_Q
    tk = vt_ref.shape[2]
    assert tq == tk, "the causal bias tile assumes the diagonal block is square"
    lam = (jnp.exp(jnp.sum(lq1_ref[...] * lk1_ref[...], axis=-1, keepdims=True))
           - jnp.exp(jnp.sum(lq2_ref[...] * lk2_ref[...], axis=-1, keepdims=True)) + lam_init)

    def query_tile(i, carry):
        _attn_query_tile(i, lam, gsub_ref, bias_ref, q_ref, k_ref, vt_ref, o_ref, m_ref, acc_ref, aprev_ref,
                         s0_ref, s1_ref, mb0_ref, mb1_ref, p0_ref, p1_ref, lam_init=lam_init)
        return carry

    lax.fori_loop(0, q_ref.shape[0] // tq, query_tile, 0)


def _attn_query_tile(i, lam, gsub_ref, bias_ref, q_ref, k_ref, vt_ref, o_ref, m_ref, acc_ref, aprev_ref,
                     s0_ref, s1_ref, mb0_ref, mb1_ref, p0_ref, p1_ref, *, lam_init):
    tq = ATT_Q
    tk = vt_ref.shape[2]
    q_rows = pl.ds(pl.multiple_of(i * tq, tq), tq)

    qt = q_ref[q_rows, :].astype(F32).T
    dim = lax.broadcasted_iota(I32, (LANES, tq), 0)
    qqt = jnp.concatenate([jnp.where(dim < HALF_DIM, qt, 0.0), jnp.where(dim >= HALF_DIM, qt, 0.0)],
                          axis=1).astype(BF16)

    n = (i * tq) // tk

    def scores(j):
        return _dot(k_ref[pl.ds(pl.multiple_of(j * tk, tk), tk), :], qqt)

    def probabilities(s, m_prev):
        m_new = jnp.maximum(m_prev, jnp.max(s, axis=0, keepdims=True))
        alpha = jnp.exp2(m_prev - m_new)
        p = jnp.exp2((s - m_new).astype(BF16))
        return p, alpha, m_new

    def store_scores(j, s_buf, mb_buf):
        s = scores(j)
        s_buf[...] = s
        mb_buf[...] = jnp.max(s, axis=0, keepdims=True)

    def pipe_step(j, cur, nxt, p_cur, p_prev):
        store_scores(j + 1, *nxt)
        s_cur, mb_cur = cur
        m_prev = m_ref[...]
        m_new = jnp.maximum(m_prev, mb_cur[...])
        p_cur[...] = jnp.exp2((s_cur[...] - m_new).astype(BF16))
        m_ref[...] = m_new
        if p_prev is not None:
            acc_ref[...] = aprev_ref[...] * acc_ref[...] + _dot(vt_ref[jnp.maximum(j - 1, 0)], p_prev[...])
        aprev_ref[...] = jnp.exp2(m_prev - m_new)

    buf0, buf1 = (s0_ref, mb0_ref), (s1_ref, mb1_ref)
    m_ref[...] = jnp.full_like(m_ref, NEG_BIG)
    acc_ref[...] = jnp.zeros_like(acc_ref)
    aprev_ref[...] = jnp.ones_like(aprev_ref)
    odd = n % 2

    @pl.when(odd == 0)
    def _():
        p1_ref[...] = jnp.zeros_like(p1_ref)
        store_scores(0, *buf0)

    @pl.when(odd == 1)
    def _():
        store_scores(0, *buf1)
        pipe_step(0, buf1, buf0, p1_ref, None)

    def pair(t, carry):
        j = 2 * t + odd
        pipe_step(j, buf0, buf1, p0_ref, p1_ref)
        pipe_step(j + 1, buf1, buf0, p1_ref, p0_ref)
        return carry

    lax.fori_loop(0, n // 2, pair, 0)

    parts = []
    for c0 in range(0, 2 * tq, ATT_TAIL_LANES):
        cols = slice(c0, c0 + ATT_TAIL_LANES)
        keys = c0 % tq + ATT_TAIL_LANES
        p, alpha, _ = probabilities(s0_ref[:keys, cols] + bias_ref[:keys, cols], m_ref[:, cols])
        part = aprev_ref[:, cols] * acc_ref[:, cols] + _dot(vt_ref[jnp.maximum(n - 1, 0)], p1_ref[:, cols])
        parts.append(alpha * part + _dot(vt_ref[n][:, :keys], p))
    acc = jnp.concatenate(parts, axis=1)

    o = acc[:V_DIM] / acc[V_DIM:V_DIM + 1]
    o = o[:, :tq] - lam * o[:, tq:]
    o = o * lax.rsqrt(jnp.mean(o * o, axis=0, keepdims=True) + EPS) * gsub_ref[...]
    o_ref[q_rows, :] = (o * (1.0 - lam_init)).T.astype(BF16)


def _attention(q, k, vt, lq1, lk1, lq2, lk2, g_sub_col, batch, seq, lam_init):
    tq = ATT_Q
    nq = seq // tq
    tk = vt.shape[2]
    nk = seq // tk
    vec = lambda b, h: (0, 0)
    per_head = lambda b, h: (b, h)
    visible = (lax.broadcasted_iota(I32, (tk, 2 * tq), 0) <= lax.broadcasted_iota(I32, (tk, 2 * tq), 1) % tq)
    bias = jnp.where(visible, 0.0, NEG_BIG).astype(F32)
    return pl.pallas_call(
        functools.partial(_attn_kernel, lam_init=lam_init),
        grid=(batch, N_HEADS),
        in_specs=[
            pl.BlockSpec((1, HALF_DIM), vec),
            pl.BlockSpec((1, HALF_DIM), vec),
            pl.BlockSpec((1, HALF_DIM), vec),
            pl.BlockSpec((1, HALF_DIM), vec),
            pl.BlockSpec((V_DIM, 1), vec),
            pl.BlockSpec((tk, 2 * tq), vec),
            pl.BlockSpec((seq, LANES), per_head),
            pl.BlockSpec((seq, LANES), per_head),
            pl.BlockSpec((nk, V_EXT, tk), lambda b, h: (b, h, 0)),
        ],
        out_specs=pl.BlockSpec((seq, V_DIM), per_head),
        out_shape=jax.ShapeDtypeStruct((batch * seq, D_ATT), BF16),
        scratch_shapes=[pltpu.VMEM((1, 2 * tq), F32),
                        pltpu.VMEM((V_EXT, 2 * tq), F32), pltpu.VMEM((1, 2 * tq), F32),
                        pltpu.VMEM((tk, 2 * tq), F32), pltpu.VMEM((tk, 2 * tq), F32),
                        pltpu.VMEM((1, 2 * tq), F32), pltpu.VMEM((1, 2 * tq), F32),
                        pltpu.VMEM((tk, 2 * tq), BF16), pltpu.VMEM((tk, 2 * tq), BF16)],
        compiler_params=pltpu.CompilerParams(
            dimension_semantics=("arbitrary", "arbitrary"), vmem_limit_bytes=VMEM_LIMIT),
        name="attention",
    )(lq1, lk1, lq2, lk2, g_sub_col, bias, q, k, vt)


def _sublane_total(x, op):
    return op(x, axis=0, keepdims=True)


def _route_kernel(x_ref, yrec_ref, yatt_ref, wo32_ref, lnmoe_ref, wrt_ref, ebias_ref, tri_ref,
                  h1_ref, hpa_ref, hpb_ref, ek_ref, wk_ref, rk_ref, cnt_ref, carry_ref, wo_ref):
    tm = x_ref.shape[0]
    e_n = N_EXPERTS

    @pl.when(pl.program_id(0) == 0)
    def _():
        wo_ref[...] = wo32_ref[...].astype(BF16)
        carry_ref[...] = jnp.zeros_like(carry_ref)

    h1 = x_ref[...] + _dot(yrec_ref[...], wo_ref[:D_REC, :]) + _dot(yatt_ref[...], wo_ref[D_REC:, :])
    h1_ref[...] = h1
    hn = _rms(h1, lnmoe_ref[...])
    pa, pb = _pack_row(hn)
    hpa_ref[...] = pa
    hpb_ref[...] = pb

    def split(v):
        head = v.astype(BF16)
        return head, (v - head.astype(F32)).astype(BF16)

    nt_dot = lambda a, b: lax.dot_general(a, b, (((1,), (1,)), ((), ())), preferred_element_type=F32)
    (w_head, w_rest), (h_head, h_rest) = split(wrt_ref[...]), split(hn)
    logits = nt_dot(w_head, h_head) + (nt_dot(w_head, h_rest) + nt_dot(w_rest, h_head))
    scores = jax.nn.sigmoid(logits)
    sel = scores + ebias_ref[...]

    sel3 = sel.reshape(N_GROUPS, GROUP_SIZE, tm)
    idx3 = lax.broadcasted_iota(I32, (N_GROUPS, GROUP_SIZE, tm), 1)
    m1 = jnp.max(sel3, axis=1, keepdims=True)
    first = jnp.min(jnp.where(sel3 == m1, idx3, GROUP_SIZE), axis=1, keepdims=True)
    m2 = jnp.max(jnp.where(idx3 == first, -jnp.inf, sel3), axis=1, keepdims=True)
    gscore = (m1 + m2).reshape(N_GROUPS, tm)

    gidx = lax.broadcasted_iota(I32, (N_GROUPS, tm), 0)
    beaten = jnp.zeros((N_GROUPS, tm), I32)
    for g in range(N_GROUPS):
        other = gscore[g:g + 1, :]
        beats = (other > gscore) | ((other == gscore) & (g < gidx))
        beaten = beaten + beats.astype(I32)
    gkeep = beaten < TOPK_GROUPS
    keep = jnp.broadcast_to(gkeep.reshape(N_GROUPS, 1, tm), (N_GROUPS, GROUP_SIZE, tm)).reshape(e_n, tm)
    selm = jnp.where(keep, sel, -jnp.inf)

    eidx = lax.broadcasted_iota(I32, (e_n, tm), 0)
    remaining = selm
    picks, ek, sk = [], [], []
    for _ in range(TOP_K):
        best = jnp.max(remaining, axis=0, keepdims=True)
        first = jnp.min(jnp.where(remaining == best, eidx, e_n), axis=0, keepdims=True)
        pick = eidx == first
        picks.append(pick)
        ek.append(first)
        sk.append(_sublane_total(jnp.where(pick, scores, 0.0), jnp.sum))
        remaining = jnp.where(pick, -jnp.inf, remaining)
    chosen_f = (remaining != selm).astype(F32)
    wsum = functools.reduce(lambda a, b: a + b, sk)
    ek_ref[...] = jnp.concatenate(ek, axis=0)
    wk_ref[...] = (jnp.concatenate(sk, axis=0) * (ROUTE_SCALE / wsum)).T

    prefix = _dot(chosen_f.astype(BF16), tri_ref[...])
    rank = prefix + carry_ref[...]
    carry_new = carry_ref[...] + jnp.sum(chosen_f, axis=1, keepdims=True)
    carry_ref[...] = carry_new
    cnt_ref[...] = carry_new.astype(I32)
    rk = [_sublane_total(jnp.where(pick, rank, 0.0), jnp.sum) for pick in picks]
    rk_ref[...] = jnp.concatenate(rk, axis=0).astype(I32)


def _route(x2, y_rec, y_att, w_out, ln_moe, w_router_t, e_bias_col):
    t = x2.shape[0]
    tm = ROUTE_ROWS
    nt = t // tm
    row_map = lambda i: (i, 0)
    col_map = lambda i: (0, i)
    fixed = lambda i: (0, 0)
    tri = (lax.broadcasted_iota(I32, (tm, tm), 0) < lax.broadcasted_iota(I32, (tm, tm), 1)).astype(BF16)
    out_shapes = (
        jax.ShapeDtypeStruct((t, D_MODEL), F32),
        jax.ShapeDtypeStruct((t, PACK_W), U32),
        jax.ShapeDtypeStruct((t, PACK_W), U32),
        jax.ShapeDtypeStruct((TOP_K, t), I32),
        jax.ShapeDtypeStruct((t, TOP_K), F32),
        jax.ShapeDtypeStruct((TOP_K, t), I32),
        jax.ShapeDtypeStruct((N_EXPERTS, 1), I32),
    )
    return pl.pallas_call(
        _route_kernel,
        grid=(nt,),
        in_specs=[
            pl.BlockSpec((tm, D_MODEL), row_map),
            pl.BlockSpec((tm, D_REC), row_map),
            pl.BlockSpec((tm, D_ATT), row_map),
            pl.BlockSpec((D_REC + D_ATT, D_MODEL), fixed, pipeline_mode=pl.Buffered(1)),
            pl.BlockSpec((1, D_MODEL), fixed),
            pl.BlockSpec((N_EXPERTS, D_MODEL), fixed),
            pl.BlockSpec((N_EXPERTS, 1), fixed),
            pl.BlockSpec((tm, tm), fixed),
        ],
        out_specs=[
            pl.BlockSpec((tm, D_MODEL), row_map),
            pl.BlockSpec((tm, PACK_W), row_map),
            pl.BlockSpec((tm, PACK_W), row_map),
            pl.BlockSpec((TOP_K, tm), col_map),
            pl.BlockSpec((tm, TOP_K), row_map),
            pl.BlockSpec((TOP_K, tm), col_map),
            pl.BlockSpec((N_EXPERTS, 1), fixed),
        ],
        out_shape=out_shapes,
        scratch_shapes=[pltpu.VMEM((N_EXPERTS, 1), F32), pltpu.VMEM((D_REC + D_ATT, D_MODEL), BF16)],
        compiler_params=pltpu.CompilerParams(dimension_semantics=("arbitrary",), vmem_limit_bytes=VMEM_LIMIT),
        name="route",
    )(x2, y_rec, y_att, w_out, ln_moe, w_router_t, e_bias_col, tri)


def _plan_kernel(pad_start_ref, ek_ref, rk_ref, dest_ref):
    ek = ek_ref[...]

    def add_expert(e, base):
        return jnp.where(ek == e, pad_start_ref[e], base)

    dest_ref[...] = rk_ref[...] + lax.fori_loop(0, N_EXPERTS, add_expert, jnp.zeros_like(ek))


def _plan(pad_start, ek, rk):
    kk, t = ek.shape
    tl = min(t, 2048)
    col_map = lambda i, ps: (0, i)
    grid_spec = pltpu.PrefetchScalarGridSpec(
        num_scalar_prefetch=1,
        grid=(t // tl,),
        in_specs=[pl.BlockSpec((kk, tl), col_map), pl.BlockSpec((kk, tl), col_map)],
        out_specs=pl.BlockSpec((kk, tl), col_map),
    )
    return pl.pallas_call(
        _plan_kernel,
        grid_spec=grid_spec,
        out_shape=jax.ShapeDtypeStruct((kk, t), I32),
        compiler_params=pltpu.CompilerParams(dimension_semantics=("arbitrary",)),
        name="plan",
    )(pad_start, ek, rk)


def _sc_mesh():
    return plsc.VectorSubcoreMesh(core_axis_name="core", subcore_axis_name="subcore")


def _sc_dispatch(rows, dest, n_out):
    t, w = rows.shape
    kk = dest.shape[0]

    @pl.kernel(out_type=jax.ShapeDtypeStruct((n_out, w), rows.dtype), mesh=_sc_mesh(), scratch_types=[])
    def kern(x_hbm, i_hbm, o_hbm):
        def body(x_vmem, i_vmem):
            for k in range(kk):
                pltpu.sync_copy(x_vmem, o_hbm.at[i_vmem.at[k]])

        pltpu.emit_pipeline(
            body,
            grid=(t // SC_WINDOW,),
            in_specs=[pl.BlockSpec((SC_WINDOW, w), lambda i: (i, 0)),
                      pl.BlockSpec((kk, SC_WINDOW), lambda i: (0, i))],
            out_specs=[],
            core_axis_name=("core", "subcore"),
            dimension_semantics=(pltpu.PARALLEL,),
        )(x_hbm, i_hbm)

    return kern(rows, dest)


def _sc_combine(rows, dest):
    kk, t = dest.shape
    w = rows.shape[1]
    flat = dest.reshape(1, kk * t)

    @pl.kernel(out_type=jax.ShapeDtypeStruct((kk * t, w), rows.dtype), mesh=_sc_mesh(), scratch_types=[])
    def kern(y_hbm, i_hbm, o_hbm):
        def body(i_vmem, o_vmem):
            pltpu.sync_copy(y_hbm.at[i_vmem.at[0]], o_vmem)

        pltpu.emit_pipeline(
            body,
            grid=(kk * t // SC_WINDOW,),
            in_specs=[pl.BlockSpec((1, SC_WINDOW), lambda i: (0, i))],
            out_specs=[pl.BlockSpec((SC_WINDOW, w), lambda i: (i, 0))],
            core_axis_name=("core", "subcore"),
            dimension_semantics=(pltpu.PARALLEL,),
        )(i_hbm, o_hbm)

    return kern(rows, flat).reshape(kk, t, w)


X_SLOTS = 3
Y_SLOTS = 2


def _experts_kernel(blk_expert_ref, n_used_ref, first_ref, slot_ref, next_ref,
                    xa_hbm, xb_hbm, w1_hbm, w3_hbm, w2_hbm, ya_hbm, yb_hbm,
                    xa_buf, xb_buf, ya_buf, yb_buf, w1f_ref, w3f_ref, w2f_ref, w1b_ref, w3b_ref, w2b_ref,
                    wsem, xsem, ysem):
    m = EXPERT_ROWS
    n_used = n_used_ref[0]

    def weight_copies(e, s):
        return (pltpu.make_async_copy(w1_hbm.at[e], w1f_ref.at[s], wsem.at[s, 0]),
                pltpu.make_async_copy(w3_hbm.at[e], w3f_ref.at[s], wsem.at[s, 1]),
                pltpu.make_async_copy(w2_hbm.at[e], w2f_ref.at[s], wsem.at[s, 2]))

    def x_copies(b):
        rows, s = pl.ds(pl.multiple_of(b * m, m), m), b % X_SLOTS
        return (pltpu.make_async_copy(xa_hbm.at[rows], xa_buf.at[s], xsem.at[s, 0]),
                pltpu.make_async_copy(xb_hbm.at[rows], xb_buf.at[s], xsem.at[s, 1]))

    def y_copies(b):
        rows, s = pl.ds(pl.multiple_of(b * m, m), m), b % Y_SLOTS
        return (pltpu.make_async_copy(ya_buf.at[s], ya_hbm.at[rows], ysem.at[s, 0]),
                pltpu.make_async_copy(yb_buf.at[s], yb_hbm.at[rows], ysem.at[s, 1]))

    def start(copies):
        for copy in copies:
            copy.start()

    def wait(copies):
        for copy in copies:
            copy.wait()

    start(weight_copies(blk_expert_ref[0], 0))
    start(x_copies(0))

    @pl.when(n_used > 1)
    def _():
        start(x_copies(1))

    def block(b, carry):
        @pl.when(b + 2 < n_used)
        def _():
            start(x_copies(b + 2))

        @pl.when(first_ref[b] == 1)
        def _():
            s = slot_ref[b]
            wait(weight_copies(blk_expert_ref[b], s))

            @pl.when(next_ref[b] >= 0)
            def _():
                start(weight_copies(next_ref[b], 1 - s))

            w1b_ref[...] = w1f_ref[s].astype(BF16)
            w3b_ref[...] = w3f_ref[s].astype(BF16)
            w2b_ref[...] = w2f_ref[s].astype(BF16)

        wait(x_copies(b))

        @pl.when(b >= Y_SLOTS)
        def _():
            wait(y_copies(b - Y_SLOTS))

        x = _unpack_row(xa_buf[b % X_SLOTS], xb_buf[b % X_SLOTS]).astype(BF16)
        a = _dot(x, w1b_ref[...])
        g = _dot(x, w3b_ref[...])
        hmid = (jax.nn.silu(a) * g).astype(BF16)
        y = _dot(hmid, w2b_ref[...])
        pa, pb = _pack_row(y)
        ya_buf[b % Y_SLOTS] = pa
        yb_buf[b % Y_SLOTS] = pb
        start(y_copies(b))
        return carry

    lax.fori_loop(0, n_used, block, 0)

    for back in range(Y_SLOTS, 0, -1):
        @pl.when(n_used - back >= 0)
        def _(back=back):
            wait(y_copies(n_used - back))


def _experts(xa, xb, w1, w3, w2, blk_expert, n_used, seg_first, seg_slot, seg_next):
    p = xa.shape[0]
    m = EXPERT_ROWS
    hbm = pl.BlockSpec(memory_space=pl.ANY)
    grid_spec = pltpu.PrefetchScalarGridSpec(
        num_scalar_prefetch=5,
        grid=(1,),
        in_specs=[hbm] * 5,
        out_specs=[hbm, hbm],
        scratch_shapes=[
            pltpu.VMEM((X_SLOTS, m, PACK_W), U32), pltpu.VMEM((X_SLOTS, m, PACK_W), U32),
            pltpu.VMEM((Y_SLOTS, m, PACK_W), U32), pltpu.VMEM((Y_SLOTS, m, PACK_W), U32),
            pltpu.VMEM((2, D_MODEL, D_EXPERT), F32), pltpu.VMEM((2, D_MODEL, D_EXPERT), F32),
            pltpu.VMEM((2, D_EXPERT, D_MODEL), F32),
            pltpu.VMEM((D_MODEL, D_EXPERT), BF16), pltpu.VMEM((D_MODEL, D_EXPERT), BF16),
            pltpu.VMEM((D_EXPERT, D_MODEL), BF16),
            pltpu.SemaphoreType.DMA((2, 3)), pltpu.SemaphoreType.DMA((X_SLOTS, 2)),
            pltpu.SemaphoreType.DMA((Y_SLOTS, 2)),
        ],
    )
    return pl.pallas_call(
        _experts_kernel,
        grid_spec=grid_spec,
        out_shape=(jax.ShapeDtypeStruct((p, PACK_W), U32), jax.ShapeDtypeStruct((p, PACK_W), U32)),
        compiler_params=pltpu.CompilerParams(dimension_semantics=("arbitrary",), vmem_limit_bytes=VMEM_LIMIT),
        name="experts",
    )(blk_expert, n_used, seg_first, seg_slot, seg_next, xa, xb, w1, w3, w2)


def _tail_kernel(h1_ref, ga_ref, gb_ref, wk_ref, p_ref, lnmoe_ref, ws1_32, ws3_32, ws2_32, lnple_ref,
                 wpg_32, wpp_32, lnf_ref, o_ref, ws1_ref, ws3_ref, ws2_ref, wpg_ref, wpp_ref):
    @pl.when(pl.program_id(0) == 0)
    def _():
        for dst, src in ((ws1_ref, ws1_32), (ws3_ref, ws3_32), (ws2_ref, ws2_32), (wpg_ref, wpg_32),
                         (wpp_ref, wpp_32)):
            dst[...] = src[...].astype(BF16)

    h1 = h1_ref[...]
    hn = _rms(h1, lnmoe_ref[...]).astype(BF16)
    shared = _dot((jax.nn.silu(_dot(hn, ws1_ref[...])) * _dot(hn, ws3_ref[...])).astype(BF16), ws2_ref[...])
    wk = wk_ref[...]
    routed = jnp.zeros_like(h1)
    for kk in range(TOP_K):
        routed = routed + wk[:, kk:kk + 1] * _unpack_row(ga_ref[kk], gb_ref[kk])
    h2 = h1 + routed + shared
    gate = jax.nn.sigmoid(_dot(_rms(h2, lnple_ref[...]).astype(BF16), wpg_ref[...]))
    h3 = h2 + gate * _dot(p_ref[...].astype(BF16), wpp_ref[...])
    o_ref[...] = _rms(h3, lnf_ref[...])


def _tail(h1, ga, gb, wk_t, p2, ln_moe, ws1, ws3, ws2, ln_ple, w_pg, w_pp, ln_f):
    t = h1.shape[0]
    tm = TAIL_ROWS
    row_map = lambda i: (i, 0)
    fixed = lambda i: (0, 0)
    g_map = lambda i: (0, i, 0)
    d_sh = ws1.shape[1]
    return pl.pallas_call(
        _tail_kernel,
        grid=(t // tm,),
        in_specs=[
            pl.BlockSpec((tm, D_MODEL), row_map),
            pl.BlockSpec((TOP_K, tm, PACK_W), g_map),
            pl.BlockSpec((TOP_K, tm, PACK_W), g_map),
            pl.BlockSpec((tm, TOP_K), row_map),
            pl.BlockSpec((tm, D_PLE), row_map),
            pl.BlockSpec((1, D_MODEL), fixed),
            pl.BlockSpec((D_MODEL, d_sh), fixed, pipeline_mode=pl.Buffered(1)),
            pl.BlockSpec((D_MODEL, d_sh), fixed, pipeline_mode=pl.Buffered(1)),
            pl.BlockSpec((d_sh, D_MODEL), fixed, pipeline_mode=pl.Buffered(1)),
            pl.BlockSpec((1, D_MODEL), fixed),
            pl.BlockSpec((D_MODEL, D_MODEL), fixed, pipeline_mode=pl.Buffered(1)),
            pl.BlockSpec((D_PLE, D_MODEL), fixed, pipeline_mode=pl.Buffered(1)),
            pl.BlockSpec((1, D_MODEL), fixed),
        ],
        out_specs=pl.BlockSpec((tm, D_MODEL), row_map),
        out_shape=jax.ShapeDtypeStruct((t, D_MODEL), F32),
        scratch_shapes=[pltpu.VMEM((D_MODEL, d_sh), BF16), pltpu.VMEM((D_MODEL, d_sh), BF16),
                        pltpu.VMEM((d_sh, D_MODEL), BF16), pltpu.VMEM((D_MODEL, D_MODEL), BF16),
                        pltpu.VMEM((D_PLE, D_MODEL), BF16)],
        compiler_params=pltpu.CompilerParams(dimension_semantics=("arbitrary",), vmem_limit_bytes=VMEM_LIMIT),
        name="tail",
    )(h1, ga, gb, wk_t, p2, ln_moe, ws1, ws3, ws2, ln_ple, w_pg, w_pp, ln_f)


def _rope_constants():
    half = ROPE_DIM // 2
    inv_freq = (ROPE_THETA ** (-jnp.arange(0, ROPE_DIM, 2, dtype=F32) / ROPE_DIM)).reshape(half, 1)
    f = lax.broadcasted_iota(I32, (ROPE_DIM, LANES), 0)
    l64 = lax.broadcasted_iota(I32, (ROPE_DIM, LANES), 1) % HALF_DIM
    cos_pat = ((f < half) & (l64 < ROPE_DIM) & (l64 % half == f)).astype(F32)
    sa_pat = -((f >= half) & (l64 < half) & (l64 == f - half)).astype(F32)
    sb_pat = ((f >= half) & (l64 >= half) & (l64 < ROPE_DIM) & (l64 - half == f - half)).astype(F32)
    return inv_freq, jnp.concatenate([cos_pat, sa_pat, sb_pat], axis=1)


def _block_diag_tiles(w):
    nb, bd, _ = w.shape
    per = nb // 2
    tiles = []
    for tix in range(2):
        rows = []
        for j in range(per):
            rows.append(jnp.concatenate(
                [w[tix * per + j] if c == j else jnp.zeros((bd, bd), w.dtype) for c in range(per)], axis=1))
        tiles.append(jnp.concatenate(rows, axis=0))
    return jnp.stack(tiles).astype(BF16)


def _layer(h, p_l, positions, lam_init, ln_mix, w_in, conv_w, conv_b, w_a, b_a, w_i, b_i, rg_lambda, g_rec,
           lq1, lk1, lq2, lk2, g_sub, w_out, ln_moe, w_router, e_bias, w1, w3, w2, ws1, ws3, ws2,
           ln_ple, w_ple_gate, w_ple_proj, ln_out):
    batch, seq, _ = h.shape
    t = batch * seq
    x2 = h.reshape(t, D_MODEL)
    row = lambda a: a.reshape(1, -1)
    inv_freq, rope_pat = _rope_constants()

    y_rec, q, k, vt = _mix_in(
        x2, positions.reshape(1, t), inv_freq, rope_pat, row(ln_mix), w_in, conv_w, row(conv_b),
        _block_diag_tiles(w_a), row(b_a), _block_diag_tiles(w_i), row(b_i), row(rg_lambda), row(g_rec),
        batch, seq)
    y_att = _attention(q, k, vt, row(lq1), row(lk1), row(lq2), row(lk2), g_sub.reshape(-1, 1), batch, seq,
                       lam_init)

    h1, hpa, hpb, ek, wk_t, rk, counts = _route(
        x2, y_rec, y_att, w_out, row(ln_moe), w_router.T, e_bias.reshape(-1, 1))

    m = EXPERT_ROWS
    counts = counts.reshape(-1)
    padded = (counts + m - 1) // m * m
    pad_end = jnp.cumsum(padded)
    pad_start = pad_end - padded
    n_rows = t * TOP_K + N_EXPERTS * m
    nblk = n_rows // m
    n_used = (pad_end[-1] // m).astype(I32).reshape(1)
    blk = jnp.arange(nblk, dtype=I32)
    blk_row = jnp.minimum(blk, n_used[0] - 1) * m
    blk_expert = jnp.sum((pad_end[None, :] <= blk_row[:, None]).astype(I32), axis=1)
    prev_expert = jnp.concatenate([jnp.full((1,), -1, I32), blk_expert[:-1]])
    seg_first = ((blk < n_used[0]) & (blk_expert != prev_expert)).astype(I32)
    seg_slot = ((jnp.cumsum(seg_first) - 1) % 2).astype(I32)
    eid = jnp.arange(N_EXPERTS, dtype=I32)
    later = (padded[None, :] > 0) & (eid[None, :] > eid[:, None])
    next_expert = jnp.min(jnp.where(later, eid[None, :], N_EXPERTS), axis=1)
    next_expert = jnp.where(next_expert == N_EXPERTS, -1, next_expert).astype(I32)
    seg_next = jnp.sum(jnp.where(blk_expert[:, None] == eid[None, :], next_expert[None, :], 0), axis=1)
    dest = _plan(pad_start.astype(I32), ek, rk)

    xa = _sc_dispatch(hpa, dest, n_rows)
    xb = _sc_dispatch(hpb, dest, n_rows)
    ya, yb = _experts(xa, xb, w1, w3, w2, blk_expert, n_used, seg_first, seg_slot, seg_next)
    ga = _sc_combine(ya, dest)
    gb = _sc_combine(yb, dest)

    out = _tail(h1, ga, gb, wk_t, p_l.reshape(t, D_PLE), row(ln_moe), ws1, ws3, ws2, row(ln_ple), w_ple_gate,
                w_ple_proj, row(ln_out))
    return out.reshape(batch, seq, D_MODEL)


def kernel(x, p, positions, ln_mix, w_in, conv_w, conv_b, w_a, b_a, w_i, b_i, rg_lambda, g_rec, lq1, lk1, lq2,
           lk2, g_sub, w_out, ln_moe, w_router, e_bias, w1, w3, w2, ws1, ws3, ws2, ln_ple, w_ple_gate,
           w_ple_proj, ln_f):
    depth = w_in.shape[0]
    assert depth == 1, "the fused tail applies the final norm; one layer supported"
    lam_init = 0.8 - 0.6 * math.exp(-0.3 * 0)
    return _layer(x, p[0], positions, lam_init, ln_mix[0], w_in[0], conv_w[0], conv_b[0], w_a[0], b_a[0], w_i[0],
                  b_i[0], rg_lambda[0], g_rec[0], lq1[0], lk1[0], lq2[0], lk2[0], g_sub[0], w_out[0], ln_moe[0],
                  w_router[0], e_bias[0], w1[0], w3[0], w2[0], ws1[0], ws3[0], ws2[0], ln_ple[0], w_ple_gate[0],
                  w_ple_proj[0], ln_f)
```

```python
import functools
import math

import jax
import jax.numpy as jnp
from jax import lax
from jax.experimental import pallas as pl
from jax.experimental.pallas import tpu as pltpu
from jax.experimental.pallas import tpu_sc as plsc

F32 = jnp.float32
BF16 = jnp.bfloat16
U32 = jnp.uint32
I32 = jnp.int32

D_MODEL = 1024
D_REC = 512
REC_BLOCKS = 8
CONV_WIDTH = 4
RG_C = 8.0
N_HEADS = 4
HALF_DIM = 64
V_DIM = 128
D_ATT = N_HEADS * V_DIM
D_QK = N_HEADS * 2 * HALF_DIM
ROPE_DIM = 16
ROPE_THETA = 500000.0
N_EXPERTS = 64
TOP_K = 8
N_GROUPS = 8
GROUP_SIZE = N_EXPERTS // N_GROUPS
TOPK_GROUPS = 4
D_EXPERT = 256
ROUTE_SCALE = 2.5
D_PLE = 256
EPS = 1e-6

LANES = 128
SUBLANES = 8
VMEM_LIMIT = 56 * 1024 * 1024

MIX_ROWS = 512
ATT_Q = 512
ATT_TAIL_LANES = 256
ONES_ROWS = 16
V_EXT = V_DIM + ONES_ROWS
ROUTE_ROWS = 1024
EXPERT_ROWS = 512
TAIL_ROWS = 512
SC_WINDOW = 128
PACK_W = 256
NEG_BIG = -1e30


def _rms(x, g):
    return x * lax.rsqrt(jnp.mean(x * x, axis=-1, keepdims=True) + EPS) * g


def _dot(a, b):
    return jnp.dot(a, b, preferred_element_type=F32)


def _pack_pair(lo, hi):
    lo_bits = lax.bitcast_convert_type(lo.astype(BF16).astype(F32), U32)
    hi_bits = lax.bitcast_convert_type(hi.astype(BF16).astype(F32), U32)
    return (lo_bits >> 16) | (hi_bits & jnp.uint32(0xFFFF0000))


def _unpack_pair(p):
    lo = lax.bitcast_convert_type(p << 16, F32)
    hi = lax.bitcast_convert_type(p & jnp.uint32(0xFFFF0000), F32)
    return lo, hi


def _pack_row(x):
    w = PACK_W
    return _pack_pair(x[:, 0:w], x[:, w:2 * w]), _pack_pair(x[:, 2 * w:3 * w], x[:, 3 * w:4 * w])


def _unpack_row(pa, pb):
    c0, c1 = _unpack_pair(pa)
    c2, c3 = _unpack_pair(pb)
    return jnp.concatenate([c0, c1, c2, c3], axis=1)


def _shift_rows(a, s, fill, row):
    n, c = a.shape
    if s % SUBLANES == 0:
        return jnp.concatenate([jnp.full((s, c), fill, a.dtype), a[:n - s]], axis=0)
    return jnp.where(row >= s, pltpu.roll(a, s, 0), fill)


def _mix_in_kernel(x_ref, pos_ref, invf_ref, pat_ref, lnm_ref, win32_ref, cw_ref, cb_ref, wa_ref, ba_ref,
                   wi_ref, bi_ref, lam_ref, grec_ref,
                   yrec_ref, q_ref, k_ref, vt_ref, tail_ref, hcarry_ref, buf_a, buf_b, win_ref):
    tm = x_ref.shape[0]
    groups = tm // SUBLANES
    chunks = D_REC // LANES

    def stage(ref, v):
        for c in range(chunks):
            ref[c] = v[:, c * LANES:(c + 1) * LANES]

    def slab(ref, r):
        return jnp.concatenate([ref[c, pl.ds(r, groups, stride=SUBLANES), :] for c in range(chunks)], axis=1)

    @pl.when((pl.program_id(0) == 0) & (pl.program_id(1) == 0))
    def _():
        win_ref[...] = win32_ref[...].astype(BF16)

    @pl.when(pl.program_id(1) == 0)
    def _():
        tail_ref[...] = jnp.zeros_like(tail_ref)
        hcarry_ref[...] = jnp.zeros_like(hcarry_ref)

    hn = _rms(x_ref[...], lnm_ref[...]).astype(BF16)

    ang = invf_ref[...] * pos_ref[...].astype(F32)
    cs = jnp.concatenate([jnp.cos(ang), jnp.sin(ang)], axis=0).T
    pat = pat_ref[...].astype(BF16)
    tabs = jnp.zeros((tm, 3 * LANES), F32)
    for _ in range(3):
        piece = cs.astype(BF16)
        tabs = tabs + _dot(piece, pat)
        cs = cs - piece.astype(F32)
    lane64 = lax.broadcasted_iota(I32, (1, LANES), 1) % HALF_DIM
    cosf = tabs[:, 0:LANES] + (lane64 >= ROPE_DIM).astype(F32)
    sa, sb = tabs[:, LANES:2 * LANES], tabs[:, 2 * LANES:3 * LANES]

    def project_rotary(out_ref, off, mul):
        tc, ta, tb = (cosf, sa, sb) if mul == 1.0 else (cosf * mul, sa * mul, sb * mul)
        for c in range(0, D_QK // LANES, 2):
            z2 = _dot(hn, win_ref[:, off + c * LANES: off + (c + 2) * LANES])
            for cc in range(2):
                zc = z2[:, cc * LANES:(cc + 1) * LANES]
                rot = (zc * tc + pltpu.roll(zc, LANES - ROPE_DIM // 2, 1) * ta
                       + pltpu.roll(zc, ROPE_DIM // 2, 1) * tb)
                out_ref[:, (c + cc) * LANES:(c + cc + 1) * LANES] = rot.astype(BF16)

    def project_v():
        vt = _dot(hn, win_ref[:, 2 * D_REC + 2 * D_QK:]).T.astype(BF16)
        for hd in range(N_HEADS):
            vt_ref[0, hd * V_EXT:hd * V_EXT + V_DIM, :] = vt[hd * V_DIM:(hd + 1) * V_DIM]
            vt_ref[0, hd * V_EXT + V_DIM:(hd + 1) * V_EXT, :] = jnp.ones((ONES_ROWS, tm), BF16)

    xr = _dot(hn, win_ref[:, 0:D_REC])
    stage(buf_a, xr)
    stage(buf_b, _dot(hn, win_ref[:, D_REC:2 * D_REC]))
    tail = tail_ref[...]
    tail_ref[...] = xr[tm - SUBLANES:, :]

    grow = lax.broadcasted_iota(I32, (groups, D_REC), 0)

    def down_one(a, first_row):
        return jnp.where(grow == 0, first_row, pltpu.roll(a, 1, 0))

    xs = [slab(buf_a, r) for r in range(SUBLANES)]
    wrapped = {r: down_one(xs[r], tail[r:r + 1, :]) for r in range(SUBLANES - CONV_WIDTH + 1, SUBLANES)}
    xc = []
    for r in range(SUBLANES):
        acc = cb_ref[...] + cw_ref[CONV_WIDTH - 1:CONV_WIDTH, :] * xs[r]
        for d in range(1, CONV_WIDTH):
            prev = xs[r - d] if r >= d else wrapped[r - d + SUBLANES]
            acc = acc + cw_ref[CONV_WIDTH - 1 - d:CONV_WIDTH - d, :] * prev
        xc.append(acc)
    xc = jnp.concatenate(xc, axis=0)

    xcb = xc.astype(BF16)
    half = D_REC // 2
    ra = jnp.concatenate([_dot(xcb[:, :half], wa_ref[0]), _dot(xcb[:, half:], wa_ref[1])], axis=1)
    ri = jnp.concatenate([_dot(xcb[:, :half], wi_ref[0]), _dot(xcb[:, half:], wi_ref[1])], axis=1)
    r_gate = jax.nn.sigmoid(ra + ba_ref[...])
    i_gate = jax.nn.sigmoid(ri + bi_ref[...])
    lam = lam_ref[...]
    softplus_neg = jnp.maximum(-lam, 0.0) + jnp.log(1.0 + jnp.exp(-jnp.abs(lam)))
    log_a = -RG_C * r_gate * softplus_neg
    a = jnp.exp(log_a)
    u = jnp.sqrt(1.0 - jnp.exp(2.0 * log_a)) * i_gate * xc

    rows = lambda v, r: v[r * groups:(r + 1) * groups]
    hs, ps = [rows(u, 0)], [rows(a, 0)]
    for r in range(1, SUBLANES):
        hs.append(rows(a, r) * hs[-1] + rows(u, r))
        ps.append(rows(a, r) * ps[-1])
    tot_a, tot_h = ps[-1], hs[-1]
    s = 1
    while s < groups:
        tot_h = tot_h + tot_a * _shift_rows(tot_h, s, 0.0, grow)
        tot_a = tot_a * _shift_rows(tot_a, s, 1.0, grow)
        s *= 2
    h_in = hcarry_ref[...]
    group_end = tot_h + tot_a * h_in
    hcarry_ref[...] = group_end[groups - 1:groups, :]
    group_in = down_one(group_end, h_in)

    for r in range(SUBLANES):
        h = hs[r] + ps[r] * group_in
        y = h * jax.nn.gelu(slab(buf_b, r))
        yn = _rms(y, grec_ref[...])
        for c in range(chunks):
            buf_a[c, pl.ds(r, groups, stride=SUBLANES), :] = yn[:, c * LANES:(c + 1) * LANES]
    for c in range(chunks):
        yrec_ref[:, c * LANES:(c + 1) * LANES] = buf_a[c].astype(BF16)

    project_rotary(q_ref, 2 * D_REC, HALF_DIM ** -0.5 * math.log2(math.e))
    project_rotary(k_ref, 2 * D_REC + D_QK, 1.0)
    project_v()


def _mix_in(x2, pos_row, inv_freq, rope_pat, ln_mix, w_in, conv_w, conv_b, wa_bd, b_a, wi_bd, b_i, rg_lambda,
            g_rec, batch, seq):
    tm = MIX_ROWS
    nt = seq // tm
    d_in = w_in.shape[1]
    row_map = lambda b, i: (b * nt + i, 0)
    fixed2 = lambda b, i: (0, 0)
    fixed3 = lambda b, i: (0, 0, 0)
    t = batch * seq
    out_shapes = (
        jax.ShapeDtypeStruct((t, D_REC), BF16),
        jax.ShapeDtypeStruct((t, D_QK), BF16),
        jax.ShapeDtypeStruct((t, D_QK), BF16),
        jax.ShapeDtypeStruct((t // tm, N_HEADS * V_EXT, tm), BF16),
    )
    return pl.pallas_call(
        _mix_in_kernel,
        grid=(batch, nt),
        in_specs=[
            pl.BlockSpec((tm, D_MODEL), row_map),
            pl.BlockSpec((1, tm), lambda b, i: (0, b * nt + i)),
            pl.BlockSpec((ROPE_DIM // 2, 1), fixed2),
            pl.BlockSpec((ROPE_DIM, 3 * LANES), fixed2),
            pl.BlockSpec((1, D_MODEL), fixed2),
            pl.BlockSpec((D_MODEL, d_in), fixed2, pipeline_mode=pl.Buffered(1)),
            pl.BlockSpec((CONV_WIDTH, D_REC), fixed2),
            pl.BlockSpec((1, D_REC), fixed2),
            pl.BlockSpec((2, D_REC // 2, D_REC // 2), fixed3),
            pl.BlockSpec((1, D_REC), fixed2),
            pl.BlockSpec((2, D_REC // 2, D_REC // 2), fixed3),
            pl.BlockSpec((1, D_REC), fixed2),
            pl.BlockSpec((1, D_REC), fixed2),
            pl.BlockSpec((1, D_REC), fixed2),
        ],
        out_specs=[
            pl.BlockSpec((tm, D_REC), row_map),
            pl.BlockSpec((tm, D_QK), row_map),
            pl.BlockSpec((tm, D_QK), row_map),
            pl.BlockSpec((1, N_HEADS * V_EXT, tm), lambda b, i: (b * nt + i, 0, 0)),
        ],
        out_shape=out_shapes,
        scratch_shapes=[pltpu.VMEM((SUBLANES, D_REC), F32), pltpu.VMEM((1, D_REC), F32),
                        pltpu.VMEM((D_REC // LANES, tm, LANES), F32),
                        pltpu.VMEM((D_REC // LANES, tm, LANES), F32),
                        pltpu.VMEM((D_MODEL, d_in), BF16)],
        compiler_params=pltpu.CompilerParams(
            dimension_semantics=("arbitrary", "arbitrary"), vmem_limit_bytes=VMEM_LIMIT),
        name="mix_in",
    )(x2, pos_row, inv_freq, rope_pat, ln_mix, w_in, conv_w, conv_b, wa_bd, b_a, wi_bd, b_i, rg_lambda, g_rec)


def _attn_kernel(lq1_ref, lk1_ref, lq2_ref, lk2_ref, gsub_ref, bias_ref, q_ref, k_ref, vt_ref, o_ref,
                 m_ref, acc_ref, aprev_ref, s0_ref, s1_ref, mb0_ref, mb1_ref, p0_ref, p1_ref, *, lam_init):
    tq = ATT_Q
    tk = vt_ref.shape[2]
    assert tq == tk, "the causal bias tile assumes the diagonal block is square"
    lam = (jnp.exp(jnp.sum(lq1_ref[...] * lk1_ref[...], axis=-1, keepdims=True))
           - jnp.exp(jnp.sum(lq2_ref[...] * lk2_ref[...], axis=-1, keepdims=True)) + lam_init)

    def query_tile(i, carry):
        _attn_query_tile(i, lam, gsub_ref, bias_ref, q_ref, k_ref, vt_ref, o_ref, m_ref, acc_ref, aprev_ref,
                         s0_ref, s1_ref, mb0_ref, mb1_ref, p0_ref, p1_ref, lam_init=lam_init)
        return carry

    lax.fori_loop(0, q_ref.shape[0] // tq, query_tile, 0)


def _attn_query_tile(i, lam, gsub_ref, bias_ref, q_ref, k_ref, vt_ref, o_ref, m_ref, acc_ref, aprev_ref,
                     s0_ref, s1_ref, mb0_ref, mb1_ref, p0_ref, p1_ref, *, lam_init):
    tq = ATT_Q
    tk = vt_ref.shape[2]
    q_rows = pl.ds(pl.multiple_of(i * tq, tq), tq)

    qt = q_ref[q_rows, :].astype(F32).T
    dim = lax.broadcasted_iota(I32, (LANES, tq), 0)
    qqt = jnp.concatenate([jnp.where(dim < HALF_DIM, qt, 0.0), jnp.where(dim >= HALF_DIM, qt, 0.0)],
                          axis=1).astype(BF16)

    n = (i * tq) // tk

    def scores(j):
        return _dot(k_ref[pl.ds(pl.multiple_of(j * tk, tk), tk), :], qqt)

    def probabilities(s, m_prev):
        m_new = jnp.maximum(m_prev, jnp.max(s, axis=0, keepdims=True))
        alpha = jnp.exp2(m_prev - m_new)
        p = jnp.exp2((s - m_new).astype(BF16))
        return p, alpha, m_new

    def store_scores(j, s_buf, mb_buf):
        s = scores(j)
        s_buf[...] = s
        mb_buf[...] = jnp.max(s, axis=0, keepdims=True)

    def pipe_step(j, cur, nxt, p_cur, p_prev):
        store_scores(j + 1, *nxt)
        s_cur, mb_cur = cur
        m_prev = m_ref[...]
        m_new = jnp.maximum(m_prev, mb_cur[...])
        p_cur[...] = jnp.exp2((s_cur[...] - m_new).astype(BF16))
        m_ref[...] = m_new
        if p_prev is not None:
            acc_ref[...] = aprev_ref[...] * acc_ref[...] + _dot(vt_ref[jnp.maximum(j - 1, 0)], p_prev[...])
        aprev_ref[...] = jnp.exp2(m_prev - m_new)

    buf0, buf1 = (s0_ref, mb0_ref), (s1_ref, mb1_ref)
    m_ref[...] = jnp.full_like(m_ref, NEG_BIG)
    acc_ref[...] = jnp.zeros_like(acc_ref)
    aprev_ref[...] = jnp.ones_like(aprev_ref)
    odd = n % 2

    @pl.when(odd == 0)
    def _():
        p1_ref[...] = jnp.zeros_like(p1_ref)
        store_scores(0, *buf0)

    @pl.when(odd == 1)
    def _():
        store_scores(0, *buf1)
        pipe_step(0, buf1, buf0, p1_ref, None)

    def pair(t, carry):
        j = 2 * t + odd
        pipe_step(j, buf0, buf1, p0_ref, p1_ref)
        pipe_step(j + 1, buf1, buf0, p1_ref, p0_ref)
        return carry

    lax.fori_loop(0, n // 2, pair, 0)

    parts = []
    for c0 in range(0, 2 * tq, ATT_TAIL_LANES):
        cols = slice(c0, c0 + ATT_TAIL_LANES)
        keys = c0 % tq + ATT_TAIL_LANES
        p, alpha, _ = probabilities(s0_ref[:keys, cols] + bias_ref[:keys, cols], m_ref[:, cols])
        part = aprev_ref[:, cols] * acc_ref[:, cols] + _dot(vt_ref[jnp.maximum(n - 1, 0)], p1_ref[:, cols])
        parts.append(alpha * part + _dot(vt_ref[n][:, :keys], p))
    acc = jnp.concatenate(parts, axis=1)

    o = acc[:V_DIM] / acc[V_DIM:V_DIM + 1]
    o = o[:, :tq] - lam * o[:, tq:]
    o = o * lax.rsqrt(jnp.mean(o * o, axis=0, keepdims=True) + EPS) * gsub_ref[...]
    o_ref[q_rows, :] = (o * (1.0 - lam_init)).T.astype(BF16)


def _attention(q, k, vt, lq1, lk1, lq2, lk2, g_sub_col, batch, seq, lam_init):
    tq = ATT_Q
    nq = seq // tq
    tk = vt.shape[2]
    nk = seq // tk
    vec = lambda b, h: (0, 0)
    per_head = lambda b, h: (b, h)
    visible = (lax.broadcasted_iota(I32, (tk, 2 * tq), 0) <= lax.broadcasted_iota(I32, (tk, 2 * tq), 1) % tq)
    bias = jnp.where(visible, 0.0, NEG_BIG).astype(F32)
    return pl.pallas_call(
        functools.partial(_attn_kernel, lam_init=lam_init),
        grid=(batch, N_HEADS),
        in_specs=[
            pl.BlockSpec((1, HALF_DIM), vec),
            pl.BlockSpec((1, HALF_DIM), vec),
            pl.BlockSpec((1, HALF_DIM), vec),
            pl.BlockSpec((1, HALF_DIM), vec),
            pl.BlockSpec((V_DIM, 1), vec),
            pl.BlockSpec((tk, 2 * tq), vec),
            pl.BlockSpec((seq, LANES), per_head),
            pl.BlockSpec((seq, LANES), per_head),
            pl.BlockSpec((nk, V_EXT, tk), lambda b, h: (b, h, 0)),
        ],
        out_specs=pl.BlockSpec((seq, V_DIM), per_head),
        out_shape=jax.ShapeDtypeStruct((batch * seq, D_ATT), BF16),
        scratch_shapes=[pltpu.VMEM((1, 2 * tq), F32),
                        pltpu.VMEM((V_EXT, 2 * tq), F32), pltpu.VMEM((1, 2 * tq), F32),
                        pltpu.VMEM((tk, 2 * tq), F32), pltpu.VMEM((tk, 2 * tq), F32),
                        pltpu.VMEM((1, 2 * tq), F32), pltpu.VMEM((1, 2 * tq), F32),
                        pltpu.VMEM((tk, 2 * tq), BF16), pltpu.VMEM((tk, 2 * tq), BF16)],
        compiler_params=pltpu.CompilerParams(
            dimension_semantics=("arbitrary", "arbitrary"), vmem_limit_bytes=VMEM_LIMIT),
        name="attention",
    )(lq1, lk1, lq2, lk2, g_sub_col, bias, q, k, vt)


def _sublane_total(x, op):
    return op(x, axis=0, keepdims=True)


def _route_kernel(x_ref, yrec_ref, yatt_ref, wo32_ref, lnmoe_ref, wrt_ref, ebias_ref, tri_ref,
                  h1_ref, hpa_ref, hpb_ref, ek_ref, wk_ref, rk_ref, cnt_ref, carry_ref, wo_ref):
    tm = x_ref.shape[0]
    e_n = N_EXPERTS

    @pl.when(pl.program_id(0) == 0)
    def _():
        wo_ref[...] = wo32_ref[...].astype(BF16)
        carry_ref[...] = jnp.zeros_like(carry_ref)

    h1 = x_ref[...] + _dot(yrec_ref[...], wo_ref[:D_REC, :]) + _dot(yatt_ref[...], wo_ref[D_REC:, :])
    h1_ref[...] = h1
    hn = _rms(h1, lnmoe_ref[...])
    pa, pb = _pack_row(hn)
    hpa_ref[...] = pa
    hpb_ref[...] = pb

    def split(v):
        head = v.astype(BF16)
        return head, (v - head.astype(F32)).astype(BF16)

    nt_dot = lambda a, b: lax.dot_general(a, b, (((1,), (1,)), ((), ())), preferred_element_type=F32)
    (w_head, w_rest), (h_head, h_rest) = split(wrt_ref[...]), split(hn)
    logits = nt_dot(w_head, h_head) + (nt_dot(w_head, h_rest) + nt_dot(w_rest, h_head))
    scores = jax.nn.sigmoid(logits)
    sel = scores + ebias_ref[...]

    sel3 = sel.reshape(N_GROUPS, GROUP_SIZE, tm)
    idx3 = lax.broadcasted_iota(I32, (N_GROUPS, GROUP_SIZE, tm), 1)
    m1 = jnp.max(sel3, axis=1, keepdims=True)
    first = jnp.min(jnp.where(sel3 == m1, idx3, GROUP_SIZE), axis=1, keepdims=True)
    m2 = jnp.max(jnp.where(idx3 == first, -jnp.inf, sel3), axis=1, keepdims=True)
    gscore = (m1 + m2).reshape(N_GROUPS, tm)

    gidx = lax.broadcasted_iota(I32, (N_GROUPS, tm), 0)
    beaten = jnp.zeros((N_GROUPS, tm), I32)
    for g in range(N_GROUPS):
        other = gscore[g:g + 1, :]
        beats = (other > gscore) | ((other == gscore) & (g < gidx))
        beaten = beaten + beats.astype(I32)
    gkeep = beaten < TOPK_GROUPS
    keep = jnp.broadcast_to(gkeep.reshape(N_GROUPS, 1, tm), (N_GROUPS, GROUP_SIZE, tm)).reshape(e_n, tm)
    selm = jnp.where(keep, sel, -jnp.inf)

    eidx = lax.broadcasted_iota(I32, (e_n, tm), 0)
    remaining = selm
    picks, ek, sk = [], [], []
    for _ in range(TOP_K):
        best = jnp.max(remaining, axis=0, keepdims=True)
        first = jnp.min(jnp.where(remaining == best, eidx, e_n), axis=0, keepdims=True)
        pick = eidx == first
        picks.append(pick)
        ek.append(first)
        sk.append(_sublane_total(jnp.where(pick, scores, 0.0), jnp.sum))
        remaining = jnp.where(pick, -jnp.inf, remaining)
    chosen_f = (remaining != selm).astype(F32)
    wsum = functools.reduce(lambda a, b: a + b, sk)
    ek_ref[...] = jnp.concatenate(ek, axis=0)
    wk_ref[...] = (jnp.concatenate(sk, axis=0) * (ROUTE_SCALE / wsum)).T

    prefix = _dot(chosen_f.astype(BF16), tri_ref[...])
    rank = prefix + carry_ref[...]
    carry_new = carry_ref[...] + jnp.sum(chosen_f, axis=1, keepdims=True)
    carry_ref[...] = carry_new
    cnt_ref[...] = carry_new.astype(I32)
    rk = [_sublane_total(jnp.where(pick, rank, 0.0), jnp.sum) for pick in picks]
    rk_ref[...] = jnp.concatenate(rk, axis=0).astype(I32)


def _route(x2, y_rec, y_att, w_out, ln_moe, w_router_t, e_bias_col):
    t = x2.shape[0]
    tm = ROUTE_ROWS
    nt = t // tm
    row_map = lambda i: (i, 0)
    col_map = lambda i: (0, i)
    fixed = lambda i: (0, 0)
    tri = (lax.broadcasted_iota(I32, (tm, tm), 0) < lax.broadcasted_iota(I32, (tm, tm), 1)).astype(BF16)
    out_shapes = (
        jax.ShapeDtypeStruct((t, D_MODEL), F32),
        jax.ShapeDtypeStruct((t, PACK_W), U32),
        jax.ShapeDtypeStruct((t, PACK_W), U32),
        jax.ShapeDtypeStruct((TOP_K, t), I32),
        jax.ShapeDtypeStruct((t, TOP_K), F32),
        jax.ShapeDtypeStruct((TOP_K, t), I32),
        jax.ShapeDtypeStruct((N_EXPERTS, 1), I32),
    )
    return pl.pallas_call(
        _route_kernel,
        grid=(nt,),
        in_specs=[
            pl.BlockSpec((tm, D_MODEL), row_map),
            pl.BlockSpec((tm, D_REC), row_map),
            pl.BlockSpec((tm, D_ATT), row_map),
            pl.BlockSpec((D_REC + D_ATT, D_MODEL), fixed, pipeline_mode=pl.Buffered(1)),
            pl.BlockSpec((1, D_MODEL), fixed),
            pl.BlockSpec((N_EXPERTS, D_MODEL), fixed),
            pl.BlockSpec((N_EXPERTS, 1), fixed),
            pl.BlockSpec((tm, tm), fixed),
        ],
        out_specs=[
            pl.BlockSpec((tm, D_MODEL), row_map),
            pl.BlockSpec((tm, PACK_W), row_map),
            pl.BlockSpec((tm, PACK_W), row_map),
            pl.BlockSpec((TOP_K, tm), col_map),
            pl.BlockSpec((tm, TOP_K), row_map),
            pl.BlockSpec((TOP_K, tm), col_map),
            pl.BlockSpec((N_EXPERTS, 1), fixed),
        ],
        out_shape=out_shapes,
        scratch_shapes=[pltpu.VMEM((N_EXPERTS, 1), F32), pltpu.VMEM((D_REC + D_ATT, D_MODEL), BF16)],
        compiler_params=pltpu.CompilerParams(dimension_semantics=("arbitrary",), vmem_limit_bytes=VMEM_LIMIT),
        name="route",
    )(x2, y_rec, y_att, w_out, ln_moe, w_router_t, e_bias_col, tri)


def _plan_kernel(pad_start_ref, ek_ref, rk_ref, dest_ref):
    ek = ek_ref[...]

    def add_expert(e, base):
        return jnp.where(ek == e, pad_start_ref[e], base)

    dest_ref[...] = rk_ref[...] + lax.fori_loop(0, N_EXPERTS, add_expert, jnp.zeros_like(ek))


def _plan(pad_start, ek, rk):
    kk, t = ek.shape
    tl = min(t, 2048)
    col_map = lambda i, ps: (0, i)
    grid_spec = pltpu.PrefetchScalarGridSpec(
        num_scalar_prefetch=1,
        grid=(t // tl,),
        in_specs=[pl.BlockSpec((kk, tl), col_map), pl.BlockSpec((kk, tl), col_map)],
        out_specs=pl.BlockSpec((kk, tl), col_map),
    )
    return pl.pallas_call(
        _plan_kernel,
        grid_spec=grid_spec,
        out_shape=jax.ShapeDtypeStruct((kk, t), I32),
        compiler_params=pltpu.CompilerParams(dimension_semantics=("arbitrary",)),
        name="plan",
    )(pad_start, ek, rk)


def _sc_mesh():
    return plsc.VectorSubcoreMesh(core_axis_name="core", subcore_axis_name="subcore")


def _sc_dispatch(rows, dest, n_out):
    t, w = rows.shape
    kk = dest.shape[0]

    @pl.kernel(out_type=jax.ShapeDtypeStruct((n_out, w), rows.dtype), mesh=_sc_mesh(), scratch_types=[])
    def kern(x_hbm, i_hbm, o_hbm):
        def body(x_vmem, i_vmem):
            for k in range(kk):
                pltpu.sync_copy(x_vmem, o_hbm.at[i_vmem.at[k]])

        pltpu.emit_pipeline(
            body,
            grid=(t // SC_WINDOW,),
            in_specs=[pl.BlockSpec((SC_WINDOW, w), lambda i: (i, 0)),
                      pl.BlockSpec((kk, SC_WINDOW), lambda i: (0, i))],
            out_specs=[],
            core_axis_name=("core", "subcore"),
            dimension_semantics=(pltpu.PARALLEL,),
        )(x_hbm, i_hbm)

    return kern(rows, dest)


def _sc_combine(rows, dest):
    kk, t = dest.shape
    w = rows.shape[1]
    flat = dest.reshape(1, kk * t)

    @pl.kernel(out_type=jax.ShapeDtypeStruct((kk * t, w), rows.dtype), mesh=_sc_mesh(), scratch_types=[])
    def kern(y_hbm, i_hbm, o_hbm):
        def body(i_vmem, o_vmem):
            pltpu.sync_copy(y_hbm.at[i_vmem.at[0]], o_vmem)

        pltpu.emit_pipeline(
            body,
            grid=(kk * t // SC_WINDOW,),
            in_specs=[pl.BlockSpec((1, SC_WINDOW), lambda i: (0, i))],
            out_specs=[pl.BlockSpec((SC_WINDOW, w), lambda i: (i, 0))],
            core_axis_name=("core", "subcore"),
            dimension_semantics=(pltpu.PARALLEL,),
        )(i_hbm, o_hbm)

    return kern(rows, flat).reshape(kk, t, w)


X_SLOTS = 3
Y_SLOTS = 2


def _experts_kernel(blk_expert_ref, n_used_ref, first_ref, slot_ref, next_ref,
                    xa_hbm, xb_hbm, w1_hbm, w3_hbm, w2_hbm, ya_hbm, yb_hbm,
                    xa_buf, xb_buf, ya_buf, yb_buf, w1f_ref, w3f_ref, w2f_ref, w1b_ref, w3b_ref, w2b_ref,
                    wsem, xsem, ysem):
    m = EXPERT_ROWS
    n_used = n_used_ref[0]

    def weight_copies(e, s):
        return (pltpu.make_async_copy(w1_hbm.at[e], w1f_ref.at[s], wsem.at[s, 0]),
                pltpu.make_async_copy(w3_hbm.at[e], w3f_ref.at[s], wsem.at[s, 1]),
                pltpu.make_async_copy(w2_hbm.at[e], w2f_ref.at[s], wsem.at[s, 2]))

    def x_copies(b):
        rows, s = pl.ds(pl.multiple_of(b * m, m), m), b % X_SLOTS
        return (pltpu.make_async_copy(xa_hbm.at[rows], xa_buf.at[s], xsem.at[s, 0]),
                pltpu.make_async_copy(xb_hbm.at[rows], xb_buf.at[s], xsem.at[s, 1]))

    def y_copies(b):
        rows, s = pl.ds(pl.multiple_of(b * m, m), m), b % Y_SLOTS
        return (pltpu.make_async_copy(ya_buf.at[s], ya_hbm.at[rows], ysem.at[s, 0]),
                pltpu.make_async_copy(yb_buf.at[s], yb_hbm.at[rows], ysem.at[s, 1]))

    def start(copies):
        for copy in copies:
            copy.start()

    def wait(copies):
        for copy in copies:
            copy.wait()

    start(weight_copies(blk_expert_ref[0], 0))
    start(x_copies(0))

    @pl.when(n_used > 1)
    def _():
        start(x_copies(1))

    def block(b, carry):
        @pl.when(b + 2 < n_used)
        def _():
            start(x_copies(b + 2))

        @pl.when(first_ref[b] == 1)
        def _():
            s = slot_ref[b]
            wait(weight_copies(blk_expert_ref[b], s))

            @pl.when(next_ref[b] >= 0)
            def _():
                start(weight_copies(next_ref[b], 1 - s))

            w1b_ref[...] = w1f_ref[s].astype(BF16)
            w3b_ref[...] = w3f_ref[s].astype(BF16)
            w2b_ref[...] = w2f_ref[s].astype(BF16)

        wait(x_copies(b))

        @pl.when(b >= Y_SLOTS)
        def _():
            wait(y_copies(b - Y_SLOTS))

        x = _unpack_row(xa_buf[b % X_SLOTS], xb_buf[b % X_SLOTS]).astype(BF16)
        a = _dot(x, w1b_ref[...])
        g = _dot(x, w3b_ref[...])
        hmid = (jax.nn.silu(a) * g).astype(BF16)
        y = _dot(hmid, w2b_ref[...])
        pa, pb = _pack_row(y)
        ya_buf[b % Y_SLOTS] = pa
        yb_buf[b % Y_SLOTS] = pb
        start(y_copies(b))
        return carry

    lax.fori_loop(0, n_used, block, 0)

    for back in range(Y_SLOTS, 0, -1):
        @pl.when(n_used - back >= 0)
        def _(back=back):
            wait(y_copies(n_used - back))


def _experts(xa, xb, w1, w3, w2, blk_expert, n_used, seg_first, seg_slot, seg_next):
    p = xa.shape[0]
    m = EXPERT_ROWS
    hbm = pl.BlockSpec(memory_space=pl.ANY)
    grid_spec = pltpu.PrefetchScalarGridSpec(
        num_scalar_prefetch=5,
        grid=(1,),
        in_specs=[hbm] * 5,
        out_specs=[hbm, hbm],
        scratch_shapes=[
            pltpu.VMEM((X_SLOTS, m, PACK_W), U32), pltpu.VMEM((X_SLOTS, m, PACK_W), U32),
            pltpu.VMEM((Y_SLOTS, m, PACK_W), U32), pltpu.VMEM((Y_SLOTS, m, PACK_W), U32),
            pltpu.VMEM((2, D_MODEL, D_EXPERT), F32), pltpu.VMEM((2, D_MODEL, D_EXPERT), F32),
            pltpu.VMEM((2, D_EXPERT, D_MODEL), F32),
            pltpu.VMEM((D_MODEL, D_EXPERT), BF16), pltpu.VMEM((D_MODEL, D_EXPERT), BF16),
            pltpu.VMEM((D_EXPERT, D_MODEL), BF16),
            pltpu.SemaphoreType.DMA((2, 3)), pltpu.SemaphoreType.DMA((X_SLOTS, 2)),
            pltpu.SemaphoreType.DMA((Y_SLOTS, 2)),
        ],
    )
    return pl.pallas_call(
        _experts_kernel,
        grid_spec=grid_spec,
        out_shape=(jax.ShapeDtypeStruct((p, PACK_W), U32), jax.ShapeDtypeStruct((p, PACK_W), U32)),
        compiler_params=pltpu.CompilerParams(dimension_semantics=("arbitrary",), vmem_limit_bytes=VMEM_LIMIT),
        name="experts",
    )(blk_expert, n_used, seg_first, seg_slot, seg_next, xa, xb, w1, w3, w2)


def _tail_kernel(h1_ref, ga_ref, gb_ref, wk_ref, p_ref, lnmoe_ref, ws1_32, ws3_32, ws2_32, lnple_ref,
                 wpg_32, wpp_32, lnf_ref, o_ref, ws1_ref, ws3_ref, ws2_ref, wpg_ref, wpp_ref):
    @pl.when(pl.program_id(0) == 0)
    def _():
        for dst, src in ((ws1_ref, ws1_32), (ws3_ref, ws3_32), (ws2_ref, ws2_32), (wpg_ref, wpg_32),
                         (wpp_ref, wpp_32)):
            dst[...] = src[...].astype(BF16)

    h1 = h1_ref[...]
    hn = _rms(h1, lnmoe_ref[...]).astype(BF16)
    shared = _dot((jax.nn.silu(_dot(hn, ws1_ref[...])) * _dot(hn, ws3_ref[...])).astype(BF16), ws2_ref[...])
    wk = wk_ref[...]
    routed = jnp.zeros_like(h1)
    for kk in range(TOP_K):
        routed = routed + wk[:, kk:kk + 1] * _unpack_row(ga_ref[kk], gb_ref[kk])
    h2 = h1 + routed + shared
    gate = jax.nn.sigmoid(_dot(_rms(h2, lnple_ref[...]).astype(BF16), wpg_ref[...]))
    h3 = h2 + gate * _dot(p_ref[...].astype(BF16), wpp_ref[...])
    o_ref[...] = _rms(h3, lnf_ref[...])


def _tail(h1, ga, gb, wk_t, p2, ln_moe, ws1, ws3, ws2, ln_ple, w_pg, w_pp, ln_f):
    t = h1.shape[0]
    tm = TAIL_ROWS
    row_map = lambda i: (i, 0)
    fixed = lambda i: (0, 0)
    g_map = lambda i: (0, i, 0)
    d_sh = ws1.shape[1]
    return pl.pallas_call(
        _tail_kernel,
        grid=(t // tm,),
        in_specs=[
            pl.BlockSpec((tm, D_MODEL), row_map),
            pl.BlockSpec((TOP_K, tm, PACK_W), g_map),
            pl.BlockSpec((TOP_K, tm, PACK_W), g_map),
            pl.BlockSpec((tm, TOP_K), row_map),
            pl.BlockSpec((tm, D_PLE), row_map),
            pl.BlockSpec((1, D_MODEL), fixed),
            pl.BlockSpec((D_MODEL, d_sh), fixed, pipeline_mode=pl.Buffered(1)),
            pl.BlockSpec((D_MODEL, d_sh), fixed, pipeline_mode=pl.Buffered(1)),
            pl.BlockSpec((d_sh, D_MODEL), fixed, pipeline_mode=pl.Buffered(1)),
            pl.BlockSpec((1, D_MODEL), fixed),
            pl.BlockSpec((D_MODEL, D_MODEL), fixed, pipeline_mode=pl.Buffered(1)),
            pl.BlockSpec((D_PLE, D_MODEL), fixed, pipeline_mode=pl.Buffered(1)),
            pl.BlockSpec((1, D_MODEL), fixed),
        ],
        out_specs=pl.BlockSpec((tm, D_MODEL), row_map),
        out_shape=jax.ShapeDtypeStruct((t, D_MODEL), F32),
        scratch_shapes=[pltpu.VMEM((D_MODEL, d_sh), BF16), pltpu.VMEM((D_MODEL, d_sh), BF16),
                        pltpu.VMEM((d_sh, D_MODEL), BF16), pltpu.VMEM((D_MODEL, D_MODEL), BF16),
                        pltpu.VMEM((D_PLE, D_MODEL), BF16)],
        compiler_params=pltpu.CompilerParams(dimension_semantics=("arbitrary",), vmem_limit_bytes=VMEM_LIMIT),
        name="tail",
    )(h1, ga, gb, wk_t, p2, ln_moe, ws1, ws3, ws2, ln_ple, w_pg, w_pp, ln_f)


def _rope_constants():
    half = ROPE_DIM // 2
    inv_freq = (ROPE_THETA ** (-jnp.arange(0, ROPE_DIM, 2, dtype=F32) / ROPE_DIM)).reshape(half, 1)
    f = lax.broadcasted_iota(I32, (ROPE_DIM, LANES), 0)
    l64 = lax.broadcasted_iota(I32, (ROPE_DIM, LANES), 1) % HALF_DIM
    cos_pat = ((f < half) & (l64 < ROPE_DIM) & (l64 % half == f)).astype(F32)
    sa_pat = -((f >= half) & (l64 < half) & (l64 == f - half)).astype(F32)
    sb_pat = ((f >= half) & (l64 >= half) & (l64 < ROPE_DIM) & (l64 - half == f - half)).astype(F32)
    return inv_freq, jnp.concatenate([cos_pat, sa_pat, sb_pat], axis=1)


def _block_diag_tiles(w):
    nb, bd, _ = w.shape
    per = nb // 2
    tiles = []
    for tix in range(2):
        rows = []
        for j in range(per):
            rows.append(jnp.concatenate(
                [w[tix * per + j] if c == j else jnp.zeros((bd, bd), w.dtype) for c in range(per)], axis=1))
        tiles.append(jnp.concatenate(rows, axis=0))
    return jnp.stack(tiles).astype(BF16)


def _layer(h, p_l, positions, lam_init, ln_mix, w_in, conv_w, conv_b, w_a, b_a, w_i, b_i, rg_lambda, g_rec,
           lq1, lk1, lq2, lk2, g_sub, w_out, ln_moe, w_router, e_bias, w1, w3, w2, ws1, ws3, ws2,
           ln_ple, w_ple_gate, w_ple_proj, ln_out):
    batch, seq, _ = h.shape
    t = batch * seq
    x2 = h.reshape(t, D_MODEL)
    row = lambda a: a.reshape(1, -1)
    inv_freq, rope_pat = _rope_constants()

    y_rec, q, k, vt = _mix_in(
        x2, positions.reshape(1, t), inv_freq, rope_pat, row(ln_mix), w_in, conv_w, row(conv_b),
        _block_diag_tiles(w_a), row(b_a), _block_diag_tiles(w_i), row(b_i), row(rg_lambda), row(g_rec),
        batch, seq)
    y_att = _attention(q, k, vt, row(lq1), row(lk1), row(lq2), row(lk2), g_sub.reshape(-1, 1), batch, seq,
                       lam_init)

    h1, hpa, hpb, ek, wk_t, rk, counts = _route(
        x2, y_rec, y_att, w_out, row(ln_moe), w_router.T, e_bias.reshape(-1, 1))

    m = EXPERT_ROWS
    counts = counts.reshape(-1)
    padded = (counts + m - 1) // m * m
    pad_end = jnp.cumsum(padded)
    pad_start = pad_end - padded
    n_rows = t * TOP_K + N_EXPERTS * m
    nblk = n_rows // m
    n_used = (pad_end[-1] // m).astype(I32).reshape(1)
    blk = jnp.arange(nblk, dtype=I32)
    blk_row = jnp.minimum(blk, n_used[0] - 1) * m
    blk_expert = jnp.sum((pad_end[None, :] <= blk_row[:, None]).astype(I32), axis=1)
    prev_expert = jnp.concatenate([jnp.full((1,), -1, I32), blk_expert[:-1]])
    seg_first = ((blk < n_used[0]) & (blk_expert != prev_expert)).astype(I32)
    seg_slot = ((jnp.cumsum(seg_first) - 1) % 2).astype(I32)
    eid = jnp.arange(N_EXPERTS, dtype=I32)
    later = (padded[None, :] > 0) & (eid[None, :] > eid[:, None])
    next_expert = jnp.min(jnp.where(later, eid[None, :], N_EXPERTS), axis=1)
    next_expert = jnp.where(next_expert == N_EXPERTS, -1, next_expert).astype(I32)
    seg_next = jnp.sum(jnp.where(blk_expert[:, None] == eid[None, :], next_expert[None, :], 0), axis=1)
    dest = _plan(pad_start.astype(I32), ek, rk)

    xa = _sc_dispatch(hpa, dest, n_rows)
    xb = _sc_dispatch(hpb, dest, n_rows)
    ya, yb = _experts(xa, xb, w1, w3, w2, blk_expert, n_used, seg_first, seg_slot, seg_next)
    ga = _sc_combine(ya, dest)
    gb = _sc_combine(yb, dest)

    out = _tail(h1, ga, gb, wk_t, p_l.reshape(t, D_PLE), row(ln_moe), ws1, ws3, ws2, row(ln_ple), w_ple_gate,
                w_ple_proj, row(ln_out))
    return out.reshape(batch, seq, D_MODEL)


def kernel(x, p, positions, ln_mix, w_in, conv_w, conv_b, w_a, b_a, w_i, b_i, rg_lambda, g_rec, lq1, lk1, lq2,
           lk2, g_sub, w_out, ln_moe, w_router, e_bias, w1, w3, w2, ws1, ws3, ws2, ln_ple, w_ple_gate,
           w_ple_proj, ln_f):
    depth = w_in.shape[0]
    assert depth == 1, "the fused tail applies the final norm; one layer supported"
    lam_init = 0.8 - 0.6 * math.exp(-0.3 * 0)
    return _layer(x, p[0], positions, lam_init, ln_mix[0], w_in[0], conv_w[0], conv_b[0], w_a[0], b_a[0], w_i[0],
                  b_i[0], rg_lambda[0], g_rec[0], lq1[0], lk1[0], lq2[0], lk2[0], g_sub[0], w_out[0], ln_moe[0],
                  w_router[0], e_bias[0], w1[0], w3[0], w2[0], ws1[0], ws3[0], ws2[0], ln_ple[0], w_ple_gate[0],
                  w_ple_proj[0], ln_f)
```

```python
import functools
import math

import jax
import jax.numpy as jnp
from jax import lax
from jax.experimental import pallas as pl
from jax.experimental.pallas import tpu as pltpu
from jax.experimental.pallas import tpu_sc as plsc

F32 = jnp.float32
BF16 = jnp.bfloat16
U32 = jnp.uint32
I32 = jnp.int32

D_MODEL = 1024
D_REC = 512
CONV_WIDTH = 4
RG_C = 8.0
N_HEADS = 4
HALF_DIM = 64
V_DIM = 128
D_ATT = N_HEADS * V_DIM
D_QK = N_HEADS * 2 * HALF_DIM
ROPE_DIM = 16
ROPE_THETA = 500000.0
N_EXPERTS = 64
TOP_K = 8
N_GROUPS = 8
GROUP_SIZE = N_EXPERTS // N_GROUPS
TOPK_GROUPS = 4
D_EXPERT = 256
ROUTE_SCALE = 2.5
D_PLE = 256
EPS = 1e-6

LANES = 128
SUBLANES = 8
VMEM_LIMIT = 56 * 1024 * 1024

MIX_ROWS = 512
ATT_Q = 512
ATT_LANES = 256
ONES_ROWS = 16
V_EXT = V_DIM + ONES_ROWS
ROUTE_ROWS = 1024
EXPERT_ROWS = 512
TAIL_ROWS = 512
SC_WINDOW = 128
PACK_W = 256
NEG_BIG = -1e30


def _rms(x, g):
    return x * lax.rsqrt(jnp.mean(x * x, axis=-1, keepdims=True) + EPS) * g


def _dot(a, b):
    return jnp.dot(a, b, preferred_element_type=F32)


def _pack_pair(lo, hi):
    lo_bits = lax.bitcast_convert_type(lo.astype(BF16).astype(F32), U32)
    hi_bits = lax.bitcast_convert_type(hi.astype(BF16).astype(F32), U32)
    return (lo_bits >> 16) | (hi_bits & jnp.uint32(0xFFFF0000))


def _unpack_pair(p):
    lo = lax.bitcast_convert_type(p << 16, F32)
    hi = lax.bitcast_convert_type(p & jnp.uint32(0xFFFF0000), F32)
    return lo, hi


def _pack_row(x):
    w = PACK_W
    return _pack_pair(x[:, 0:w], x[:, w:2 * w]), _pack_pair(x[:, 2 * w:3 * w], x[:, 3 * w:4 * w])


def _unpack_row(pa, pb):
    c0, c1 = _unpack_pair(pa)
    c2, c3 = _unpack_pair(pb)
    return jnp.concatenate([c0, c1, c2, c3], axis=1)


def _shift_rows(a, s, fill, row):
    n, c = a.shape
    if s % SUBLANES == 0:
        return jnp.concatenate([jnp.full((s, c), fill, a.dtype), a[:n - s]], axis=0)
    return jnp.where(row >= s, pltpu.roll(a, s, 0), fill)


def _mix_in_kernel(x_ref, pos_ref, invf_ref, pat_ref, lnm_ref, win32_ref, cw_ref, cb_ref, wa_ref, ba_ref,
                   wi_ref, bi_ref, lam_ref, grec_ref,
                   yrec_ref, q_ref, k_ref, vt_ref, tail_ref, hcarry_ref, buf_a, buf_b, win_ref):
    tm = x_ref.shape[0]
    groups = tm // SUBLANES
    chunks = D_REC // LANES

    def stage(ref, v):
        for c in range(chunks):
            ref[c] = v[:, c * LANES:(c + 1) * LANES]

    def slab(ref, r):
        return jnp.concatenate([ref[c, pl.ds(r, groups, stride=SUBLANES), :] for c in range(chunks)], axis=1)

    @pl.when((pl.program_id(0) == 0) & (pl.program_id(1) == 0))
    def _():
        win_ref[...] = win32_ref[...].astype(BF16)

    @pl.when(pl.program_id(1) == 0)
    def _():
        tail_ref[...] = jnp.zeros_like(tail_ref)
        hcarry_ref[...] = jnp.zeros_like(hcarry_ref)

    hn = _rms(x_ref[...], lnm_ref[...]).astype(BF16)

    ang = invf_ref[...] * pos_ref[...].astype(F32)
    cs = jnp.concatenate([jnp.cos(ang), jnp.sin(ang)], axis=0).T
    pat = pat_ref[...].astype(BF16)
    tabs = jnp.zeros((tm, 3 * LANES), F32)
    for _ in range(3):
        piece = cs.astype(BF16)
        tabs = tabs + _dot(piece, pat)
        cs = cs - piece.astype(F32)
    lane64 = lax.broadcasted_iota(I32, (1, LANES), 1) % HALF_DIM
    cosf = tabs[:, 0:LANES] + (lane64 >= ROPE_DIM).astype(F32)
    sa, sb = tabs[:, LANES:2 * LANES], tabs[:, 2 * LANES:3 * LANES]

    def project_rotary(out_ref, off, mul):
        tc, ta, tb = (cosf, sa, sb) if mul == 1.0 else (cosf * mul, sa * mul, sb * mul)
        for c in range(0, D_QK // LANES, 2):
            z2 = _dot(hn, win_ref[:, off + c * LANES: off + (c + 2) * LANES])
            for cc in range(2):
                zc = z2[:, cc * LANES:(cc + 1) * LANES]
                rot = (zc * tc + pltpu.roll(zc, LANES - ROPE_DIM // 2, 1) * ta
                       + pltpu.roll(zc, ROPE_DIM // 2, 1) * tb)
                out_ref[:, (c + cc) * LANES:(c + cc + 1) * LANES] = rot.astype(BF16)

    def project_v():
        vt = _dot(hn, win_ref[:, 2 * D_REC + 2 * D_QK:]).T.astype(BF16)
        for hd in range(N_HEADS):
            vt_ref[0, hd * V_EXT:hd * V_EXT + V_DIM, :] = vt[hd * V_DIM:(hd + 1) * V_DIM]
            vt_ref[0, hd * V_EXT + V_DIM:(hd + 1) * V_EXT, :] = jnp.ones((ONES_ROWS, tm), BF16)

    xr = _dot(hn, win_ref[:, 0:D_REC])
    stage(buf_a, xr)
    stage(buf_b, _dot(hn, win_ref[:, D_REC:2 * D_REC]))
    tail = tail_ref[...]
    tail_ref[...] = xr[tm - SUBLANES:, :]

    grow = lax.broadcasted_iota(I32, (groups, D_REC), 0)

    def down_one(a, first_row):
        return jnp.where(grow == 0, first_row, pltpu.roll(a, 1, 0))

    xs = [slab(buf_a, r) for r in range(SUBLANES)]
    wrapped = {r: down_one(xs[r], tail[r:r + 1, :]) for r in range(SUBLANES - CONV_WIDTH + 1, SUBLANES)}
    xc = []
    for r in range(SUBLANES):
        acc = cb_ref[...] + cw_ref[CONV_WIDTH - 1:CONV_WIDTH, :] * xs[r]
        for d in range(1, CONV_WIDTH):
            prev = xs[r - d] if r >= d else wrapped[r - d + SUBLANES]
            acc = acc + cw_ref[CONV_WIDTH - 1 - d:CONV_WIDTH - d, :] * prev
        xc.append(acc)
    xc = jnp.concatenate(xc, axis=0)

    xcb = xc.astype(BF16)
    half = D_REC // 2
    ra = jnp.concatenate([_dot(xcb[:, :half], wa_ref[0]), _dot(xcb[:, half:], wa_ref[1])], axis=1)
    ri = jnp.concatenate([_dot(xcb[:, :half], wi_ref[0]), _dot(xcb[:, half:], wi_ref[1])], axis=1)
    r_gate = jax.nn.sigmoid(ra + ba_ref[...])
    i_gate = jax.nn.sigmoid(ri + bi_ref[...])
    lam = lam_ref[...]
    softplus_neg = jnp.maximum(-lam, 0.0) + jnp.log(1.0 + jnp.exp(-jnp.abs(lam)))
    log_a = -RG_C * r_gate * softplus_neg
    a = jnp.exp(log_a)
    u = jnp.sqrt(1.0 - jnp.exp(2.0 * log_a)) * i_gate * xc

    rows = lambda v, r: v[r * groups:(r + 1) * groups]
    hs, ps = [rows(u, 0)], [rows(a, 0)]
    for r in range(1, SUBLANES):
        hs.append(rows(a, r) * hs[-1] + rows(u, r))
        ps.append(rows(a, r) * ps[-1])
    tot_a, tot_h = ps[-1], hs[-1]
    s = 1
    while s < groups:
        tot_h = tot_h + tot_a * _shift_rows(tot_h, s, 0.0, grow)
        tot_a = tot_a * _shift_rows(tot_a, s, 1.0, grow)
        s *= 2
    h_in = hcarry_ref[...]
    group_end = tot_h + tot_a * h_in
    hcarry_ref[...] = group_end[groups - 1:groups, :]
    group_in = down_one(group_end, h_in)

    for r in range(SUBLANES):
        h = hs[r] + ps[r] * group_in
        y = h * jax.nn.gelu(slab(buf_b, r))
        yn = _rms(y, grec_ref[...])
        for c in range(chunks):
            buf_a[c, pl.ds(r, groups, stride=SUBLANES), :] = yn[:, c * LANES:(c + 1) * LANES]
    for c in range(chunks):
        yrec_ref[:, c * LANES:(c + 1) * LANES] = buf_a[c].astype(BF16)

    project_rotary(q_ref, 2 * D_REC, HALF_DIM ** -0.5 * math.log2(math.e))
    project_rotary(k_ref, 2 * D_REC + D_QK, 1.0)
    project_v()


def _mix_in(x2, pos_row, inv_freq, rope_pat, ln_mix, w_in, conv_w, conv_b, wa_bd, b_a, wi_bd, b_i, rg_lambda,
            g_rec, batch, seq):
    tm = MIX_ROWS
    nt = seq // tm
    d_in = w_in.shape[1]
    row_map = lambda b, i: (b * nt + i, 0)
    fixed2 = lambda b, i: (0, 0)
    fixed3 = lambda b, i: (0, 0, 0)
    t = batch * seq
    out_shapes = (
        jax.ShapeDtypeStruct((t, D_REC), BF16),
        jax.ShapeDtypeStruct((t, D_QK), BF16),
        jax.ShapeDtypeStruct((t, D_QK), BF16),
        jax.ShapeDtypeStruct((t // tm, N_HEADS * V_EXT, tm), BF16),
    )
    return pl.pallas_call(
        _mix_in_kernel,
        grid=(batch, nt),
        in_specs=[
            pl.BlockSpec((tm, D_MODEL), row_map),
            pl.BlockSpec((1, tm), lambda b, i: (0, b * nt + i)),
            pl.BlockSpec((ROPE_DIM // 2, 1), fixed2),
            pl.BlockSpec((ROPE_DIM, 3 * LANES), fixed2),
            pl.BlockSpec((1, D_MODEL), fixed2),
            pl.BlockSpec((D_MODEL, d_in), fixed2, pipeline_mode=pl.Buffered(1)),
            pl.BlockSpec((CONV_WIDTH, D_REC), fixed2),
            pl.BlockSpec((1, D_REC), fixed2),
            pl.BlockSpec((2, D_REC // 2, D_REC // 2), fixed3),
            pl.BlockSpec((1, D_REC), fixed2),
            pl.BlockSpec((2, D_REC // 2, D_REC // 2), fixed3),
            pl.BlockSpec((1, D_REC), fixed2),
            pl.BlockSpec((1, D_REC), fixed2),
            pl.BlockSpec((1, D_REC), fixed2),
        ],
        out_specs=[
            pl.BlockSpec((tm, D_REC), row_map),
            pl.BlockSpec((tm, D_QK), row_map),
            pl.BlockSpec((tm, D_QK), row_map),
            pl.BlockSpec((1, N_HEADS * V_EXT, tm), lambda b, i: (b * nt + i, 0, 0)),
        ],
        out_shape=out_shapes,
        scratch_shapes=[pltpu.VMEM((SUBLANES, D_REC), F32), pltpu.VMEM((1, D_REC), F32),
                        pltpu.VMEM((D_REC // LANES, tm, LANES), F32),
                        pltpu.VMEM((D_REC // LANES, tm, LANES), F32),
                        pltpu.VMEM((D_MODEL, d_in), BF16)],
        compiler_params=pltpu.CompilerParams(
            dimension_semantics=("arbitrary", "arbitrary"), vmem_limit_bytes=VMEM_LIMIT),
        name="mix_in",
    )(x2, pos_row, inv_freq, rope_pat, ln_mix, w_in, conv_w, conv_b, wa_bd, b_a, wi_bd, b_i, rg_lambda, g_rec)


def _attn_kernel(lq1_ref, lk1_ref, lq2_ref, lk2_ref, gsub_ref, bias_ref, q_ref, k_ref, vt_ref, o_ref,
                 m_ref, acc_ref, s0_ref, s1_ref, mb0_ref, mb1_ref, *, lam_init):
    tq = ATT_Q
    tk = vt_ref.shape[2]
    assert tq == tk, "the causal bias tile assumes the diagonal block is square"
    lam = (jnp.exp(jnp.sum(lq1_ref[...] * lk1_ref[...], axis=-1, keepdims=True))
           - jnp.exp(jnp.sum(lq2_ref[...] * lk2_ref[...], axis=-1, keepdims=True)) + lam_init)

    def query_tile(i, carry):
        _attn_query_tile(i, lam, gsub_ref, bias_ref, q_ref, k_ref, vt_ref, o_ref, m_ref, acc_ref,
                         s0_ref, s1_ref, mb0_ref, mb1_ref, lam_init=lam_init)
        return carry

    lax.fori_loop(0, q_ref.shape[0] // tq, query_tile, 0)


def _attn_query_tile(i, lam, gsub_ref, bias_ref, q_ref, k_ref, vt_ref, o_ref, m_ref, acc_ref,
                     s0_ref, s1_ref, mb0_ref, mb1_ref, *, lam_init):
    tq = ATT_Q
    tk = vt_ref.shape[2]
    q_rows = pl.ds(pl.multiple_of(i * tq, tq), tq)

    qt = q_ref[q_rows, :].astype(F32).T
    dim = lax.broadcasted_iota(I32, (LANES, tq), 0)
    qqt = jnp.concatenate([jnp.where(dim < HALF_DIM, qt, 0.0), jnp.where(dim >= HALF_DIM, qt, 0.0)],
                          axis=1).astype(BF16)

    n = (i * tq) // tk

    def scores(j):
        return _dot(k_ref[pl.ds(pl.multiple_of(j * tk, tk), tk), :], qqt)

    chunks = [slice(c0, c0 + ATT_LANES) for c0 in range(0, 2 * tq, ATT_LANES)]

    def store_scores(j, s_buf, mb_buf):
        s = scores(j)
        s_buf[...] = s
        mb_buf[...] = jnp.max(s, axis=0, keepdims=True)

    def accumulate(cols, s, block_max, vt):
        m_prev = m_ref[:, cols]
        m_new = jnp.maximum(m_prev, block_max)
        p = jnp.exp2((s - m_new).astype(BF16))
        acc_ref[:, cols] = jnp.exp2(m_prev - m_new) * acc_ref[:, cols] + _dot(vt, p)
        m_ref[:, cols] = m_new

    def pipe_step(j, cur, nxt):
        store_scores(j + 1, *nxt)
        s_cur, mb_cur = cur
        for cols in chunks:
            accumulate(cols, s_cur[:, cols], mb_cur[:, cols], vt_ref[j])

    buf0, buf1 = (s0_ref, mb0_ref), (s1_ref, mb1_ref)
    m_ref[...] = jnp.full_like(m_ref, NEG_BIG)
    acc_ref[...] = jnp.zeros_like(acc_ref)
    odd = n % 2

    @pl.when(odd == 0)
    def _():
        store_scores(0, *buf0)

    @pl.when(odd == 1)
    def _():
        store_scores(0, *buf1)
        pipe_step(0, buf1, buf0)

    def pair(t, carry):
        j = 2 * t + odd
        pipe_step(j, buf0, buf1)
        pipe_step(j + 1, buf1, buf0)
        return carry

    lax.fori_loop(0, n // 2, pair, 0)

    for cols in chunks:
        keys = cols.start % tq + ATT_LANES
        s = s0_ref[:keys, cols] + bias_ref[:keys, cols]
        accumulate(cols, s, jnp.max(s, axis=0, keepdims=True), vt_ref[n][:, :keys])
    acc = acc_ref[...]

    o = acc[:V_DIM] / acc[V_DIM:V_DIM + 1]
    o = o[:, :tq] - lam * o[:, tq:]
    o = o * lax.rsqrt(jnp.mean(o * o, axis=0, keepdims=True) + EPS) * gsub_ref[...]
    o_ref[q_rows, :] = (o * (1.0 - lam_init)).T.astype(BF16)


def _attention(q, k, vt, lq1, lk1, lq2, lk2, g_sub_col, batch, seq, lam_init):
    tq = ATT_Q
    tk = vt.shape[2]
    nk = seq // tk
    vec = lambda b, h: (0, 0)
    per_head = lambda b, h: (b, h)
    visible = (lax.broadcasted_iota(I32, (tk, 2 * tq), 0) <= lax.broadcasted_iota(I32, (tk, 2 * tq), 1) % tq)
    bias = jnp.where(visible, 0.0, NEG_BIG).astype(F32)
    return pl.pallas_call(
        functools.partial(_attn_kernel, lam_init=lam_init),
        grid=(batch, N_HEADS),
        in_specs=[
            pl.BlockSpec((1, HALF_DIM), vec),
            pl.BlockSpec((1, HALF_DIM), vec),
            pl.BlockSpec((1, HALF_DIM), vec),
            pl.BlockSpec((1, HALF_DIM), vec),
            pl.BlockSpec((V_DIM, 1), vec),
            pl.BlockSpec((tk, 2 * tq), vec),
            pl.BlockSpec((seq, LANES), per_head),
            pl.BlockSpec((seq, LANES), per_head),
            pl.BlockSpec((nk, V_EXT, tk), lambda b, h: (b, h, 0)),
        ],
        out_specs=pl.BlockSpec((seq, V_DIM), per_head),
        out_shape=jax.ShapeDtypeStruct((batch * seq, D_ATT), BF16),
        scratch_shapes=[pltpu.VMEM((1, 2 * tq), F32), pltpu.VMEM((V_EXT, 2 * tq), F32),
                        pltpu.VMEM((tk, 2 * tq), F32), pltpu.VMEM((tk, 2 * tq), F32),
                        pltpu.VMEM((1, 2 * tq), F32), pltpu.VMEM((1, 2 * tq), F32)],
        compiler_params=pltpu.CompilerParams(
            dimension_semantics=("arbitrary", "arbitrary"), vmem_limit_bytes=VMEM_LIMIT),
        name="attention",
    )(lq1, lk1, lq2, lk2, g_sub_col, bias, q, k, vt)


def _sublane_total(x, op):
    return op(x, axis=0, keepdims=True)


def _route_kernel(x_ref, yrec_ref, yatt_ref, wo32_ref, lnmoe_ref, wrt_ref, ebias_ref, tri_ref,
                  h1_ref, hpa_ref, hpb_ref, ek_ref, wk_ref, rk_ref, cnt_ref, carry_ref, wo_ref):
    tm = x_ref.shape[0]
    e_n = N_EXPERTS

    @pl.when(pl.program_id(0) == 0)
    def _():
        wo_ref[...] = wo32_ref[...].astype(BF16)
        carry_ref[...] = jnp.zeros_like(carry_ref)

    h1 = x_ref[...] + _dot(yrec_ref[...], wo_ref[:D_REC, :]) + _dot(yatt_ref[...], wo_ref[D_REC:, :])
    h1_ref[...] = h1
    hn = _rms(h1, lnmoe_ref[...])
    pa, pb = _pack_row(hn)
    hpa_ref[...] = pa
    hpb_ref[...] = pb

    def split(v):
        head = v.astype(BF16)
        return head, (v - head.astype(F32)).astype(BF16)

    nt_dot = lambda a, b: lax.dot_general(a, b, (((1,), (1,)), ((), ())), preferred_element_type=F32)
    (w_head, w_rest), (h_head, h_rest) = split(wrt_ref[...]), split(hn)
    logits = nt_dot(w_head, h_head) + (nt_dot(w_head, h_rest) + nt_dot(w_rest, h_head))
    scores = jax.nn.sigmoid(logits)
    sel = scores + ebias_ref[...]

    sel3 = sel.reshape(N_GROUPS, GROUP_SIZE, tm)
    idx3 = lax.broadcasted_iota(I32, (N_GROUPS, GROUP_SIZE, tm), 1)
    m1 = jnp.max(sel3, axis=1, keepdims=True)
    first = jnp.min(jnp.where(sel3 == m1, idx3, GROUP_SIZE), axis=1, keepdims=True)
    m2 = jnp.max(jnp.where(idx3 == first, -jnp.inf, sel3), axis=1, keepdims=True)
    gscore = (m1 + m2).reshape(N_GROUPS, tm)

    gidx = lax.broadcasted_iota(I32, (N_GROUPS, tm), 0)
    beaten = jnp.zeros((N_GROUPS, tm), I32)
    for g in range(N_GROUPS):
        other = gscore[g:g + 1, :]
        beats = (other > gscore) | ((other == gscore) & (g < gidx))
        beaten = beaten + beats.astype(I32)
    gkeep = beaten < TOPK_GROUPS
    keep = jnp.broadcast_to(gkeep.reshape(N_GROUPS, 1, tm), (N_GROUPS, GROUP_SIZE, tm)).reshape(e_n, tm)
    selm = jnp.where(keep, sel, -jnp.inf)

    eidx = lax.broadcasted_iota(I32, (e_n, tm), 0)
    remaining = selm
    picks, ek, sk = [], [], []
    for _ in range(TOP_K):
        best = jnp.max(remaining, axis=0, keepdims=True)
        first = jnp.min(jnp.where(remaining == best, eidx, e_n), axis=0, keepdims=True)
        pick = eidx == first
        picks.append(pick)
        ek.append(first)
        sk.append(_sublane_total(jnp.where(pick, scores, 0.0), jnp.sum))
        remaining = jnp.where(pick, -jnp.inf, remaining)
    chosen_f = (remaining != selm).astype(F32)
    wsum = functools.reduce(lambda a, b: a + b, sk)
    ek_ref[...] = jnp.concatenate(ek, axis=0)
    wk_ref[...] = (jnp.concatenate(sk, axis=0) * (ROUTE_SCALE / wsum)).T

    prefix = _dot(chosen_f.astype(BF16), tri_ref[...])
    rank = prefix + carry_ref[...]
    carry_new = carry_ref[...] + jnp.sum(chosen_f, axis=1, keepdims=True)
    carry_ref[...] = carry_new
    cnt_ref[...] = carry_new.astype(I32)
    rk = [_sublane_total(jnp.where(pick, rank, 0.0), jnp.sum) for pick in picks]
    rk_ref[...] = jnp.concatenate(rk, axis=0).astype(I32)


def _route(x2, y_rec, y_att, w_out, ln_moe, w_router_t, e_bias_col):
    t = x2.shape[0]
    tm = ROUTE_ROWS
    nt = t // tm
    row_map = lambda i: (i, 0)
    col_map = lambda i: (0, i)
    fixed = lambda i: (0, 0)
    tri = (lax.broadcasted_iota(I32, (tm, tm), 0) < lax.broadcasted_iota(I32, (tm, tm), 1)).astype(BF16)
    out_shapes = (
        jax.ShapeDtypeStruct((t, D_MODEL), F32),
        jax.ShapeDtypeStruct((t, PACK_W), U32),
        jax.ShapeDtypeStruct((t, PACK_W), U32),
        jax.ShapeDtypeStruct((TOP_K, t), I32),
        jax.ShapeDtypeStruct((t, TOP_K), F32),
        jax.ShapeDtypeStruct((TOP_K, t), I32),
        jax.ShapeDtypeStruct((N_EXPERTS, 1), I32),
    )
    return pl.pallas_call(
        _route_kernel,
        grid=(nt,),
        in_specs=[
            pl.BlockSpec((tm, D_MODEL), row_map),
            pl.BlockSpec((tm, D_REC), row_map),
            pl.BlockSpec((tm, D_ATT), row_map),
            pl.BlockSpec((D_REC + D_ATT, D_MODEL), fixed, pipeline_mode=pl.Buffered(1)),
            pl.BlockSpec((1, D_MODEL), fixed),
            pl.BlockSpec((N_EXPERTS, D_MODEL), fixed),
            pl.BlockSpec((N_EXPERTS, 1), fixed),
            pl.BlockSpec((tm, tm), fixed),
        ],
        out_specs=[
            pl.BlockSpec((tm, D_MODEL), row_map),
            pl.BlockSpec((tm, PACK_W), row_map),
            pl.BlockSpec((tm, PACK_W), row_map),
            pl.BlockSpec((TOP_K, tm), col_map),
            pl.BlockSpec((tm, TOP_K), row_map),
            pl.BlockSpec((TOP_K, tm), col_map),
            pl.BlockSpec((N_EXPERTS, 1), fixed),
        ],
        out_shape=out_shapes,
        scratch_shapes=[pltpu.VMEM((N_EXPERTS, 1), F32), pltpu.VMEM((D_REC + D_ATT, D_MODEL), BF16)],
        compiler_params=pltpu.CompilerParams(dimension_semantics=("arbitrary",), vmem_limit_bytes=VMEM_LIMIT),
        name="route",
    )(x2, y_rec, y_att, w_out, ln_moe, w_router_t, e_bias_col, tri)


def _plan_kernel(pad_start_ref, ek_ref, rk_ref, dest_ref):
    ek = ek_ref[...]

    def add_expert(e, base):
        return jnp.where(ek == e, pad_start_ref[e], base)

    dest_ref[...] = rk_ref[...] + lax.fori_loop(0, N_EXPERTS, add_expert, jnp.zeros_like(ek))


def _plan(pad_start, ek, rk):
    kk, t = ek.shape
    tl = min(t, 2048)
    col_map = lambda i, ps: (0, i)
    grid_spec = pltpu.PrefetchScalarGridSpec(
        num_scalar_prefetch=1,
        grid=(t // tl,),
        in_specs=[pl.BlockSpec((kk, tl), col_map), pl.BlockSpec((kk, tl), col_map)],
        out_specs=pl.BlockSpec((kk, tl), col_map),
    )
    return pl.pallas_call(
        _plan_kernel,
        grid_spec=grid_spec,
        out_shape=jax.ShapeDtypeStruct((kk, t), I32),
        compiler_params=pltpu.CompilerParams(dimension_semantics=("arbitrary",)),
        name="plan",
    )(pad_start, ek, rk)


def _sc_mesh():
    return plsc.VectorSubcoreMesh(core_axis_name="core", subcore_axis_name="subcore")


def _sc_dispatch(rows, dest, n_out):
    t, w = rows.shape
    kk = dest.shape[0]

    @pl.kernel(out_type=jax.ShapeDtypeStruct((n_out, w), rows.dtype), mesh=_sc_mesh(), scratch_types=[])
    def kern(x_hbm, i_hbm, o_hbm):
        def body(x_vmem, i_vmem):
            for k in range(kk):
                pltpu.sync_copy(x_vmem, o_hbm.at[i_vmem.at[k]])

        pltpu.emit_pipeline(
            body,
            grid=(t // SC_WINDOW,),
            in_specs=[pl.BlockSpec((SC_WINDOW, w), lambda i: (i, 0)),
                      pl.BlockSpec((kk, SC_WINDOW), lambda i: (0, i))],
            out_specs=[],
            core_axis_name=("core", "subcore"),
            dimension_semantics=(pltpu.PARALLEL,),
        )(x_hbm, i_hbm)

    return kern(rows, dest)


def _sc_combine(rows, dest):
    kk, t = dest.shape
    w = rows.shape[1]
    flat = dest.reshape(1, kk * t)

    @pl.kernel(out_type=jax.ShapeDtypeStruct((kk * t, w), rows.dtype), mesh=_sc_mesh(), scratch_types=[])
    def kern(y_hbm, i_hbm, o_hbm):
        def body(i_vmem, o_vmem):
            pltpu.sync_copy(y_hbm.at[i_vmem.at[0]], o_vmem)

        pltpu.emit_pipeline(
            body,
            grid=(kk * t // SC_WINDOW,),
            in_specs=[pl.BlockSpec((1, SC_WINDOW), lambda i: (0, i))],
            out_specs=[pl.BlockSpec((SC_WINDOW, w), lambda i: (i, 0))],
            core_axis_name=("core", "subcore"),
            dimension_semantics=(pltpu.PARALLEL,),
        )(i_hbm, o_hbm)

    return kern(rows, flat).reshape(kk, t, w)


X_SLOTS = 3
Y_SLOTS = 2


def _experts_kernel(blk_expert_ref, n_used_ref, first_ref, slot_ref, next_ref,
                    xa_hbm, xb_hbm, w1_hbm, w3_hbm, w2_hbm, ya_hbm, yb_hbm,
                    xa_buf, xb_buf, ya_buf, yb_buf, w1f_ref, w3f_ref, w2f_ref, w1b_ref, w3b_ref, w2b_ref,
                    wsem, xsem, ysem):
    m = EXPERT_ROWS
    n_used = n_used_ref[0]

    def weight_copies(e, s):
        return (pltpu.make_async_copy(w1_hbm.at[e], w1f_ref.at[s], wsem.at[s, 0]),
                pltpu.make_async_copy(w3_hbm.at[e], w3f_ref.at[s], wsem.at[s, 1]),
                pltpu.make_async_copy(w2_hbm.at[e], w2f_ref.at[s], wsem.at[s, 2]))

    def x_copies(b):
        rows, s = pl.ds(pl.multiple_of(b * m, m), m), b % X_SLOTS
        return (pltpu.make_async_copy(xa_hbm.at[rows], xa_buf.at[s], xsem.at[s, 0]),
                pltpu.make_async_copy(xb_hbm.at[rows], xb_buf.at[s], xsem.at[s, 1]))

    def y_copies(b):
        rows, s = pl.ds(pl.multiple_of(b * m, m), m), b % Y_SLOTS
        return (pltpu.make_async_copy(ya_buf.at[s], ya_hbm.at[rows], ysem.at[s, 0]),
                pltpu.make_async_copy(yb_buf.at[s], yb_hbm.at[rows], ysem.at[s, 1]))

    def start(copies):
        for copy in copies:
            copy.start()

    def wait(copies):
        for copy in copies:
            copy.wait()

    start(weight_copies(blk_expert_ref[0], 0))
    start(x_copies(0))

    @pl.when(n_used > 1)
    def _():
        start(x_copies(1))

    def block(b, carry):
        @pl.when(b + 2 < n_used)
        def _():
            start(x_copies(b + 2))

        @pl.when(first_ref[b] == 1)
        def _():
            s = slot_ref[b]
            wait(weight_copies(blk_expert_ref[b], s))

            @pl.when(next_ref[b] >= 0)
            def _():
                start(weight_copies(next_ref[b], 1 - s))

            w1b_ref[...] = w1f_ref[s].astype(BF16)
            w3b_ref[...] = w3f_ref[s].astype(BF16)
            w2b_ref[...] = w2f_ref[s].astype(BF16)

        wait(x_copies(b))

        @pl.when(b >= Y_SLOTS)
        def _():
            wait(y_copies(b - Y_SLOTS))

        x = _unpack_row(xa_buf[b % X_SLOTS], xb_buf[b % X_SLOTS]).astype(BF16)
        a = _dot(x, w1b_ref[...])
        g = _dot(x, w3b_ref[...])
        hmid = (jax.nn.silu(a) * g).astype(BF16)
        y = _dot(hmid, w2b_ref[...])
        pa, pb = _pack_row(y)
        ya_buf[b % Y_SLOTS] = pa
        yb_buf[b % Y_SLOTS] = pb
        start(y_copies(b))
        return carry

    lax.fori_loop(0, n_used, block, 0)

    for back in range(Y_SLOTS, 0, -1):
        @pl.when(n_used - back >= 0)
        def _(back=back):
            wait(y_copies(n_used - back))


def _experts(xa, xb, w1, w3, w2, blk_expert, n_used, seg_first, seg_slot, seg_next):
    p = xa.shape[0]
    m = EXPERT_ROWS
    hbm = pl.BlockSpec(memory_space=pl.ANY)
    grid_spec = pltpu.PrefetchScalarGridSpec(
        num_scalar_prefetch=5,
        grid=(1,),
        in_specs=[hbm] * 5,
        out_specs=[hbm, hbm],
        scratch_shapes=[
            pltpu.VMEM((X_SLOTS, m, PACK_W), U32), pltpu.VMEM((X_SLOTS, m, PACK_W), U32),
            pltpu.VMEM((Y_SLOTS, m, PACK_W), U32), pltpu.VMEM((Y_SLOTS, m, PACK_W), U32),
            pltpu.VMEM((2, D_MODEL, D_EXPERT), F32), pltpu.VMEM((2, D_MODEL, D_EXPERT), F32),
            pltpu.VMEM((2, D_EXPERT, D_MODEL), F32),
            pltpu.VMEM((D_MODEL, D_EXPERT), BF16), pltpu.VMEM((D_MODEL, D_EXPERT), BF16),
            pltpu.VMEM((D_EXPERT, D_MODEL), BF16),
            pltpu.SemaphoreType.DMA((2, 3)), pltpu.SemaphoreType.DMA((X_SLOTS, 2)),
            pltpu.SemaphoreType.DMA((Y_SLOTS, 2)),
        ],
    )
    return pl.pallas_call(
        _experts_kernel,
        grid_spec=grid_spec,
        out_shape=(jax.ShapeDtypeStruct((p, PACK_W), U32), jax.ShapeDtypeStruct((p, PACK_W), U32)),
        compiler_params=pltpu.CompilerParams(dimension_semantics=("arbitrary",), vmem_limit_bytes=VMEM_LIMIT),
        name="experts",
    )(blk_expert, n_used, seg_first, seg_slot, seg_next, xa, xb, w1, w3, w2)


def _tail_kernel(h1_ref, ga_ref, gb_ref, wk_ref, p_ref, lnmoe_ref, ws1_32, ws3_32, ws2_32, lnple_ref,
                 wpg_32, wpp_32, lnf_ref, o_ref, ws1_ref, ws3_ref, ws2_ref, wpg_ref, wpp_ref):
    @pl.when(pl.program_id(0) == 0)
    def _():
        for dst, src in ((ws1_ref, ws1_32), (ws3_ref, ws3_32), (ws2_ref, ws2_32), (wpg_ref, wpg_32),
                         (wpp_ref, wpp_32)):
            dst[...] = src[...].astype(BF16)

    h1 = h1_ref[...]
    hn = _rms(h1, lnmoe_ref[...]).astype(BF16)
    shared = _dot((jax.nn.silu(_dot(hn, ws1_ref[...])) * _dot(hn, ws3_ref[...])).astype(BF16), ws2_ref[...])
    wk = wk_ref[...]
    routed = jnp.zeros_like(h1)
    for kk in range(TOP_K):
        routed = routed + wk[:, kk:kk + 1] * _unpack_row(ga_ref[kk], gb_ref[kk])
    h2 = h1 + routed + shared
    gate = jax.nn.sigmoid(_dot(_rms(h2, lnple_ref[...]).astype(BF16), wpg_ref[...]))
    h3 = h2 + gate * _dot(p_ref[...].astype(BF16), wpp_ref[...])
    o_ref[...] = _rms(h3, lnf_ref[...])


def _tail(h1, ga, gb, wk_t, p2, ln_moe, ws1, ws3, ws2, ln_ple, w_pg, w_pp, ln_f):
    t = h1.shape[0]
    tm = TAIL_ROWS
    row_map = lambda i: (i, 0)
    fixed = lambda i: (0, 0)
    g_map = lambda i: (0, i, 0)
    d_sh = ws1.shape[1]
    return pl.pallas_call(
        _tail_kernel,
        grid=(t // tm,),
        in_specs=[
            pl.BlockSpec((tm, D_MODEL), row_map),
            pl.BlockSpec((TOP_K, tm, PACK_W), g_map),
            pl.BlockSpec((TOP_K, tm, PACK_W), g_map),
            pl.BlockSpec((tm, TOP_K), row_map),
            pl.BlockSpec((tm, D_PLE), row_map),
            pl.BlockSpec((1, D_MODEL), fixed),
            pl.BlockSpec((D_MODEL, d_sh), fixed, pipeline_mode=pl.Buffered(1)),
            pl.BlockSpec((D_MODEL, d_sh), fixed, pipeline_mode=pl.Buffered(1)),
            pl.BlockSpec((d_sh, D_MODEL), fixed, pipeline_mode=pl.Buffered(1)),
            pl.BlockSpec((1, D_MODEL), fixed),
            pl.BlockSpec((D_MODEL, D_MODEL), fixed, pipeline_mode=pl.Buffered(1)),
            pl.BlockSpec((D_PLE, D_MODEL), fixed, pipeline_mode=pl.Buffered(1)),
            pl.BlockSpec((1, D_MODEL), fixed),
        ],
        out_specs=pl.BlockSpec((tm, D_MODEL), row_map),
        out_shape=jax.ShapeDtypeStruct((t, D_MODEL), F32),
        scratch_shapes=[pltpu.VMEM((D_MODEL, d_sh), BF16), pltpu.VMEM((D_MODEL, d_sh), BF16),
                        pltpu.VMEM((d_sh, D_MODEL), BF16), pltpu.VMEM((D_MODEL, D_MODEL), BF16),
                        pltpu.VMEM((D_PLE, D_MODEL), BF16)],
        compiler_params=pltpu.CompilerParams(dimension_semantics=("arbitrary",), vmem_limit_bytes=VMEM_LIMIT),
        name="tail",
    )(h1, ga, gb, wk_t, p2, ln_moe, ws1, ws3, ws2, ln_ple, w_pg, w_pp, ln_f)


def _rope_constants():
    half = ROPE_DIM // 2
    inv_freq = (ROPE_THETA ** (-jnp.arange(0, ROPE_DIM, 2, dtype=F32) / ROPE_DIM)).reshape(half, 1)
    f = lax.broadcasted_iota(I32, (ROPE_DIM, LANES), 0)
    l64 = lax.broadcasted_iota(I32, (ROPE_DIM, LANES), 1) % HALF_DIM
    cos_pat = ((f < half) & (l64 < ROPE_DIM) & (l64 % half == f)).astype(F32)
    sa_pat = -((f >= half) & (l64 < half) & (l64 == f - half)).astype(F32)
    sb_pat = ((f >= half) & (l64 >= half) & (l64 < ROPE_DIM) & (l64 - half == f - half)).astype(F32)
    return inv_freq, jnp.concatenate([cos_pat, sa_pat, sb_pat], axis=1)


def _block_diag_tiles(w):
    nb, bd, _ = w.shape
    per = nb // 2
    tiles = []
    for tix in range(2):
        rows = []
        for j in range(per):
            rows.append(jnp.concatenate(
                [w[tix * per + j] if c == j else jnp.zeros((bd, bd), w.dtype) for c in range(per)], axis=1))
        tiles.append(jnp.concatenate(rows, axis=0))
    return jnp.stack(tiles).astype(BF16)


def _layer(h, p_l, positions, lam_init, ln_mix, w_in, conv_w, conv_b, w_a, b_a, w_i, b_i, rg_lambda, g_rec,
           lq1, lk1, lq2, lk2, g_sub, w_out, ln_moe, w_router, e_bias, w1, w3, w2, ws1, ws3, ws2,
           ln_ple, w_ple_gate, w_ple_proj, ln_out):
    batch, seq, _ = h.shape
    t = batch * seq
    x2 = h.reshape(t, D_MODEL)
    row = lambda a: a.reshape(1, -1)
    inv_freq, rope_pat = _rope_constants()

    y_rec, q, k, vt = _mix_in(
        x2, positions.reshape(1, t), inv_freq, rope_pat, row(ln_mix), w_in, conv_w, row(conv_b),
        _block_diag_tiles(w_a), row(b_a), _block_diag_tiles(w_i), row(b_i), row(rg_lambda), row(g_rec),
        batch, seq)
    y_att = _attention(q, k, vt, row(lq1), row(lk1), row(lq2), row(lk2), g_sub.reshape(-1, 1), batch, seq,
                       lam_init)

    h1, hpa, hpb, ek, wk_t, rk, counts = _route(
        x2, y_rec, y_att, w_out, row(ln_moe), w_router.T, e_bias.reshape(-1, 1))

    m = EXPERT_ROWS
    counts = counts.reshape(-1)
    padded = (counts + m - 1) // m * m
    pad_end = jnp.cumsum(padded)
    pad_start = pad_end - padded
    n_rows = t * TOP_K + N_EXPERTS * m
    nblk = n_rows // m
    n_used = (pad_end[-1] // m).astype(I32).reshape(1)
    blk = jnp.arange(nblk, dtype=I32)
    blk_row = jnp.minimum(blk, n_used[0] - 1) * m
    blk_expert = jnp.sum((pad_end[None, :] <= blk_row[:, None]).astype(I32), axis=1)
    prev_expert = jnp.concatenate([jnp.full((1,), -1, I32), blk_expert[:-1]])
    seg_first = ((blk < n_used[0]) & (blk_expert != prev_expert)).astype(I32)
    seg_slot = ((jnp.cumsum(seg_first) - 1) % 2).astype(I32)
    eid = jnp.arange(N_EXPERTS, dtype=I32)
    later = (padded[None, :] > 0) & (eid[None, :] > eid[:, None])
    next_expert = jnp.min(jnp.where(later, eid[None, :], N_EXPERTS), axis=1)
    next_expert = jnp.where(next_expert == N_EXPERTS, -1, next_expert).astype(I32)
    seg_next = jnp.sum(jnp.where(blk_expert[:, None] == eid[None, :], next_expert[None, :], 0), axis=1)
    dest = _plan(pad_start.astype(I32), ek, rk)

    xa = _sc_dispatch(hpa, dest, n_rows)
    xb = _sc_dispatch(hpb, dest, n_rows)
    ya, yb = _experts(xa, xb, w1, w3, w2, blk_expert, n_used, seg_first, seg_slot, seg_next)
    ga = _sc_combine(ya, dest)
    gb = _sc_combine(yb, dest)

    out = _tail(h1, ga, gb, wk_t, p_l.reshape(t, D_PLE), row(ln_moe), ws1, ws3, ws2, row(ln_ple), w_ple_gate,
                w_ple_proj, row(ln_out))
    return out.reshape(batch, seq, D_MODEL)


def kernel(x, p, positions, ln_mix, w_in, conv_w, conv_b, w_a, b_a, w_i, b_i, rg_lambda, g_rec, lq1, lk1, lq2,
           lk2, g_sub, w_out, ln_moe, w_router, e_bias, w1, w3, w2, ws1, ws3, ws2, ln_ple, w_ple_gate,
           w_ple_proj, ln_f):
    depth = w_in.shape[0]
    assert depth == 1, "the fused tail applies the final norm; one layer supported"
    lam_init = 0.8 - 0.6 * math.exp(-0.3 * 0)
    return _layer(x, p[0], positions, lam_init, ln_mix[0], w_in[0], conv_w[0], conv_b[0], w_a[0], b_a[0], w_i[0],
                  b_i[0], rg_lambda[0], g_rec[0], lq1[0], lk1[0], lq2[0], lk2[0], g_sub[0], w_out[0], ln_moe[0],
                  w_router[0], e_bias[0], w1[0], w3[0], w2[0], ws1[0], ws3[0], ws2[0], ln_ple[0], w_ple_gate[0],
                  w_ple_proj[0], ln_f)
```

```python
import functools
import math

import jax
import jax.numpy as jnp
from jax import lax
from jax.experimental import pallas as pl
from jax.experimental.pallas import tpu as pltpu
from jax.experimental.pallas import tpu_sc as plsc

F32 = jnp.float32
BF16 = jnp.bfloat16
U32 = jnp.uint32
I32 = jnp.int32

D_MODEL = 1024
D_REC = 512
CONV_WIDTH = 4
RG_C = 8.0
N_HEADS = 4
HALF_DIM = 64
V_DIM = 128
D_ATT = N_HEADS * V_DIM
D_QK = N_HEADS * 2 * HALF_DIM
ROPE_DIM = 16
ROPE_THETA = 500000.0
N_EXPERTS = 64
TOP_K = 8
N_GROUPS = 8
GROUP_SIZE = N_EXPERTS // N_GROUPS
TOPK_GROUPS = 4
D_EXPERT = 256
ROUTE_SCALE = 2.5
D_PLE = 256
EPS = 1e-6

LANES = 128
SUBLANES = 8
VMEM_LIMIT = 56 * 1024 * 1024

MIX_ROWS = 512
ATT_Q = 512
ATT_LANES = 256
ONES_ROWS = 16
V_EXT = V_DIM + ONES_ROWS
ROUTE_ROWS = 1024
EXPERT_ROWS = 512
TAIL_ROWS = 512
SC_WINDOW = 128
PACK_W = 256
NEG_BIG = -1e30


def _rms(x, g):
    return x * lax.rsqrt(jnp.mean(x * x, axis=-1, keepdims=True) + EPS) * g


def _dot(a, b):
    return jnp.dot(a, b, preferred_element_type=F32)


def _pack_pair(lo, hi):
    lo_bits = lax.bitcast_convert_type(lo.astype(BF16).astype(F32), U32)
    hi_bits = lax.bitcast_convert_type(hi.astype(BF16).astype(F32), U32)
    return (lo_bits >> 16) | (hi_bits & jnp.uint32(0xFFFF0000))


def _unpack_pair(p):
    lo = lax.bitcast_convert_type(p << 16, F32)
    hi = lax.bitcast_convert_type(p & jnp.uint32(0xFFFF0000), F32)
    return lo, hi


def _pack_row(x):
    w = PACK_W
    return _pack_pair(x[:, 0:w], x[:, w:2 * w]), _pack_pair(x[:, 2 * w:3 * w], x[:, 3 * w:4 * w])


def _unpack_row(pa, pb):
    c0, c1 = _unpack_pair(pa)
    c2, c3 = _unpack_pair(pb)
    return jnp.concatenate([c0, c1, c2, c3], axis=1)


def _shift_rows(a, s, fill, row):
    n, c = a.shape
    if s % SUBLANES == 0:
        return jnp.concatenate([jnp.full((s, c), fill, a.dtype), a[:n - s]], axis=0)
    return jnp.where(row >= s, pltpu.roll(a, s, 0), fill)


def _mix_in_kernel(x_ref, pos_ref, invf_ref, pat_ref, lnm_ref, win32_ref, cw_ref, cb_ref, wa_ref, ba_ref,
                   wi_ref, bi_ref, lam_ref, grec_ref,
                   yrec_ref, q_ref, k_ref, vt_ref, tail_ref, hcarry_ref, buf_a, buf_b, win_ref):
    tm = x_ref.shape[0]
    groups = tm // SUBLANES
    chunks = D_REC // LANES

    def stage(ref, v):
        for c in range(chunks):
            ref[c] = v[:, c * LANES:(c + 1) * LANES]

    def slab(ref, r):
        return jnp.concatenate([ref[c, pl.ds(r, groups, stride=SUBLANES), :] for c in range(chunks)], axis=1)

    @pl.when((pl.program_id(0) == 0) & (pl.program_id(1) == 0))
    def _():
        win_ref[...] = win32_ref[...].astype(BF16)

    @pl.when(pl.program_id(1) == 0)
    def _():
        tail_ref[...] = jnp.zeros_like(tail_ref)
        hcarry_ref[...] = jnp.zeros_like(hcarry_ref)

    hn = _rms(x_ref[...], lnm_ref[...]).astype(BF16)

    ang = invf_ref[...] * pos_ref[...].astype(F32)
    cs = jnp.concatenate([jnp.cos(ang), jnp.sin(ang)], axis=0).T
    pat = pat_ref[...].astype(BF16)
    tabs = jnp.zeros((tm, 3 * LANES), F32)
    for _ in range(3):
        piece = cs.astype(BF16)
        tabs = tabs + _dot(piece, pat)
        cs = cs - piece.astype(F32)
    lane64 = lax.broadcasted_iota(I32, (1, LANES), 1) % HALF_DIM
    cosf = tabs[:, 0:LANES] + (lane64 >= ROPE_DIM).astype(F32)
    sa, sb = tabs[:, LANES:2 * LANES], tabs[:, 2 * LANES:3 * LANES]

    def project_rotary(out_ref, off, mul):
        tc, ta, tb = (cosf, sa, sb) if mul == 1.0 else (cosf * mul, sa * mul, sb * mul)
        for c in range(0, D_QK // LANES, 2):
            z2 = _dot(hn, win_ref[:, off + c * LANES: off + (c + 2) * LANES])
            for cc in range(2):
                zc = z2[:, cc * LANES:(cc + 1) * LANES]
                rot = (zc * tc + pltpu.roll(zc, LANES - ROPE_DIM // 2, 1) * ta
                       + pltpu.roll(zc, ROPE_DIM // 2, 1) * tb)
                out_ref[:, (c + cc) * LANES:(c + cc + 1) * LANES] = rot.astype(BF16)

    def project_v():
        vt = _dot(hn, win_ref[:, 2 * D_REC + 2 * D_QK:]).T.astype(BF16)
        for hd in range(N_HEADS):
            vt_ref[0, hd * V_EXT:hd * V_EXT + V_DIM, :] = vt[hd * V_DIM:(hd + 1) * V_DIM]
            vt_ref[0, hd * V_EXT + V_DIM:(hd + 1) * V_EXT, :] = jnp.ones((ONES_ROWS, tm), BF16)

    xr = _dot(hn, win_ref[:, 0:D_REC])
    stage(buf_a, xr)
    stage(buf_b, _dot(hn, win_ref[:, D_REC:2 * D_REC]))
    tail = tail_ref[...]
    tail_ref[...] = xr[tm - SUBLANES:, :]

    grow = lax.broadcasted_iota(I32, (groups, D_REC), 0)

    def down_one(a, first_row):
        return jnp.where(grow == 0, first_row, pltpu.roll(a, 1, 0))

    xs = [slab(buf_a, r) for r in range(SUBLANES)]
    wrapped = {r: down_one(xs[r], tail[r:r + 1, :]) for r in range(SUBLANES - CONV_WIDTH + 1, SUBLANES)}
    xc = []
    for r in range(SUBLANES):
        acc = cb_ref[...] + cw_ref[CONV_WIDTH - 1:CONV_WIDTH, :] * xs[r]
        for d in range(1, CONV_WIDTH):
            prev = xs[r - d] if r >= d else wrapped[r - d + SUBLANES]
            acc = acc + cw_ref[CONV_WIDTH - 1 - d:CONV_WIDTH - d, :] * prev
        xc.append(acc)
    xc = jnp.concatenate(xc, axis=0)

    xcb = xc.astype(BF16)
    half = D_REC // 2
    ra = jnp.concatenate([_dot(xcb[:, :half], wa_ref[0]), _dot(xcb[:, half:], wa_ref[1])], axis=1)
    ri = jnp.concatenate([_dot(xcb[:, :half], wi_ref[0]), _dot(xcb[:, half:], wi_ref[1])], axis=1)
    r_gate = jax.nn.sigmoid(ra + ba_ref[...])
    i_gate = jax.nn.sigmoid(ri + bi_ref[...])
    lam = lam_ref[...]
    softplus_neg = jnp.maximum(-lam, 0.0) + jnp.log(1.0 + jnp.exp(-jnp.abs(lam)))
    log_a = -RG_C * r_gate * softplus_neg
    a = jnp.exp(log_a)
    u = jnp.sqrt(1.0 - jnp.exp(2.0 * log_a)) * i_gate * xc

    rows = lambda v, r: v[r * groups:(r + 1) * groups]
    hs, ps = [rows(u, 0)], [rows(a, 0)]
    for r in range(1, SUBLANES):
        hs.append(rows(a, r) * hs[-1] + rows(u, r))
        ps.append(rows(a, r) * ps[-1])
    tot_a, tot_h = ps[-1], hs[-1]
    s = 1
    while s < groups:
        tot_h = tot_h + tot_a * _shift_rows(tot_h, s, 0.0, grow)
        tot_a = tot_a * _shift_rows(tot_a, s, 1.0, grow)
        s *= 2
    h_in = hcarry_ref[...]
    group_end = tot_h + tot_a * h_in
    hcarry_ref[...] = group_end[groups - 1:groups, :]
    group_in = down_one(group_end, h_in)

    for r in range(SUBLANES):
        h = hs[r] + ps[r] * group_in
        y = h * jax.nn.gelu(slab(buf_b, r))
        yn = _rms(y, grec_ref[...])
        for c in range(chunks):
            buf_a[c, pl.ds(r, groups, stride=SUBLANES), :] = yn[:, c * LANES:(c + 1) * LANES]
    for c in range(chunks):
        yrec_ref[:, c * LANES:(c + 1) * LANES] = buf_a[c].astype(BF16)

    project_rotary(q_ref, 2 * D_REC, HALF_DIM ** -0.5 * math.log2(math.e))
    project_rotary(k_ref, 2 * D_REC + D_QK, 1.0)
    project_v()


def _mix_in(x2, pos_row, inv_freq, rope_pat, ln_mix, w_in, conv_w, conv_b, wa_bd, b_a, wi_bd, b_i, rg_lambda,
            g_rec, batch, seq):
    tm = MIX_ROWS
    nt = seq // tm
    d_in = w_in.shape[1]
    row_map = lambda b, i: (b * nt + i, 0)
    fixed2 = lambda b, i: (0, 0)
    fixed3 = lambda b, i: (0, 0, 0)
    t = batch * seq
    out_shapes = (
        jax.ShapeDtypeStruct((t, D_REC), BF16),
        jax.ShapeDtypeStruct((t, D_QK), BF16),
        jax.ShapeDtypeStruct((t, D_QK), BF16),
        jax.ShapeDtypeStruct((t // tm, N_HEADS * V_EXT, tm), BF16),
    )
    return pl.pallas_call(
        _mix_in_kernel,
        grid=(batch, nt),
        in_specs=[
            pl.BlockSpec((tm, D_MODEL), row_map),
            pl.BlockSpec((1, tm), lambda b, i: (0, b * nt + i)),
            pl.BlockSpec((ROPE_DIM // 2, 1), fixed2),
            pl.BlockSpec((ROPE_DIM, 3 * LANES), fixed2),
            pl.BlockSpec((1, D_MODEL), fixed2),
            pl.BlockSpec((D_MODEL, d_in), fixed2, pipeline_mode=pl.Buffered(1)),
            pl.BlockSpec((CONV_WIDTH, D_REC), fixed2),
            pl.BlockSpec((1, D_REC), fixed2),
            pl.BlockSpec((2, D_REC // 2, D_REC // 2), fixed3),
            pl.BlockSpec((1, D_REC), fixed2),
            pl.BlockSpec((2, D_REC // 2, D_REC // 2), fixed3),
            pl.BlockSpec((1, D_REC), fixed2),
            pl.BlockSpec((1, D_REC), fixed2),
            pl.BlockSpec((1, D_REC), fixed2),
        ],
        out_specs=[
            pl.BlockSpec((tm, D_REC), row_map),
            pl.BlockSpec((tm, D_QK), row_map),
            pl.BlockSpec((tm, D_QK), row_map),
            pl.BlockSpec((1, N_HEADS * V_EXT, tm), lambda b, i: (b * nt + i, 0, 0)),
        ],
        out_shape=out_shapes,
        scratch_shapes=[pltpu.VMEM((SUBLANES, D_REC), F32), pltpu.VMEM((1, D_REC), F32),
                        pltpu.VMEM((D_REC // LANES, tm, LANES), F32),
                        pltpu.VMEM((D_REC // LANES, tm, LANES), F32),
                        pltpu.VMEM((D_MODEL, d_in), BF16)],
        compiler_params=pltpu.CompilerParams(
            dimension_semantics=("arbitrary", "arbitrary"), vmem_limit_bytes=VMEM_LIMIT),
        name="mix_in",
    )(x2, pos_row, inv_freq, rope_pat, ln_mix, w_in, conv_w, conv_b, wa_bd, b_a, wi_bd, b_i, rg_lambda, g_rec)


def _attn_kernel(lq1_ref, lk1_ref, lq2_ref, lk2_ref, gsub_ref, bias_ref, q_ref, k_ref, vt_ref, o_ref,
                 m_ref, acc_ref, s0_ref, s1_ref, mb0_ref, mb1_ref, *, lam_init):
    tq = ATT_Q
    tk = vt_ref.shape[2]
    assert tq == tk, "the causal bias tile assumes the diagonal block is square"
    lam = (jnp.exp(jnp.sum(lq1_ref[...] * lk1_ref[...], axis=-1, keepdims=True))
           - jnp.exp(jnp.sum(lq2_ref[...] * lk2_ref[...], axis=-1, keepdims=True)) + lam_init)

    def query_tile(i, carry):
        _attn_query_tile(i, lam, gsub_ref, bias_ref, q_ref, k_ref, vt_ref, o_ref, m_ref, acc_ref,
                         s0_ref, s1_ref, mb0_ref, mb1_ref, lam_init=lam_init)
        return carry

    lax.fori_loop(0, q_ref.shape[0] // tq, query_tile, 0)


def _attn_query_tile(i, lam, gsub_ref, bias_ref, q_ref, k_ref, vt_ref, o_ref, m_ref, acc_ref,
                     s0_ref, s1_ref, mb0_ref, mb1_ref, *, lam_init):
    tq = ATT_Q
    tk = vt_ref.shape[2]
    q_rows = pl.ds(pl.multiple_of(i * tq, tq), tq)

    qt = q_ref[q_rows, :].astype(F32).T
    dim = lax.broadcasted_iota(I32, (LANES, tq), 0)
    qqt = jnp.concatenate([jnp.where(dim < HALF_DIM, qt, 0.0), jnp.where(dim >= HALF_DIM, qt, 0.0)],
                          axis=1).astype(BF16)

    n = (i * tq) // tk

    def scores(j):
        return _dot(k_ref[pl.ds(pl.multiple_of(j * tk, tk), tk), :], qqt)

    chunks = [slice(c0, c0 + ATT_LANES) for c0 in range(0, 2 * tq, ATT_LANES)]

    def store_scores(j, s_buf, mb_buf):
        s = scores(j)
        s_buf[...] = s
        mb_buf[...] = jnp.max(s, axis=0, keepdims=True)

    def accumulate(cols, s, block_max, vt):
        m_prev = m_ref[:, cols]
        m_new = jnp.maximum(m_prev, block_max)
        p = jnp.exp2((s - m_new).astype(BF16))
        acc_ref[:, cols] = jnp.exp2(m_prev - m_new) * acc_ref[:, cols] + _dot(vt, p)
        m_ref[:, cols] = m_new

    def pipe_step(j, cur, nxt):
        store_scores(j + 1, *nxt)
        s_cur, mb_cur = cur
        for cols in chunks:
            accumulate(cols, s_cur[:, cols], mb_cur[:, cols], vt_ref[j])

    buf0, buf1 = (s0_ref, mb0_ref), (s1_ref, mb1_ref)
    m_ref[...] = jnp.full_like(m_ref, NEG_BIG)
    acc_ref[...] = jnp.zeros_like(acc_ref)
    odd = n % 2

    @pl.when(odd == 0)
    def _():
        store_scores(0, *buf0)

    @pl.when(odd == 1)
    def _():
        store_scores(0, *buf1)
        pipe_step(0, buf1, buf0)

    def pair(t, carry):
        j = 2 * t + odd
        pipe_step(j, buf0, buf1)
        pipe_step(j + 1, buf1, buf0)
        return carry

    lax.fori_loop(0, n // 2, pair, 0)

    for cols in chunks:
        keys = cols.start % tq + ATT_LANES
        s = s0_ref[:keys, cols] + bias_ref[:keys, cols]
        accumulate(cols, s, jnp.max(s, axis=0, keepdims=True), vt_ref[n][:, :keys])
    acc = acc_ref[...]

    o = acc[:V_DIM] / acc[V_DIM:V_DIM + 1]
    o = o[:, :tq] - lam * o[:, tq:]
    o = o * lax.rsqrt(jnp.mean(o * o, axis=0, keepdims=True) + EPS) * gsub_ref[...]
    o_ref[q_rows, :] = (o * (1.0 - lam_init)).T.astype(BF16)


def _attention(q, k, vt, lq1, lk1, lq2, lk2, g_sub_col, batch, seq, lam_init):
    tq = ATT_Q
    tk = vt.shape[2]
    nk = seq // tk
    vec = lambda b, h: (0, 0)
    per_head = lambda b, h: (b, h)
    visible = (lax.broadcasted_iota(I32, (tk, 2 * tq), 0) <= lax.broadcasted_iota(I32, (tk, 2 * tq), 1) % tq)
    bias = jnp.where(visible, 0.0, NEG_BIG).astype(F32)
    return pl.pallas_call(
        functools.partial(_attn_kernel, lam_init=lam_init),
        grid=(batch, N_HEADS),
        in_specs=[
            pl.BlockSpec((1, HALF_DIM), vec),
            pl.BlockSpec((1, HALF_DIM), vec),
            pl.BlockSpec((1, HALF_DIM), vec),
            pl.BlockSpec((1, HALF_DIM), vec),
            pl.BlockSpec((V_DIM, 1), vec),
            pl.BlockSpec((tk, 2 * tq), vec),
            pl.BlockSpec((seq, LANES), per_head),
            pl.BlockSpec((seq, LANES), per_head),
            pl.BlockSpec((nk, V_EXT, tk), lambda b, h: (b, h, 0)),
        ],
        out_specs=pl.BlockSpec((seq, V_DIM), per_head),
        out_shape=jax.ShapeDtypeStruct((batch * seq, D_ATT), BF16),
        scratch_shapes=[pltpu.VMEM((1, 2 * tq), F32), pltpu.VMEM((V_EXT, 2 * tq), F32),
                        pltpu.VMEM((tk, 2 * tq), F32), pltpu.VMEM((tk, 2 * tq), F32),
                        pltpu.VMEM((1, 2 * tq), F32), pltpu.VMEM((1, 2 * tq), F32)],
        compiler_params=pltpu.CompilerParams(
            dimension_semantics=("arbitrary", "arbitrary"), vmem_limit_bytes=VMEM_LIMIT),
        name="attention",
    )(lq1, lk1, lq2, lk2, g_sub_col, bias, q, k, vt)


def _sublane_total(x, op):
    return op(x, axis=0, keepdims=True)


def _route_kernel(x_ref, yrec_ref, yatt_ref, wo32_ref, lnmoe_ref, wrt_ref, ebias_ref, tri_ref,
                  h1_ref, hpa_ref, hpb_ref, ek_ref, wk_ref, rk_ref, cnt_ref, carry_ref, wo_ref):
    tm = x_ref.shape[0]
    e_n = N_EXPERTS

    @pl.when(pl.program_id(0) == 0)
    def _():
        wo_ref[...] = wo32_ref[...].astype(BF16)
        carry_ref[...] = jnp.zeros_like(carry_ref)

    h1 = x_ref[...] + _dot(yrec_ref[...], wo_ref[:D_REC, :]) + _dot(yatt_ref[...], wo_ref[D_REC:, :])
    h1_ref[...] = h1
    hn = _rms(h1, lnmoe_ref[...])
    pa, pb = _pack_row(hn)
    hpa_ref[...] = pa
    hpb_ref[...] = pb

    def split(v):
        head = v.astype(BF16)
        return head, (v - head.astype(F32)).astype(BF16)

    nt_dot = lambda a, b: lax.dot_general(a, b, (((1,), (1,)), ((), ())), preferred_element_type=F32)
    (w_head, w_rest), (h_head, h_rest) = split(wrt_ref[...]), split(hn)
    logits = nt_dot(w_head, h_head) + (nt_dot(w_head, h_rest) + nt_dot(w_rest, h_head))
    scores = jax.nn.sigmoid(logits)
    sel = scores + ebias_ref[...]

    sel3 = sel.reshape(N_GROUPS, GROUP_SIZE, tm)
    idx3 = lax.broadcasted_iota(I32, (N_GROUPS, GROUP_SIZE, tm), 1)
    m1 = jnp.max(sel3, axis=1, keepdims=True)
    first = jnp.min(jnp.where(sel3 == m1, idx3, GROUP_SIZE), axis=1, keepdims=True)
    m2 = jnp.max(jnp.where(idx3 == first, -jnp.inf, sel3), axis=1, keepdims=True)
    gscore = (m1 + m2).reshape(N_GROUPS, tm)

    gidx = lax.broadcasted_iota(I32, (N_GROUPS, tm), 0)
    beaten = jnp.zeros((N_GROUPS, tm), I32)
    for g in range(N_GROUPS):
        other = gscore[g:g + 1, :]
        beats = (other > gscore) | ((other == gscore) & (g < gidx))
        beaten = beaten + beats.astype(I32)
    gkeep = beaten < TOPK_GROUPS
    keep = jnp.broadcast_to(gkeep.reshape(N_GROUPS, 1, tm), (N_GROUPS, GROUP_SIZE, tm)).reshape(e_n, tm)
    selm = jnp.where(keep, sel, -jnp.inf)

    eidx = lax.broadcasted_iota(I32, (e_n, tm), 0)
    remaining = selm
    picks, ek, sk = [], [], []
    for _ in range(TOP_K):
        best = jnp.max(remaining, axis=0, keepdims=True)
        first = jnp.min(jnp.where(remaining == best, eidx, e_n), axis=0, keepdims=True)
        pick = eidx == first
        picks.append(pick)
        ek.append(first)
        sk.append(_sublane_total(jnp.where(pick, scores, 0.0), jnp.sum))
        remaining = jnp.where(pick, -jnp.inf, remaining)
    chosen_f = (remaining != selm).astype(F32)
    wsum = functools.reduce(lambda a, b: a + b, sk)
    ek_ref[...] = jnp.concatenate(ek, axis=0)
    wk_ref[...] = (jnp.concatenate(sk, axis=0) * (ROUTE_SCALE / wsum)).T

    prefix = _dot(chosen_f.astype(BF16), tri_ref[...])
    rank = prefix + carry_ref[...]
    carry_new = carry_ref[...] + jnp.sum(chosen_f, axis=1, keepdims=True)
    carry_ref[...] = carry_new
    cnt_ref[...] = carry_new.astype(I32)
    rk = [_sublane_total(jnp.where(pick, rank, 0.0), jnp.sum) for pick in picks]
    rk_ref[...] = jnp.concatenate(rk, axis=0).astype(I32)


def _route(x2, y_rec, y_att, w_out, ln_moe, w_router_t, e_bias_col):
    t = x2.shape[0]
    tm = ROUTE_ROWS
    nt = t // tm
    row_map = lambda i: (i, 0)
    col_map = lambda i: (0, i)
    fixed = lambda i: (0, 0)
    tri = (lax.broadcasted_iota(I32, (tm, tm), 0) < lax.broadcasted_iota(I32, (tm, tm), 1)).astype(BF16)
    out_shapes = (
        jax.ShapeDtypeStruct((t, D_MODEL), F32),
        jax.ShapeDtypeStruct((t, PACK_W), U32),
        jax.ShapeDtypeStruct((t, PACK_W), U32),
        jax.ShapeDtypeStruct((TOP_K, t), I32),
        jax.ShapeDtypeStruct((t, TOP_K), F32),
        jax.ShapeDtypeStruct((TOP_K, t), I32),
        jax.ShapeDtypeStruct((N_EXPERTS, 1), I32),
    )
    return pl.pallas_call(
        _route_kernel,
        grid=(nt,),
        in_specs=[
            pl.BlockSpec((tm, D_MODEL), row_map),
            pl.BlockSpec((tm, D_REC), row_map),
            pl.BlockSpec((tm, D_ATT), row_map),
            pl.BlockSpec((D_REC + D_ATT, D_MODEL), fixed, pipeline_mode=pl.Buffered(1)),
            pl.BlockSpec((1, D_MODEL), fixed),
            pl.BlockSpec((N_EXPERTS, D_MODEL), fixed),
            pl.BlockSpec((N_EXPERTS, 1), fixed),
            pl.BlockSpec((tm, tm), fixed),
        ],
        out_specs=[
            pl.BlockSpec((tm, D_MODEL), row_map),
            pl.BlockSpec((tm, PACK_W), row_map),
            pl.BlockSpec((tm, PACK_W), row_map),
            pl.BlockSpec((TOP_K, tm), col_map),
            pl.BlockSpec((tm, TOP_K), row_map),
            pl.BlockSpec((TOP_K, tm), col_map),
            pl.BlockSpec((N_EXPERTS, 1), fixed),
        ],
        out_shape=out_shapes,
        scratch_shapes=[pltpu.VMEM((N_EXPERTS, 1), F32), pltpu.VMEM((D_REC + D_ATT, D_MODEL), BF16)],
        compiler_params=pltpu.CompilerParams(dimension_semantics=("arbitrary",), vmem_limit_bytes=VMEM_LIMIT),
        name="route",
    )(x2, y_rec, y_att, w_out, ln_moe, w_router_t, e_bias_col, tri)


def _plan_kernel(pad_start_ref, ek_ref, rk_ref, dest_ref):
    ek = ek_ref[...]

    def add_expert(e, base):
        return jnp.where(ek == e, pad_start_ref[e], base)

    dest_ref[...] = rk_ref[...] + lax.fori_loop(0, N_EXPERTS, add_expert, jnp.zeros_like(ek))


def _plan(pad_start, ek, rk):
    kk, t = ek.shape
    tl = min(t, 8192)
    col_map = lambda i, ps: (0, i)
    grid_spec = pltpu.PrefetchScalarGridSpec(
        num_scalar_prefetch=1,
        grid=(t // tl,),
        in_specs=[pl.BlockSpec((kk, tl), col_map), pl.BlockSpec((kk, tl), col_map)],
        out_specs=pl.BlockSpec((kk, tl), col_map),
    )
    return pl.pallas_call(
        _plan_kernel,
        grid_spec=grid_spec,
        out_shape=jax.ShapeDtypeStruct((kk, t), I32),
        compiler_params=pltpu.CompilerParams(dimension_semantics=("arbitrary",)),
        name="plan",
    )(pad_start, ek, rk)


def _sc_mesh():
    return plsc.VectorSubcoreMesh(core_axis_name="core", subcore_axis_name="subcore")


def _sc_dispatch(rows, dest, n_out):
    t, w = rows.shape
    kk = dest.shape[0]

    @pl.kernel(out_type=jax.ShapeDtypeStruct((n_out, w), rows.dtype), mesh=_sc_mesh(), scratch_types=[])
    def kern(x_hbm, i_hbm, o_hbm):
        def body(x_vmem, i_vmem):
            for k in range(kk):
                pltpu.sync_copy(x_vmem, o_hbm.at[i_vmem.at[k]])

        pltpu.emit_pipeline(
            body,
            grid=(t // SC_WINDOW,),
            in_specs=[pl.BlockSpec((SC_WINDOW, w), lambda i: (i, 0)),
                      pl.BlockSpec((kk, SC_WINDOW), lambda i: (0, i))],
            out_specs=[],
            core_axis_name=("core", "subcore"),
            dimension_semantics=(pltpu.PARALLEL,),
        )(x_hbm, i_hbm)

    return kern(rows, dest)


def _sc_combine(rows, dest):
    kk, t = dest.shape
    w = rows.shape[1]
    flat = dest.reshape(1, kk * t)

    @pl.kernel(out_type=jax.ShapeDtypeStruct((kk * t, w), rows.dtype), mesh=_sc_mesh(), scratch_types=[])
    def kern(y_hbm, i_hbm, o_hbm):
        def body(i_vmem, o_vmem):
            pltpu.sync_copy(y_hbm.at[i_vmem.at[0]], o_vmem)

        pltpu.emit_pipeline(
            body,
            grid=(kk * t // SC_WINDOW,),
            in_specs=[pl.BlockSpec((1, SC_WINDOW), lambda i: (0, i))],
            out_specs=[pl.BlockSpec((SC_WINDOW, w), lambda i: (i, 0))],
            core_axis_name=("core", "subcore"),
            dimension_semantics=(pltpu.PARALLEL,),
        )(i_hbm, o_hbm)

    return kern(rows, flat).reshape(kk, t, w)


X_SLOTS = 3
Y_SLOTS = 2


def _experts_kernel(blk_expert_ref, n_used_ref, first_ref, slot_ref, next_ref,
                    xa_hbm, xb_hbm, w1_hbm, w3_hbm, w2_hbm, ya_hbm, yb_hbm,
                    xa_buf, xb_buf, ya_buf, yb_buf, w1f_ref, w3f_ref, w2f_ref, w1b_ref, w3b_ref, w2b_ref,
                    wsem, xsem, ysem):
    m = EXPERT_ROWS
    n_used = n_used_ref[0]

    def weight_copies(e, s):
        return (pltpu.make_async_copy(w1_hbm.at[e], w1f_ref.at[s], wsem.at[s, 0]),
                pltpu.make_async_copy(w3_hbm.at[e], w3f_ref.at[s], wsem.at[s, 1]),
                pltpu.make_async_copy(w2_hbm.at[e], w2f_ref.at[s], wsem.at[s, 2]))

    def x_copies(b):
        rows, s = pl.ds(pl.multiple_of(b * m, m), m), b % X_SLOTS
        return (pltpu.make_async_copy(xa_hbm.at[rows], xa_buf.at[s], xsem.at[s, 0]),
                pltpu.make_async_copy(xb_hbm.at[rows], xb_buf.at[s], xsem.at[s, 1]))

    def y_copies(b):
        rows, s = pl.ds(pl.multiple_of(b * m, m), m), b % Y_SLOTS
        return (pltpu.make_async_copy(ya_buf.at[s], ya_hbm.at[rows], ysem.at[s, 0]),
                pltpu.make_async_copy(yb_buf.at[s], yb_hbm.at[rows], ysem.at[s, 1]))

    def start(copies):
        for copy in copies:
            copy.start()

    def wait(copies):
        for copy in copies:
            copy.wait()

    start(weight_copies(blk_expert_ref[0], 0))
    start(x_copies(0))

    @pl.when(n_used > 1)
    def _():
        start(x_copies(1))

    def block(b, carry):
        @pl.when(b + 2 < n_used)
        def _():
            start(x_copies(b + 2))

        @pl.when(first_ref[b] == 1)
        def _():
            s = slot_ref[b]
            wait(weight_copies(blk_expert_ref[b], s))

            @pl.when(next_ref[b] >= 0)
            def _():
                start(weight_copies(next_ref[b], 1 - s))

            w1b_ref[...] = w1f_ref[s].astype(BF16)
            w3b_ref[...] = w3f_ref[s].astype(BF16)
            w2b_ref[...] = w2f_ref[s].astype(BF16)

        wait(x_copies(b))

        @pl.when(b >= Y_SLOTS)
        def _():
            wait(y_copies(b - Y_SLOTS))

        x = _unpack_row(xa_buf[b % X_SLOTS], xb_buf[b % X_SLOTS]).astype(BF16)
        a = _dot(x, w1b_ref[...])
        g = _dot(x, w3b_ref[...])
        hmid = (jax.nn.silu(a) * g).astype(BF16)
        y = _dot(hmid, w2b_ref[...])
        pa, pb = _pack_row(y)
        ya_buf[b % Y_SLOTS] = pa
        yb_buf[b % Y_SLOTS] = pb
        start(y_copies(b))
        return carry

    lax.fori_loop(0, n_used, block, 0)

    for back in range(Y_SLOTS, 0, -1):
        @pl.when(n_used - back >= 0)
        def _(back=back):
            wait(y_copies(n_used - back))


def _experts(xa, xb, w1, w3, w2, blk_expert, n_used, seg_first, seg_slot, seg_next):
    p = xa.shape[0]
    m = EXPERT_ROWS
    hbm = pl.BlockSpec(memory_space=pl.ANY)
    grid_spec = pltpu.PrefetchScalarGridSpec(
        num_scalar_prefetch=5,
        grid=(1,),
        in_specs=[hbm] * 5,
        out_specs=[hbm, hbm],
        scratch_shapes=[
            pltpu.VMEM((X_SLOTS, m, PACK_W), U32), pltpu.VMEM((X_SLOTS, m, PACK_W), U32),
            pltpu.VMEM((Y_SLOTS, m, PACK_W), U32), pltpu.VMEM((Y_SLOTS, m, PACK_W), U32),
            pltpu.VMEM((2, D_MODEL, D_EXPERT), F32), pltpu.VMEM((2, D_MODEL, D_EXPERT), F32),
            pltpu.VMEM((2, D_EXPERT, D_MODEL), F32),
            pltpu.VMEM((D_MODEL, D_EXPERT), BF16), pltpu.VMEM((D_MODEL, D_EXPERT), BF16),
            pltpu.VMEM((D_EXPERT, D_MODEL), BF16),
            pltpu.SemaphoreType.DMA((2, 3)), pltpu.SemaphoreType.DMA((X_SLOTS, 2)),
            pltpu.SemaphoreType.DMA((Y_SLOTS, 2)),
        ],
    )
    return pl.pallas_call(
        _experts_kernel,
        grid_spec=grid_spec,
        out_shape=(jax.ShapeDtypeStruct((p, PACK_W), U32), jax.ShapeDtypeStruct((p, PACK_W), U32)),
        compiler_params=pltpu.CompilerParams(dimension_semantics=("arbitrary",), vmem_limit_bytes=VMEM_LIMIT),
        name="experts",
    )(blk_expert, n_used, seg_first, seg_slot, seg_next, xa, xb, w1, w3, w2)


def _tail_kernel(h1_ref, ga_ref, gb_ref, wk_ref, p_ref, lnmoe_ref, ws1_32, ws3_32, ws2_32, lnple_ref,
                 wpg_32, wpp_32, lnf_ref, o_ref, ws1_ref, ws3_ref, ws2_ref, wpg_ref, wpp_ref):
    @pl.when(pl.program_id(0) == 0)
    def _():
        for dst, src in ((ws1_ref, ws1_32), (ws3_ref, ws3_32), (ws2_ref, ws2_32), (wpg_ref, wpg_32),
                         (wpp_ref, wpp_32)):
            dst[...] = src[...].astype(BF16)

    h1 = h1_ref[...]
    hn = _rms(h1, lnmoe_ref[...]).astype(BF16)
    shared = _dot((jax.nn.silu(_dot(hn, ws1_ref[...])) * _dot(hn, ws3_ref[...])).astype(BF16), ws2_ref[...])
    wk = wk_ref[...]
    routed = jnp.zeros_like(h1)
    for kk in range(TOP_K):
        routed = routed + wk[:, kk:kk + 1] * _unpack_row(ga_ref[kk], gb_ref[kk])
    h2 = h1 + routed + shared
    gate = jax.nn.sigmoid(_dot(_rms(h2, lnple_ref[...]).astype(BF16), wpg_ref[...]))
    h3 = h2 + gate * _dot(p_ref[...].astype(BF16), wpp_ref[...])
    o_ref[...] = _rms(h3, lnf_ref[...])


def _tail(h1, ga, gb, wk_t, p2, ln_moe, ws1, ws3, ws2, ln_ple, w_pg, w_pp, ln_f):
    t = h1.shape[0]
    tm = TAIL_ROWS
    row_map = lambda i: (i, 0)
    fixed = lambda i: (0, 0)
    g_map = lambda i: (0, i, 0)
    d_sh = ws1.shape[1]
    return pl.pallas_call(
        _tail_kernel,
        grid=(t // tm,),
        in_specs=[
            pl.BlockSpec((tm, D_MODEL), row_map),
            pl.BlockSpec((TOP_K, tm, PACK_W), g_map),
            pl.BlockSpec((TOP_K, tm, PACK_W), g_map),
            pl.BlockSpec((tm, TOP_K), row_map),
            pl.BlockSpec((tm, D_PLE), row_map),
            pl.BlockSpec((1, D_MODEL), fixed),
            pl.BlockSpec((D_MODEL, d_sh), fixed, pipeline_mode=pl.Buffered(1)),
            pl.BlockSpec((D_MODEL, d_sh), fixed, pipeline_mode=pl.Buffered(1)),
            pl.BlockSpec((d_sh, D_MODEL), fixed, pipeline_mode=pl.Buffered(1)),
            pl.BlockSpec((1, D_MODEL), fixed),
            pl.BlockSpec((D_MODEL, D_MODEL), fixed, pipeline_mode=pl.Buffered(1)),
            pl.BlockSpec((D_PLE, D_MODEL), fixed, pipeline_mode=pl.Buffered(1)),
            pl.BlockSpec((1, D_MODEL), fixed),
        ],
        out_specs=pl.BlockSpec((tm, D_MODEL), row_map),
        out_shape=jax.ShapeDtypeStruct((t, D_MODEL), F32),
        scratch_shapes=[pltpu.VMEM((D_MODEL, d_sh), BF16), pltpu.VMEM((D_MODEL, d_sh), BF16),
                        pltpu.VMEM((d_sh, D_MODEL), BF16), pltpu.VMEM((D_MODEL, D_MODEL), BF16),
                        pltpu.VMEM((D_PLE, D_MODEL), BF16)],
        compiler_params=pltpu.CompilerParams(dimension_semantics=("arbitrary",), vmem_limit_bytes=VMEM_LIMIT),
        name="tail",
    )(h1, ga, gb, wk_t, p2, ln_moe, ws1, ws3, ws2, ln_ple, w_pg, w_pp, ln_f)


def _rope_constants():
    half = ROPE_DIM // 2
    inv_freq = (ROPE_THETA ** (-jnp.arange(0, ROPE_DIM, 2, dtype=F32) / ROPE_DIM)).reshape(half, 1)
    f = lax.broadcasted_iota(I32, (ROPE_DIM, LANES), 0)
    l64 = lax.broadcasted_iota(I32, (ROPE_DIM, LANES), 1) % HALF_DIM
    cos_pat = ((f < half) & (l64 < ROPE_DIM) & (l64 % half == f)).astype(F32)
    sa_pat = -((f >= half) & (l64 < half) & (l64 == f - half)).astype(F32)
    sb_pat = ((f >= half) & (l64 >= half) & (l64 < ROPE_DIM) & (l64 - half == f - half)).astype(F32)
    return inv_freq, jnp.concatenate([cos_pat, sa_pat, sb_pat], axis=1)


def _block_diag_tiles(w):
    nb, bd, _ = w.shape
    per = nb // 2
    tiles = []
    for tix in range(2):
        rows = []
        for j in range(per):
            rows.append(jnp.concatenate(
                [w[tix * per + j] if c == j else jnp.zeros((bd, bd), w.dtype) for c in range(per)], axis=1))
        tiles.append(jnp.concatenate(rows, axis=0))
    return jnp.stack(tiles).astype(BF16)


def _layer(h, p_l, positions, lam_init, ln_mix, w_in, conv_w, conv_b, w_a, b_a, w_i, b_i, rg_lambda, g_rec,
           lq1, lk1, lq2, lk2, g_sub, w_out, ln_moe, w_router, e_bias, w1, w3, w2, ws1, ws3, ws2,
           ln_ple, w_ple_gate, w_ple_proj, ln_out):
    batch, seq, _ = h.shape
    t = batch * seq
    x2 = h.reshape(t, D_MODEL)
    row = lambda a: a.reshape(1, -1)
    inv_freq, rope_pat = _rope_constants()

    y_rec, q, k, vt = _mix_in(
        x2, positions.reshape(1, t), inv_freq, rope_pat, row(ln_mix), w_in, conv_w, row(conv_b),
        _block_diag_tiles(w_a), row(b_a), _block_diag_tiles(w_i), row(b_i), row(rg_lambda), row(g_rec),
        batch, seq)
    y_att = _attention(q, k, vt, row(lq1), row(lk1), row(lq2), row(lk2), g_sub.reshape(-1, 1), batch, seq,
                       lam_init)

    h1, hpa, hpb, ek, wk_t, rk, counts = _route(
        x2, y_rec, y_att, w_out, row(ln_moe), w_router.T, e_bias.reshape(-1, 1))

    m = EXPERT_ROWS
    counts = counts.reshape(-1)
    padded = (counts + m - 1) // m * m
    pad_end = jnp.cumsum(padded)
    pad_start = pad_end - padded
    n_rows = t * TOP_K + N_EXPERTS * m
    nblk = n_rows // m
    n_used = (pad_end[-1] // m).astype(I32).reshape(1)
    blk = jnp.arange(nblk, dtype=I32)
    blk_row = jnp.minimum(blk, n_used[0] - 1) * m
    blk_expert = jnp.sum((pad_end[None, :] <= blk_row[:, None]).astype(I32), axis=1)
    prev_expert = jnp.concatenate([jnp.full((1,), -1, I32), blk_expert[:-1]])
    seg_first = ((blk < n_used[0]) & (blk_expert != prev_expert)).astype(I32)
    seg_slot = ((jnp.cumsum(seg_first) - 1) % 2).astype(I32)
    eid = jnp.arange(N_EXPERTS, dtype=I32)
    later = (padded[None, :] > 0) & (eid[None, :] > eid[:, None])
    next_expert = jnp.min(jnp.where(later, eid[None, :], N_EXPERTS), axis=1)
    next_expert = jnp.where(next_expert == N_EXPERTS, -1, next_expert).astype(I32)
    seg_next = jnp.sum(jnp.where(blk_expert[:, None] == eid[None, :], next_expert[None, :], 0), axis=1)
    dest = _plan(pad_start.astype(I32), ek, rk)

    xa = _sc_dispatch(hpa, dest, n_rows)
    xb = _sc_dispatch(hpb, dest, n_rows)
    ya, yb = _experts(xa, xb, w1, w3, w2, blk_expert, n_used, seg_first, seg_slot, seg_next)
    ga = _sc_combine(ya, dest)
    gb = _sc_combine(yb, dest)

    out = _tail(h1, ga, gb, wk_t, p_l.reshape(t, D_PLE), row(ln_moe), ws1, ws3, ws2, row(ln_ple), w_ple_gate,
                w_ple_proj, row(ln_out))
    return out.reshape(batch, seq, D_MODEL)


def kernel(x, p, positions, ln_mix, w_in, conv_w, conv_b, w_a, b_a, w_i, b_i, rg_lambda, g_rec, lq1, lk1, lq2,
           lk2, g_sub, w_out, ln_moe, w_router, e_bias, w1, w3, w2, ws1, ws3, ws2, ln_ple, w_ple_gate,
           w_ple_proj, ln_f):
    depth = w_in.shape[0]
    assert depth == 1, "the fused tail applies the final norm; one layer supported"
    lam_init = 0.8 - 0.6 * math.exp(-0.3 * 0)
    return _layer(x, p[0], positions, lam_init, ln_mix[0], w_in[0], conv_w[0], conv_b[0], w_a[0], b_a[0], w_i[0],
                  b_i[0], rg_lambda[0], g_rec[0], lq1[0], lk1[0], lq2[0], lk2[0], g_sub[0], w_out[0], ln_moe[0],
                  w_router[0], e_bias[0], w1[0], w3[0], w2[0], ws1[0], ws3[0], ws2[0], ln_ple[0], w_ple_gate[0],
                  w_ple_proj[0], ln_f)
```

```python
import functools
import math

import jax
import jax.numpy as jnp
from jax import lax
from jax.experimental import pallas as pl
from jax.experimental.pallas import tpu as pltpu
from jax.experimental.pallas import tpu_sc as plsc

F32 = jnp.float32
BF16 = jnp.bfloat16
U32 = jnp.uint32
I32 = jnp.int32

D_MODEL = 1024
D_REC = 512
CONV_WIDTH = 4
RG_C = 8.0
N_HEADS = 4
HALF_DIM = 64
V_DIM = 128
D_ATT = N_HEADS * V_DIM
D_QK = N_HEADS * 2 * HALF_DIM
ROPE_DIM = 16
ROPE_THETA = 500000.0
N_EXPERTS = 64
TOP_K = 8
N_GROUPS = 8
GROUP_SIZE = N_EXPERTS // N_GROUPS
TOPK_GROUPS = 4
D_EXPERT = 256
ROUTE_SCALE = 2.5
D_PLE = 256
EPS = 1e-6

LANES = 128
SUBLANES = 8
VMEM_LIMIT = 56 * 1024 * 1024

MIX_ROWS = 512
ATT_Q = 512
ATT_LANES = 256
ONES_ROWS = 16
V_EXT = V_DIM + ONES_ROWS
ROUTE_ROWS = 1024
EXPERT_ROWS = 512
TAIL_ROWS = 512
SC_WINDOW = 128
PACK_W = 256
NEG_BIG = -1e30


def _rms(x, g):
    return x * lax.rsqrt(jnp.mean(x * x, axis=-1, keepdims=True) + EPS) * g


def _dot(a, b):
    return jnp.dot(a, b, preferred_element_type=F32)


def _pack_pair(lo, hi):
    lo_bits = lax.bitcast_convert_type(lo.astype(BF16).astype(F32), U32)
    hi_bits = lax.bitcast_convert_type(hi.astype(BF16).astype(F32), U32)
    return (lo_bits >> 16) | (hi_bits & jnp.uint32(0xFFFF0000))


def _unpack_pair(p):
    lo = lax.bitcast_convert_type(p << 16, F32)
    hi = lax.bitcast_convert_type(p & jnp.uint32(0xFFFF0000), F32)
    return lo, hi


def _pack_row(x):
    w = PACK_W
    return _pack_pair(x[:, 0:w], x[:, w:2 * w]), _pack_pair(x[:, 2 * w:3 * w], x[:, 3 * w:4 * w])


def _unpack_row(pa, pb):
    c0, c1 = _unpack_pair(pa)
    c2, c3 = _unpack_pair(pb)
    return jnp.concatenate([c0, c1, c2, c3], axis=1)


def _shift_rows(a, s, fill, row):
    n, c = a.shape
    if s % SUBLANES == 0:
        return jnp.concatenate([jnp.full((s, c), fill, a.dtype), a[:n - s]], axis=0)
    return jnp.where(row >= s, pltpu.roll(a, s, 0), fill)


def _mix_in_kernel(x_ref, pos_ref, invf_ref, pat_ref, lnm_ref, win32_ref, cw_ref, cb_ref, wa_ref, ba_ref,
                   wi_ref, bi_ref, lam_ref, grec_ref,
                   yrec_ref, q_ref, k_ref, vt_ref, tail_ref, hcarry_ref, buf_a, buf_b, win_ref):
    tm = x_ref.shape[0]
    groups = tm // SUBLANES
    chunks = D_REC // LANES

    def stage(ref, v):
        for c in range(chunks):
            ref[c] = v[:, c * LANES:(c + 1) * LANES]

    def slab(ref, r):
        return jnp.concatenate([ref[c, pl.ds(r, groups, stride=SUBLANES), :] for c in range(chunks)], axis=1)

    @pl.when((pl.program_id(0) == 0) & (pl.program_id(1) == 0))
    def _():
        win_ref[...] = win32_ref[...].astype(BF16)

    @pl.when(pl.program_id(1) == 0)
    def _():
        tail_ref[...] = jnp.zeros_like(tail_ref)
        hcarry_ref[...] = jnp.zeros_like(hcarry_ref)

    hn = _rms(x_ref[...], lnm_ref[...]).astype(BF16)

    ang = invf_ref[...] * pos_ref[...].astype(F32)
    cs = jnp.concatenate([jnp.cos(ang), jnp.sin(ang)], axis=0).T
    pat = pat_ref[...].astype(BF16)
    tabs = jnp.zeros((tm, 3 * LANES), F32)
    for _ in range(3):
        piece = cs.astype(BF16)
        tabs = tabs + _dot(piece, pat)
        cs = cs - piece.astype(F32)
    lane64 = lax.broadcasted_iota(I32, (1, LANES), 1) % HALF_DIM
    cosf = tabs[:, 0:LANES] + (lane64 >= ROPE_DIM).astype(F32)
    sa, sb = tabs[:, LANES:2 * LANES], tabs[:, 2 * LANES:3 * LANES]

    def project_rotary(out_ref, off, mul):
        tc, ta, tb = (cosf, sa, sb) if mul == 1.0 else (cosf * mul, sa * mul, sb * mul)
        for c in range(0, D_QK // LANES, 2):
            z2 = _dot(hn, win_ref[:, off + c * LANES: off + (c + 2) * LANES])
            for cc in range(2):
                zc = z2[:, cc * LANES:(cc + 1) * LANES]
                rot = (zc * tc + pltpu.roll(zc, LANES - ROPE_DIM // 2, 1) * ta
                       + pltpu.roll(zc, ROPE_DIM // 2, 1) * tb)
                out_ref[:, (c + cc) * LANES:(c + cc + 1) * LANES] = rot.astype(BF16)

    def project_v():
        vt = _dot(hn, win_ref[:, 2 * D_REC + 2 * D_QK:]).T.astype(BF16)
        for hd in range(N_HEADS):
            vt_ref[0, hd * V_EXT:hd * V_EXT + V_DIM, :] = vt[hd * V_DIM:(hd + 1) * V_DIM]
            vt_ref[0, hd * V_EXT + V_DIM:(hd + 1) * V_EXT, :] = jnp.ones((ONES_ROWS, tm), BF16)

    xr = _dot(hn, win_ref[:, 0:D_REC])
    stage(buf_a, xr)
    stage(buf_b, _dot(hn, win_ref[:, D_REC:2 * D_REC]))
    tail = tail_ref[...]
    tail_ref[...] = xr[tm - SUBLANES:, :]

    grow = lax.broadcasted_iota(I32, (groups, D_REC), 0)

    def down_one(a, first_row):
        return jnp.where(grow == 0, first_row, pltpu.roll(a, 1, 0))

    xs = [slab(buf_a, r) for r in range(SUBLANES)]
    wrapped = {r: down_one(xs[r], tail[r:r + 1, :]) for r in range(SUBLANES - CONV_WIDTH + 1, SUBLANES)}
    xc = []
    for r in range(SUBLANES):
        acc = cb_ref[...] + cw_ref[CONV_WIDTH - 1:CONV_WIDTH, :] * xs[r]
        for d in range(1, CONV_WIDTH):
            prev = xs[r - d] if r >= d else wrapped[r - d + SUBLANES]
            acc = acc + cw_ref[CONV_WIDTH - 1 - d:CONV_WIDTH - d, :] * prev
        xc.append(acc)
    xc = jnp.concatenate(xc, axis=0)

    xcb = xc.astype(BF16)
    half = D_REC // 2
    ra = jnp.concatenate([_dot(xcb[:, :half], wa_ref[0]), _dot(xcb[:, half:], wa_ref[1])], axis=1)
    ri = jnp.concatenate([_dot(xcb[:, :half], wi_ref[0]), _dot(xcb[:, half:], wi_ref[1])], axis=1)
    r_gate = jax.nn.sigmoid(ra + ba_ref[...])
    i_gate = jax.nn.sigmoid(ri + bi_ref[...])
    lam = lam_ref[...]
    softplus_neg = jnp.maximum(-lam, 0.0) + jnp.log(1.0 + jnp.exp(-jnp.abs(lam)))
    log_a = -RG_C * r_gate * softplus_neg
    a = jnp.exp(log_a)
    u = jnp.sqrt(1.0 - jnp.exp(2.0 * log_a)) * i_gate * xc

    rows = lambda v, r: v[r * groups:(r + 1) * groups]
    hs, ps = [rows(u, 0)], [rows(a, 0)]
    for r in range(1, SUBLANES):
        hs.append(rows(a, r) * hs[-1] + rows(u, r))
        ps.append(rows(a, r) * ps[-1])
    tot_a, tot_h = ps[-1], hs[-1]
    s = 1
    while s < groups:
        tot_h = tot_h + tot_a * _shift_rows(tot_h, s, 0.0, grow)
        tot_a = tot_a * _shift_rows(tot_a, s, 1.0, grow)
        s *= 2
    h_in = hcarry_ref[...]
    group_end = tot_h + tot_a * h_in
    hcarry_ref[...] = group_end[groups - 1:groups, :]
    group_in = down_one(group_end, h_in)

    for r in range(SUBLANES):
        h = hs[r] + ps[r] * group_in
        y = h * jax.nn.gelu(slab(buf_b, r))
        yn = _rms(y, grec_ref[...])
        for c in range(chunks):
            buf_a[c, pl.ds(r, groups, stride=SUBLANES), :] = yn[:, c * LANES:(c + 1) * LANES]
    for c in range(chunks):
        yrec_ref[:, c * LANES:(c + 1) * LANES] = buf_a[c].astype(BF16)

    project_rotary(q_ref, 2 * D_REC, HALF_DIM ** -0.5 * math.log2(math.e))
    project_rotary(k_ref, 2 * D_REC + D_QK, 1.0)
    project_v()


def _mix_in(x2, pos_row, inv_freq, rope_pat, ln_mix, w_in, conv_w, conv_b, wa_bd, b_a, wi_bd, b_i, rg_lambda,
            g_rec, batch, seq):
    tm = MIX_ROWS
    nt = seq // tm
    d_in = w_in.shape[1]
    row_map = lambda b, i: (b * nt + i, 0)
    fixed2 = lambda b, i: (0, 0)
    fixed3 = lambda b, i: (0, 0, 0)
    t = batch * seq
    out_shapes = (
        jax.ShapeDtypeStruct((t, D_REC), BF16),
        jax.ShapeDtypeStruct((t, D_QK), BF16),
        jax.ShapeDtypeStruct((t, D_QK), BF16),
        jax.ShapeDtypeStruct((t // tm, N_HEADS * V_EXT, tm), BF16),
    )
    return pl.pallas_call(
        _mix_in_kernel,
        grid=(batch, nt),
        in_specs=[
            pl.BlockSpec((tm, D_MODEL), row_map),
            pl.BlockSpec((1, tm), lambda b, i: (0, b * nt + i)),
            pl.BlockSpec((ROPE_DIM // 2, 1), fixed2),
            pl.BlockSpec((ROPE_DIM, 3 * LANES), fixed2),
            pl.BlockSpec((1, D_MODEL), fixed2),
            pl.BlockSpec((D_MODEL, d_in), fixed2, pipeline_mode=pl.Buffered(1)),
            pl.BlockSpec((CONV_WIDTH, D_REC), fixed2),
            pl.BlockSpec((1, D_REC), fixed2),
            pl.BlockSpec((2, D_REC // 2, D_REC // 2), fixed3),
            pl.BlockSpec((1, D_REC), fixed2),
            pl.BlockSpec((2, D_REC // 2, D_REC // 2), fixed3),
            pl.BlockSpec((1, D_REC), fixed2),
            pl.BlockSpec((1, D_REC), fixed2),
            pl.BlockSpec((1, D_REC), fixed2),
        ],
        out_specs=[
            pl.BlockSpec((tm, D_REC), row_map),
            pl.BlockSpec((tm, D_QK), row_map),
            pl.BlockSpec((tm, D_QK), row_map),
            pl.BlockSpec((1, N_HEADS * V_EXT, tm), lambda b, i: (b * nt + i, 0, 0)),
        ],
        out_shape=out_shapes,
        scratch_shapes=[pltpu.VMEM((SUBLANES, D_REC), F32), pltpu.VMEM((1, D_REC), F32),
                        pltpu.VMEM((D_REC // LANES, tm, LANES), F32),
                        pltpu.VMEM((D_REC // LANES, tm, LANES), F32),
                        pltpu.VMEM((D_MODEL, d_in), BF16)],
        compiler_params=pltpu.CompilerParams(
            dimension_semantics=("arbitrary", "arbitrary"), vmem_limit_bytes=VMEM_LIMIT),
        name="mix_in",
    )(x2, pos_row, inv_freq, rope_pat, ln_mix, w_in, conv_w, conv_b, wa_bd, b_a, wi_bd, b_i, rg_lambda, g_rec)


def _attn_kernel(lq1_ref, lk1_ref, lq2_ref, lk2_ref, gsub_ref, bias_ref, q_ref, k_ref, vt_ref, o_ref,
                 m_ref, acc_ref, s0_ref, s1_ref, mb0_ref, mb1_ref, *, lam_init):
    tq = ATT_Q
    tk = vt_ref.shape[2]
    assert tq == tk, "the causal bias tile assumes the diagonal block is square"
    lam = (jnp.exp(jnp.sum(lq1_ref[...] * lk1_ref[...], axis=-1, keepdims=True))
           - jnp.exp(jnp.sum(lq2_ref[...] * lk2_ref[...], axis=-1, keepdims=True)) + lam_init)

    def query_tile(i, carry):
        _attn_query_tile(i, lam, gsub_ref, bias_ref, q_ref, k_ref, vt_ref, o_ref, m_ref, acc_ref,
                         s0_ref, s1_ref, mb0_ref, mb1_ref, lam_init=lam_init)
        return carry

    lax.fori_loop(0, q_ref.shape[0] // tq, query_tile, 0)


def _attn_query_tile(i, lam, gsub_ref, bias_ref, q_ref, k_ref, vt_ref, o_ref, m_ref, acc_ref,
                     s0_ref, s1_ref, mb0_ref, mb1_ref, *, lam_init):
    tq = ATT_Q
    tk = vt_ref.shape[2]
    q_rows = pl.ds(pl.multiple_of(i * tq, tq), tq)

    qt = q_ref[q_rows, :].astype(F32).T
    dim = lax.broadcasted_iota(I32, (LANES, tq), 0)
    qqt = jnp.concatenate([jnp.where(dim < HALF_DIM, qt, 0.0), jnp.where(dim >= HALF_DIM, qt, 0.0)],
                          axis=1).astype(BF16)

    n = (i * tq) // tk

    def scores(j):
        return _dot(k_ref[pl.ds(pl.multiple_of(j * tk, tk), tk), :], qqt)

    chunks = [slice(c0, c0 + ATT_LANES) for c0 in range(0, 2 * tq, ATT_LANES)]

    def store_scores(j, s_buf, mb_buf):
        s = scores(j)
        s_buf[...] = s
        mb_buf[...] = jnp.max(s, axis=0, keepdims=True)

    def accumulate(cols, s, block_max, vt):
        m_prev = m_ref[:, cols]
        m_new = jnp.maximum(m_prev, block_max)
        p = jnp.exp2((s - m_new).astype(BF16))
        acc_ref[:, cols] = jnp.exp2(m_prev - m_new) * acc_ref[:, cols] + _dot(vt, p)
        m_ref[:, cols] = m_new

    def pipe_step(j, cur, nxt):
        store_scores(j + 1, *nxt)
        s_cur, mb_cur = cur
        for cols in chunks:
            accumulate(cols, s_cur[:, cols], mb_cur[:, cols], vt_ref[j])

    buf0, buf1 = (s0_ref, mb0_ref), (s1_ref, mb1_ref)
    m_ref[...] = jnp.full_like(m_ref, NEG_BIG)
    acc_ref[...] = jnp.zeros_like(acc_ref)
    odd = n % 2

    @pl.when(odd == 0)
    def _():
        store_scores(0, *buf0)

    @pl.when(odd == 1)
    def _():
        store_scores(0, *buf1)
        pipe_step(0, buf1, buf0)

    def pair(t, carry):
        j = 2 * t + odd
        pipe_step(j, buf0, buf1)
        pipe_step(j + 1, buf1, buf0)
        return carry

    lax.fori_loop(0, n // 2, pair, 0)

    for cols in chunks:
        keys = cols.start % tq + ATT_LANES
        s = s0_ref[:keys, cols] + bias_ref[:keys, cols]
        accumulate(cols, s, jnp.max(s, axis=0, keepdims=True), vt_ref[n][:, :keys])
    acc = acc_ref[...]

    o = acc[:V_DIM] / acc[V_DIM:V_DIM + 1]
    o = o[:, :tq] - lam * o[:, tq:]
    o = o * lax.rsqrt(jnp.mean(o * o, axis=0, keepdims=True) + EPS) * gsub_ref[...]
    o_ref[q_rows, :] = (o * (1.0 - lam_init)).T.astype(BF16)


def _attention(q, k, vt, lq1, lk1, lq2, lk2, g_sub_col, batch, seq, lam_init):
    tq = ATT_Q
    tk = vt.shape[2]
    nk = seq // tk
    vec = lambda b, h: (0, 0)
    per_head = lambda b, h: (b, h)
    visible = (lax.broadcasted_iota(I32, (tk, 2 * tq), 0) <= lax.broadcasted_iota(I32, (tk, 2 * tq), 1) % tq)
    bias = jnp.where(visible, 0.0, NEG_BIG).astype(F32)
    return pl.pallas_call(
        functools.partial(_attn_kernel, lam_init=lam_init),
        grid=(batch, N_HEADS),
        in_specs=[
            pl.BlockSpec((1, HALF_DIM), vec),
            pl.BlockSpec((1, HALF_DIM), vec),
            pl.BlockSpec((1, HALF_DIM), vec),
            pl.BlockSpec((1, HALF_DIM), vec),
            pl.BlockSpec((V_DIM, 1), vec),
            pl.BlockSpec((tk, 2 * tq), vec),
            pl.BlockSpec((seq, LANES), per_head),
            pl.BlockSpec((seq, LANES), per_head),
            pl.BlockSpec((nk, V_EXT, tk), lambda b, h: (b, h, 0)),
        ],
        out_specs=pl.BlockSpec((seq, V_DIM), per_head),
        out_shape=jax.ShapeDtypeStruct((batch * seq, D_ATT), BF16),
        scratch_shapes=[pltpu.VMEM((1, 2 * tq), F32), pltpu.VMEM((V_EXT, 2 * tq), F32),
                        pltpu.VMEM((tk, 2 * tq), F32), pltpu.VMEM((tk, 2 * tq), F32),
                        pltpu.VMEM((1, 2 * tq), F32), pltpu.VMEM((1, 2 * tq), F32)],
        compiler_params=pltpu.CompilerParams(
            dimension_semantics=("arbitrary", "arbitrary"), vmem_limit_bytes=VMEM_LIMIT),
        name="attention",
    )(lq1, lk1, lq2, lk2, g_sub_col, bias, q, k, vt)


def _sublane_total(x, op):
    return op(x, axis=0, keepdims=True)


def _route_kernel(x_ref, yrec_ref, yatt_ref, wo32_ref, lnmoe_ref, wrt_ref, ebias_ref, tri_ref,
                  h1_ref, hpa_ref, hpb_ref, ek_ref, wk_ref, rk_ref, cnt_ref, carry_ref, wo_ref):
    tm = x_ref.shape[0]
    e_n = N_EXPERTS

    @pl.when(pl.program_id(0) == 0)
    def _():
        wo_ref[...] = wo32_ref[...].astype(BF16)
        carry_ref[...] = jnp.zeros_like(carry_ref)

    h1 = x_ref[...] + _dot(yrec_ref[...], wo_ref[:D_REC, :]) + _dot(yatt_ref[...], wo_ref[D_REC:, :])
    h1_ref[...] = h1
    hn = _rms(h1, lnmoe_ref[...])
    pa, pb = _pack_row(hn)
    hpa_ref[...] = pa
    hpb_ref[...] = pb

    def split(v):
        head = v.astype(BF16)
        return head, (v - head.astype(F32)).astype(BF16)

    nt_dot = lambda a, b: lax.dot_general(a, b, (((1,), (1,)), ((), ())), preferred_element_type=F32)
    (w_head, w_rest), (h_head, h_rest) = split(wrt_ref[...]), split(hn)
    logits = nt_dot(w_head, h_head) + (nt_dot(w_head, h_rest) + nt_dot(w_rest, h_head))
    scores = jax.nn.sigmoid(logits)
    sel = scores + ebias_ref[...]

    sel3 = sel.reshape(N_GROUPS, GROUP_SIZE, tm)
    idx3 = lax.broadcasted_iota(I32, (N_GROUPS, GROUP_SIZE, tm), 1)
    m1 = jnp.max(sel3, axis=1, keepdims=True)
    first = jnp.min(jnp.where(sel3 == m1, idx3, GROUP_SIZE), axis=1, keepdims=True)
    m2 = jnp.max(jnp.where(idx3 == first, -jnp.inf, sel3), axis=1, keepdims=True)
    gscore = (m1 + m2).reshape(N_GROUPS, tm)

    gidx = lax.broadcasted_iota(I32, (N_GROUPS, tm), 0)
    beaten = jnp.zeros((N_GROUPS, tm), I32)
    for g in range(N_GROUPS):
        other = gscore[g:g + 1, :]
        beats = (other > gscore) | ((other == gscore) & (g < gidx))
        beaten = beaten + beats.astype(I32)
    gkeep = beaten < TOPK_GROUPS
    keep = jnp.broadcast_to(gkeep.reshape(N_GROUPS, 1, tm), (N_GROUPS, GROUP_SIZE, tm)).reshape(e_n, tm)
    selm = jnp.where(keep, sel, -jnp.inf)

    eidx = lax.broadcasted_iota(I32, (e_n, tm), 0)
    remaining = selm
    picks, ek, sk = [], [], []
    for _ in range(TOP_K):
        best = jnp.max(remaining, axis=0, keepdims=True)
        first = jnp.min(jnp.where(remaining == best, eidx, e_n), axis=0, keepdims=True)
        pick = eidx == first
        picks.append(pick)
        ek.append(first)
        sk.append(_sublane_total(jnp.where(pick, scores, 0.0), jnp.sum))
        remaining = jnp.where(pick, -jnp.inf, remaining)
    chosen_f = (remaining != selm).astype(F32)
    wsum = functools.reduce(lambda a, b: a + b, sk)
    ek_ref[...] = jnp.concatenate(ek, axis=0)
    wk_ref[...] = (jnp.concatenate(sk, axis=0) * (ROUTE_SCALE / wsum)).T

    prefix = _dot(chosen_f.astype(BF16), tri_ref[...])
    rank = prefix + carry_ref[...]
    carry_new = carry_ref[...] + jnp.sum(chosen_f, axis=1, keepdims=True)
    carry_ref[...] = carry_new
    cnt_ref[...] = carry_new.astype(I32)
    rk = [_sublane_total(jnp.where(pick, rank, 0.0), jnp.sum) for pick in picks]
    rk_ref[...] = jnp.concatenate(rk, axis=0).astype(I32)


def _route(x2, y_rec, y_att, w_out, ln_moe, w_router_t, e_bias_col):
    t = x2.shape[0]
    tm = ROUTE_ROWS
    nt = t // tm
    row_map = lambda i: (i, 0)
    col_map = lambda i: (0, i)
    fixed = lambda i: (0, 0)
    tri = (lax.broadcasted_iota(I32, (tm, tm), 0) < lax.broadcasted_iota(I32, (tm, tm), 1)).astype(BF16)
    out_shapes = (
        jax.ShapeDtypeStruct((t, D_MODEL), F32),
        jax.ShapeDtypeStruct((t, PACK_W), U32),
        jax.ShapeDtypeStruct((t, PACK_W), U32),
        jax.ShapeDtypeStruct((TOP_K, t), I32),
        jax.ShapeDtypeStruct((t, TOP_K), F32),
        jax.ShapeDtypeStruct((TOP_K, t), I32),
        jax.ShapeDtypeStruct((N_EXPERTS, 1), I32),
    )
    return pl.pallas_call(
        _route_kernel,
        grid=(nt,),
        in_specs=[
            pl.BlockSpec((tm, D_MODEL), row_map),
            pl.BlockSpec((tm, D_REC), row_map),
            pl.BlockSpec((tm, D_ATT), row_map),
            pl.BlockSpec((D_REC + D_ATT, D_MODEL), fixed, pipeline_mode=pl.Buffered(1)),
            pl.BlockSpec((1, D_MODEL), fixed),
            pl.BlockSpec((N_EXPERTS, D_MODEL), fixed),
            pl.BlockSpec((N_EXPERTS, 1), fixed),
            pl.BlockSpec((tm, tm), fixed),
        ],
        out_specs=[
            pl.BlockSpec((tm, D_MODEL), row_map),
            pl.BlockSpec((tm, PACK_W), row_map),
            pl.BlockSpec((tm, PACK_W), row_map),
            pl.BlockSpec((TOP_K, tm), col_map),
            pl.BlockSpec((tm, TOP_K), row_map),
            pl.BlockSpec((TOP_K, tm), col_map),
            pl.BlockSpec((N_EXPERTS, 1), fixed),
        ],
        out_shape=out_shapes,
        scratch_shapes=[pltpu.VMEM((N_EXPERTS, 1), F32), pltpu.VMEM((D_REC + D_ATT, D_MODEL), BF16)],
        compiler_params=pltpu.CompilerParams(dimension_semantics=("arbitrary",), vmem_limit_bytes=VMEM_LIMIT),
        name="route",
    )(x2, y_rec, y_att, w_out, ln_moe, w_router_t, e_bias_col, tri)


def _plan_kernel(pad_start_ref, ek_ref, rk_ref, dest_ref):
    ek = ek_ref[...]

    def add_expert(e, base):
        return jnp.where(ek == e, pad_start_ref[e], base)

    dest_ref[...] = rk_ref[...] + lax.fori_loop(0, N_EXPERTS, add_expert, jnp.zeros_like(ek))


def _plan(pad_start, ek, rk):
    kk, t = ek.shape
    tl = min(t, 8192)
    col_map = lambda i, ps: (0, i)
    grid_spec = pltpu.PrefetchScalarGridSpec(
        num_scalar_prefetch=1,
        grid=(t // tl,),
        in_specs=[pl.BlockSpec((kk, tl), col_map), pl.BlockSpec((kk, tl), col_map)],
        out_specs=pl.BlockSpec((kk, tl), col_map),
    )
    return pl.pallas_call(
        _plan_kernel,
        grid_spec=grid_spec,
        out_shape=jax.ShapeDtypeStruct((kk, t), I32),
        compiler_params=pltpu.CompilerParams(dimension_semantics=("arbitrary",)),
        name="plan",
    )(pad_start, ek, rk)


def _sc_mesh():
    return plsc.VectorSubcoreMesh(core_axis_name="core", subcore_axis_name="subcore")


def _sc_dispatch(rows_a, rows_b, dest, n_out):
    t, w = rows_a.shape
    kk = dest.shape[0]
    out = jax.ShapeDtypeStruct((n_out, w), rows_a.dtype)

    @pl.kernel(out_type=(out, out), mesh=_sc_mesh(), scratch_types=[])
    def kern(xa_hbm, xb_hbm, i_hbm, oa_hbm, ob_hbm):
        for x_hbm, o_hbm in ((xa_hbm, oa_hbm), (xb_hbm, ob_hbm)):
            def body(x_vmem, i_vmem, o_hbm=o_hbm):
                for k in range(kk):
                    pltpu.sync_copy(x_vmem, o_hbm.at[i_vmem.at[k]])

            pltpu.emit_pipeline(
                body,
                grid=(t // SC_WINDOW,),
                in_specs=[pl.BlockSpec((SC_WINDOW, w), lambda i: (i, 0)),
                          pl.BlockSpec((kk, SC_WINDOW), lambda i: (0, i))],
                out_specs=[],
                core_axis_name=("core", "subcore"),
                dimension_semantics=(pltpu.PARALLEL,),
            )(x_hbm, i_hbm)

    return kern(rows_a, rows_b, dest)


def _sc_combine(rows_a, rows_b, dest):
    kk, t = dest.shape
    w = rows_a.shape[1]
    flat = dest.reshape(1, kk * t)
    out = jax.ShapeDtypeStruct((kk * t, w), rows_a.dtype)

    @pl.kernel(out_type=(out, out), mesh=_sc_mesh(), scratch_types=[])
    def kern(ya_hbm, yb_hbm, i_hbm, oa_hbm, ob_hbm):
        for y_hbm, o_hbm in ((ya_hbm, oa_hbm), (yb_hbm, ob_hbm)):
            def body(i_vmem, o_vmem, y_hbm=y_hbm):
                pltpu.sync_copy(y_hbm.at[i_vmem.at[0]], o_vmem)

            pltpu.emit_pipeline(
                body,
                grid=(kk * t // SC_WINDOW,),
                in_specs=[pl.BlockSpec((1, SC_WINDOW), lambda i: (0, i))],
                out_specs=[pl.BlockSpec((SC_WINDOW, w), lambda i: (i, 0))],
                core_axis_name=("core", "subcore"),
                dimension_semantics=(pltpu.PARALLEL,),
            )(i_hbm, o_hbm)

    ga, gb = kern(rows_a, rows_b, flat)
    return ga.reshape(kk, t, w), gb.reshape(kk, t, w)


X_SLOTS = 3
Y_SLOTS = 2


def _experts_kernel(blk_expert_ref, n_used_ref, first_ref, slot_ref, next_ref,
                    xa_hbm, xb_hbm, w1_hbm, w3_hbm, w2_hbm, ya_hbm, yb_hbm,
                    xa_buf, xb_buf, ya_buf, yb_buf, w1f_ref, w3f_ref, w2f_ref, w1b_ref, w3b_ref, w2b_ref,
                    wsem, xsem, ysem):
    m = EXPERT_ROWS
    n_used = n_used_ref[0]

    def weight_copies(e, s):
        return (pltpu.make_async_copy(w1_hbm.at[e], w1f_ref.at[s], wsem.at[s, 0]),
                pltpu.make_async_copy(w3_hbm.at[e], w3f_ref.at[s], wsem.at[s, 1]),
                pltpu.make_async_copy(w2_hbm.at[e], w2f_ref.at[s], wsem.at[s, 2]))

    def x_copies(b):
        rows, s = pl.ds(pl.multiple_of(b * m, m), m), b % X_SLOTS
        return (pltpu.make_async_copy(xa_hbm.at[rows], xa_buf.at[s], xsem.at[s, 0]),
                pltpu.make_async_copy(xb_hbm.at[rows], xb_buf.at[s], xsem.at[s, 1]))

    def y_copies(b):
        rows, s = pl.ds(pl.multiple_of(b * m, m), m), b % Y_SLOTS
        return (pltpu.make_async_copy(ya_buf.at[s], ya_hbm.at[rows], ysem.at[s, 0]),
                pltpu.make_async_copy(yb_buf.at[s], yb_hbm.at[rows], ysem.at[s, 1]))

    def start(copies):
        for copy in copies:
            copy.start()

    def wait(copies):
        for copy in copies:
            copy.wait()

    start(weight_copies(blk_expert_ref[0], 0))
    start(x_copies(0))

    @pl.when(n_used > 1)
    def _():
        start(x_copies(1))

    def block(b, carry):
        @pl.when(b + 2 < n_used)
        def _():
            start(x_copies(b + 2))

        @pl.when(first_ref[b] == 1)
        def _():
            s = slot_ref[b]
            wait(weight_copies(blk_expert_ref[b], s))

            @pl.when(next_ref[b] >= 0)
            def _():
                start(weight_copies(next_ref[b], 1 - s))

            w1b_ref[...] = w1f_ref[s].astype(BF16)
            w3b_ref[...] = w3f_ref[s].astype(BF16)
            w2b_ref[...] = w2f_ref[s].astype(BF16)

        wait(x_copies(b))

        @pl.when(b >= Y_SLOTS)
        def _():
            wait(y_copies(b - Y_SLOTS))

        x = _unpack_row(xa_buf[b % X_SLOTS], xb_buf[b % X_SLOTS]).astype(BF16)
        a = _dot(x, w1b_ref[...])
        g = _dot(x, w3b_ref[...])
        hmid = (jax.nn.silu(a) * g).astype(BF16)
        y = _dot(hmid, w2b_ref[...])
        pa, pb = _pack_row(y)
        ya_buf[b % Y_SLOTS] = pa
        yb_buf[b % Y_SLOTS] = pb
        start(y_copies(b))
        return carry

    lax.fori_loop(0, n_used, block, 0)

    for back in range(Y_SLOTS, 0, -1):
        @pl.when(n_used - back >= 0)
        def _(back=back):
            wait(y_copies(n_used - back))


def _experts(xa, xb, w1, w3, w2, blk_expert, n_used, seg_first, seg_slot, seg_next):
    p = xa.shape[0]
    m = EXPERT_ROWS
    hbm = pl.BlockSpec(memory_space=pl.ANY)
    grid_spec = pltpu.PrefetchScalarGridSpec(
        num_scalar_prefetch=5,
        grid=(1,),
        in_specs=[hbm] * 5,
        out_specs=[hbm, hbm],
        scratch_shapes=[
            pltpu.VMEM((X_SLOTS, m, PACK_W), U32), pltpu.VMEM((X_SLOTS, m, PACK_W), U32),
            pltpu.VMEM((Y_SLOTS, m, PACK_W), U32), pltpu.VMEM((Y_SLOTS, m, PACK_W), U32),
            pltpu.VMEM((2, D_MODEL, D_EXPERT), F32), pltpu.VMEM((2, D_MODEL, D_EXPERT), F32),
            pltpu.VMEM((2, D_EXPERT, D_MODEL), F32),
            pltpu.VMEM((D_MODEL, D_EXPERT), BF16), pltpu.VMEM((D_MODEL, D_EXPERT), BF16),
            pltpu.VMEM((D_EXPERT, D_MODEL), BF16),
            pltpu.SemaphoreType.DMA((2, 3)), pltpu.SemaphoreType.DMA((X_SLOTS, 2)),
            pltpu.SemaphoreType.DMA((Y_SLOTS, 2)),
        ],
    )
    return pl.pallas_call(
        _experts_kernel,
        grid_spec=grid_spec,
        out_shape=(jax.ShapeDtypeStruct((p, PACK_W), U32), jax.ShapeDtypeStruct((p, PACK_W), U32)),
        compiler_params=pltpu.CompilerParams(dimension_semantics=("arbitrary",), vmem_limit_bytes=VMEM_LIMIT),
        name="experts",
    )(blk_expert, n_used, seg_first, seg_slot, seg_next, xa, xb, w1, w3, w2)


def _tail_kernel(h1_ref, ga_ref, gb_ref, wk_ref, p_ref, lnmoe_ref, ws1_32, ws3_32, ws2_32, lnple_ref,
                 wpg_32, wpp_32, lnf_ref, o_ref, ws1_ref, ws3_ref, ws2_ref, wpg_ref, wpp_ref):
    @pl.when(pl.program_id(0) == 0)
    def _():
        for dst, src in ((ws1_ref, ws1_32), (ws3_ref, ws3_32), (ws2_ref, ws2_32), (wpg_ref, wpg_32),
                         (wpp_ref, wpp_32)):
            dst[...] = src[...].astype(BF16)

    h1 = h1_ref[...]
    hn = _rms(h1, lnmoe_ref[...]).astype(BF16)
    shared = _dot((jax.nn.silu(_dot(hn, ws1_ref[...])) * _dot(hn, ws3_ref[...])).astype(BF16), ws2_ref[...])
    wk = wk_ref[...]
    routed = jnp.zeros_like(h1)
    for kk in range(TOP_K):
        routed = routed + wk[:, kk:kk + 1] * _unpack_row(ga_ref[kk], gb_ref[kk])
    h2 = h1 + routed + shared
    gate = jax.nn.sigmoid(_dot(_rms(h2, lnple_ref[...]).astype(BF16), wpg_ref[...]))
    h3 = h2 + gate * _dot(p_ref[...].astype(BF16), wpp_ref[...])
    o_ref[...] = _rms(h3, lnf_ref[...])


def _tail(h1, ga, gb, wk_t, p2, ln_moe, ws1, ws3, ws2, ln_ple, w_pg, w_pp, ln_f):
    t = h1.shape[0]
    tm = TAIL_ROWS
    row_map = lambda i: (i, 0)
    fixed = lambda i: (0, 0)
    g_map = lambda i: (0, i, 0)
    d_sh = ws1.shape[1]
    return pl.pallas_call(
        _tail_kernel,
        grid=(t // tm,),
        in_specs=[
            pl.BlockSpec((tm, D_MODEL), row_map),
            pl.BlockSpec((TOP_K, tm, PACK_W), g_map),
            pl.BlockSpec((TOP_K, tm, PACK_W), g_map),
            pl.BlockSpec((tm, TOP_K), row_map),
            pl.BlockSpec((tm, D_PLE), row_map),
            pl.BlockSpec((1, D_MODEL), fixed),
            pl.BlockSpec((D_MODEL, d_sh), fixed, pipeline_mode=pl.Buffered(1)),
            pl.BlockSpec((D_MODEL, d_sh), fixed, pipeline_mode=pl.Buffered(1)),
            pl.BlockSpec((d_sh, D_MODEL), fixed, pipeline_mode=pl.Buffered(1)),
            pl.BlockSpec((1, D_MODEL), fixed),
            pl.BlockSpec((D_MODEL, D_MODEL), fixed, pipeline_mode=pl.Buffered(1)),
            pl.BlockSpec((D_PLE, D_MODEL), fixed, pipeline_mode=pl.Buffered(1)),
            pl.BlockSpec((1, D_MODEL), fixed),
        ],
        out_specs=pl.BlockSpec((tm, D_MODEL), row_map),
        out_shape=jax.ShapeDtypeStruct((t, D_MODEL), F32),
        scratch_shapes=[pltpu.VMEM((D_MODEL, d_sh), BF16), pltpu.VMEM((D_MODEL, d_sh), BF16),
                        pltpu.VMEM((d_sh, D_MODEL), BF16), pltpu.VMEM((D_MODEL, D_MODEL), BF16),
                        pltpu.VMEM((D_PLE, D_MODEL), BF16)],
        compiler_params=pltpu.CompilerParams(dimension_semantics=("arbitrary",), vmem_limit_bytes=VMEM_LIMIT),
        name="tail",
    )(h1, ga, gb, wk_t, p2, ln_moe, ws1, ws3, ws2, ln_ple, w_pg, w_pp, ln_f)


def _rope_constants():
    half = ROPE_DIM // 2
    inv_freq = (ROPE_THETA ** (-jnp.arange(0, ROPE_DIM, 2, dtype=F32) / ROPE_DIM)).reshape(half, 1)
    f = lax.broadcasted_iota(I32, (ROPE_DIM, LANES), 0)
    l64 = lax.broadcasted_iota(I32, (ROPE_DIM, LANES), 1) % HALF_DIM
    cos_pat = ((f < half) & (l64 < ROPE_DIM) & (l64 % half == f)).astype(F32)
    sa_pat = -((f >= half) & (l64 < half) & (l64 == f - half)).astype(F32)
    sb_pat = ((f >= half) & (l64 >= half) & (l64 < ROPE_DIM) & (l64 - half == f - half)).astype(F32)
    return inv_freq, jnp.concatenate([cos_pat, sa_pat, sb_pat], axis=1)


def _block_diag_tiles(w):
    nb, bd, _ = w.shape
    per = nb // 2
    tiles = []
    for tix in range(2):
        rows = []
        for j in range(per):
            rows.append(jnp.concatenate(
                [w[tix * per + j] if c == j else jnp.zeros((bd, bd), w.dtype) for c in range(per)], axis=1))
        tiles.append(jnp.concatenate(rows, axis=0))
    return jnp.stack(tiles).astype(BF16)


def _layer(h, p_l, positions, lam_init, ln_mix, w_in, conv_w, conv_b, w_a, b_a, w_i, b_i, rg_lambda, g_rec,
           lq1, lk1, lq2, lk2, g_sub, w_out, ln_moe, w_router, e_bias, w1, w3, w2, ws1, ws3, ws2,
           ln_ple, w_ple_gate, w_ple_proj, ln_out):
    batch, seq, _ = h.shape
    t = batch * seq
    x2 = h.reshape(t, D_MODEL)
    row = lambda a: a.reshape(1, -1)
    inv_freq, rope_pat = _rope_constants()

    y_rec, q, k, vt = _mix_in(
        x2, positions.reshape(1, t), inv_freq, rope_pat, row(ln_mix), w_in, conv_w, row(conv_b),
        _block_diag_tiles(w_a), row(b_a), _block_diag_tiles(w_i), row(b_i), row(rg_lambda), row(g_rec),
        batch, seq)
    y_att = _attention(q, k, vt, row(lq1), row(lk1), row(lq2), row(lk2), g_sub.reshape(-1, 1), batch, seq,
                       lam_init)

    h1, hpa, hpb, ek, wk_t, rk, counts = _route(
        x2, y_rec, y_att, w_out, row(ln_moe), w_router.T, e_bias.reshape(-1, 1))

    m = EXPERT_ROWS
    counts = counts.reshape(-1)
    padded = (counts + m - 1) // m * m
    pad_end = jnp.cumsum(padded)
    pad_start = pad_end - padded
    n_rows = t * TOP_K + N_EXPERTS * m
    nblk = n_rows // m
    n_used = (pad_end[-1] // m).astype(I32).reshape(1)
    blk = jnp.arange(nblk, dtype=I32)
    blk_row = jnp.minimum(blk, n_used[0] - 1) * m
    blk_expert = jnp.sum((pad_end[None, :] <= blk_row[:, None]).astype(I32), axis=1)
    prev_expert = jnp.concatenate([jnp.full((1,), -1, I32), blk_expert[:-1]])
    seg_first = ((blk < n_used[0]) & (blk_expert != prev_expert)).astype(I32)
    seg_slot = ((jnp.cumsum(seg_first) - 1) % 2).astype(I32)
    eid = jnp.arange(N_EXPERTS, dtype=I32)
    later = (padded[None, :] > 0) & (eid[None, :] > eid[:, None])
    next_expert = jnp.min(jnp.where(later, eid[None, :], N_EXPERTS), axis=1)
    next_expert = jnp.where(next_expert == N_EXPERTS, -1, next_expert).astype(I32)
    seg_next = jnp.sum(jnp.where(blk_expert[:, None] == eid[None, :], next_expert[None, :], 0), axis=1)
    dest = _plan(pad_start.astype(I32), ek, rk)

    xa, xb = _sc_dispatch(hpa, hpb, dest, n_rows)
    ya, yb = _experts(xa, xb, w1, w3, w2, blk_expert, n_used, seg_first, seg_slot, seg_next)
    ga, gb = _sc_combine(ya, yb, dest)

    out = _tail(h1, ga, gb, wk_t, p_l.reshape(t, D_PLE), row(ln_moe), ws1, ws3, ws2, row(ln_ple), w_ple_gate,
                w_ple_proj, row(ln_out))
    return out.reshape(batch, seq, D_MODEL)


def kernel(x, p, positions, ln_mix, w_in, conv_w, conv_b, w_a, b_a, w_i, b_i, rg_lambda, g_rec, lq1, lk1, lq2,
           lk2, g_sub, w_out, ln_moe, w_router, e_bias, w1, w3, w2, ws1, ws3, ws2, ln_ple, w_ple_gate,
           w_ple_proj, ln_f):
    depth = w_in.shape[0]
    assert depth == 1, "the fused tail applies the final norm; one layer supported"
    lam_init = 0.8 - 0.6 * math.exp(-0.3 * 0)
    return _layer(x, p[0], positions, lam_init, ln_mix[0], w_in[0], conv_w[0], conv_b[0], w_a[0], b_a[0], w_i[0],
                  b_i[0], rg_lambda[0], g_rec[0], lq1[0], lk1[0], lq2[0], lk2[0], g_sub[0], w_out[0], ln_moe[0],
                  w_router[0], e_bias[0], w1[0], w3[0], w2[0], ws1[0], ws3[0], ws2[0], ln_ple[0], w_ple_gate[0],
                  w_ple_proj[0], ln_f)
```
